```python
import jax, jax.numpy as jnp
from jax import lax
import numpy as np

D_MODEL = 2048
BATCH = 2
SEQ = 4096
DEPTH = 1

CHUNK = 64
MIX_WIDTH = D_MODEL
RWKV_WIDTH = MIX_WIDTH // 2
ATTN_WIDTH = MIX_WIDTH - RWKV_WIDTH
HEAD_DIM = 64
RWKV_HEADS = RWKV_WIDTH // HEAD_DIM
ATTN_HEADS = ATTN_WIDTH // HEAD_DIM
DECAY_LORA = 64
AAA_LORA = 64
GATE_LORA = 160
LEFT_CHUNKS = 8
BAND = (LEFT_CHUNKS + 1) * CHUNK
REL_CLIP = 128
N_REL = 2 * REL_CLIP + 1
N_GROUPS = 4
EXPERTS_PER_GROUP = 8
TOP_K = 2
EXPERT_HIDDEN = 512
RMS_EPS = 1e-6
GN_EPS = 64e-5
L2_EPS = 1e-12
NEG_INF = -1e30
SHIFT_WIDTH = 3 * RWKV_WIDTH + DECAY_LORA + AAA_LORA + GATE_LORA
PROJ_WIDTH = SHIFT_WIDTH + 3 * ATTN_WIDTH

kernel_name = "hymba_rwkv7_chunkattn_hiermoe_adaln"


def rms_norm(x, gain):
    xf = x.astype(jnp.float32)
    y = xf * lax.rsqrt(jnp.mean(xf * xf, axis=-1, keepdims=True) + RMS_EPS)
    return (y * gain.astype(jnp.float32)).astype(x.dtype)


def modulate(h, shift, scale):
    return h * (1 + scale[:, None, :]) + shift[:, None, :]


def token_shift(p, mu):
    p_prev = jnp.pad(p, ((0, 0), (1, 0), (0, 0)))[:, :-1]
    return p + (p_prev - p) * mu


def rwkv7_mix(p, mu_shift, w0, w_decay_up, a0, w_aaa_up, w_gate_up, k_k, k_a, r_k, ln_x_w, ln_x_b):
    B, T, _ = p.shape
    C, H, N = RWKV_WIDTH, RWKV_HEADS, HEAD_DIM
    p = token_shift(p, mu_shift)
    r = p[..., :C]
    k = p[..., C:2 * C]
    v = p[..., 2 * C:3 * C]
    o = 3 * C
    xw = p[..., o:o + DECAY_LORA]
    o += DECAY_LORA
    xa = p[..., o:o + AAA_LORA]
    o += AAA_LORA
    xg = p[..., o:o + GATE_LORA]
    w_log = -jax.nn.softplus(-(w0 + jnp.tanh(xw) @ w_decay_up)) - 0.5
    decay = jnp.exp(-jnp.exp(w_log.astype(jnp.float32)))
    a = jax.nn.sigmoid(a0 + xa @ w_aaa_up)
    g = jax.nn.sigmoid(xg) @ w_gate_up
    kk = (k * k_k).reshape(B, T, H, N).astype(jnp.float32)
    kk = kk / jnp.maximum(jnp.sqrt(jnp.sum(kk * kk, axis=-1, keepdims=True)), L2_EPS)
    k = k * (1 + (a - 1) * k_a)
    heads = lambda z: z.reshape(B, T, H, N).astype(jnp.float32)
    r_h, k_h, v_h, w_h, a_h = heads(r), heads(k), heads(v), heads(decay), heads(a)
    xs = tuple(jnp.moveaxis(z, 1, 0) for z in (r_h, w_h, k_h, v_h, kk, a_h))

    def step(S, inp):
        r_t, w_t, k_t, v_t, kk_t, a_t = inp
        sa = jnp.einsum('bhvk,bhk->bhv', S, -kk_t)
        S = (S * w_t[:, :, None, :] + sa[..., None] * (kk_t * a_t)[:, :, None, :]
             + v_t[..., None] * k_t[:, :, None, :])
        y = jnp.einsum('bhvk,bhk->bhv', S, r_t)
        return S, y

    S0 = jnp.zeros((B, H, N, N), jnp.float32)
    _, ys = lax.scan(step, S0, xs)
    y = jnp.moveaxis(ys, 0, 1)
    mean = jnp.mean(y, axis=-1, keepdims=True)
    var = jnp.mean(jnp.square(y - mean), axis=-1, keepdims=True)
    y = ((y - mean) * lax.rsqrt(var + GN_EPS)).reshape(B, T, C)
    y = y * ln_x_w.astype(jnp.float32) + ln_x_b.astype(jnp.float32)
    bonus = jnp.sum(r_h * k_h * r_k.astype(jnp.float32), axis=-1, keepdims=True) * v_h
    y = (y + bonus.reshape(B, T, C)) * g.astype(jnp.float32)
    return y.astype(p.dtype)


def chunk_band_attention(p, rel_bias):
    B, T, _ = p.shape
    H, Dh, W = ATTN_HEADS, HEAD_DIM, ATTN_WIDTH
    q = p[..., :W].reshape(B, T, H, Dh)
    k = p[..., W:2 * W].reshape(B, T, H, Dh)
    v = p[..., 2 * W:3 * W].reshape(B, T, H, Dh)
    pad = LEFT_CHUNKS * CHUNK
    kp = jnp.pad(k, ((0, 0), (pad, 0), (0, 0), (0, 0)))
    vp = jnp.pad(v, ((0, 0), (pad, 0), (0, 0), (0, 0)))
    n_chunks = T // CHUNK
    rel = jnp.arange(CHUNK)[:, None] - jnp.arange(BAND)[None, :] + pad
    rel_idx = jnp.clip(rel, -REL_CLIP, REL_CLIP) + REL_CLIP
    bias = rel_bias[:, rel_idx].astype(jnp.float32)
    scale = HEAD_DIM ** -0.5

    def one_chunk(i):
        start = i * CHUNK
        q_c = lax.dynamic_slice_in_dim(q, start, CHUNK, axis=1)
        k_c = lax.dynamic_slice_in_dim(kp, start, BAND, axis=1)
        v_c = lax.dynamic_slice_in_dim(vp, start, BAND, axis=1)
        s = jnp.einsum('bqhd,bkhd->bhqk', q_c, k_c).astype(jnp.float32) * scale + bias
        key_pos = start - pad + jnp.arange(BAND)
        s = jnp.where((key_pos >= 0)[None, None, None, :], s, NEG_INF)
        prob = jax.nn.softmax(s, axis=-1).astype(v_c.dtype)
        return jnp.einsum('bhqk,bkhd->bqhd', prob, v_c)

    out = lax.map(one_chunk, jnp.arange(n_chunks))
    return jnp.moveaxis(out, 0, 1).reshape(B, T, W)


def hier_moe(h, w_group, b_group, w_expert, b_expert, w_gate, w_up, w_down):
    B, T, D = h.shape
    t = h.reshape(B * T, D)
    gl = (t @ w_group).astype(jnp.float32) + b_group.astype(jnp.float32)
    gp = jax.nn.softmax(gl, axis=-1)
    g_idx = jnp.argmax(gl, axis=-1)
    g_w = jnp.take_along_axis(gp, g_idx[:, None], axis=1)
    el = jnp.einsum('nd,gde->nge', t, w_expert).astype(jnp.float32) + b_expert.astype(jnp.float32)
    el_sel = jnp.take_along_axis(el, g_idx[:, None, None], axis=1)[:, 0]
    ep = jax.nn.softmax(el_sel, axis=-1)
    top_p, top_i = lax.top_k(ep, TOP_K)
    top_p = top_p / jnp.sum(top_p, axis=-1, keepdims=True)
    e_w = jnp.sum(jax.nn.one_hot(top_i, EXPERTS_PER_GROUP, dtype=jnp.float32) * top_p[..., None], axis=1)
    comb = jax.nn.one_hot(g_idx, N_GROUPS, dtype=jnp.float32)[:, :, None] * (g_w * e_w)[:, None, :]
    y = jnp.zeros((B * T, D), jnp.float32)
    for gi in range(N_GROUPS):
        hg = jax.nn.silu(jnp.einsum('nd,edf->nef', t, w_gate[gi])) * jnp.einsum('nd,edf->nef', t, w_up[gi])
        hg = hg * comb[:, gi, :, None].astype(hg.dtype)
        y = y + jnp.einsum('nef,efd->nd', hg, w_down[gi]).astype(jnp.float32)
    return y.astype(h.dtype).reshape(B, T, D)


def setup_inputs(seed: int = 0) -> dict:
    key = jax.random.key(seed)
    ks = iter(jax.random.split(key, 40))
    nrm = lambda shape, s: jax.random.normal(next(ks), shape, jnp.float32) * s
    L, D, C = DEPTH, D_MODEL, RWKV_WIDTH
    G, E, F = N_GROUPS, EXPERTS_PER_GROUP, EXPERT_HIDDEN
    return {
        "x": nrm((BATCH, SEQ, D), 1.0),
        "c": nrm((BATCH, D), 1.0),
        "w_ada": nrm((L, D, 6 * D), 0.3 * D ** -0.5),
        "b_ada": nrm((L, 6 * D), 0.02),
        "norm1_g": 1.0 + nrm((L, D), 0.02),
        "w_in": nrm((L, D, PROJ_WIDTH), D ** -0.5),
        "mu_shift": jax.random.uniform(next(ks), (L, SHIFT_WIDTH), jnp.float32),
        "w0": jax.random.uniform(next(ks), (L, C), jnp.float32, -6.0, 1.0),
        "w_decay_up": nrm((L, DECAY_LORA, C), 0.5 * DECAY_LORA ** -0.5),
        "a0": nrm((L, C), 0.5),
        "w_aaa_up": nrm((L, AAA_LORA, C), 0.5 * AAA_LORA ** -0.5),
        "w_gate_up": nrm((L, GATE_LORA, C), GATE_LORA ** -0.5),
        "k_k": 0.85 + nrm((L, C), 0.1),
        "k_a": 1.0 + nrm((L, C), 0.1),
        "r_k": nrm((L, RWKV_HEADS, HEAD_DIM), 0.1),
        "ln_x_w": 1.0 + nrm((L, C), 0.02),
        "ln_x_b": nrm((L, C), 0.02),
        "rel_bias": nrm((L, ATTN_HEADS, N_REL), 0.5),
        "beta_rwkv": 1.0 + nrm((L, RWKV_WIDTH), 0.02),
        "beta_attn": 1.0 + nrm((L, ATTN_WIDTH), 0.02),
        "w_out": nrm((L, MIX_WIDTH, D), MIX_WIDTH ** -0.5),
        "norm2_g": 1.0 + nrm((L, D), 0.02),
        "w_group": nrm((L, D, G), D ** -0.5),
        "b_group": nrm((L, G), 0.01),
        "w_expert": nrm((L, G, D, E), D ** -0.5),
        "b_expert": nrm((L, G, E), 0.01),
        "w_gate": nrm((L, G, E, D, F), D ** -0.5),
        "w_up": nrm((L, G, E, D, F), D ** -0.5),
        "w_down": nrm((L, G, E, F, D), F ** -0.5),
        "norm_f_g": 1.0 + nrm((D,), 0.02),
    }


def reference(x, c, w_ada, b_ada, norm1_g, w_in, mu_shift, w0, w_decay_up, a0, w_aaa_up,
              w_gate_up, k_k, k_a, r_k, ln_x_w, ln_x_b, rel_bias, beta_rwkv, beta_attn,
              w_out, norm2_g, w_group, b_group, w_expert, b_expert, w_gate, w_up, w_down,
              norm_f_g):
    c_act = jax.nn.silu(c)
    for l in range(DEPTH):
        mod = c_act @ w_ada[l] + b_ada[l]
        shift1, scale1, gate1, shift2, scale2, gate2 = jnp.split(mod, 6, axis=-1)
        h = modulate(rms_norm(x, norm1_g[l]), shift1, scale1)
        p = h @ w_in[l]
        y_r = rwkv7_mix(p[..., :SHIFT_WIDTH], mu_shift[l], w0[l], w_decay_up[l], a0[l],
                        w_aaa_up[l], w_gate_up[l], k_k[l], k_a[l], r_k[l], ln_x_w[l], ln_x_b[l])
        y_a = chunk_band_attention(p[..., SHIFT_WIDTH:], rel_bias[l])
        y = jnp.concatenate([y_r * beta_rwkv[l], y_a * beta_attn[l]], axis=-1) @ w_out[l]
        x = x + gate1[:, None, :] * y
        h2 = modulate(rms_norm(x, norm2_g[l]), shift2, scale2)
        f = hier_moe(h2, w_group[l], b_group[l], w_expert[l], b_expert[l],
                     w_gate[l], w_up[l], w_down[l])
        x = x + gate2[:, None, :] * f
    return rms_norm(x, norm_f_g)
```

```python
import functools

import jax
import jax.numpy as jnp
from jax import lax
from jax.experimental import pallas as pl
from jax.experimental.pallas import tpu as pltpu

F32 = jnp.float32
BF16 = jnp.bfloat16
HI = lax.Precision.HIGHEST

LANES = 128
HEAD_DIM = 64
PAIR = 2 * HEAD_DIM
CHUNK = 64
LEFT_CHUNKS = 8
BAND = (LEFT_CHUNKS + 1) * CHUNK
REL_CLIP = 128
DECAY_LORA = 64
AAA_LORA = 64
GATE_LORA = 160
N_GROUPS = 4
EXPERTS_PER_GROUP = 8
N_EXPERTS = N_GROUPS * EXPERTS_PER_GROUP
RMS_EPS = 1e-6
GN_EPS = 64e-5
L2_EPS = 1e-12
NEG_INF = -1e30
VMEM_LIMIT = 56 * 1024 * 1024

SHIFT1, SCALE1, GATE1, SHIFT2, SCALE2, GATE2 = range(6)


def _dot(a, b, precision=None):
    return jnp.dot(a, b, precision=precision, preferred_element_type=F32)


def _dot_nt(a, b, precision=None):
    return lax.dot_general(a, b, (((1,), (1,)), ((), ())), precision=precision,
                           preferred_element_type=F32)


def _sigmoid(x):
    return 1.0 / (1.0 + jnp.exp(-x))


def _rms(x):
    return x * lax.rsqrt(jnp.mean(x * x, axis=-1, keepdims=True) + RMS_EPS)


def _ada_kernel(c_ref, w_ref, b_ref, o_ref):
    c = c_ref[...]
    o_ref[...] = _dot(c * _sigmoid(c), w_ref[...], HI) + b_ref[...]


def _ada_call(c8, w, b, tn):
    rows, d = c8.shape
    n = w.shape[1]
    return pl.pallas_call(
        _ada_kernel,
        grid=(n // tn,),
        in_specs=[pl.BlockSpec((rows, d), lambda j: (0, 0)),
                  pl.BlockSpec((d, tn), lambda j: (0, j)),
                  pl.BlockSpec((1, tn), lambda j: (0, j))],
        out_specs=pl.BlockSpec((rows, tn), lambda j: (0, j)),
        out_shape=jax.ShapeDtypeStruct((rows, n), F32),
        compiler_params=pltpu.CompilerParams(vmem_limit_bytes=VMEM_LIMIT),
    )(c8, w, b)


def _proj_kernel(x_ref, g_ref, mod_ref, w_ref, o_ref, h_scr):
    @pl.when(pl.program_id(1) == 0)
    def _():
        m = mod_ref[0]
        h = _rms(x_ref[...]) * g_ref[...] * (1.0 + m[SCALE1:SCALE1 + 1]) + m[SHIFT1:SHIFT1 + 1]
        h_scr[...] = h.astype(BF16)

    o_ref[...] = _dot(h_scr[...], w_ref[...].astype(BF16)).astype(o_ref.dtype)


def _proj_call(x2d, g, mod, w, n_out, tn, tm, seq, out_dtype):
    rows, d = x2d.shape
    tiles_per_seq = seq // tm
    return pl.pallas_call(
        _proj_kernel,
        grid=(rows // tm, n_out // tn),
        in_specs=[pl.BlockSpec((tm, d), lambda i, j: (i, 0)),
                  pl.BlockSpec((1, d), lambda i, j: (0, 0)),
                  pl.BlockSpec((1, 6, d), lambda i, j: (i // tiles_per_seq, 0, 0)),
                  pl.BlockSpec((d, tn), lambda i, j: (0, j))],
        out_specs=pl.BlockSpec((tm, tn), lambda i, j: (i, j)),
        out_shape=jax.ShapeDtypeStruct((rows, n_out), out_dtype),
        scratch_shapes=[pltpu.VMEM((tm, d), BF16)],
        compiler_params=pltpu.CompilerParams(
            dimension_semantics=("parallel", "arbitrary"), vmem_limit_bytes=VMEM_LIMIT),
    )(x2d, g, mod, w)


def _rwkv_kernel(p_ref, mu_ref, w0_ref, a0_ref, kk_ref, ka_ref, rk_ref, lnw_ref, lnb_ref, beta_ref,
                 wd_ref, wa_ref, wg_ref, o_ref,
                 carry_scr, state_scr, at_scr, bt_scr, kt_scr, rt_scr, bh_scr, kh_scr, v_scr,
                 et_scr, y_scr, *, width, tb):
    n_pairs = width // PAIR
    n_chunks = tb // CHUNK

    @pl.when(pl.program_id(1) == 0)
    def _():
        carry_scr[...] = jnp.zeros_like(carry_scr)
        state_scr[...] = jnp.zeros_like(state_scr)

    p = p_ref[...]
    row = lax.broadcasted_iota(jnp.int32, (tb, 1), 0)
    prev = jnp.where(row == 0, carry_scr[...], pltpu.roll(p, 1, axis=0))
    carry_scr[...] = p[tb - 1:tb, :]
    ps = p + (prev - p) * mu_ref[...]

    c = width
    r = ps[:, :c]
    k = ps[:, c:2 * c]
    v = ps[:, 2 * c:3 * c]
    o = 3 * c
    xw = ps[:, o:o + DECAY_LORA]
    xa = ps[:, o + DECAY_LORA:o + DECAY_LORA + AAA_LORA]
    xg = ps[:, o + DECAY_LORA + AAA_LORA:o + DECAY_LORA + AAA_LORA + GATE_LORA]

    z = w0_ref[...] + _dot(jnp.tanh(xw), wd_ref[...], HI)
    softplus_neg_z = jnp.maximum(-z, 0.0) + jnp.log(1.0 + jnp.exp(-jnp.abs(z)))
    lw = -jnp.exp(-softplus_neg_z - 0.5)
    a = _sigmoid(a0_ref[...] + _dot(xa, wa_ref[...], HI))
    g = _dot(_sigmoid(xg), wg_ref[...], HI)

    li = lax.broadcasted_iota(jnp.int32, (PAIR, PAIR), 0)
    lj = lax.broadcasted_iota(jnp.int32, (PAIR, PAIR), 1)
    head_ones = ((li // HEAD_DIM) == (lj // HEAD_DIM)).astype(F32)

    def head_sum(t):
        return jnp.concatenate(
            [_dot(t[:, q * PAIR:(q + 1) * PAIR], head_ones, HI) for q in range(n_pairs)], axis=1)

    kk = k * kk_ref[...]
    kk = kk / jnp.maximum(jnp.sqrt(head_sum(kk * kk)), L2_EPS)
    k2 = k * (1.0 + (a - 1.0) * ka_ref[...])

    ti = lax.broadcasted_iota(jnp.int32, (tb, tb), 0)
    tj = lax.broadcasted_iota(jnp.int32, (tb, tb), 1)
    same_chunk = (ti // CHUNK) == (tj // CHUNK)
    cum = _dot((same_chunk & (tj <= ti)).astype(F32), lw, HI)
    tot = _dot(same_chunk.astype(F32), lw, HI)

    e_neg = jnp.exp(-cum)
    e_rem = jnp.exp(tot - cum)
    kka = kk * a
    staged = ((at_scr, -kk * jnp.exp(cum - lw)), (bt_scr, kka * e_neg), (kt_scr, k2 * e_neg),
              (rt_scr, r * jnp.exp(cum)), (bh_scr, kka * e_rem), (kh_scr, k2 * e_rem),
              (v_scr, v), (et_scr, jnp.exp(tot)))
    for scr, val in staged:
        for q in range(n_pairs):
            scr[q] = val[:, q * PAIR:(q + 1) * PAIR]

    lane = lax.broadcasted_iota(jnp.int32, (1, PAIR), 1)
    first_head = lane < HEAD_DIM
    strict = li > lj
    incl = li >= lj

    def stack(t):
        return jnp.concatenate([jnp.where(first_head, t, 0.0), jnp.where(first_head, 0.0, t)], axis=0)

    def pair_body(q, carry):
        s = state_scr[q]
        for ci in range(n_chunks):
            rows = slice(ci * CHUNK, (ci + 1) * CHUNK)
            am = stack(at_scr[q, rows, :])
            bm = stack(bt_scr[q, rows, :])
            km = stack(kt_scr[q, rows, :])
            rm = stack(rt_scr[q, rows, :])
            bhm = stack(bh_scr[q, rows, :])
            khm = stack(kh_scr[q, rows, :])
            vm = stack(v_scr[q, rows, :])
            n = jnp.where(strict, _dot_nt(am, bm, HI), 0.0)
            a_ak = jnp.where(strict, _dot_nt(am, km, HI), 0.0)
            u = _dot_nt(am, s, HI) + _dot(a_ak, vm, HI)
            u = u + _dot(n, u, HI)
            npow = n
            for _ in range(5):
                npow = _dot(npow, npow, HI)
                u = u + _dot(npow, u, HI)
            a_rb = jnp.where(incl, _dot_nt(rm, bm, HI), 0.0)
            a_rk = jnp.where(incl, _dot_nt(rm, km, HI), 0.0)
            ym = _dot_nt(rm, s, HI) + _dot(a_rb, u, HI) + _dot(a_rk, vm, HI)
            y_scr[q, rows, :] = ym[:CHUNK] + ym[CHUNK:]
            s = (s * et_scr[q, ci * CHUNK:ci * CHUNK + 1, :]
                 + _dot(u.T, bhm, HI) + _dot(vm.T, khm, HI))
        state_scr[q] = s
        return carry

    lax.fori_loop(0, n_pairs, pair_body, 0)

    y = jnp.concatenate([y_scr[q] for q in range(n_pairs)], axis=1)
    mean = head_sum(y) * (1.0 / HEAD_DIM)
    d = y - mean
    var = head_sum(d * d) * (1.0 / HEAD_DIM)
    yn = d * lax.rsqrt(var + GN_EPS) * lnw_ref[...] + lnb_ref[...]
    bonus = head_sum(r * k2 * rk_ref[...]) * v
    o_ref[...] = (yn + bonus) * g * beta_ref[...]


def _rwkv_call(p_r, vecs, wd, wa, wg, batch, seq, width, tb):
    pw = p_r.shape[1]
    n_pairs = width // PAIR
    tiles = seq // tb
    vec_specs = [pl.BlockSpec((1, v.shape[1]), lambda b, t: (0, 0)) for v in vecs]
    mat_specs = [pl.BlockSpec(m.shape, lambda b, t: (0, 0)) for m in (wd, wa, wg)]
    pair_buf = pltpu.VMEM((n_pairs, tb, PAIR), F32)
    return pl.pallas_call(
        functools.partial(_rwkv_kernel, width=width, tb=tb),
        grid=(batch, tiles),
        in_specs=[pl.BlockSpec((tb, pw), lambda b, t: (b * tiles + t, 0))] + vec_specs + mat_specs,
        out_specs=pl.BlockSpec((tb, width), lambda b, t: (b * tiles + t, 0)),
        out_shape=jax.ShapeDtypeStruct((batch * seq, width), F32),
        scratch_shapes=[pltpu.VMEM((1, pw), F32), pltpu.VMEM((n_pairs, PAIR, PAIR), F32)]
        + [pair_buf] * 9,
        compiler_params=pltpu.CompilerParams(
            dimension_semantics=("parallel", "arbitrary"), vmem_limit_bytes=VMEM_LIMIT),
    )(p_r, *vecs, wd, wa, wg)


def _attn_kernel(q_ref, kp_ref, kc_ref, vp_ref, vc_ref, bias_ref, beta_ref, o_ref, kwin, vwin,
                 *, width, qb):
    i = pl.program_id(1)
    n_pairs = width // PAIR
    kwin[0:qb, :] = kp_ref[...]
    kwin[qb:2 * qb, :] = kc_ref[...]
    vwin[0:qb, :] = vp_ref[...]
    vwin[qb:2 * qb, :] = vc_ref[...]
    lane = lax.broadcasted_iota(jnp.int32, (1, PAIR), 1)
    first_head = lane < HEAD_DIM
    col = lax.broadcasted_iota(jnp.int32, (1, BAND), 1)
    scale = HEAD_DIM ** -0.5
    left = LEFT_CHUNKS * CHUNK

    for q in range(n_pairs):
        lanes = slice(q * PAIR, (q + 1) * PAIR)

        def chunk_body(ci, carry, lanes=lanes, q=q):
            r0 = pl.multiple_of(ci * CHUNK, CHUNK)
            qc = q_ref[pl.ds(r0, CHUNK), lanes]
            zero = jnp.zeros_like(qc)
            qm = jnp.concatenate([jnp.where(first_head, qc, zero), jnp.where(first_head, zero, qc)],
                                 axis=0)
            s = _dot_nt(qm, kwin[pl.ds(r0, BAND), lanes]) * scale + bias_ref[q]
            valid = jnp.logical_or(i > 0, col + ci * CHUNK >= left)
            s = jnp.where(valid, s, NEG_INF)
            e = jnp.exp(s - jnp.max(s, axis=-1, keepdims=True))
            denom = jnp.sum(e, axis=-1, keepdims=True)
            o = _dot(e.astype(BF16), vwin[pl.ds(r0, BAND), lanes]) / denom
            oc = jnp.where(first_head, o[:CHUNK], o[CHUNK:])
            o_ref[pl.ds(r0, CHUNK), lanes] = oc * beta_ref[:, lanes]
            return carry

        lax.fori_loop(0, qb // CHUNK, chunk_body, 0)


def _attn_call(p_a, bias, beta, batch, seq, width, qb):
    tiles = seq // qb
    n_pairs = width // PAIR

    def cur(col):
        return pl.BlockSpec((qb, width), lambda b, t: (b * tiles + t, col))

    def prev(col):
        return pl.BlockSpec((qb, width), lambda b, t: (b * tiles + jnp.maximum(t - 1, 0), col))

    return pl.pallas_call(
        functools.partial(_attn_kernel, width=width, qb=qb),
        grid=(batch, tiles),
        in_specs=[cur(0), prev(1), cur(1), prev(2), cur(2),
                  pl.BlockSpec((n_pairs, PAIR, BAND), lambda b, t: (0, 0, 0)),
                  pl.BlockSpec((1, width), lambda b, t: (0, 0))],
        out_specs=pl.BlockSpec((qb, width), lambda b, t: (b * tiles + t, 0)),
        out_shape=jax.ShapeDtypeStruct((batch * seq, width), F32),
        scratch_shapes=[pltpu.VMEM((2 * qb, width), BF16), pltpu.VMEM((2 * qb, width), BF16)],
        compiler_params=pltpu.CompilerParams(
            dimension_semantics=("parallel", "arbitrary"), vmem_limit_bytes=VMEM_LIMIT),
    )(p_a, p_a, p_a, p_a, p_a, bias, beta)


def _out_router_kernel(yr_ref, ya_ref, wo_ref, x_ref, mod_ref, g2_ref, wr_ref, br_ref,
                       x1_ref, h2_ref, comb_ref, *, r_width):
    y = (_dot(yr_ref[...].astype(BF16), wo_ref[0:r_width, :])
         + _dot(ya_ref[...].astype(BF16), wo_ref[r_width:, :]))
    m = mod_ref[0]
    x1 = x_ref[...] + m[GATE1:GATE1 + 1] * y
    x1_ref[...] = x1
    h2 = _rms(x1) * g2_ref[...] * (1.0 + m[SCALE2:SCALE2 + 1]) + m[SHIFT2:SHIFT2 + 1]
    h2_ref[...] = h2.astype(BF16)

    logits = _dot(h2, wr_ref[...], HI) + br_ref[...]
    li = lax.broadcasted_iota(jnp.int32, logits.shape, 1).astype(F32)
    none = jnp.float32(-jnp.inf)
    far = jnp.float32(LANES)
    is_group = li < N_GROUPS
    gl = jnp.where(is_group, logits, none)
    gmax = jnp.max(gl, axis=-1, keepdims=True)
    gidx = jnp.min(jnp.where(gl == gmax, li, far), axis=-1, keepdims=True)
    g_w = 1.0 / jnp.sum(jnp.where(is_group, jnp.exp(logits - gmax), 0.0), axis=-1, keepdims=True)
    lo = N_GROUPS + EXPERTS_PER_GROUP * gidx
    el = jnp.where(jnp.logical_and(li >= lo, li < lo + EXPERTS_PER_GROUP), logits, none)
    m1 = jnp.max(el, axis=-1, keepdims=True)
    i1 = jnp.min(jnp.where(el == m1, li, far), axis=-1, keepdims=True)
    el2 = jnp.where(li == i1, none, el)
    m2 = jnp.max(el2, axis=-1, keepdims=True)
    i2 = jnp.min(jnp.where(el2 == m2, li, far), axis=-1, keepdims=True)
    e2 = jnp.exp(m2 - m1)
    w1 = g_w / (1.0 + e2)
    w2 = g_w * e2 / (1.0 + e2)
    comb_ref[...] = jnp.where(li == i1, w1, 0.0) + jnp.where(li == i2, w2, 0.0)


def _out_router_call(y_r, y_a, wo, x2d, mod, g2, wr, br, seq, tm):
    rows, d = x2d.shape
    r_width = y_r.shape[1]
    a_width = y_a.shape[1]
    tiles_per_seq = seq // tm
    return pl.pallas_call(
        functools.partial(_out_router_kernel, r_width=r_width),
        grid=(rows // tm,),
        in_specs=[pl.BlockSpec((tm, r_width), lambda i: (i, 0)),
                  pl.BlockSpec((tm, a_width), lambda i: (i, 0)),
                  pl.BlockSpec(wo.shape, lambda i: (0, 0)),
                  pl.BlockSpec((tm, d), lambda i: (i, 0)),
                  pl.BlockSpec((1, 6, d), lambda i: (i // tiles_per_seq, 0, 0)),
                  pl.BlockSpec((1, d), lambda i: (0, 0)),
                  pl.BlockSpec((d, LANES), lambda i: (0, 0)),
                  pl.BlockSpec((1, LANES), lambda i: (0, 0))],
        out_specs=[pl.BlockSpec((tm, d), lambda i: (i, 0)),
                   pl.BlockSpec((tm, d), lambda i: (i, 0)),
                   pl.BlockSpec((tm, LANES), lambda i: (i, 0))],
        out_shape=[jax.ShapeDtypeStruct((rows, d), F32),
                   jax.ShapeDtypeStruct((rows, d), BF16),
                   jax.ShapeDtypeStruct((rows, LANES), F32)],
        compiler_params=pltpu.CompilerParams(
            dimension_semantics=("parallel",), vmem_limit_bytes=VMEM_LIMIT),
    )(y_r, y_a, wo, x2d, mod, g2, wr, br)


def _moe_kernel(t_ref, comb_ref, wg_ref, wu_ref, wd_ref, x1_ref, mod_ref, gf_ref, o_ref, acc_scr):
    e = pl.program_id(1)

    @pl.when(e == 0)
    def _():
        acc_scr[...] = jnp.zeros_like(acc_scr)

    t = t_ref[...]
    gate = _dot(t, wg_ref[0].astype(BF16))
    up = _dot(t, wu_ref[0].astype(BF16))
    lane = lax.broadcasted_iota(jnp.int32, (1, LANES), 1)
    cw = jnp.sum(jnp.where(lane == e + N_GROUPS, comb_ref[...], 0.0), axis=-1, keepdims=True)
    h = gate * _sigmoid(gate) * up * cw
    acc_scr[...] += _dot(h.astype(BF16), wd_ref[0].astype(BF16))

    @pl.when(e == pl.num_programs(1) - 1)
    def _():
        x2 = x1_ref[...] + mod_ref[0][GATE2:GATE2 + 1] * acc_scr[...]
        o_ref[...] = _rms(x2) * gf_ref[...]


def _moe_call(h2, comb, wg, wu, wd, x1, mod, gf, seq, tm):
    rows, d = x1.shape
    n_e, _, f = wg.shape
    tiles_per_seq = seq // tm
    return pl.pallas_call(
        _moe_kernel,
        grid=(rows // tm, n_e),
        in_specs=[pl.BlockSpec((tm, d), lambda i, e: (i, 0)),
                  pl.BlockSpec((tm, LANES), lambda i, e: (i, 0)),
                  pl.BlockSpec((1, d, f), lambda i, e: (e, 0, 0)),
                  pl.BlockSpec((1, d, f), lambda i, e: (e, 0, 0)),
                  pl.BlockSpec((1, f, d), lambda i, e: (e, 0, 0)),
                  pl.BlockSpec((tm, d), lambda i, e: (i, 0)),
                  pl.BlockSpec((1, 6, d), lambda i, e: (i // tiles_per_seq, 0, 0)),
                  pl.BlockSpec((1, d), lambda i, e: (0, 0))],
        out_specs=pl.BlockSpec((tm, d), lambda i, e: (i, 0)),
        out_shape=jax.ShapeDtypeStruct((rows, d), F32),
        scratch_shapes=[pltpu.VMEM((tm, d), F32)],
        compiler_params=pltpu.CompilerParams(
            dimension_semantics=("parallel", "arbitrary"), vmem_limit_bytes=VMEM_LIMIT),
    )(h2, comb, wg, wu, wd, x1, mod, gf)


def _round_up(n, m):
    return (n + m - 1) // m * m


def kernel(x, c, w_ada, b_ada, norm1_g, w_in, mu_shift, w0, w_decay_up, a0, w_aaa_up, w_gate_up,
           k_k, k_a, r_k, ln_x_w, ln_x_b, rel_bias, beta_rwkv, beta_attn, w_out, norm2_g, w_group,
           b_group, w_expert, b_expert, w_gate, w_up, w_down, norm_f_g):
    assert w_ada.shape[0] == 1, "single trunk layer"
    batch, seq, d = x.shape
    r_width = w0.shape[1]
    a_width = beta_attn.shape[1]
    shift_width = mu_shift.shape[1]
    assert shift_width == 3 * r_width + DECAY_LORA + AAA_LORA + GATE_LORA
    x2d = x.reshape(batch * seq, d)
    row = lambda t: t.reshape(1, -1)

    c8 = jnp.pad(c, ((0, 8 - batch), (0, 0)))
    mod = _ada_call(c8, w_ada[0], row(b_ada[0]), tn=1536)[:batch].reshape(batch, 6, d)

    tm = min(512, seq)
    pw = _round_up(shift_width, 3 * LANES)
    p_r = _proj_call(x2d, row(norm1_g[0]), mod, w_in[0], pw, pw // 3, tm, seq, F32)
    w_attn = w_in[0][:, shift_width:].astype(BF16)
    p_a = _proj_call(x2d, row(norm1_g[0]), mod, w_attn, 3 * a_width, a_width, tm, seq, BF16)

    mu = jnp.pad(mu_shift[0], (0, pw - shift_width))
    vecs = [row(mu), row(w0[0]), row(a0[0]), row(k_k[0]), row(k_a[0]), row(r_k[0]), row(ln_x_w[0]),
            row(ln_x_b[0]), row(beta_rwkv[0])]
    y_r = _rwkv_call(p_r, vecs, w_decay_up[0], w_aaa_up[0], w_gate_up[0], batch, seq, r_width,
                     tb=min(256, seq))

    rel = jnp.arange(CHUNK)[:, None] - jnp.arange(BAND)[None, :] + LEFT_CHUNKS * CHUNK
    bias = rel_bias[0][:, jnp.clip(rel, -REL_CLIP, REL_CLIP) + REL_CLIP].astype(F32)
    bias = bias.reshape(a_width // PAIR, PAIR, BAND)
    y_a = _attn_call(p_a, bias, row(beta_attn[0]), batch, seq, a_width, qb=min(512, seq))

    n_route = N_GROUPS + N_EXPERTS
    wr = jnp.concatenate([w_group[0], w_expert[0].transpose(1, 0, 2).reshape(d, N_EXPERTS)], axis=1)
    wr = jnp.pad(wr, ((0, 0), (0, LANES - n_route)))
    br = jnp.pad(jnp.concatenate([b_group[0], b_expert[0].reshape(-1)]), (0, LANES - n_route))
    x1, h2, comb = _out_router_call(y_r, y_a, w_out[0].astype(BF16), x2d, mod, row(norm2_g[0]),
                                    wr, row(br), seq, tm)

    f = w_gate.shape[-1]
    out = _moe_call(h2, comb, w_gate[0].reshape(N_EXPERTS, d, f), w_up[0].reshape(N_EXPERTS, d, f),
                    w_down[0].reshape(N_EXPERTS, f, d), x1, mod, row(norm_f_g), seq, tm)
    return out.reshape(batch, seq, d)
```

```python
import functools

import jax
import jax.numpy as jnp
from jax import lax
from jax.experimental import pallas as pl
from jax.experimental.pallas import tpu as pltpu

F32 = jnp.float32
BF16 = jnp.bfloat16
HI = lax.Precision.HIGHEST

LANES = 128
HEAD_DIM = 64
PAIR = 2 * HEAD_DIM
CHUNK = 64
LEFT_CHUNKS = 8
BAND = (LEFT_CHUNKS + 1) * CHUNK
REL_CLIP = 128
DECAY_LORA = 64
AAA_LORA = 64
GATE_LORA = 160
N_GROUPS = 4
EXPERTS_PER_GROUP = 8
N_EXPERTS = N_GROUPS * EXPERTS_PER_GROUP
RMS_EPS = 1e-6
GN_EPS = 64e-5
L2_EPS = 1e-12
NEG_INF = -1e30
VMEM_LIMIT = 56 * 1024 * 1024

SHIFT1, SCALE1, GATE1, SHIFT2, SCALE2, GATE2 = range(6)


def _dot(a, b, precision=None):
    return jnp.dot(a, b, precision=precision, preferred_element_type=F32)


def _dot_nt(a, b, precision=None):
    return lax.dot_general(a, b, (((1,), (1,)), ((), ())), precision=precision,
                           preferred_element_type=F32)


def _sigmoid(x):
    return 1.0 / (1.0 + jnp.exp(-x))


def _rms(x):
    return x * lax.rsqrt(jnp.mean(x * x, axis=-1, keepdims=True) + RMS_EPS)


def _ada_kernel(c_ref, w_ref, b_ref, o_ref):
    c = c_ref[...]
    o_ref[...] = _dot(c * _sigmoid(c), w_ref[...], HI) + b_ref[...]


def _ada_call(c8, w, b, tn):
    rows, d = c8.shape
    n = w.shape[1]
    return pl.pallas_call(
        _ada_kernel,
        grid=(n // tn,),
        in_specs=[pl.BlockSpec((rows, d), lambda j: (0, 0)),
                  pl.BlockSpec((d, tn), lambda j: (0, j)),
                  pl.BlockSpec((1, tn), lambda j: (0, j))],
        out_specs=pl.BlockSpec((rows, tn), lambda j: (0, j)),
        out_shape=jax.ShapeDtypeStruct((rows, n), F32),
        compiler_params=pltpu.CompilerParams(vmem_limit_bytes=VMEM_LIMIT),
    )(c8, w, b)


def _proj_kernel(x_ref, g_ref, mod_ref, w_ref, o_ref, h_scr):
    @pl.when(pl.program_id(1) == 0)
    def _():
        m = mod_ref[0]
        h = _rms(x_ref[...]) * g_ref[...] * (1.0 + m[SCALE1:SCALE1 + 1]) + m[SHIFT1:SHIFT1 + 1]
        h_scr[...] = h.astype(BF16)

    o_ref[...] = _dot(h_scr[...], w_ref[...].astype(BF16)).astype(o_ref.dtype)


def _proj_call(x2d, g, mod, w, n_out, tn, tm, seq, out_dtype):
    rows, d = x2d.shape
    tiles_per_seq = seq // tm
    return pl.pallas_call(
        _proj_kernel,
        grid=(rows // tm, n_out // tn),
        in_specs=[pl.BlockSpec((tm, d), lambda i, j: (i, 0)),
                  pl.BlockSpec((1, d), lambda i, j: (0, 0)),
                  pl.BlockSpec((1, 6, d), lambda i, j: (i // tiles_per_seq, 0, 0)),
                  pl.BlockSpec((d, tn), lambda i, j: (0, j))],
        out_specs=pl.BlockSpec((tm, tn), lambda i, j: (i, j)),
        out_shape=jax.ShapeDtypeStruct((rows, n_out), out_dtype),
        scratch_shapes=[pltpu.VMEM((tm, d), BF16)],
        compiler_params=pltpu.CompilerParams(
            dimension_semantics=("parallel", "arbitrary"), vmem_limit_bytes=VMEM_LIMIT),
    )(x2d, g, mod, w)


def _split3(t):
    hi = t.astype(BF16)
    rest = t - hi.astype(F32)
    mid = rest.astype(BF16)
    lo = (rest - mid.astype(F32)).astype(BF16)
    return hi, mid, lo


def _dot_f32_lhs(t, m):
    hi, mid, lo = _split3(t)
    return _dot(lo, m) + _dot(mid, m) + _dot(hi, m)


def _dot_f32_rhs(m, t):
    hi, mid, lo = _split3(t)
    return _dot(m, lo) + _dot(m, mid) + _dot(m, hi)


def _dot_x3(a, b):
    ah = a.astype(BF16)
    al = (a - ah.astype(F32)).astype(BF16)
    bh = b.astype(BF16)
    bl = (b - bh.astype(F32)).astype(BF16)
    return _dot(al, bh) + _dot(ah, bl) + _dot(ah, bh)


def _rwkv_kernel(p_ref, mu_ref, w0_ref, a0_ref, kk_ref, ka_ref, rk_ref, lnw_ref, lnb_ref, beta_ref,
                 wd_ref, wa_ref, wg_ref, o_ref,
                 carry_scr, state_scr, ar_scr, bk_scr, bkh_scr, v_scr, et_scr, y_scr, *, width, tb):
    n_pairs = width // PAIR
    n_chunks = tb // CHUNK
    stacked = 2 * CHUNK

    @pl.when(pl.program_id(1) == 0)
    def _():
        carry_scr[...] = jnp.zeros_like(carry_scr)
        state_scr[...] = jnp.zeros_like(state_scr)

    p = p_ref[...]
    row = lax.broadcasted_iota(jnp.int32, (tb, 1), 0)
    prev = jnp.where(row == 0, carry_scr[...], pltpu.roll(p, 1, axis=0))
    carry_scr[...] = p[tb - 1:tb, :]
    ps = p + (prev - p) * mu_ref[...]

    c = width
    r = ps[:, :c]
    k = ps[:, c:2 * c]
    v = ps[:, 2 * c:3 * c]
    o = 3 * c
    xw = ps[:, o:o + DECAY_LORA]
    xa = ps[:, o + DECAY_LORA:o + DECAY_LORA + AAA_LORA]
    xg = ps[:, o + DECAY_LORA + AAA_LORA:o + DECAY_LORA + AAA_LORA + GATE_LORA]

    z = w0_ref[...] + _dot_x3(jnp.tanh(xw), wd_ref[...])
    softplus_neg_z = jnp.maximum(-z, 0.0) + jnp.log(1.0 + jnp.exp(-jnp.abs(z)))
    lw = -jnp.exp(-softplus_neg_z - 0.5)
    a = _sigmoid(a0_ref[...] + _dot(xa.astype(BF16), wa_ref[...].astype(BF16)))
    g = _dot(_sigmoid(xg).astype(BF16), wg_ref[...].astype(BF16))

    li = lax.broadcasted_iota(jnp.int32, (PAIR, PAIR), 0)
    lj = lax.broadcasted_iota(jnp.int32, (PAIR, PAIR), 1)
    head_ones = ((li // HEAD_DIM) == (lj // HEAD_DIM)).astype(BF16)

    def head_sum(t):
        return jnp.concatenate(
            [_dot_f32_lhs(t[:, q * PAIR:(q + 1) * PAIR], head_ones) for q in range(n_pairs)], axis=1)

    kk = k * kk_ref[...]
    kk = kk / jnp.maximum(jnp.sqrt(head_sum(kk * kk)), L2_EPS)
    k2 = k * (1.0 + (a - 1.0) * ka_ref[...])

    ti = lax.broadcasted_iota(jnp.int32, (tb, tb), 0)
    tj = lax.broadcasted_iota(jnp.int32, (tb, tb), 1)
    tri = jnp.logical_and((ti // CHUNK) == (tj // CHUNK), tj <= ti).astype(BF16)
    cum = _dot_f32_rhs(tri, lw)
    tot = jnp.concatenate(
        [jnp.broadcast_to(cum[(ci + 1) * CHUNK - 1:(ci + 1) * CHUNK, :], (CHUNK, c))
         for ci in range(n_chunks)], axis=0)

    e_neg = jnp.exp(-cum)
    e_rem = jnp.exp(tot - cum)
    kka = kk * a
    lane = lax.broadcasted_iota(jnp.int32, (1, PAIR), 1)
    first_head = lane < HEAD_DIM

    def put(scr, base, val):
        for q in range(n_pairs):
            vq = val[:, q * PAIR:(q + 1) * PAIR]
            h0 = jnp.where(first_head, vq, 0.0).astype(scr.dtype)
            h1 = jnp.where(first_head, 0.0, vq).astype(scr.dtype)
            for ci in range(n_chunks):
                rows = slice(ci * CHUNK, (ci + 1) * CHUNK)
                scr[q, ci, base:base + CHUNK, :] = h0[rows]
                scr[q, ci, base + CHUNK:base + stacked, :] = h1[rows]

    put(ar_scr, 0, -kk * jnp.exp(cum - lw))
    put(ar_scr, stacked, r * jnp.exp(cum))
    put(bk_scr, 0, kka * e_neg)
    put(bk_scr, stacked, k2 * e_neg)
    put(bkh_scr, 0, kka * e_rem)
    put(bkh_scr, stacked, k2 * e_rem)
    put(v_scr, 0, v)
    e_tot = jnp.exp(tot)
    for q in range(n_pairs):
        et_scr[q] = e_tot[:, q * PAIR:(q + 1) * PAIR]

    si = lax.broadcasted_iota(jnp.int32, (2 * stacked, 2 * stacked), 0)
    sj = lax.broadcasted_iota(jnp.int32, (2 * stacked, 2 * stacked), 1)
    keep = jnp.where(si < stacked, si, si - stacked + 1) > (sj % stacked)
    eye = (li == lj).astype(F32)

    def pair_body(q, carry):
        s = state_scr[q]
        for ci in range(n_chunks):
            ar = ar_scr[q, ci]
            bk = bk_scr[q, ci]
            bkh = bkh_scr[q, ci]
            vm = v_scr[q, ci]
            sc = jnp.where(keep, _dot_nt(ar, bk), 0.0)
            scb = sc.astype(BF16)
            npow = sc[:stacked, :stacked]
            pm = eye + npow
            for _ in range(5):
                nb = npow.astype(BF16)
                npow = _dot(nb, nb)
                pm = pm + _dot(npow.astype(BF16), pm.astype(BF16))
            av = _dot(scb[:, stacked:], vm.astype(BF16))
            sr = _dot_nt(ar, s.astype(BF16))
            u = _dot(pm.astype(BF16), (sr[:stacked] + av[:stacked]).astype(BF16))
            ym = sr[stacked:] + av[stacked:] + _dot(scb[stacked:, :stacked], u.astype(BF16))
            y_scr[q, ci * CHUNK:(ci + 1) * CHUNK, :] = ym[:CHUNK] + ym[CHUNK:]
            s = (s * et_scr[q, ci * CHUNK:ci * CHUNK + 1, :]
                 + _dot(u.T.astype(BF16), bkh[:stacked]) + _dot(vm.T.astype(BF16), bkh[stacked:]))
        state_scr[q] = s
        return carry

    lax.fori_loop(0, n_pairs, pair_body, 0)

    y = jnp.concatenate([y_scr[q] for q in range(n_pairs)], axis=1)
    mean = head_sum(y) * (1.0 / HEAD_DIM)
    d = y - mean
    var = head_sum(d * d) * (1.0 / HEAD_DIM)
    yn = d * lax.rsqrt(var + GN_EPS) * lnw_ref[...] + lnb_ref[...]
    bonus = head_sum(r * k2 * rk_ref[...]) * v
    o_ref[...] = (yn + bonus) * g * beta_ref[...]


def _rwkv_call(p_r, vecs, wd, wa, wg, batch, seq, width, tb):
    pw = p_r.shape[1]
    n_pairs = width // PAIR
    n_chunks = tb // CHUNK
    tiles = seq // tb
    vec_specs = [pl.BlockSpec((1, v.shape[1]), lambda b, t: (0, 0)) for v in vecs]
    mat_specs = [pl.BlockSpec(m.shape, lambda b, t: (0, 0)) for m in (wd, wa, wg)]
    stacked2 = pltpu.VMEM((n_pairs, n_chunks, 4 * CHUNK, PAIR), BF16)
    return pl.pallas_call(
        functools.partial(_rwkv_kernel, width=width, tb=tb),
        grid=(batch, tiles),
        in_specs=[pl.BlockSpec((tb, pw), lambda b, t: (b * tiles + t, 0))] + vec_specs + mat_specs,
        out_specs=pl.BlockSpec((tb, width), lambda b, t: (b * tiles + t, 0)),
        out_shape=jax.ShapeDtypeStruct((batch * seq, width), F32),
        scratch_shapes=[pltpu.VMEM((1, pw), F32),
                        pltpu.VMEM((n_pairs, PAIR, PAIR), F32),
                        stacked2, stacked2, stacked2,
                        pltpu.VMEM((n_pairs, n_chunks, 2 * CHUNK, PAIR), F32),
                        pltpu.VMEM((n_pairs, tb, PAIR), F32),
                        pltpu.VMEM((n_pairs, tb, PAIR), F32)],
        compiler_params=pltpu.CompilerParams(
            dimension_semantics=("parallel", "arbitrary"), vmem_limit_bytes=VMEM_LIMIT),
    )(p_r, *vecs, wd, wa, wg)


def _attn_kernel(q_ref, kp_ref, kc_ref, vp_ref, vc_ref, bias_ref, beta_ref, o_ref, kwin, vwin,
                 *, width, qb):
    i = pl.program_id(1)
    n_pairs = width // PAIR
    kwin[0:qb, :] = kp_ref[...]
    kwin[qb:2 * qb, :] = kc_ref[...]
    vwin[0:qb, :] = vp_ref[...]
    vwin[qb:2 * qb, :] = vc_ref[...]
    lane = lax.broadcasted_iota(jnp.int32, (1, PAIR), 1)
    first_head = lane < HEAD_DIM
    col = lax.broadcasted_iota(jnp.int32, (1, BAND), 1)
    scale = HEAD_DIM ** -0.5
    left = LEFT_CHUNKS * CHUNK

    for q in range(n_pairs):
        lanes = slice(q * PAIR, (q + 1) * PAIR)

        def chunk_body(ci, carry, lanes=lanes, q=q):
            r0 = pl.multiple_of(ci * CHUNK, CHUNK)
            qc = q_ref[pl.ds(r0, CHUNK), lanes]
            zero = jnp.zeros_like(qc)
            qm = jnp.concatenate([jnp.where(first_head, qc, zero), jnp.where(first_head, zero, qc)],
                                 axis=0)
            s = _dot_nt(qm, kwin[pl.ds(r0, BAND), lanes]) * scale + bias_ref[q]
            valid = jnp.logical_or(i > 0, col + ci * CHUNK >= left)
            s = jnp.where(valid, s, NEG_INF)
            e = jnp.exp(s - jnp.max(s, axis=-1, keepdims=True))
            denom = jnp.sum(e, axis=-1, keepdims=True)
            o = _dot(e.astype(BF16), vwin[pl.ds(r0, BAND), lanes]) / denom
            oc = jnp.where(first_head, o[:CHUNK], o[CHUNK:])
            o_ref[pl.ds(r0, CHUNK), lanes] = oc * beta_ref[:, lanes]
            return carry

        lax.fori_loop(0, qb // CHUNK, chunk_body, 0)


def _attn_call(p_a, bias, beta, batch, seq, width, qb):
    assert qb == LEFT_CHUNKS * CHUNK, "key window = previous block + current block"
    tiles = seq // qb
    n_pairs = width // PAIR

    def cur(col):
        return pl.BlockSpec((qb, width), lambda b, t: (b * tiles + t, col))

    def prev(col):
        return pl.BlockSpec((qb, width), lambda b, t: (b * tiles + jnp.maximum(t - 1, 0), col))

    return pl.pallas_call(
        functools.partial(_attn_kernel, width=width, qb=qb),
        grid=(batch, tiles),
        in_specs=[cur(0), prev(1), cur(1), prev(2), cur(2),
                  pl.BlockSpec((n_pairs, PAIR, BAND), lambda b, t: (0, 0, 0)),
                  pl.BlockSpec((1, width), lambda b, t: (0, 0))],
        out_specs=pl.BlockSpec((qb, width), lambda b, t: (b * tiles + t, 0)),
        out_shape=jax.ShapeDtypeStruct((batch * seq, width), F32),
        scratch_shapes=[pltpu.VMEM((2 * qb, width), BF16), pltpu.VMEM((2 * qb, width), BF16)],
        compiler_params=pltpu.CompilerParams(
            dimension_semantics=("parallel", "arbitrary"), vmem_limit_bytes=VMEM_LIMIT),
    )(p_a, p_a, p_a, p_a, p_a, bias, beta)


def _out_router_kernel(yr_ref, ya_ref, wo_ref, x_ref, mod_ref, g2_ref, wr_ref, br_ref,
                       x1_ref, h2_ref, route_ref, *, r_width):
    y = (_dot(yr_ref[...].astype(BF16), wo_ref[0:r_width, :])
         + _dot(ya_ref[...].astype(BF16), wo_ref[r_width:, :]))
    m = mod_ref[0]
    x1 = x_ref[...] + m[GATE1:GATE1 + 1] * y
    x1_ref[...] = x1
    h2 = _rms(x1) * g2_ref[...] * (1.0 + m[SCALE2:SCALE2 + 1]) + m[SHIFT2:SHIFT2 + 1]
    h2_ref[...] = h2

    logits = _dot(h2, wr_ref[...], HI) + br_ref[...]
    li = lax.broadcasted_iota(jnp.int32, logits.shape, 1).astype(F32)
    none = jnp.float32(-jnp.inf)
    far = jnp.float32(LANES)
    is_group = li < N_GROUPS
    gl = jnp.where(is_group, logits, none)
    gmax = jnp.max(gl, axis=-1, keepdims=True)
    gidx = jnp.min(jnp.where(gl == gmax, li, far), axis=-1, keepdims=True)
    g_w = 1.0 / jnp.sum(jnp.where(is_group, jnp.exp(logits - gmax), 0.0), axis=-1, keepdims=True)
    lo = N_GROUPS + EXPERTS_PER_GROUP * gidx
    el = jnp.where(jnp.logical_and(li >= lo, li < lo + EXPERTS_PER_GROUP), logits, none)
    m1 = jnp.max(el, axis=-1, keepdims=True)
    i1 = jnp.min(jnp.where(el == m1, li, far), axis=-1, keepdims=True)
    el2 = jnp.where(li == i1, none, el)
    m2 = jnp.max(el2, axis=-1, keepdims=True)
    i2 = jnp.min(jnp.where(el2 == m2, li, far), axis=-1, keepdims=True)
    e2 = jnp.exp(m2 - m1)
    w1 = g_w / (1.0 + e2)
    w2 = g_w * e2 / (1.0 + e2)
    route = jnp.where(li == 0.0, i1 - N_GROUPS, 0.0) + jnp.where(li == 1.0, i2 - N_GROUPS, 0.0)
    route_ref[...] = route + jnp.where(li == 2.0, w1, 0.0) + jnp.where(li == 3.0, w2, 0.0)


def _out_router_call(y_r, y_a, wo, x2d, mod, g2, wr, br, seq, tm):
    rows, d = x2d.shape
    r_width = y_r.shape[1]
    a_width = y_a.shape[1]
    tiles_per_seq = seq // tm
    return pl.pallas_call(
        functools.partial(_out_router_kernel, r_width=r_width),
        grid=(rows // tm,),
        in_specs=[pl.BlockSpec((tm, r_width), lambda i: (i, 0)),
                  pl.BlockSpec((tm, a_width), lambda i: (i, 0)),
                  pl.BlockSpec(wo.shape, lambda i: (0, 0)),
                  pl.BlockSpec((tm, d), lambda i: (i, 0)),
                  pl.BlockSpec((1, 6, d), lambda i: (i // tiles_per_seq, 0, 0)),
                  pl.BlockSpec((1, d), lambda i: (0, 0)),
                  pl.BlockSpec((d, LANES), lambda i: (0, 0)),
                  pl.BlockSpec((1, LANES), lambda i: (0, 0))],
        out_specs=[pl.BlockSpec((tm, d), lambda i: (i, 0)),
                   pl.BlockSpec((tm, d), lambda i: (i, 0)),
                   pl.BlockSpec((tm, LANES), lambda i: (i, 0))],
        out_shape=[jax.ShapeDtypeStruct((rows, d), F32),
                   jax.ShapeDtypeStruct((rows, d), F32),
                   jax.ShapeDtypeStruct((rows, LANES), F32)],
        compiler_params=pltpu.CompilerParams(
            dimension_semantics=("parallel",), vmem_limit_bytes=VMEM_LIMIT),
    )(y_r, y_a, wo, x2d, mod, g2, wr, br)


def _moe_kernel(texp_ref, first_ref, meta_ref, src_ref,
                h2_hbm, w_ref, wg_ref, wu_ref, wd_ref, o_ref,
                xbuf, sem, wg_bf, wu_bf, wd_bf, *, tm):
    t = pl.program_id(0)
    n_used = meta_ref[0]
    slot = t % 2

    def gather_rows(tile, buf_slot):
        base = tile * tm

        def body(i, carry):
            tok = src_ref[base + i]
            pltpu.make_async_copy(h2_hbm.at[pl.ds(tok, 1)], xbuf.at[buf_slot, pl.ds(i, 1)],
                                  sem.at[buf_slot]).start()
            return carry

        lax.fori_loop(0, tm, body, 0, unroll=8)

    @pl.when(t == 0)
    def _():
        gather_rows(0, 0)

    @pl.when(t + 1 < n_used)
    def _():
        gather_rows(t + 1, 1 - slot)

    @pl.when(t < n_used)
    def _():
        pltpu.make_async_copy(h2_hbm.at[pl.ds(0, tm)], xbuf.at[slot], sem.at[slot]).wait()

        @pl.when(first_ref[t] == 1)
        def _():
            wg_bf[...] = wg_ref[0].astype(BF16)
            wu_bf[...] = wu_ref[0].astype(BF16)
            wd_bf[...] = wd_ref[0].astype(BF16)

        x = xbuf[slot].astype(BF16)
        gate = _dot(x, wg_bf[...])
        up = _dot(x, wu_bf[...])
        h = gate * _sigmoid(gate) * up * w_ref[...]
        o_ref[...] = _dot(h.astype(BF16), wd_bf[...])

    @pl.when(t >= n_used)
    def _():
        o_ref[...] = jnp.zeros_like(o_ref)


def _moe_call(texp, first, meta, src, h2, w_sorted, wg, wu, wd, tm):
    d = h2.shape[1]
    n_e, _, f = wg.shape
    n_tiles = texp.shape[0]

    def row_tile(t, texp, first, meta, src):
        return (jnp.minimum(t, meta[0] - 1), 0)

    def expert(t, texp, first, meta, src):
        return (texp[t], 0, 0)

    grid_spec = pltpu.PrefetchScalarGridSpec(
        num_scalar_prefetch=4,
        grid=(n_tiles,),
        in_specs=[pl.BlockSpec(memory_space=pl.ANY),
                  pl.BlockSpec((tm, 1), row_tile),
                  pl.BlockSpec((1, d, f), expert),
                  pl.BlockSpec((1, d, f), expert),
                  pl.BlockSpec((1, f, d), expert)],
        out_specs=pl.BlockSpec((tm, d), lambda t, texp, first, meta, src: (t, 0)),
        scratch_shapes=[pltpu.VMEM((2, tm, d), F32), pltpu.SemaphoreType.DMA((2,)),
                        pltpu.VMEM((d, f), BF16), pltpu.VMEM((d, f), BF16), pltpu.VMEM((f, d), BF16)])
    return pl.pallas_call(
        functools.partial(_moe_kernel, tm=tm),
        grid_spec=grid_spec,
        out_shape=jax.ShapeDtypeStruct((n_tiles * tm, d), F32),
        compiler_params=pltpu.CompilerParams(
            dimension_semantics=("arbitrary",), vmem_limit_bytes=VMEM_LIMIT),
    )(texp, first, meta, src, h2, w_sorted, wg, wu, wd)


def _final_kernel(pos_ref, ys_hbm, x1_ref, mod_ref, gf_ref, o_ref, ybuf, sem, *, tm):
    i = pl.program_id(0)
    n = pl.num_programs(0)
    slot = i % 2

    def gather_rows(tile, buf_slot):
        base = tile * tm * 2

        def body(r, carry):
            for s in range(2):
                pltpu.make_async_copy(ys_hbm.at[pl.ds(pos_ref[base + 2 * r + s], 1)],
                                      ybuf.at[buf_slot, s, pl.ds(r, 1)], sem.at[buf_slot]).start()
            return carry

        lax.fori_loop(0, tm, body, 0, unroll=4)

    @pl.when(i == 0)
    def _():
        gather_rows(0, 0)

    @pl.when(i + 1 < n)
    def _():
        gather_rows(i + 1, 1 - slot)

    for s in range(2):
        pltpu.make_async_copy(ys_hbm.at[pl.ds(0, tm)], ybuf.at[slot, s], sem.at[slot]).wait()
    f = ybuf[slot, 0] + ybuf[slot, 1]
    x2 = x1_ref[...] + mod_ref[0][GATE2:GATE2 + 1] * f
    o_ref[...] = _rms(x2) * gf_ref[...]


def _final_call(pos, ys, x1, mod, gf, seq, tm):
    rows, d = x1.shape
    tiles_per_seq = seq // tm
    grid_spec = pltpu.PrefetchScalarGridSpec(
        num_scalar_prefetch=1,
        grid=(rows // tm,),
        in_specs=[pl.BlockSpec(memory_space=pl.ANY),
                  pl.BlockSpec((tm, d), lambda i, pos: (i, 0)),
                  pl.BlockSpec((1, 6, d), lambda i, pos: (i // tiles_per_seq, 0, 0)),
                  pl.BlockSpec((1, d), lambda i, pos: (0, 0))],
        out_specs=pl.BlockSpec((tm, d), lambda i, pos: (i, 0)),
        scratch_shapes=[pltpu.VMEM((2, 2, tm, d), F32), pltpu.SemaphoreType.DMA((2,))])
    return pl.pallas_call(
        functools.partial(_final_kernel, tm=tm),
        grid_spec=grid_spec,
        out_shape=jax.ShapeDtypeStruct((rows, d), F32),
        compiler_params=pltpu.CompilerParams(
            dimension_semantics=("arbitrary",), vmem_limit_bytes=VMEM_LIMIT),
    )(pos, ys, x1, mod, gf)


def _route_plan(route, tm):
    n = route.shape[0]
    flat_e = route[:, 0:2].astype(jnp.int32).reshape(-1)
    flat_w = route[:, 2:4].reshape(-1)
    n_tiles = (2 * n + N_EXPERTS * (tm - 1)) // tm
    onehot = (flat_e[:, None] == jnp.arange(N_EXPERTS, dtype=jnp.int32)[None, :]).astype(jnp.int32)
    csum = jnp.cumsum(onehot, axis=0)
    rank = jnp.sum(csum * onehot, axis=1) - 1
    counts = csum[-1]
    tiles_per_e = (counts + tm - 1) // tm
    tile_end = jnp.cumsum(tiles_per_e)
    tile_start = tile_end - tiles_per_e
    n_used = tile_end[-1]
    pos = jnp.sum(onehot * tile_start[None, :], axis=1) * tm + rank
    token = jnp.arange(2 * n, dtype=jnp.int32) // 2
    src = jnp.zeros((n_tiles * tm,), jnp.int32).at[pos].set(token)
    w_sorted = jnp.zeros((n_tiles * tm,), F32).at[pos].set(flat_w)
    tiles = jnp.arange(n_tiles, dtype=jnp.int32)
    texp = jnp.sum((tiles[:, None] >= tile_end[None, :]).astype(jnp.int32), axis=1)
    texp = jnp.where(tiles < n_used, texp, texp[n_used - 1])
    first = jnp.concatenate([jnp.ones((1,), jnp.int32), (texp[1:] != texp[:-1]).astype(jnp.int32)])
    return texp, first, n_used.reshape(1), src, w_sorted.reshape(-1, 1), pos


def _round_up(n, m):
    return (n + m - 1) // m * m


def kernel(x, c, w_ada, b_ada, norm1_g, w_in, mu_shift, w0, w_decay_up, a0, w_aaa_up, w_gate_up,
           k_k, k_a, r_k, ln_x_w, ln_x_b, rel_bias, beta_rwkv, beta_attn, w_out, norm2_g, w_group,
           b_group, w_expert, b_expert, w_gate, w_up, w_down, norm_f_g):
    assert w_ada.shape[0] == 1, "single trunk layer"
    batch, seq, d = x.shape
    r_width = w0.shape[1]
    a_width = beta_attn.shape[1]
    shift_width = mu_shift.shape[1]
    assert shift_width == 3 * r_width + DECAY_LORA + AAA_LORA + GATE_LORA
    x2d = x.reshape(batch * seq, d)
    row = lambda t: t.reshape(1, -1)

    c8 = jnp.pad(c, ((0, 8 - batch), (0, 0)))
    mod = _ada_call(c8, w_ada[0], row(b_ada[0]), tn=1536)[:batch].reshape(batch, 6, d)

    tm = min(512, seq)
    pw = _round_up(shift_width, 3 * LANES)
    p_r = _proj_call(x2d, row(norm1_g[0]), mod, w_in[0], pw, pw // 3, tm, seq, F32)
    w_attn = w_in[0][:, shift_width:].astype(BF16)
    p_a = _proj_call(x2d, row(norm1_g[0]), mod, w_attn, 3 * a_width, a_width, tm, seq, BF16)

    mu = jnp.pad(mu_shift[0], (0, pw - shift_width))
    vecs = [row(mu), row(w0[0]), row(a0[0]), row(k_k[0]), row(k_a[0]), row(r_k[0]), row(ln_x_w[0]),
            row(ln_x_b[0]), row(beta_rwkv[0])]
    y_r = _rwkv_call(p_r, vecs, w_decay_up[0], w_aaa_up[0], w_gate_up[0], batch, seq, r_width,
                     tb=min(256, seq))

    left = LEFT_CHUNKS * CHUNK
    dist = left + CHUNK - 1 - jnp.arange(BAND + CHUNK - 1)
    base = rel_bias[0][:, jnp.clip(dist, -REL_CLIP, REL_CLIP) + REL_CLIP].astype(F32)
    bias = jnp.stack([base[:, CHUNK - 1 - qi:CHUNK - 1 - qi + BAND] for qi in range(CHUNK)], axis=1)
    bias = bias.reshape(a_width // PAIR, PAIR, BAND)
    y_a = _attn_call(p_a, bias, row(beta_attn[0]), batch, seq, a_width, qb=min(512, seq))

    n_route = N_GROUPS + N_EXPERTS
    wr = jnp.concatenate([w_group[0], w_expert[0].transpose(1, 0, 2).reshape(d, N_EXPERTS)], axis=1)
    wr = jnp.pad(wr, ((0, 0), (0, LANES - n_route)))
    br = jnp.pad(jnp.concatenate([b_group[0], b_expert[0].reshape(-1)]), (0, LANES - n_route))
    x1, h2, route = _out_router_call(y_r, y_a, w_out[0].astype(BF16), x2d, mod, row(norm2_g[0]),
                                     wr, row(br), seq, tm)

    f = w_gate.shape[-1]
    tm_e = 256
    texp, first, meta, src, w_sorted, pos = _route_plan(route, tm_e)
    ys = _moe_call(texp, first, meta, src, h2, w_sorted, w_gate[0].reshape(N_EXPERTS, d, f),
                   w_up[0].reshape(N_EXPERTS, d, f), w_down[0].reshape(N_EXPERTS, f, d), tm_e)
    out = _final_call(pos, ys, x1, mod, row(norm_f_g), seq, min(256, seq))
    return out.reshape(batch, seq, d)
```

```python
import functools

import jax
import jax.numpy as jnp
from jax import lax
from jax.experimental import pallas as pl
from jax.experimental.pallas import tpu as pltpu

F32 = jnp.float32
BF16 = jnp.bfloat16
HI = lax.Precision.HIGHEST

LANES = 128
HEAD_DIM = 64
PAIR = 2 * HEAD_DIM
CHUNK = 64
LEFT_CHUNKS = 8
BAND = (LEFT_CHUNKS + 1) * CHUNK
ATTN_BLOCK_CHUNKS = 2
PREP_PAIRS = 4
REL_CLIP = 128
DECAY_LORA = 64
AAA_LORA = 64
GATE_LORA = 160
N_GROUPS = 4
EXPERTS_PER_GROUP = 8
N_EXPERTS = N_GROUPS * EXPERTS_PER_GROUP
RMS_EPS = 1e-6
GN_EPS = 64e-5
L2_EPS = 1e-12
NEG_INF = -1e30
VMEM_LIMIT = 56 * 1024 * 1024

SHIFT1, SCALE1, GATE1, SHIFT2, SCALE2, GATE2 = range(6)


def _dot(a, b, precision=None):
    return jnp.dot(a, b, precision=precision, preferred_element_type=F32)


def _dot_nt(a, b, precision=None):
    return lax.dot_general(a, b, (((1,), (1,)), ((), ())), precision=precision,
                           preferred_element_type=F32)


def _sigmoid(x):
    return 1.0 / (1.0 + jnp.exp(-x))


def _rms(x):
    return x * lax.rsqrt(jnp.mean(x * x, axis=-1, keepdims=True) + RMS_EPS)


def _ada_kernel(c_ref, w_ref, b_ref, o_ref):
    c = c_ref[...]
    o_ref[...] = _dot(c * _sigmoid(c), w_ref[...], HI) + b_ref[...]


def _ada_call(c8, w, b, tn):
    rows, d = c8.shape
    n = w.shape[1]
    return pl.pallas_call(
        _ada_kernel,
        grid=(n // tn,),
        in_specs=[pl.BlockSpec((rows, d), lambda j: (0, 0)),
                  pl.BlockSpec((d, tn), lambda j: (0, j)),
                  pl.BlockSpec((1, tn), lambda j: (0, j))],
        out_specs=pl.BlockSpec((rows, tn), lambda j: (0, j)),
        out_shape=jax.ShapeDtypeStruct((rows, n), F32),
        compiler_params=pltpu.CompilerParams(vmem_limit_bytes=VMEM_LIMIT),
    )(c8, w, b)


def _hmod_kernel(x_ref, g_ref, mod_ref, o_ref):
    m = mod_ref[0]
    h = _rms(x_ref[...]) * g_ref[...] * (1.0 + m[SCALE1:SCALE1 + 1]) + m[SHIFT1:SHIFT1 + 1]
    o_ref[...] = h.astype(BF16)


def _hmod_call(x2d, g, mod, tm, seq):
    rows, d = x2d.shape
    tiles_per_seq = seq // tm
    return pl.pallas_call(
        _hmod_kernel,
        grid=(rows // tm,),
        in_specs=[pl.BlockSpec((tm, d), lambda i: (i, 0)),
                  pl.BlockSpec((1, d), lambda i: (0, 0)),
                  pl.BlockSpec((1, 6, d), lambda i: (i // tiles_per_seq, 0, 0))],
        out_specs=pl.BlockSpec((tm, d), lambda i: (i, 0)),
        out_shape=jax.ShapeDtypeStruct((rows, d), BF16),
        compiler_params=pltpu.CompilerParams(
            dimension_semantics=("parallel",), vmem_limit_bytes=VMEM_LIMIT),
    )(x2d, g, mod)


def _matmul_kernel(a_ref, w_ref, o_ref):
    o_ref[...] = _dot(a_ref[...], w_ref[...]).astype(o_ref.dtype)


def _matmul_call(a, w, tn, tm, out_dtype):
    rows, d = a.shape
    n_out = w.shape[1]
    return pl.pallas_call(
        _matmul_kernel,
        grid=(n_out // tn, rows // tm),
        in_specs=[pl.BlockSpec((tm, d), lambda j, i: (i, 0)),
                  pl.BlockSpec((d, tn), lambda j, i: (0, j))],
        out_specs=pl.BlockSpec((tm, tn), lambda j, i: (i, j)),
        out_shape=jax.ShapeDtypeStruct((rows, n_out), out_dtype),
        compiler_params=pltpu.CompilerParams(
            dimension_semantics=("parallel", "parallel"), vmem_limit_bytes=VMEM_LIMIT),
    )(a, w)


def _split3(t):
    hi = t.astype(BF16)
    rest = t - hi.astype(F32)
    mid = rest.astype(BF16)
    lo = (rest - mid.astype(F32)).astype(BF16)
    return hi, mid, lo


def _dot_f32_lhs(t, m):
    hi, mid, lo = _split3(t)
    return _dot(lo, m) + _dot(mid, m) + _dot(hi, m)


def _dot_f32_rhs(m, t):
    hi, mid, lo = _split3(t)
    return _dot(m, lo) + _dot(m, mid) + _dot(m, hi)


def _dot_x3(a, b):
    ah = a.astype(BF16)
    al = (a - ah.astype(F32)).astype(BF16)
    bh = b.astype(BF16)
    bl = (b - bh.astype(F32)).astype(BF16)
    return _dot(al, bh) + _dot(ah, bl) + _dot(ah, bh)


def _rwkv_kernel(p_ref, mu_ref, w0_ref, a0_ref, kk_ref, ka_ref, rk_ref, lnw_ref, lnb_ref, beta_ref,
                 wd_ref, wa_ref, wg_ref, o_ref,
                 carry_scr, state_scr, ar_scr, bk_scr, bkh_scr, v_scr, et_scr, y_scr,
                 pm_scr, arb_scr, av_scr, vk_scr, *, width, tb):
    n_pairs = width // PAIR
    n_chunks = tb // CHUNK
    stacked = 2 * CHUNK

    @pl.when(pl.program_id(1) == 0)
    def _():
        carry_scr[...] = jnp.zeros_like(carry_scr)
        state_scr[...] = jnp.zeros_like(state_scr)

    p = p_ref[...]
    row = lax.broadcasted_iota(jnp.int32, (tb, 1), 0)
    prev = jnp.where(row == 0, carry_scr[...], pltpu.roll(p, 1, axis=0))
    carry_scr[...] = p[tb - 1:tb, :]
    ps = p + (prev - p) * mu_ref[...]

    c = width
    r = ps[:, :c]
    k = ps[:, c:2 * c]
    v = ps[:, 2 * c:3 * c]
    o = 3 * c
    xw = ps[:, o:o + DECAY_LORA]
    xa = ps[:, o + DECAY_LORA:o + DECAY_LORA + AAA_LORA]
    xg = ps[:, o + DECAY_LORA + AAA_LORA:o + DECAY_LORA + AAA_LORA + GATE_LORA]

    z = w0_ref[...] + _dot_x3(jnp.tanh(xw), wd_ref[...])
    softplus_neg_z = jnp.maximum(-z, 0.0) + jnp.log(1.0 + jnp.exp(-jnp.abs(z)))
    lw = -jnp.exp(-softplus_neg_z - 0.5)
    a = _sigmoid(a0_ref[...] + _dot(xa.astype(BF16), wa_ref[...].astype(BF16)))
    g = _dot(_sigmoid(xg).astype(BF16), wg_ref[...].astype(BF16))

    li = lax.broadcasted_iota(jnp.int32, (PAIR, PAIR), 0)
    lj = lax.broadcasted_iota(jnp.int32, (PAIR, PAIR), 1)
    head_ones = ((li // HEAD_DIM) == (lj // HEAD_DIM)).astype(BF16)

    def head_sum(t):
        return jnp.concatenate(
            [_dot_f32_lhs(t[:, q * PAIR:(q + 1) * PAIR], head_ones) for q in range(n_pairs)], axis=1)

    kk = k * kk_ref[...]
    kk = kk / jnp.maximum(jnp.sqrt(head_sum(kk * kk)), L2_EPS)
    k2 = k * (1.0 + (a - 1.0) * ka_ref[...])

    ti = lax.broadcasted_iota(jnp.int32, (tb, tb), 0)
    tj = lax.broadcasted_iota(jnp.int32, (tb, tb), 1)
    tri = jnp.logical_and((ti // CHUNK) == (tj // CHUNK), tj <= ti).astype(BF16)
    cum = _dot_f32_rhs(tri, lw)
    tot = jnp.concatenate(
        [jnp.broadcast_to(cum[(ci + 1) * CHUNK - 1:(ci + 1) * CHUNK, :], (CHUNK, c))
         for ci in range(n_chunks)], axis=0)

    e_neg = jnp.exp(-cum)
    e_rem = jnp.exp(tot - cum)
    kka = kk * a
    lane = lax.broadcasted_iota(jnp.int32, (1, PAIR), 1)
    first_head = lane < HEAD_DIM

    def put(scr, base, val):
        for q in range(n_pairs):
            vq = val[:, q * PAIR:(q + 1) * PAIR]
            h0 = jnp.where(first_head, vq, 0.0).astype(scr.dtype)
            h1 = jnp.where(first_head, 0.0, vq).astype(scr.dtype)
            for ci in range(n_chunks):
                rows = slice(ci * CHUNK, (ci + 1) * CHUNK)
                scr[q, ci, base:base + CHUNK, :] = h0[rows]
                scr[q, ci, base + CHUNK:base + stacked, :] = h1[rows]

    put(ar_scr, 0, -kk * jnp.exp(cum - lw))
    put(ar_scr, stacked, r * jnp.exp(cum))
    put(bk_scr, 0, kka * e_neg)
    put(bk_scr, stacked, k2 * e_neg)
    put(bkh_scr, 0, kka * e_rem)
    put(bkh_scr, stacked, k2 * e_rem)
    put(v_scr, 0, v)
    e_tot = jnp.exp(tot)
    for q in range(n_pairs):
        et_scr[q] = e_tot[:, q * PAIR:(q + 1) * PAIR]

    si = lax.broadcasted_iota(jnp.int32, (2 * stacked, 2 * stacked), 0)
    sj = lax.broadcasted_iota(jnp.int32, (2 * stacked, 2 * stacked), 1)
    keep = jnp.where(si < stacked, si, si - stacked + 1) > (sj % stacked)
    eye = (li == lj).astype(F32)

    chunks = range(n_chunks)
    pairs = range(n_pairs)

    def prepare(step, carry):
        chains = [(step * PREP_PAIRS + dq, ci) for dq in range(PREP_PAIRS) for ci in chunks]
        n = []
        for q, ci in chains:
            sc = jnp.where(keep, _dot_nt(ar_scr[q, ci], bk_scr[q, ci]), 0.0)
            scb = sc.astype(BF16)
            vm = v_scr[q, ci]
            arb_scr[q, ci] = scb[stacked:, :stacked]
            av_scr[q, ci] = _dot(scb[:, stacked:], vm.astype(BF16))
            vk_scr[q, ci] = _dot(vm.T.astype(BF16), bkh_scr[q, ci, stacked:, :])
            n.append(sc[:stacked, :stacked])
        acc = [eye + t for t in n]
        pw = [t.astype(BF16) for t in n]
        pw = [_dot(t, t).astype(BF16) for t in pw]
        for it in range(1, 6):
            if it < 5:
                both = [_dot(jnp.concatenate([p_, a_.astype(BF16)], axis=0), p_)
                        for p_, a_ in zip(pw, acc)]
                pw = [t[:stacked].astype(BF16) for t in both]
                acc = [a_ + t[stacked:] for a_, t in zip(acc, both)]
            else:
                acc = [a_ + _dot(a_.astype(BF16), p_) for p_, a_ in zip(pw, acc)]
        for (q, ci), a_ in zip(chains, acc):
            pm_scr[q, ci] = a_.astype(BF16)
        return carry

    lax.fori_loop(0, n_pairs // PREP_PAIRS, prepare, 0)

    s = [state_scr[q] for q in pairs]
    for ci in chunks:
        sr = [_dot_nt(ar_scr[q, ci], s[q].astype(BF16)) for q in pairs]
        u = [_dot(pm_scr[q, ci], (sr[q][:stacked] + av_scr[q, ci, :stacked, :]).astype(BF16))
             for q in pairs]
        for q in pairs:
            ym = (sr[q][stacked:] + av_scr[q, ci, stacked:, :]
                  + _dot(arb_scr[q, ci], u[q].astype(BF16)))
            y_scr[q, ci * CHUNK:(ci + 1) * CHUNK, :] = ym[:CHUNK] + ym[CHUNK:]
        s = [s[q] * et_scr[q, ci * CHUNK:ci * CHUNK + 1, :]
             + _dot(u[q].T.astype(BF16), bkh_scr[q, ci, :stacked, :]) + vk_scr[q, ci] for q in pairs]
    for q in pairs:
        state_scr[q] = s[q]

    y = jnp.concatenate([y_scr[q] for q in range(n_pairs)], axis=1)
    mean = head_sum(y) * (1.0 / HEAD_DIM)
    d = y - mean
    var = head_sum(d * d) * (1.0 / HEAD_DIM)
    yn = d * lax.rsqrt(var + GN_EPS) * lnw_ref[...] + lnb_ref[...]
    bonus = head_sum(r * k2 * rk_ref[...]) * v
    o_ref[...] = (yn + bonus) * g * beta_ref[...]


def _rwkv_call(p_r, vecs, wd, wa, wg, batch, seq, width, tb):
    pw = p_r.shape[1]
    n_pairs = width // PAIR
    n_chunks = tb // CHUNK
    tiles = seq // tb
    vec_specs = [pl.BlockSpec((1, v.shape[1]), lambda b, t: (0, 0)) for v in vecs]
    mat_specs = [pl.BlockSpec(m.shape, lambda b, t: (0, 0)) for m in (wd, wa, wg)]
    stacked2 = pltpu.VMEM((n_pairs, n_chunks, 4 * CHUNK, PAIR), BF16)
    return pl.pallas_call(
        functools.partial(_rwkv_kernel, width=width, tb=tb),
        grid=(batch, tiles),
        in_specs=[pl.BlockSpec((tb, pw), lambda b, t: (b * tiles + t, 0))] + vec_specs + mat_specs,
        out_specs=pl.BlockSpec((tb, width), lambda b, t: (b * tiles + t, 0)),
        out_shape=jax.ShapeDtypeStruct((batch * seq, width), F32),
        scratch_shapes=[pltpu.VMEM((1, pw), F32),
                        pltpu.VMEM((n_pairs, PAIR, PAIR), F32),
                        stacked2, stacked2, stacked2,
                        pltpu.VMEM((n_pairs, n_chunks, 2 * CHUNK, PAIR), F32),
                        pltpu.VMEM((n_pairs, tb, PAIR), F32),
                        pltpu.VMEM((n_pairs, tb, PAIR), F32),
                        pltpu.VMEM((n_pairs, n_chunks, 2 * CHUNK, PAIR), BF16),
                        pltpu.VMEM((n_pairs, n_chunks, 2 * CHUNK, PAIR), BF16),
                        pltpu.VMEM((n_pairs, n_chunks, 4 * CHUNK, PAIR), F32),
                        pltpu.VMEM((n_pairs, n_chunks, 2 * CHUNK, PAIR), F32)],
        compiler_params=pltpu.CompilerParams(
            dimension_semantics=("parallel", "arbitrary"), vmem_limit_bytes=VMEM_LIMIT),
    )(p_r, *vecs, wd, wa, wg)


def _attn_kernel(q_ref, kp_ref, kc_ref, vp_ref, vc_ref, bias_ref, beta_ref, o_ref, kwin, vwin,
                 *, width, qb):
    i = pl.program_id(1)
    n_pairs = width // PAIR
    kwin[0:qb, :] = kp_ref[...]
    kwin[qb:2 * qb, :] = kc_ref[...]
    vwin[0:qb, :] = vp_ref[...]
    vwin[qb:2 * qb, :] = vc_ref[...]
    lane = lax.broadcasted_iota(jnp.int32, (1, PAIR), 1)
    first_head = lane < HEAD_DIM
    rows = ATTN_BLOCK_CHUNKS * CHUNK
    win = BAND + rows - CHUNK
    col = lax.broadcasted_iota(jnp.int32, (1, win), 1)
    scale = HEAD_DIM ** -0.5
    left = LEFT_CHUNKS * CHUNK

    for q in range(n_pairs):
        lanes = slice(q * PAIR, (q + 1) * PAIR)
        for blk in range(qb // rows):
            r0 = blk * rows
            qc = q_ref[r0:r0 + rows, lanes]
            zero = jnp.zeros_like(qc)
            qm = jnp.concatenate([jnp.where(first_head, qc, zero), jnp.where(first_head, zero, qc)],
                                 axis=0)
            s = _dot_nt(qm, kwin[r0:r0 + win, lanes]) * scale + bias_ref[q]
            valid = jnp.logical_or(i > 0, col + r0 >= left)
            s = jnp.where(valid, s, NEG_INF)
            e = jnp.exp(s - jnp.max(s, axis=-1, keepdims=True))
            denom = jnp.sum(e, axis=-1, keepdims=True)
            o = _dot(e.astype(BF16), vwin[r0:r0 + win, lanes]) / denom
            oc = jnp.where(first_head, o[:rows], o[rows:])
            o_ref[r0:r0 + rows, lanes] = oc * beta_ref[:, lanes]


def _attn_call(p_a, bias, beta, batch, seq, width, qb):
    assert qb == LEFT_CHUNKS * CHUNK, "key window = previous block + current block"
    tiles = seq // qb
    n_pairs = width // PAIR

    def cur(col):
        return pl.BlockSpec((qb, width), lambda b, t: (b * tiles + t, col))

    def prev(col):
        return pl.BlockSpec((qb, width), lambda b, t: (b * tiles + jnp.maximum(t - 1, 0), col))

    return pl.pallas_call(
        functools.partial(_attn_kernel, width=width, qb=qb),
        grid=(batch, tiles),
        in_specs=[cur(0), prev(1), cur(1), prev(2), cur(2),
                  pl.BlockSpec(bias.shape, lambda b, t: (0, 0, 0)),
                  pl.BlockSpec((1, width), lambda b, t: (0, 0))],
        out_specs=pl.BlockSpec((qb, width), lambda b, t: (b * tiles + t, 0)),
        out_shape=jax.ShapeDtypeStruct((batch * seq, width), F32),
        scratch_shapes=[pltpu.VMEM((2 * qb, width), BF16), pltpu.VMEM((2 * qb, width), BF16)],
        compiler_params=pltpu.CompilerParams(
            dimension_semantics=("parallel", "arbitrary"), vmem_limit_bytes=VMEM_LIMIT),
    )(p_a, p_a, p_a, p_a, p_a, bias, beta)


def _out_router_kernel(yr_ref, ya_ref, wo_ref, x_ref, mod_ref, g2_ref, wr_ref, br_ref,
                       x1_ref, h2_ref, route_ref, *, r_width):
    y = (_dot(yr_ref[...].astype(BF16), wo_ref[0:r_width, :])
         + _dot(ya_ref[...].astype(BF16), wo_ref[r_width:, :]))
    m = mod_ref[0]
    x1 = x_ref[...] + m[GATE1:GATE1 + 1] * y
    x1_ref[...] = x1
    h2 = _rms(x1) * g2_ref[...] * (1.0 + m[SCALE2:SCALE2 + 1]) + m[SHIFT2:SHIFT2 + 1]
    h2_ref[...] = h2

    logits = _dot_x3(h2, wr_ref[...]) + br_ref[...]
    li = lax.broadcasted_iota(jnp.int32, logits.shape, 1).astype(F32)
    none = jnp.float32(-jnp.inf)
    far = jnp.float32(LANES)
    is_group = li < N_GROUPS
    gl = jnp.where(is_group, logits, none)
    gmax = jnp.max(gl, axis=-1, keepdims=True)
    gidx = jnp.min(jnp.where(gl == gmax, li, far), axis=-1, keepdims=True)
    g_w = 1.0 / jnp.sum(jnp.where(is_group, jnp.exp(logits - gmax), 0.0), axis=-1, keepdims=True)
    lo = N_GROUPS + EXPERTS_PER_GROUP * gidx
    el = jnp.where(jnp.logical_and(li >= lo, li < lo + EXPERTS_PER_GROUP), logits, none)
    m1 = jnp.max(el, axis=-1, keepdims=True)
    i1 = jnp.min(jnp.where(el == m1, li, far), axis=-1, keepdims=True)
    el2 = jnp.where(li == i1, none, el)
    m2 = jnp.max(el2, axis=-1, keepdims=True)
    i2 = jnp.min(jnp.where(el2 == m2, li, far), axis=-1, keepdims=True)
    e2 = jnp.exp(m2 - m1)
    w1 = g_w / (1.0 + e2)
    w2 = g_w * e2 / (1.0 + e2)
    route = jnp.where(li == 0.0, i1 - N_GROUPS, 0.0) + jnp.where(li == 1.0, i2 - N_GROUPS, 0.0)
    route_ref[...] = route + jnp.where(li == 2.0, w1, 0.0) + jnp.where(li == 3.0, w2, 0.0)


def _out_router_call(y_r, y_a, wo, x2d, mod, g2, wr, br, seq, tm):
    rows, d = x2d.shape
    r_width = y_r.shape[1]
    a_width = y_a.shape[1]
    tiles_per_seq = seq // tm
    return pl.pallas_call(
        functools.partial(_out_router_kernel, r_width=r_width),
        grid=(rows // tm,),
        in_specs=[pl.BlockSpec((tm, r_width), lambda i: (i, 0)),
                  pl.BlockSpec((tm, a_width), lambda i: (i, 0)),
                  pl.BlockSpec(wo.shape, lambda i: (0, 0)),
                  pl.BlockSpec((tm, d), lambda i: (i, 0)),
                  pl.BlockSpec((1, 6, d), lambda i: (i // tiles_per_seq, 0, 0)),
                  pl.BlockSpec((1, d), lambda i: (0, 0)),
                  pl.BlockSpec((d, LANES), lambda i: (0, 0)),
                  pl.BlockSpec((1, LANES), lambda i: (0, 0))],
        out_specs=[pl.BlockSpec((tm, d), lambda i: (i, 0)),
                   pl.BlockSpec((tm, d), lambda i: (i, 0)),
                   pl.BlockSpec((tm, LANES), lambda i: (i, 0))],
        out_shape=[jax.ShapeDtypeStruct((rows, d), F32),
                   jax.ShapeDtypeStruct((rows, d), F32),
                   jax.ShapeDtypeStruct((rows, LANES), F32)],
        compiler_params=pltpu.CompilerParams(
            dimension_semantics=("parallel",), vmem_limit_bytes=VMEM_LIMIT),
    )(y_r, y_a, wo, x2d, mod, g2, wr, br)


def _moe_kernel(texp_ref, first_ref, nexp_ref, meta_ref, src_ref,
                h2_hbm, wg_hbm, wu_hbm, wd_hbm, o_ref,
                xbuf, xsem, wg_land, wu_land, wd_land, wsem, wg_bf, wu_bf, wd_bf, *, tm):
    t = pl.program_id(0)
    n_used = meta_ref[0]
    slot = t % 2

    def weight_copies(e):
        return (pltpu.make_async_copy(wg_hbm.at[e], wg_land, wsem.at[0]),
                pltpu.make_async_copy(wu_hbm.at[e], wu_land, wsem.at[1]),
                pltpu.make_async_copy(wd_hbm.at[e], wd_land, wsem.at[2]))

    def row_copy(tile, i, buf_slot):
        tok = src_ref[tile * tm + i]
        return pltpu.make_async_copy(h2_hbm.at[pl.ds(tok, 1)], xbuf.at[buf_slot, pl.ds(i, 1)],
                                     xsem.at[buf_slot])

    @pl.when(t == 0)
    def _():
        for cp in weight_copies(texp_ref[0]):
            cp.start()

        def body(i, carry):
            row_copy(0, i, 0).start()
            return carry

        lax.fori_loop(0, tm, body, 0, unroll=8)

    @pl.when(jnp.logical_and(t < n_used, first_ref[t] == 1))
    def _():
        for cp in weight_copies(texp_ref[t]):
            cp.wait()
        wg_bf[...] = wg_land[...].astype(BF16)
        wu_bf[...] = wu_land[...].astype(BF16)
        wd_bf[...] = wd_land[...].astype(BF16)

        @pl.when(nexp_ref[t] >= 0)
        def _():
            for cp in weight_copies(nexp_ref[t]):
                cp.start()

    def compute():
        pltpu.make_async_copy(h2_hbm.at[pl.ds(0, tm)], xbuf.at[slot], xsem.at[slot]).wait()
        x = xbuf[slot].astype(BF16)
        gate = _dot(x, wg_bf[...])
        up = _dot(x, wu_bf[...])
        h = gate * _sigmoid(gate) * up
        o_ref[...] = _dot(h.astype(BF16), wd_bf[...])

    @pl.when(t + 1 < n_used)
    def _():
        for i in range(tm):
            row_copy(t + 1, i, 1 - slot).start()
        compute()

    @pl.when(t + 1 == n_used)
    def _():
        compute()

    @pl.when(t >= n_used)
    def _():
        o_ref[...] = jnp.zeros_like(o_ref)


def _moe_call(texp, first, nexp, meta, src, h2, wg, wu, wd, tm):
    d = h2.shape[1]
    n_e, _, f = wg.shape
    n_tiles = texp.shape[0]
    grid_spec = pltpu.PrefetchScalarGridSpec(
        num_scalar_prefetch=5,
        grid=(n_tiles,),
        in_specs=[pl.BlockSpec(memory_space=pl.ANY)] * 4,
        out_specs=pl.BlockSpec((tm, d), lambda t, *_: (t, 0)),
        scratch_shapes=[pltpu.VMEM((2, tm, d), F32), pltpu.SemaphoreType.DMA((2,)),
                        pltpu.VMEM((d, f), F32), pltpu.VMEM((d, f), F32), pltpu.VMEM((f, d), F32),
                        pltpu.SemaphoreType.DMA((3,)),
                        pltpu.VMEM((d, f), BF16), pltpu.VMEM((d, f), BF16), pltpu.VMEM((f, d), BF16)])
    return pl.pallas_call(
        functools.partial(_moe_kernel, tm=tm),
        grid_spec=grid_spec,
        out_shape=jax.ShapeDtypeStruct((n_tiles * tm, d), F32),
        compiler_params=pltpu.CompilerParams(
            dimension_semantics=("arbitrary",), vmem_limit_bytes=VMEM_LIMIT),
    )(texp, first, nexp, meta, src, h2, wg, wu, wd)


def _final_kernel(pos_ref, ys_hbm, route_ref, x1_ref, mod_ref, gf_ref, o_ref, ybuf, sem, *, tm):
    i = pl.program_id(0)
    n = pl.num_programs(0)
    slot = i % 2

    def gather_rows(tile, buf_slot):
        base = tile * tm * 2

        def body(r, carry):
            for s in range(2):
                pltpu.make_async_copy(ys_hbm.at[pl.ds(pos_ref[base + 2 * r + s], 1)],
                                      ybuf.at[buf_slot, s, pl.ds(r, 1)], sem.at[buf_slot]).start()
            return carry

        lax.fori_loop(0, tm, body, 0, unroll=4)

    @pl.when(i == 0)
    def _():
        gather_rows(0, 0)

    @pl.when(i + 1 < n)
    def _():
        gather_rows(i + 1, 1 - slot)

    for s in range(2):
        pltpu.make_async_copy(ys_hbm.at[pl.ds(0, tm)], ybuf.at[slot, s], sem.at[slot]).wait()
    route = route_ref[...]
    f = route[:, 2:3] * ybuf[slot, 0] + route[:, 3:4] * ybuf[slot, 1]
    x2 = x1_ref[...] + mod_ref[0][GATE2:GATE2 + 1] * f
    o_ref[...] = _rms(x2) * gf_ref[...]


def _final_call(pos, ys, route, x1, mod, gf, seq, tm):
    rows, d = x1.shape
    tiles_per_seq = seq // tm
    grid_spec = pltpu.PrefetchScalarGridSpec(
        num_scalar_prefetch=1,
        grid=(rows // tm,),
        in_specs=[pl.BlockSpec(memory_space=pl.ANY),
                  pl.BlockSpec((tm, LANES), lambda i, pos: (i, 0)),
                  pl.BlockSpec((tm, d), lambda i, pos: (i, 0)),
                  pl.BlockSpec((1, 6, d), lambda i, pos: (i // tiles_per_seq, 0, 0)),
                  pl.BlockSpec((1, d), lambda i, pos: (0, 0))],
        out_specs=pl.BlockSpec((tm, d), lambda i, pos: (i, 0)),
        scratch_shapes=[pltpu.VMEM((2, 2, tm, d), F32), pltpu.SemaphoreType.DMA((2,))])
    return pl.pallas_call(
        functools.partial(_final_kernel, tm=tm),
        grid_spec=grid_spec,
        out_shape=jax.ShapeDtypeStruct((rows, d), F32),
        compiler_params=pltpu.CompilerParams(
            dimension_semantics=("arbitrary",), vmem_limit_bytes=VMEM_LIMIT),
    )(pos, ys, route, x1, mod, gf)


def _route_plan(route, tm):
    n = route.shape[0]
    flat_e = route[:, 0:2].astype(jnp.int32).reshape(-1)
    n_tiles = (2 * n + N_EXPERTS * (tm - 1)) // tm
    experts = jnp.arange(N_EXPERTS, dtype=jnp.int32)
    onehot = (flat_e[:, None] == experts[None, :]).astype(jnp.int32)
    csum = jnp.cumsum(onehot, axis=0)
    rank = jnp.sum(csum * onehot, axis=1) - 1
    counts = csum[-1]
    tiles_per_e = (counts + tm - 1) // tm
    tile_end = jnp.cumsum(tiles_per_e)
    tile_start = tile_end - tiles_per_e
    n_used = tile_end[-1]
    pos = jnp.sum(onehot * tile_start[None, :], axis=1) * tm + rank
    token = jnp.arange(2 * n, dtype=jnp.int32) // 2
    src = jnp.zeros((n_tiles * tm,), jnp.int32).at[pos].set(token)
    tiles = jnp.arange(n_tiles, dtype=jnp.int32)
    texp = jnp.sum((tiles[:, None] >= tile_end[None, :]).astype(jnp.int32), axis=1)
    texp = jnp.where(tiles < n_used, texp, texp[n_used - 1])
    first = jnp.concatenate([jnp.ones((1,), jnp.int32), (texp[1:] != texp[:-1]).astype(jnp.int32)])
    later_used = jnp.logical_and(experts[None, :] > experts[:, None], (tiles_per_e > 0)[None, :])
    next_used = jnp.min(jnp.where(later_used, experts[None, :], N_EXPERTS), axis=1)
    next_used = jnp.where(next_used == N_EXPERTS, -1, next_used)
    return texp, first, next_used[texp], n_used.reshape(1), src, pos


def _round_up(n, m):
    return (n + m - 1) // m * m


def kernel(x, c, w_ada, b_ada, norm1_g, w_in, mu_shift, w0, w_decay_up, a0, w_aaa_up, w_gate_up,
           k_k, k_a, r_k, ln_x_w, ln_x_b, rel_bias, beta_rwkv, beta_attn, w_out, norm2_g, w_group,
           b_group, w_expert, b_expert, w_gate, w_up, w_down, norm_f_g):
    assert w_ada.shape[0] == 1, "single trunk layer"
    batch, seq, d = x.shape
    r_width = w0.shape[1]
    a_width = beta_attn.shape[1]
    shift_width = mu_shift.shape[1]
    assert shift_width == 3 * r_width + DECAY_LORA + AAA_LORA + GATE_LORA
    x2d = x.reshape(batch * seq, d)
    row = lambda t: t.reshape(1, -1)

    c8 = jnp.pad(c, ((0, 8 - batch), (0, 0)))
    mod = _ada_call(c8, w_ada[0], row(b_ada[0]), tn=1536)[:batch].reshape(batch, 6, d)

    tm = min(512, seq)
    pw = _round_up(shift_width, 3 * LANES)
    h1 = _hmod_call(x2d, row(norm1_g[0]), mod, tm, seq)
    tm_p = min(1024, seq)
    p_r = _matmul_call(h1, w_in[0][:, :pw].astype(BF16), pw // 3, tm_p, F32)
    p_a = _matmul_call(h1, w_in[0][:, shift_width:].astype(BF16), a_width, tm_p, BF16)

    mu = jnp.pad(mu_shift[0], (0, pw - shift_width))
    vecs = [row(mu), row(w0[0]), row(a0[0]), row(k_k[0]), row(k_a[0]), row(r_k[0]), row(ln_x_w[0]),
            row(ln_x_b[0]), row(beta_rwkv[0])]
    y_r = _rwkv_call(p_r, vecs, w_decay_up[0], w_aaa_up[0], w_gate_up[0], batch, seq, r_width,
                     tb=min(256, seq))

    left = LEFT_CHUNKS * CHUNK
    dist = left + CHUNK - 1 - jnp.arange(BAND + CHUNK - 1)
    base = rel_bias[0][:, jnp.clip(dist, -REL_CLIP, REL_CLIP) + REL_CLIP].astype(F32)
    bias = jnp.stack([base[:, CHUNK - 1 - qi:CHUNK - 1 - qi + BAND] for qi in range(CHUNK)], axis=1)
    nb = ATTN_BLOCK_CHUNKS
    bias = jnp.concatenate(
        [jnp.pad(bias, ((0, 0), (0, 0), (cb * CHUNK, (nb - 1 - cb) * CHUNK)), constant_values=NEG_INF)
         for cb in range(nb)], axis=1)
    bias = bias.reshape(a_width // PAIR, 2 * nb * CHUNK, BAND + (nb - 1) * CHUNK)
    y_a = _attn_call(p_a, bias, row(beta_attn[0]), batch, seq, a_width, qb=min(512, seq))

    n_route = N_GROUPS + N_EXPERTS
    wr = jnp.concatenate([w_group[0], w_expert[0].transpose(1, 0, 2).reshape(d, N_EXPERTS)], axis=1)
    wr = jnp.pad(wr, ((0, 0), (0, LANES - n_route)))
    br = jnp.pad(jnp.concatenate([b_group[0], b_expert[0].reshape(-1)]), (0, LANES - n_route))
    x1, h2, route = _out_router_call(y_r, y_a, w_out[0].astype(BF16), x2d, mod, row(norm2_g[0]),
                                     wr, row(br), seq, tm)

    f = w_gate.shape[-1]
    tm_e = 256
    texp, first, nexp, meta, src, pos = _route_plan(route, tm_e)
    ys = _moe_call(texp, first, nexp, meta, src, h2, w_gate[0].reshape(N_EXPERTS, d, f),
                   w_up[0].reshape(N_EXPERTS, d, f), w_down[0].reshape(N_EXPERTS, f, d), tm_e)
    out = _final_call(pos, ys, route, x1, mod, row(norm_f_g), seq, min(256, seq))
    return out.reshape(batch, seq, d)
```

```python
import functools

import jax
import jax.numpy as jnp
from jax import lax
from jax.experimental import pallas as pl
from jax.experimental.pallas import tpu as pltpu

F32 = jnp.float32
BF16 = jnp.bfloat16
HI = lax.Precision.HIGHEST

LANES = 128
HEAD_DIM = 64
PAIR = 2 * HEAD_DIM
CHUNK = 64
LEFT_CHUNKS = 8
BAND = (LEFT_CHUNKS + 1) * CHUNK
ATTN_BLOCK_CHUNKS = 4
PREP_PAIRS = 4
REL_CLIP = 128
DECAY_LORA = 64
AAA_LORA = 64
GATE_LORA = 160
N_GROUPS = 4
EXPERTS_PER_GROUP = 8
N_EXPERTS = N_GROUPS * EXPERTS_PER_GROUP
RMS_EPS = 1e-6
GN_EPS = 64e-5
L2_EPS = 1e-12
NEG_INF = -1e30
VMEM_LIMIT = 56 * 1024 * 1024

SHIFT1, SCALE1, GATE1, SHIFT2, SCALE2, GATE2 = range(6)


def _dot(a, b, precision=None):
    return jnp.dot(a, b, precision=precision, preferred_element_type=F32)


def _dot_nt(a, b, precision=None):
    return lax.dot_general(a, b, (((1,), (1,)), ((), ())), precision=precision,
                           preferred_element_type=F32)


def _sigmoid(x):
    return 1.0 / (1.0 + jnp.exp(-x))


def _rms(x):
    return x * lax.rsqrt(jnp.mean(x * x, axis=-1, keepdims=True) + RMS_EPS)


HIGH_HALF = 0xFFFF0000


def _pack_rows(x):
    bits = lax.bitcast_convert_type(x.astype(BF16).astype(F32), jnp.uint32)
    half = x.shape[1] // 2
    return (bits[:, :half] >> 16) | (bits[:, half:] & jnp.uint32(HIGH_HALF))


def _unpack_rows(w):
    lo = lax.bitcast_convert_type(w << 16, F32)
    hi = lax.bitcast_convert_type(w & jnp.uint32(HIGH_HALF), F32)
    return lo, hi


def _ada_kernel(c_ref, w_ref, b_ref, o_ref):
    c = c_ref[...]
    o_ref[...] = _dot_x3(c * _sigmoid(c), w_ref[...]) + b_ref[...]


def _ada_call(c8, w, b, tn):
    rows, d = c8.shape
    n = w.shape[1]
    return pl.pallas_call(
        _ada_kernel,
        grid=(n // tn,),
        in_specs=[pl.BlockSpec((rows, d), lambda j: (0, 0)),
                  pl.BlockSpec((d, tn), lambda j: (0, j)),
                  pl.BlockSpec((1, tn), lambda j: (0, j))],
        out_specs=pl.BlockSpec((rows, tn), lambda j: (0, j)),
        out_shape=jax.ShapeDtypeStruct((rows, n), F32),
        compiler_params=pltpu.CompilerParams(vmem_limit_bytes=VMEM_LIMIT),
    )(c8, w, b)


def _hmod_kernel(x_ref, g_ref, mod_ref, o_ref):
    m = mod_ref[0]
    h = _rms(x_ref[...]) * g_ref[...] * (1.0 + m[SCALE1:SCALE1 + 1]) + m[SHIFT1:SHIFT1 + 1]
    o_ref[...] = h.astype(BF16)


def _hmod_call(x2d, g, mod, tm, seq):
    rows, d = x2d.shape
    tiles_per_seq = seq // tm
    return pl.pallas_call(
        _hmod_kernel,
        grid=(rows // tm,),
        in_specs=[pl.BlockSpec((tm, d), lambda i: (i, 0)),
                  pl.BlockSpec((1, d), lambda i: (0, 0)),
                  pl.BlockSpec((1, 6, d), lambda i: (i // tiles_per_seq, 0, 0))],
        out_specs=pl.BlockSpec((tm, d), lambda i: (i, 0)),
        out_shape=jax.ShapeDtypeStruct((rows, d), BF16),
        compiler_params=pltpu.CompilerParams(
            dimension_semantics=("parallel",), vmem_limit_bytes=VMEM_LIMIT),
    )(x2d, g, mod)


def _matmul_kernel(a_ref, w_ref, o_ref):
    o_ref[...] = _dot(a_ref[...], w_ref[...]).astype(o_ref.dtype)


def _matmul_call(a, w, n_out, tn, tm, out_dtype):
    rows, d = a.shape
    return pl.pallas_call(
        _matmul_kernel,
        grid=(n_out // tn, rows // tm),
        in_specs=[pl.BlockSpec((tm, d), lambda j, i: (i, 0)),
                  pl.BlockSpec((d, tn), lambda j, i: (0, j))],
        out_specs=pl.BlockSpec((tm, tn), lambda j, i: (i, j)),
        out_shape=jax.ShapeDtypeStruct((rows, n_out), out_dtype),
        compiler_params=pltpu.CompilerParams(
            dimension_semantics=("parallel", "parallel"), vmem_limit_bytes=VMEM_LIMIT),
    )(a, w)


def _split3(t):
    hi = t.astype(BF16)
    rest = t - hi.astype(F32)
    mid = rest.astype(BF16)
    lo = (rest - mid.astype(F32)).astype(BF16)
    return hi, mid, lo


def _dot_hilo_lhs(t, m):
    hi = t.astype(BF16)
    lo = (t - hi.astype(F32)).astype(BF16)
    return _dot(lo, m) + _dot(hi, m)


def _dot_f32_rhs(m, t):
    hi, mid, lo = _split3(t)
    return _dot(m, lo) + _dot(m, mid) + _dot(m, hi)


def _dot_x3(a, b):
    ah = a.astype(BF16)
    al = (a - ah.astype(F32)).astype(BF16)
    bh = b.astype(BF16)
    bl = (b - bh.astype(F32)).astype(BF16)
    return _dot(al, bh) + _dot(ah, bl) + _dot(ah, bh)


def _rwkv_kernel(p_ref, mu_ref, w0_ref, a0_ref, kk_ref, ka_ref, rk_ref, lnw_ref, lnb_ref, beta_ref,
                 wd_ref, wa_ref, wg_ref, o_ref,
                 carry_scr, state_scr, ar_scr, bk_scr, bkh_scr, v_scr, et_scr, y_scr,
                 pm_scr, arb_scr, av_scr, vk_scr, *, width, tb):
    n_pairs = width // PAIR
    n_chunks = tb // CHUNK
    stacked = 2 * CHUNK

    @pl.when(pl.program_id(1) == 0)
    def _():
        carry_scr[...] = jnp.zeros_like(carry_scr)
        state_scr[...] = jnp.zeros_like(state_scr)

    p = p_ref[...]
    row = lax.broadcasted_iota(jnp.int32, (tb, 1), 0)
    prev = jnp.where(row == 0, carry_scr[...], pltpu.roll(p, 1, axis=0))
    carry_scr[...] = p[tb - 1:tb, :]
    ps = p + (prev - p) * mu_ref[...]

    c = width
    r = ps[:, :c]
    k = ps[:, c:2 * c]
    v = ps[:, 2 * c:3 * c]
    o = 3 * c
    xw = ps[:, o:o + DECAY_LORA]
    xa = ps[:, o + DECAY_LORA:o + DECAY_LORA + AAA_LORA]
    xg = ps[:, o + DECAY_LORA + AAA_LORA:o + DECAY_LORA + AAA_LORA + GATE_LORA]

    z = w0_ref[...] + _dot_x3(jnp.tanh(xw), wd_ref[...])
    softplus_neg_z = jnp.maximum(-z, 0.0) + jnp.log(1.0 + jnp.exp(-jnp.abs(z)))
    lw = -jnp.exp(-softplus_neg_z - 0.5)
    a = _sigmoid(a0_ref[...] + _dot(xa.astype(BF16), wa_ref[...].astype(BF16)))
    g = _dot(_sigmoid(xg).astype(BF16), wg_ref[...].astype(BF16))

    li = lax.broadcasted_iota(jnp.int32, (PAIR, PAIR), 0)
    lj = lax.broadcasted_iota(jnp.int32, (PAIR, PAIR), 1)
    head_ones = ((li // HEAD_DIM) == (lj // HEAD_DIM)).astype(BF16)

    def head_sum(t):
        return jnp.concatenate(
            [_dot_hilo_lhs(t[:, q * PAIR:(q + 1) * PAIR], head_ones) for q in range(n_pairs)], axis=1)

    kk = k * kk_ref[...]
    kk = kk / jnp.maximum(jnp.sqrt(head_sum(kk * kk)), L2_EPS)
    k2 = k * (1.0 + (a - 1.0) * ka_ref[...])

    ti = lax.broadcasted_iota(jnp.int32, (tb, tb), 0)
    tj = lax.broadcasted_iota(jnp.int32, (tb, tb), 1)
    tri = jnp.logical_and((ti // CHUNK) == (tj // CHUNK), tj <= ti).astype(BF16)
    cum = _dot_f32_rhs(tri, lw)
    tot = jnp.concatenate(
        [jnp.broadcast_to(cum[(ci + 1) * CHUNK - 1:(ci + 1) * CHUNK, :], (CHUNK, c))
         for ci in range(n_chunks)], axis=0)

    e_neg = jnp.exp(-cum)
    e_rem = jnp.exp(tot - cum)
    kka = kk * a
    lane = lax.broadcasted_iota(jnp.int32, (1, PAIR), 1)
    first_head = lane < HEAD_DIM

    def put(scr, base, val):
        for q in range(n_pairs):
            vq = val[:, q * PAIR:(q + 1) * PAIR]
            h0 = jnp.where(first_head, vq, 0.0).astype(scr.dtype)
            h1 = jnp.where(first_head, 0.0, vq).astype(scr.dtype)
            for ci in range(n_chunks):
                rows = slice(ci * CHUNK, (ci + 1) * CHUNK)
                scr[q, ci, base:base + CHUNK, :] = h0[rows]
                scr[q, ci, base + CHUNK:base + stacked, :] = h1[rows]

    put(ar_scr, 0, -kk * jnp.exp(cum - lw))
    put(ar_scr, stacked, r * jnp.exp(cum))
    put(bk_scr, 0, kka * e_neg)
    put(bk_scr, stacked, k2 * e_neg)
    put(bkh_scr, 0, kka * e_rem)
    put(bkh_scr, stacked, k2 * e_rem)
    put(v_scr, 0, v)
    e_tot = jnp.exp(tot)
    for q in range(n_pairs):
        et_scr[q] = e_tot[:, q * PAIR:(q + 1) * PAIR]

    si = lax.broadcasted_iota(jnp.int32, (2 * stacked, 2 * stacked), 0)
    sj = lax.broadcasted_iota(jnp.int32, (2 * stacked, 2 * stacked), 1)
    keep = jnp.where(si < stacked, si, si - stacked + 1) > (sj % stacked)
    eye = (li == lj).astype(F32)

    chunks = range(n_chunks)
    pairs = range(n_pairs)

    def prepare(step, carry):
        chains = [(step * PREP_PAIRS + dq, ci) for dq in range(PREP_PAIRS) for ci in chunks]
        n = []
        for q, ci in chains:
            sc = jnp.where(keep, _dot_nt(ar_scr[q, ci], bk_scr[q, ci]), 0.0)
            scb = sc.astype(BF16)
            vm = v_scr[q, ci]
            arb_scr[q, ci] = scb[stacked:, :stacked]
            av_scr[q, ci] = _dot(scb[:, stacked:], vm.astype(BF16))
            vk_scr[q, ci] = _dot(vm.T.astype(BF16), bkh_scr[q, ci, stacked:, :])
            n.append(sc[:stacked, :stacked])
        acc = [eye + t for t in n]
        pw = [t.astype(BF16) for t in n]
        pw = [_dot(t, t).astype(BF16) for t in pw]
        for it in range(1, 6):
            if it < 5:
                both = [_dot(jnp.concatenate([p_, a_.astype(BF16)], axis=0), p_)
                        for p_, a_ in zip(pw, acc)]
                pw = [t[:stacked].astype(BF16) for t in both]
                acc = [a_ + t[stacked:] for a_, t in zip(acc, both)]
            else:
                acc = [a_ + _dot(a_.astype(BF16), p_) for p_, a_ in zip(pw, acc)]
        for (q, ci), a_ in zip(chains, acc):
            pm_scr[q, ci] = a_.astype(BF16)
        return carry

    lax.fori_loop(0, n_pairs // PREP_PAIRS, prepare, 0)

    s = [state_scr[q] for q in pairs]
    for ci in chunks:
        sr = [_dot_nt(ar_scr[q, ci], s[q].astype(BF16)) for q in pairs]
        u = [_dot(pm_scr[q, ci], (sr[q][:stacked] + av_scr[q, ci, :stacked, :]).astype(BF16))
             for q in pairs]
        for q in pairs:
            ym = (sr[q][stacked:] + av_scr[q, ci, stacked:, :]
                  + _dot(arb_scr[q, ci], u[q].astype(BF16)))
            y_scr[q, ci * CHUNK:(ci + 1) * CHUNK, :] = ym[:CHUNK] + ym[CHUNK:]
        s = [s[q] * et_scr[q, ci * CHUNK:ci * CHUNK + 1, :]
             + _dot(u[q].T.astype(BF16), bkh_scr[q, ci, :stacked, :]) + vk_scr[q, ci] for q in pairs]
    for q in pairs:
        state_scr[q] = s[q]

    y = jnp.concatenate([y_scr[q] for q in range(n_pairs)], axis=1)
    mean = head_sum(y) * (1.0 / HEAD_DIM)
    d = y - mean
    var = head_sum(d * d) * (1.0 / HEAD_DIM)
    yn = d * lax.rsqrt(var + GN_EPS) * lnw_ref[...] + lnb_ref[...]
    bonus = head_sum(r * k2 * rk_ref[...]) * v
    o_ref[...] = (yn + bonus) * g * beta_ref[...]


def _rwkv_call(p_r, vecs, wd, wa, wg, batch, seq, width, tb):
    pw = p_r.shape[1]
    n_pairs = width // PAIR
    n_chunks = tb // CHUNK
    tiles = seq // tb
    vec_specs = [pl.BlockSpec((1, v.shape[1]), lambda b, t: (0, 0)) for v in vecs]
    mat_specs = [pl.BlockSpec(m.shape, lambda b, t: (0, 0)) for m in (wd, wa, wg)]
    stacked2 = pltpu.VMEM((n_pairs, n_chunks, 4 * CHUNK, PAIR), BF16)
    return pl.pallas_call(
        functools.partial(_rwkv_kernel, width=width, tb=tb),
        grid=(batch, tiles),
        in_specs=[pl.BlockSpec((tb, pw), lambda b, t: (b * tiles + t, 0))] + vec_specs + mat_specs,
        out_specs=pl.BlockSpec((tb, width), lambda b, t: (b * tiles + t, 0)),
        out_shape=jax.ShapeDtypeStruct((batch * seq, width), F32),
        scratch_shapes=[pltpu.VMEM((1, pw), F32),
                        pltpu.VMEM((n_pairs, PAIR, PAIR), F32),
                        stacked2, stacked2, stacked2,
                        pltpu.VMEM((n_pairs, n_chunks, 2 * CHUNK, PAIR), F32),
                        pltpu.VMEM((n_pairs, tb, PAIR), F32),
                        pltpu.VMEM((n_pairs, tb, PAIR), F32),
                        pltpu.VMEM((n_pairs, n_chunks, 2 * CHUNK, PAIR), BF16),
                        pltpu.VMEM((n_pairs, n_chunks, 2 * CHUNK, PAIR), BF16),
                        pltpu.VMEM((n_pairs, n_chunks, 4 * CHUNK, PAIR), F32),
                        pltpu.VMEM((n_pairs, n_chunks, 2 * CHUNK, PAIR), F32)],
        compiler_params=pltpu.CompilerParams(
            dimension_semantics=("parallel", "arbitrary"), vmem_limit_bytes=VMEM_LIMIT),
    )(p_r, *vecs, wd, wa, wg)


def _attn_kernel(q_ref, kp_ref, kc_ref, vp_ref, vc_ref, bias_ref, beta_ref, o_ref, kwin, vwin,
                 *, width, qb):
    i = pl.program_id(1)
    n_pairs = width // PAIR
    kwin[0:qb, :] = kp_ref[...]
    kwin[qb:2 * qb, :] = kc_ref[...]
    vwin[0:qb, :] = vp_ref[...]
    vwin[qb:2 * qb, :] = vc_ref[...]
    lane = lax.broadcasted_iota(jnp.int32, (1, PAIR), 1)
    first_head = lane < HEAD_DIM
    rows = ATTN_BLOCK_CHUNKS * CHUNK
    win = BAND + rows - CHUNK
    col = lax.broadcasted_iota(jnp.int32, (1, win), 1)
    scale = HEAD_DIM ** -0.5
    left = LEFT_CHUNKS * CHUNK

    for q in range(n_pairs):
        lanes = slice(q * PAIR, (q + 1) * PAIR)
        for blk in range(qb // rows):
            r0 = blk * rows
            qc = q_ref[r0:r0 + rows, lanes]
            zero = jnp.zeros_like(qc)
            qm = jnp.concatenate([jnp.where(first_head, qc, zero), jnp.where(first_head, zero, qc)],
                                 axis=0)
            s = _dot_nt(qm, kwin[r0:r0 + win, lanes]) * scale + bias_ref[q]
            valid = jnp.logical_or(i > 0, col + r0 >= left)
            s = jnp.where(valid, s, NEG_INF)
            e = jnp.exp(s - jnp.max(s, axis=-1, keepdims=True))
            denom = jnp.sum(e, axis=-1, keepdims=True)
            o = _dot(e.astype(BF16), vwin[r0:r0 + win, lanes]) / denom
            oc = jnp.where(first_head, o[:rows], o[rows:])
            o_ref[r0:r0 + rows, lanes] = oc * beta_ref[:, lanes]


def _attn_call(p_a, bias, beta, batch, seq, width, qb):
    assert qb == LEFT_CHUNKS * CHUNK, "key window = previous block + current block"
    tiles = seq // qb
    n_pairs = width // PAIR

    def cur(col):
        return pl.BlockSpec((qb, width), lambda b, t: (b * tiles + t, col))

    def prev(col):
        return pl.BlockSpec((qb, width), lambda b, t: (b * tiles + jnp.maximum(t - 1, 0), col))

    return pl.pallas_call(
        functools.partial(_attn_kernel, width=width, qb=qb),
        grid=(batch, tiles),
        in_specs=[cur(0), prev(1), cur(1), prev(2), cur(2),
                  pl.BlockSpec(bias.shape, lambda b, t: (0, 0, 0)),
                  pl.BlockSpec((1, width), lambda b, t: (0, 0))],
        out_specs=pl.BlockSpec((qb, width), lambda b, t: (b * tiles + t, 0)),
        out_shape=jax.ShapeDtypeStruct((batch * seq, width), F32),
        scratch_shapes=[pltpu.VMEM((2 * qb, width), BF16), pltpu.VMEM((2 * qb, width), BF16)],
        compiler_params=pltpu.CompilerParams(
            dimension_semantics=("parallel", "arbitrary"), vmem_limit_bytes=VMEM_LIMIT),
    )(p_a, p_a, p_a, p_a, p_a, bias, beta)


def _out_router_kernel(yr_ref, ya_ref, wo_ref, x_ref, mod_ref, g2_ref, wr_ref, br_ref,
                       x1_ref, h2_ref, route_ref, *, r_width):
    y = (_dot(yr_ref[...].astype(BF16), wo_ref[0:r_width, :])
         + _dot(ya_ref[...].astype(BF16), wo_ref[r_width:, :]))
    m = mod_ref[0]
    x1 = x_ref[...] + m[GATE1:GATE1 + 1] * y
    x1_ref[...] = x1
    h2 = _rms(x1) * g2_ref[...] * (1.0 + m[SCALE2:SCALE2 + 1]) + m[SHIFT2:SHIFT2 + 1]
    h2_ref[...] = _pack_rows(h2)

    logits = _dot_x3(h2, wr_ref[...]) + br_ref[...]
    li = lax.broadcasted_iota(jnp.int32, logits.shape, 1).astype(F32)
    none = jnp.float32(-jnp.inf)
    far = jnp.float32(LANES)
    is_group = li < N_GROUPS
    gl = jnp.where(is_group, logits, none)
    gmax = jnp.max(gl, axis=-1, keepdims=True)
    gidx = jnp.min(jnp.where(gl == gmax, li, far), axis=-1, keepdims=True)
    g_w = 1.0 / jnp.sum(jnp.where(is_group, jnp.exp(logits - gmax), 0.0), axis=-1, keepdims=True)
    lo = N_GROUPS + EXPERTS_PER_GROUP * gidx
    el = jnp.where(jnp.logical_and(li >= lo, li < lo + EXPERTS_PER_GROUP), logits, none)
    m1 = jnp.max(el, axis=-1, keepdims=True)
    i1 = jnp.min(jnp.where(el == m1, li, far), axis=-1, keepdims=True)
    el2 = jnp.where(li == i1, none, el)
    m2 = jnp.max(el2, axis=-1, keepdims=True)
    i2 = jnp.min(jnp.where(el2 == m2, li, far), axis=-1, keepdims=True)
    e2 = jnp.exp(m2 - m1)
    w1 = g_w / (1.0 + e2)
    w2 = g_w * e2 / (1.0 + e2)
    route = jnp.where(li == 0.0, i1 - N_GROUPS, 0.0) + jnp.where(li == 1.0, i2 - N_GROUPS, 0.0)
    route_ref[...] = route + jnp.where(li == 2.0, w1, 0.0) + jnp.where(li == 3.0, w2, 0.0)


def _out_router_call(y_r, y_a, wo, x2d, mod, g2, wr, br, seq, tm):
    rows, d = x2d.shape
    r_width = y_r.shape[1]
    a_width = y_a.shape[1]
    tiles_per_seq = seq // tm
    return pl.pallas_call(
        functools.partial(_out_router_kernel, r_width=r_width),
        grid=(rows // tm,),
        in_specs=[pl.BlockSpec((tm, r_width), lambda i: (i, 0)),
                  pl.BlockSpec((tm, a_width), lambda i: (i, 0)),
                  pl.BlockSpec(wo.shape, lambda i: (0, 0)),
                  pl.BlockSpec((tm, d), lambda i: (i, 0)),
                  pl.BlockSpec((1, 6, d), lambda i: (i // tiles_per_seq, 0, 0)),
                  pl.BlockSpec((1, d), lambda i: (0, 0)),
                  pl.BlockSpec((d, LANES), lambda i: (0, 0)),
                  pl.BlockSpec((1, LANES), lambda i: (0, 0))],
        out_specs=[pl.BlockSpec((tm, d), lambda i: (i, 0)),
                   pl.BlockSpec((tm, d // 2), lambda i: (i, 0)),
                   pl.BlockSpec((tm, LANES), lambda i: (i, 0))],
        out_shape=[jax.ShapeDtypeStruct((rows, d), F32),
                   jax.ShapeDtypeStruct((rows, d // 2), jnp.uint32),
                   jax.ShapeDtypeStruct((rows, LANES), F32)],
        compiler_params=pltpu.CompilerParams(
            dimension_semantics=("parallel",), vmem_limit_bytes=VMEM_LIMIT),
    )(y_r, y_a, wo, x2d, mod, g2, wr, br)


def _moe_kernel(texp_ref, first_ref, nexp_ref, meta_ref, src_ref,
                h2_hbm, wg_hbm, wu_hbm, wd_hbm, o_ref,
                xbuf, xsem, wg_land, wu_land, wd_land, wsem, wg_bf, wu_bf, wd_bf, *, tm):
    t = pl.program_id(0)
    n_used = meta_ref[0]
    slot = t % 2

    def weight_copies(e):
        return (pltpu.make_async_copy(wg_hbm.at[e], wg_land, wsem.at[0]),
                pltpu.make_async_copy(wu_hbm.at[e], wu_land, wsem.at[1]),
                pltpu.make_async_copy(wd_hbm.at[e], wd_land, wsem.at[2]))

    def row_copy(tile, i, buf_slot):
        tok = src_ref[tile * tm + i]
        return pltpu.make_async_copy(h2_hbm.at[pl.ds(tok, 1)], xbuf.at[buf_slot, pl.ds(i, 1)],
                                     xsem.at[buf_slot])

    @pl.when(t == 0)
    def _():
        for cp in weight_copies(texp_ref[0]):
            cp.start()

        def body(i, carry):
            row_copy(0, i, 0).start()
            return carry

        lax.fori_loop(0, tm, body, 0, unroll=8)

    @pl.when(jnp.logical_and(t < n_used, first_ref[t] == 1))
    def _():
        for cp in weight_copies(texp_ref[t]):
            cp.wait()
        wg_bf[...] = wg_land[...].astype(BF16)
        wu_bf[...] = wu_land[...].astype(BF16)
        wd_bf[...] = wd_land[...].astype(BF16)

        @pl.when(nexp_ref[t] >= 0)
        def _():
            for cp in weight_copies(nexp_ref[t]):
                cp.start()

    def compute():
        pltpu.make_async_copy(h2_hbm.at[pl.ds(0, tm)], xbuf.at[slot], xsem.at[slot]).wait()
        lo, hi = _unpack_rows(xbuf[slot])
        x = jnp.concatenate([lo.astype(BF16), hi.astype(BF16)], axis=1)
        gate = _dot(x, wg_bf[...])
        up = _dot(x, wu_bf[...])
        h = gate * _sigmoid(gate) * up
        o_ref[...] = _pack_rows(_dot(h.astype(BF16), wd_bf[...]))

    @pl.when(t + 1 < n_used)
    def _():
        for i in range(tm):
            row_copy(t + 1, i, 1 - slot).start()
        compute()

    @pl.when(t + 1 == n_used)
    def _():
        compute()

    @pl.when(t >= n_used)
    def _():
        o_ref[...] = jnp.zeros_like(o_ref)


def _moe_call(texp, first, nexp, meta, src, h2, wg, wu, wd, tm):
    n_e, d, f = wg.shape
    packed = h2.shape[1]
    n_tiles = texp.shape[0]
    grid_spec = pltpu.PrefetchScalarGridSpec(
        num_scalar_prefetch=5,
        grid=(n_tiles,),
        in_specs=[pl.BlockSpec(memory_space=pl.ANY)] * 4,
        out_specs=pl.BlockSpec((tm, packed), lambda t, *_: (t, 0)),
        scratch_shapes=[pltpu.VMEM((2, tm, packed), jnp.uint32), pltpu.SemaphoreType.DMA((2,)),
                        pltpu.VMEM((d, f), F32), pltpu.VMEM((d, f), F32), pltpu.VMEM((f, d), F32),
                        pltpu.SemaphoreType.DMA((3,)),
                        pltpu.VMEM((d, f), BF16), pltpu.VMEM((d, f), BF16), pltpu.VMEM((f, d), BF16)])
    return pl.pallas_call(
        functools.partial(_moe_kernel, tm=tm),
        grid_spec=grid_spec,
        out_shape=jax.ShapeDtypeStruct((n_tiles * tm, packed), jnp.uint32),
        compiler_params=pltpu.CompilerParams(
            dimension_semantics=("arbitrary",), vmem_limit_bytes=VMEM_LIMIT),
    )(texp, first, nexp, meta, src, h2, wg, wu, wd)


def _final_kernel(pos_ref, ys_hbm, route_ref, x1_ref, mod_ref, gf_ref, o_ref, ybuf, sem, *, tm):
    i = pl.program_id(0)
    n = pl.num_programs(0)
    slot = i % 2

    def gather_rows(tile, buf_slot):
        base = tile * tm * 2

        def body(r, carry):
            for s in range(2):
                pltpu.make_async_copy(ys_hbm.at[pl.ds(pos_ref[base + 2 * r + s], 1)],
                                      ybuf.at[buf_slot, s, pl.ds(r, 1)], sem.at[buf_slot]).start()
            return carry

        lax.fori_loop(0, tm, body, 0, unroll=4)

    @pl.when(i == 0)
    def _():
        gather_rows(0, 0)

    @pl.when(i + 1 < n)
    def _():
        gather_rows(i + 1, 1 - slot)

    for s in range(2):
        pltpu.make_async_copy(ys_hbm.at[pl.ds(0, tm)], ybuf.at[slot, s], sem.at[slot]).wait()
    route = route_ref[...]
    w0 = route[:, 2:3]
    w1 = route[:, 3:4]
    lo0, hi0 = _unpack_rows(ybuf[slot, 0])
    lo1, hi1 = _unpack_rows(ybuf[slot, 1])
    f = jnp.concatenate([w0 * lo0 + w1 * lo1, w0 * hi0 + w1 * hi1], axis=1)
    x2 = x1_ref[...] + mod_ref[0][GATE2:GATE2 + 1] * f
    o_ref[...] = _rms(x2) * gf_ref[...]


def _final_call(pos, ys, route, x1, mod, gf, seq, tm):
    rows, d = x1.shape
    tiles_per_seq = seq // tm
    grid_spec = pltpu.PrefetchScalarGridSpec(
        num_scalar_prefetch=1,
        grid=(rows // tm,),
        in_specs=[pl.BlockSpec(memory_space=pl.ANY),
                  pl.BlockSpec((tm, LANES), lambda i, pos: (i, 0)),
                  pl.BlockSpec((tm, d), lambda i, pos: (i, 0)),
                  pl.BlockSpec((1, 6, d), lambda i, pos: (i // tiles_per_seq, 0, 0)),
                  pl.BlockSpec((1, d), lambda i, pos: (0, 0))],
        out_specs=pl.BlockSpec((tm, d), lambda i, pos: (i, 0)),
        scratch_shapes=[pltpu.VMEM((2, 2, tm) + ys.shape[1:], jnp.uint32),
                        pltpu.SemaphoreType.DMA((2,))])
    return pl.pallas_call(
        functools.partial(_final_kernel, tm=tm),
        grid_spec=grid_spec,
        out_shape=jax.ShapeDtypeStruct((rows, d), F32),
        compiler_params=pltpu.CompilerParams(
            dimension_semantics=("arbitrary",), vmem_limit_bytes=VMEM_LIMIT),
    )(pos, ys, route, x1, mod, gf)


def _route_plan(route, tm):
    n = route.shape[0]
    flat_e = route[:, 0:2].astype(jnp.int32).reshape(-1)
    n_tiles = (2 * n + N_EXPERTS * (tm - 1)) // tm
    experts = jnp.arange(N_EXPERTS, dtype=jnp.int32)
    onehot = (flat_e[:, None] == experts[None, :]).astype(jnp.int32)
    csum = jnp.cumsum(onehot, axis=0)
    rank = jnp.sum(csum * onehot, axis=1) - 1
    counts = csum[-1]
    tiles_per_e = (counts + tm - 1) // tm
    tile_end = jnp.cumsum(tiles_per_e)
    tile_start = tile_end - tiles_per_e
    n_used = tile_end[-1]
    pos = jnp.sum(onehot * tile_start[None, :], axis=1) * tm + rank
    token = jnp.arange(2 * n, dtype=jnp.int32) // 2
    src = jnp.zeros((n_tiles * tm,), jnp.int32).at[pos].set(token)
    tiles = jnp.arange(n_tiles, dtype=jnp.int32)
    texp = jnp.sum((tiles[:, None] >= tile_end[None, :]).astype(jnp.int32), axis=1)
    texp = jnp.where(tiles < n_used, texp, texp[n_used - 1])
    first = jnp.concatenate([jnp.ones((1,), jnp.int32), (texp[1:] != texp[:-1]).astype(jnp.int32)])
    later_used = jnp.logical_and(experts[None, :] > experts[:, None], (tiles_per_e > 0)[None, :])
    next_used = jnp.min(jnp.where(later_used, experts[None, :], N_EXPERTS), axis=1)
    next_used = jnp.where(next_used == N_EXPERTS, -1, next_used)
    return texp, first, next_used[texp], n_used.reshape(1), src, pos


def _round_up(n, m):
    return (n + m - 1) // m * m


def kernel(x, c, w_ada, b_ada, norm1_g, w_in, mu_shift, w0, w_decay_up, a0, w_aaa_up, w_gate_up,
           k_k, k_a, r_k, ln_x_w, ln_x_b, rel_bias, beta_rwkv, beta_attn, w_out, norm2_g, w_group,
           b_group, w_expert, b_expert, w_gate, w_up, w_down, norm_f_g):
    assert w_ada.shape[0] == 1, "single trunk layer"
    batch, seq, d = x.shape
    r_width = w0.shape[1]
    a_width = beta_attn.shape[1]
    shift_width = mu_shift.shape[1]
    assert shift_width == 3 * r_width + DECAY_LORA + AAA_LORA + GATE_LORA
    x2d = x.reshape(batch * seq, d)
    row = lambda t: t.reshape(1, -1)

    c8 = jnp.pad(c, ((0, 8 - batch), (0, 0)))
    mod = _ada_call(c8, w_ada[0], row(b_ada[0]), tn=1536)[:batch].reshape(batch, 6, d)

    tm = min(512, seq)
    pw = _round_up(shift_width, 3 * LANES)
    h1 = _hmod_call(x2d, row(norm1_g[0]), mod, tm, seq)
    tm_p = min(1024, seq)
    w_in_bf = w_in[0].astype(BF16)
    p_r = _matmul_call(h1, w_in_bf, pw, pw // 3, tm_p, F32)
    p_a = _matmul_call(h1, w_in_bf[:, shift_width:], 3 * a_width, a_width, tm_p, BF16)

    mu = jnp.pad(mu_shift[0], (0, pw - shift_width))
    vecs = [row(mu), row(w0[0]), row(a0[0]), row(k_k[0]), row(k_a[0]), row(r_k[0]), row(ln_x_w[0]),
            row(ln_x_b[0]), row(beta_rwkv[0])]
    y_r = _rwkv_call(p_r, vecs, w_decay_up[0], w_aaa_up[0], w_gate_up[0], batch, seq, r_width,
                     tb=min(256, seq))

    left = LEFT_CHUNKS * CHUNK
    dist = left + CHUNK - 1 - jnp.arange(BAND + CHUNK - 1)
    base = rel_bias[0][:, jnp.clip(dist, -REL_CLIP, REL_CLIP) + REL_CLIP].astype(F32)
    bias = jnp.stack([base[:, CHUNK - 1 - qi:CHUNK - 1 - qi + BAND] for qi in range(CHUNK)], axis=1)
    nb = ATTN_BLOCK_CHUNKS
    bias = jnp.concatenate(
        [jnp.pad(bias, ((0, 0), (0, 0), (cb * CHUNK, (nb - 1 - cb) * CHUNK)), constant_values=NEG_INF)
         for cb in range(nb)], axis=1)
    bias = bias.reshape(a_width // PAIR, 2 * nb * CHUNK, BAND + (nb - 1) * CHUNK)
    y_a = _attn_call(p_a, bias, row(beta_attn[0]), batch, seq, a_width, qb=min(512, seq))

    n_route = N_GROUPS + N_EXPERTS
    wr = jnp.concatenate([w_group[0], w_expert[0].transpose(1, 0, 2).reshape(d, N_EXPERTS)], axis=1)
    wr = jnp.pad(wr, ((0, 0), (0, LANES - n_route)))
    br = jnp.pad(jnp.concatenate([b_group[0], b_expert[0].reshape(-1)]), (0, LANES - n_route))
    x1, h2, route = _out_router_call(y_r, y_a, w_out[0].astype(BF16), x2d, mod, row(norm2_g[0]),
                                     wr, row(br), seq, tm)

    f = w_gate.shape[-1]
    tm_e = 256
    texp, first, nexp, meta, src, pos = _route_plan(route, tm_e)
    ys = _moe_call(texp, first, nexp, meta, src, h2, w_gate[0].reshape(N_EXPERTS, d, f),
                   w_up[0].reshape(N_EXPERTS, d, f), w_down[0].reshape(N_EXPERTS, f, d), tm_e)
    out = _final_call(pos, ys, route, x1, mod, row(norm_f_g), seq, min(256, seq))
    return out.reshape(batch, seq, d)
```

```python
import functools

import jax
import jax.numpy as jnp
from jax import lax
from jax.experimental import pallas as pl
from jax.experimental.pallas import tpu as pltpu

F32 = jnp.float32
BF16 = jnp.bfloat16
HI = lax.Precision.HIGHEST

LANES = 128
HEAD_DIM = 64
PAIR = 2 * HEAD_DIM
CHUNK = 64
LEFT_CHUNKS = 8
BAND = (LEFT_CHUNKS + 1) * CHUNK
ATTN_BLOCK_CHUNKS = 4
PREP_PAIRS = 4
REL_CLIP = 128
DECAY_LORA = 64
AAA_LORA = 64
GATE_LORA = 160
N_GROUPS = 4
EXPERTS_PER_GROUP = 8
N_EXPERTS = N_GROUPS * EXPERTS_PER_GROUP
RMS_EPS = 1e-6
GN_EPS = 64e-5
L2_EPS = 1e-12
NEG_INF = -1e30
VMEM_LIMIT = 56 * 1024 * 1024

SHIFT1, SCALE1, GATE1, SHIFT2, SCALE2, GATE2 = range(6)


def _dot(a, b, precision=None):
    return jnp.dot(a, b, precision=precision, preferred_element_type=F32)


def _dot_nt(a, b, precision=None):
    return lax.dot_general(a, b, (((1,), (1,)), ((), ())), precision=precision,
                           preferred_element_type=F32)


def _sigmoid(x):
    return 1.0 / (1.0 + jnp.exp(-x))


def _rms(x):
    return x * lax.rsqrt(jnp.mean(x * x, axis=-1, keepdims=True) + RMS_EPS)


def _ada_kernel(c_ref, w_ref, b_ref, o_ref):
    c = c_ref[...]
    o_ref[...] = _dot_x3(c * _sigmoid(c), w_ref[...]) + b_ref[...]


def _ada_call(c8, w, b, tn):
    rows, d = c8.shape
    n = w.shape[1]
    return pl.pallas_call(
        _ada_kernel,
        grid=(n // tn,),
        in_specs=[pl.BlockSpec((rows, d), lambda j: (0, 0)),
                  pl.BlockSpec((d, tn), lambda j: (0, j)),
                  pl.BlockSpec((1, tn), lambda j: (0, j))],
        out_specs=pl.BlockSpec((rows, tn), lambda j: (0, j)),
        out_shape=jax.ShapeDtypeStruct((rows, n), F32),
        compiler_params=pltpu.CompilerParams(vmem_limit_bytes=VMEM_LIMIT),
    )(c8, w, b)


def _hmod_kernel(x_ref, g_ref, mod_ref, o_ref):
    m = mod_ref[0]
    h = _rms(x_ref[...]) * g_ref[...] * (1.0 + m[SCALE1:SCALE1 + 1]) + m[SHIFT1:SHIFT1 + 1]
    o_ref[...] = h.astype(BF16)


def _hmod_call(x2d, g, mod, tm, seq):
    rows, d = x2d.shape
    tiles_per_seq = seq // tm
    return pl.pallas_call(
        _hmod_kernel,
        grid=(rows // tm,),
        in_specs=[pl.BlockSpec((tm, d), lambda i: (i, 0)),
                  pl.BlockSpec((1, d), lambda i: (0, 0)),
                  pl.BlockSpec((1, 6, d), lambda i: (i // tiles_per_seq, 0, 0))],
        out_specs=pl.BlockSpec((tm, d), lambda i: (i, 0)),
        out_shape=jax.ShapeDtypeStruct((rows, d), BF16),
        compiler_params=pltpu.CompilerParams(
            dimension_semantics=("parallel",), vmem_limit_bytes=VMEM_LIMIT),
    )(x2d, g, mod)


def _matmul_kernel(a_ref, w_ref, o_ref):
    o_ref[...] = _dot(a_ref[...], w_ref[...]).astype(o_ref.dtype)


def _matmul_call(a, w, n_out, tn, tm, out_dtype):
    rows, d = a.shape
    return pl.pallas_call(
        _matmul_kernel,
        grid=(n_out // tn, rows // tm),
        in_specs=[pl.BlockSpec((tm, d), lambda j, i: (i, 0)),
                  pl.BlockSpec((d, tn), lambda j, i: (0, j))],
        out_specs=pl.BlockSpec((tm, tn), lambda j, i: (i, j)),
        out_shape=jax.ShapeDtypeStruct((rows, n_out), out_dtype),
        compiler_params=pltpu.CompilerParams(
            dimension_semantics=("parallel", "parallel"), vmem_limit_bytes=VMEM_LIMIT),
    )(a, w)


def _split3(t):
    hi = t.astype(BF16)
    rest = t - hi.astype(F32)
    mid = rest.astype(BF16)
    lo = (rest - mid.astype(F32)).astype(BF16)
    return hi, mid, lo


def _dot_hilo_lhs(t, m):
    hi = t.astype(BF16)
    lo = (t - hi.astype(F32)).astype(BF16)
    return _dot(lo, m) + _dot(hi, m)


def _dot_f32_rhs(m, t):
    hi, mid, lo = _split3(t)
    return _dot(m, lo) + _dot(m, mid) + _dot(m, hi)


def _dot_x3(a, b):
    ah = a.astype(BF16)
    al = (a - ah.astype(F32)).astype(BF16)
    bh = b.astype(BF16)
    bl = (b - bh.astype(F32)).astype(BF16)
    return _dot(al, bh) + _dot(ah, bl) + _dot(ah, bh)


def _rwkv_kernel(p_ref, mu_ref, w0_ref, a0_ref, kk_ref, ka_ref, rk_ref, lnw_ref, lnb_ref, beta_ref,
                 wd_ref, wa_ref, wg_ref, o_ref,
                 carry_scr, state_scr, ar_scr, bk_scr, bkh_scr, v_scr, et_scr, y_scr,
                 pm_scr, arb_scr, av_scr, vk_scr, *, width, tb):
    n_pairs = width // PAIR
    n_chunks = tb // CHUNK
    stacked = 2 * CHUNK

    @pl.when(pl.program_id(1) == 0)
    def _():
        carry_scr[...] = jnp.zeros_like(carry_scr)
        state_scr[...] = jnp.zeros_like(state_scr)

    p = p_ref[...]
    row = lax.broadcasted_iota(jnp.int32, (tb, 1), 0)
    prev = jnp.where(row == 0, carry_scr[...], pltpu.roll(p, 1, axis=0))
    carry_scr[...] = p[tb - 1:tb, :]
    ps = p + (prev - p) * mu_ref[...]

    c = width
    r = ps[:, :c]
    k = ps[:, c:2 * c]
    v = ps[:, 2 * c:3 * c]
    o = 3 * c
    xw = ps[:, o:o + DECAY_LORA]
    xa = ps[:, o + DECAY_LORA:o + DECAY_LORA + AAA_LORA]
    xg = ps[:, o + DECAY_LORA + AAA_LORA:o + DECAY_LORA + AAA_LORA + GATE_LORA]

    z = w0_ref[...] + _dot_x3(jnp.tanh(xw), wd_ref[...])
    softplus_neg_z = jnp.maximum(-z, 0.0) + jnp.log(1.0 + jnp.exp(-jnp.abs(z)))
    lw = -jnp.exp(-softplus_neg_z - 0.5)
    a = _sigmoid(a0_ref[...] + _dot(xa.astype(BF16), wa_ref[...].astype(BF16)))
    g = _dot(_sigmoid(xg).astype(BF16), wg_ref[...].astype(BF16))

    li = lax.broadcasted_iota(jnp.int32, (PAIR, PAIR), 0)
    lj = lax.broadcasted_iota(jnp.int32, (PAIR, PAIR), 1)
    head_ones = ((li // HEAD_DIM) == (lj // HEAD_DIM)).astype(BF16)

    def head_sum(t):
        return jnp.concatenate(
            [_dot_hilo_lhs(t[:, q * PAIR:(q + 1) * PAIR], head_ones) for q in range(n_pairs)], axis=1)

    kk = k * kk_ref[...]
    kk = kk / jnp.maximum(jnp.sqrt(head_sum(kk * kk)), L2_EPS)
    k2 = k * (1.0 + (a - 1.0) * ka_ref[...])

    ti = lax.broadcasted_iota(jnp.int32, (tb, tb), 0)
    tj = lax.broadcasted_iota(jnp.int32, (tb, tb), 1)
    tri = jnp.logical_and((ti // CHUNK) == (tj // CHUNK), tj <= ti).astype(BF16)
    cum = _dot_f32_rhs(tri, lw)
    tot = jnp.concatenate(
        [jnp.broadcast_to(cum[(ci + 1) * CHUNK - 1:(ci + 1) * CHUNK, :], (CHUNK, c))
         for ci in range(n_chunks)], axis=0)

    e_neg = jnp.exp(-cum)
    e_rem = jnp.exp(tot - cum)
    kka = kk * a
    lane = lax.broadcasted_iota(jnp.int32, (1, PAIR), 1)
    first_head = lane < HEAD_DIM

    def put(scr, base, val):
        for q in range(n_pairs):
            vq = val[:, q * PAIR:(q + 1) * PAIR]
            h0 = jnp.where(first_head, vq, 0.0).astype(scr.dtype)
            h1 = jnp.where(first_head, 0.0, vq).astype(scr.dtype)
            for ci in range(n_chunks):
                rows = slice(ci * CHUNK, (ci + 1) * CHUNK)
                scr[q, ci, base:base + CHUNK, :] = h0[rows]
                scr[q, ci, base + CHUNK:base + stacked, :] = h1[rows]

    put(ar_scr, 0, -kk * jnp.exp(cum - lw))
    put(ar_scr, stacked, r * jnp.exp(cum))
    put(bk_scr, 0, kka * e_neg)
    put(bk_scr, stacked, k2 * e_neg)
    put(bkh_scr, 0, kka * e_rem)
    put(bkh_scr, stacked, k2 * e_rem)
    put(v_scr, 0, v)
    e_tot = jnp.exp(tot)
    for q in range(n_pairs):
        et_scr[q] = e_tot[:, q * PAIR:(q + 1) * PAIR]

    si = lax.broadcasted_iota(jnp.int32, (2 * stacked, 2 * stacked), 0)
    sj = lax.broadcasted_iota(jnp.int32, (2 * stacked, 2 * stacked), 1)
    keep = jnp.where(si < stacked, si, si - stacked + 1) > (sj % stacked)
    eye = (li == lj).astype(F32)

    chunks = range(n_chunks)
    pairs = range(n_pairs)

    def prepare(step, carry):
        chains = [(step * PREP_PAIRS + dq, ci) for dq in range(PREP_PAIRS) for ci in chunks]
        n = []
        for q, ci in chains:
            sc = jnp.where(keep, _dot_nt(ar_scr[q, ci], bk_scr[q, ci]), 0.0)
            scb = sc.astype(BF16)
            vm = v_scr[q, ci]
            arb_scr[q, ci] = scb[stacked:, :stacked]
            av_scr[q, ci] = _dot(scb[:, stacked:], vm.astype(BF16))
            vk_scr[q, ci] = _dot(vm.T.astype(BF16), bkh_scr[q, ci, stacked:, :])
            n.append(sc[:stacked, :stacked])
        acc = [eye + t for t in n]
        pw = [t.astype(BF16) for t in n]
        pw = [_dot(t, t).astype(BF16) for t in pw]
        for it in range(1, 6):
            if it < 5:
                both = [_dot(jnp.concatenate([p_, a_.astype(BF16)], axis=0), p_)
                        for p_, a_ in zip(pw, acc)]
                pw = [t[:stacked].astype(BF16) for t in both]
                acc = [a_ + t[stacked:] for a_, t in zip(acc, both)]
            else:
                acc = [a_ + _dot(a_.astype(BF16), p_) for p_, a_ in zip(pw, acc)]
        for (q, ci), a_ in zip(chains, acc):
            pm_scr[q, ci] = a_.astype(BF16)
        return carry

    for step in range(n_pairs // PREP_PAIRS):
        prepare(step, 0)

    s = [state_scr[q] for q in pairs]
    for ci in chunks:
        sr = [_dot_nt(ar_scr[q, ci], s[q].astype(BF16)) for q in pairs]
        u = [_dot(pm_scr[q, ci], (sr[q][:stacked] + av_scr[q, ci, :stacked, :]).astype(BF16))
             for q in pairs]
        for q in pairs:
            ym = (sr[q][stacked:] + av_scr[q, ci, stacked:, :]
                  + _dot(arb_scr[q, ci], u[q].astype(BF16)))
            y_scr[q, ci * CHUNK:(ci + 1) * CHUNK, :] = ym[:CHUNK] + ym[CHUNK:]
        s = [s[q] * et_scr[q, ci * CHUNK:ci * CHUNK + 1, :]
             + _dot(u[q].T.astype(BF16), bkh_scr[q, ci, :stacked, :]) + vk_scr[q, ci] for q in pairs]
    for q in pairs:
        state_scr[q] = s[q]

    y = jnp.concatenate([y_scr[q] for q in range(n_pairs)], axis=1)
    mean = head_sum(y) * (1.0 / HEAD_DIM)
    d = y - mean
    var = head_sum(d * d) * (1.0 / HEAD_DIM)
    yn = d * lax.rsqrt(var + GN_EPS) * lnw_ref[...] + lnb_ref[...]
    bonus = head_sum(r * k2 * rk_ref[...]) * v
    o_ref[...] = (yn + bonus) * g * beta_ref[...]


def _rwkv_call(p_r, vecs, wd, wa, wg, batch, seq, width, tb):
    pw = p_r.shape[1]
    n_pairs = width // PAIR
    n_chunks = tb // CHUNK
    tiles = seq // tb
    vec_specs = [pl.BlockSpec((1, v.shape[1]), lambda b, t: (0, 0)) for v in vecs]
    mat_specs = [pl.BlockSpec(m.shape, lambda b, t: (0, 0)) for m in (wd, wa, wg)]
    stacked2 = pltpu.VMEM((n_pairs, n_chunks, 4 * CHUNK, PAIR), BF16)
    return pl.pallas_call(
        functools.partial(_rwkv_kernel, width=width, tb=tb),
        grid=(batch, tiles),
        in_specs=[pl.BlockSpec((tb, pw), lambda b, t: (b * tiles + t, 0))] + vec_specs + mat_specs,
        out_specs=pl.BlockSpec((tb, width), lambda b, t: (b * tiles + t, 0)),
        out_shape=jax.ShapeDtypeStruct((batch * seq, width), F32),
        scratch_shapes=[pltpu.VMEM((1, pw), F32),
                        pltpu.VMEM((n_pairs, PAIR, PAIR), F32),
                        stacked2, stacked2, stacked2,
                        pltpu.VMEM((n_pairs, n_chunks, 2 * CHUNK, PAIR), F32),
                        pltpu.VMEM((n_pairs, tb, PAIR), F32),
                        pltpu.VMEM((n_pairs, tb, PAIR), F32),
                        pltpu.VMEM((n_pairs, n_chunks, 2 * CHUNK, PAIR), BF16),
                        pltpu.VMEM((n_pairs, n_chunks, 2 * CHUNK, PAIR), BF16),
                        pltpu.VMEM((n_pairs, n_chunks, 4 * CHUNK, PAIR), F32),
                        pltpu.VMEM((n_pairs, n_chunks, 2 * CHUNK, PAIR), F32)],
        compiler_params=pltpu.CompilerParams(
            dimension_semantics=("parallel", "arbitrary"), vmem_limit_bytes=VMEM_LIMIT),
    )(p_r, *vecs, wd, wa, wg)


def _attn_kernel(q_ref, kp_ref, kc_ref, vp_ref, vc_ref, bias_ref, beta_ref, o_ref, kwin, vwin,
                 *, width, qb):
    i = pl.program_id(1)
    n_pairs = width // PAIR
    kwin[0:qb, :] = kp_ref[...]
    kwin[qb:2 * qb, :] = kc_ref[...]
    vwin[0:qb, :] = vp_ref[...]
    vwin[qb:2 * qb, :] = vc_ref[...]
    lane = lax.broadcasted_iota(jnp.int32, (1, PAIR), 1)
    first_head = lane < HEAD_DIM
    rows = ATTN_BLOCK_CHUNKS * CHUNK
    win = BAND + rows - CHUNK
    col = lax.broadcasted_iota(jnp.int32, (1, win), 1)
    scale = HEAD_DIM ** -0.5
    left = LEFT_CHUNKS * CHUNK

    for q in range(n_pairs):
        lanes = slice(q * PAIR, (q + 1) * PAIR)
        for blk in range(qb // rows):
            r0 = blk * rows
            qc = q_ref[r0:r0 + rows, lanes]
            zero = jnp.zeros_like(qc)
            qm = jnp.concatenate([jnp.where(first_head, qc, zero), jnp.where(first_head, zero, qc)],
                                 axis=0)
            s = _dot_nt(qm, kwin[r0:r0 + win, lanes]) * scale + bias_ref[q]
            valid = jnp.logical_or(i > 0, col + r0 >= left)
            s = jnp.where(valid, s, NEG_INF)
            e = jnp.exp(s - jnp.max(s, axis=-1, keepdims=True))
            denom = jnp.sum(e, axis=-1, keepdims=True)
            o = _dot(e.astype(BF16), vwin[r0:r0 + win, lanes]) / denom
            oc = jnp.where(first_head, o[:rows], o[rows:])
            o_ref[r0:r0 + rows, lanes] = oc * beta_ref[:, lanes]


def _attn_call(p_a, bias, beta, batch, seq, width, qb):
    assert qb == LEFT_CHUNKS * CHUNK, "key window = previous block + current block"
    tiles = seq // qb
    n_pairs = width // PAIR

    def cur(col):
        return pl.BlockSpec((qb, width), lambda b, t: (b * tiles + t, col))

    def prev(col):
        return pl.BlockSpec((qb, width), lambda b, t: (b * tiles + jnp.maximum(t - 1, 0), col))

    return pl.pallas_call(
        functools.partial(_attn_kernel, width=width, qb=qb),
        grid=(batch, tiles),
        in_specs=[cur(0), prev(1), cur(1), prev(2), cur(2),
                  pl.BlockSpec(bias.shape, lambda b, t: (0, 0, 0)),
                  pl.BlockSpec((1, width), lambda b, t: (0, 0))],
        out_specs=pl.BlockSpec((qb, width), lambda b, t: (b * tiles + t, 0)),
        out_shape=jax.ShapeDtypeStruct((batch * seq, width), F32),
        scratch_shapes=[pltpu.VMEM((2 * qb, width), BF16), pltpu.VMEM((2 * qb, width), BF16)],
        compiler_params=pltpu.CompilerParams(
            dimension_semantics=("parallel", "arbitrary"), vmem_limit_bytes=VMEM_LIMIT),
    )(p_a, p_a, p_a, p_a, p_a, bias, beta)


def _out_router_kernel(yr_ref, ya_ref, wo_ref, x_ref, mod_ref, g2_ref, wr_ref, br_ref,
                       x1_ref, h2_ref, route_ref, *, r_width):
    y = (_dot(yr_ref[...].astype(BF16), wo_ref[0:r_width, :])
         + _dot(ya_ref[...].astype(BF16), wo_ref[r_width:, :]))
    m = mod_ref[0]
    x1 = x_ref[...] + m[GATE1:GATE1 + 1] * y
    x1_ref[...] = x1
    h2 = _rms(x1) * g2_ref[...] * (1.0 + m[SCALE2:SCALE2 + 1]) + m[SHIFT2:SHIFT2 + 1]
    h2_ref[...] = h2

    logits = _dot_x3(h2, wr_ref[...]) + br_ref[...]
    li = lax.broadcasted_iota(jnp.int32, logits.shape, 1).astype(F32)
    none = jnp.float32(-jnp.inf)
    far = jnp.float32(LANES)
    is_group = li < N_GROUPS
    gl = jnp.where(is_group, logits, none)
    gmax = jnp.max(gl, axis=-1, keepdims=True)
    gidx = jnp.min(jnp.where(gl == gmax, li, far), axis=-1, keepdims=True)
    g_w = 1.0 / jnp.sum(jnp.where(is_group, jnp.exp(logits - gmax), 0.0), axis=-1, keepdims=True)
    lo = N_GROUPS + EXPERTS_PER_GROUP * gidx
    el = jnp.where(jnp.logical_and(li >= lo, li < lo + EXPERTS_PER_GROUP), logits, none)
    m1 = jnp.max(el, axis=-1, keepdims=True)
    i1 = jnp.min(jnp.where(el == m1, li, far), axis=-1, keepdims=True)
    el2 = jnp.where(li == i1, none, el)
    m2 = jnp.max(el2, axis=-1, keepdims=True)
    i2 = jnp.min(jnp.where(el2 == m2, li, far), axis=-1, keepdims=True)
    e2 = jnp.exp(m2 - m1)
    w1 = g_w / (1.0 + e2)
    w2 = g_w * e2 / (1.0 + e2)
    route = jnp.where(li == 0.0, i1 - N_GROUPS, 0.0) + jnp.where(li == 1.0, i2 - N_GROUPS, 0.0)
    route_ref[...] = route + jnp.where(li == 2.0, w1, 0.0) + jnp.where(li == 3.0, w2, 0.0)


def _out_router_call(y_r, y_a, wo, x2d, mod, g2, wr, br, seq, tm):
    rows, d = x2d.shape
    r_width = y_r.shape[1]
    a_width = y_a.shape[1]
    tiles_per_seq = seq // tm
    return pl.pallas_call(
        functools.partial(_out_router_kernel, r_width=r_width),
        grid=(rows // tm,),
        in_specs=[pl.BlockSpec((tm, r_width), lambda i: (i, 0)),
                  pl.BlockSpec((tm, a_width), lambda i: (i, 0)),
                  pl.BlockSpec(wo.shape, lambda i: (0, 0)),
                  pl.BlockSpec((tm, d), lambda i: (i, 0)),
                  pl.BlockSpec((1, 6, d), lambda i: (i // tiles_per_seq, 0, 0)),
                  pl.BlockSpec((1, d), lambda i: (0, 0)),
                  pl.BlockSpec((d, LANES), lambda i: (0, 0)),
                  pl.BlockSpec((1, LANES), lambda i: (0, 0))],
        out_specs=[pl.BlockSpec((tm, d), lambda i: (i, 0)),
                   pl.BlockSpec((tm, d), lambda i: (i, 0)),
                   pl.BlockSpec((tm, LANES), lambda i: (i, 0))],
        out_shape=[jax.ShapeDtypeStruct((rows, d), F32),
                   jax.ShapeDtypeStruct((rows, d), F32),
                   jax.ShapeDtypeStruct((rows, LANES), F32)],
        compiler_params=pltpu.CompilerParams(
            dimension_semantics=("parallel",), vmem_limit_bytes=VMEM_LIMIT),
    )(y_r, y_a, wo, x2d, mod, g2, wr, br)


def _moe_kernel(texp_ref, first_ref, nexp_ref, meta_ref, src_ref,
                h2_hbm, wg_hbm, wu_hbm, wd_hbm, o_ref,
                xbuf, xsem, wg_land, wu_land, wd_land, wsem, wg_bf, wu_bf, wd_bf, *, tm):
    t = pl.program_id(0)
    n_used = meta_ref[0]
    slot = t % 2

    def weight_copies(e):
        return (pltpu.make_async_copy(wg_hbm.at[e], wg_land, wsem.at[0]),
                pltpu.make_async_copy(wu_hbm.at[e], wu_land, wsem.at[1]),
                pltpu.make_async_copy(wd_hbm.at[e], wd_land, wsem.at[2]))

    def row_copy(tile, i, buf_slot):
        tok = src_ref[tile * tm + i]
        return pltpu.make_async_copy(h2_hbm.at[pl.ds(tok, 1)], xbuf.at[buf_slot, pl.ds(i, 1)],
                                     xsem.at[buf_slot])

    @pl.when(t == 0)
    def _():
        for cp in weight_copies(texp_ref[0]):
            cp.start()

        def body(i, carry):
            row_copy(0, i, 0).start()
            return carry

        lax.fori_loop(0, tm, body, 0, unroll=8)

    @pl.when(jnp.logical_and(t < n_used, first_ref[t] == 1))
    def _():
        for cp in weight_copies(texp_ref[t]):
            cp.wait()
        wg_bf[...] = wg_land[...].astype(BF16)
        wu_bf[...] = wu_land[...].astype(BF16)
        wd_bf[...] = wd_land[...].astype(BF16)

        @pl.when(nexp_ref[t] >= 0)
        def _():
            for cp in weight_copies(nexp_ref[t]):
                cp.start()

    def compute():
        pltpu.make_async_copy(h2_hbm.at[pl.ds(0, tm)], xbuf.at[slot], xsem.at[slot]).wait()
        x = xbuf[slot].astype(BF16)
        gate = _dot(x, wg_bf[...])
        up = _dot(x, wu_bf[...])
        h = gate * _sigmoid(gate) * up
        o_ref[...] = _dot(h.astype(BF16), wd_bf[...])

    @pl.when(t + 1 < n_used)
    def _():
        for i in range(tm):
            row_copy(t + 1, i, 1 - slot).start()
        compute()

    @pl.when(t + 1 == n_used)
    def _():
        compute()

    @pl.when(t >= n_used)
    def _():
        o_ref[...] = jnp.zeros_like(o_ref)


def _moe_call(texp, first, nexp, meta, src, h2, wg, wu, wd, tm):
    n_e, d, f = wg.shape
    n_tiles = texp.shape[0]
    grid_spec = pltpu.PrefetchScalarGridSpec(
        num_scalar_prefetch=5,
        grid=(n_tiles,),
        in_specs=[pl.BlockSpec(memory_space=pl.ANY)] * 4,
        out_specs=pl.BlockSpec((tm, d), lambda t, *_: (t, 0)),
        scratch_shapes=[pltpu.VMEM((2, tm, d), F32), pltpu.SemaphoreType.DMA((2,)),
                        pltpu.VMEM((d, f), F32), pltpu.VMEM((d, f), F32), pltpu.VMEM((f, d), F32),
                        pltpu.SemaphoreType.DMA((3,)),
                        pltpu.VMEM((d, f), BF16), pltpu.VMEM((d, f), BF16), pltpu.VMEM((f, d), BF16)])
    return pl.pallas_call(
        functools.partial(_moe_kernel, tm=tm),
        grid_spec=grid_spec,
        out_shape=jax.ShapeDtypeStruct((n_tiles * tm, d), F32),
        compiler_params=pltpu.CompilerParams(
            dimension_semantics=("arbitrary",), vmem_limit_bytes=VMEM_LIMIT),
    )(texp, first, nexp, meta, src, h2, wg, wu, wd)


def _final_kernel(pos_ref, ys_hbm, route_ref, x1_ref, mod_ref, gf_ref, o_ref, ybuf, sem, *, tm):
    i = pl.program_id(0)
    n = pl.num_programs(0)
    slot = i % 2

    def gather_rows(tile, buf_slot):
        base = tile * tm * 2

        def body(r, carry):
            for s in range(2):
                pltpu.make_async_copy(ys_hbm.at[pl.ds(pos_ref[base + 2 * r + s], 1)],
                                      ybuf.at[buf_slot, s, pl.ds(r, 1)], sem.at[buf_slot]).start()
            return carry

        lax.fori_loop(0, tm, body, 0, unroll=4)

    @pl.when(i == 0)
    def _():
        gather_rows(0, 0)

    @pl.when(i + 1 < n)
    def _():
        gather_rows(i + 1, 1 - slot)

    for s in range(2):
        pltpu.make_async_copy(ys_hbm.at[pl.ds(0, tm)], ybuf.at[slot, s], sem.at[slot]).wait()
    route = route_ref[...]
    w0 = route[:, 2:3]
    w1 = route[:, 3:4]
    f = w0 * ybuf[slot, 0] + w1 * ybuf[slot, 1]
    x2 = x1_ref[...] + mod_ref[0][GATE2:GATE2 + 1] * f
    o_ref[...] = _rms(x2) * gf_ref[...]


def _final_call(pos, ys, route, x1, mod, gf, seq, tm):
    rows, d = x1.shape
    tiles_per_seq = seq // tm
    grid_spec = pltpu.PrefetchScalarGridSpec(
        num_scalar_prefetch=1,
        grid=(rows // tm,),
        in_specs=[pl.BlockSpec(memory_space=pl.ANY),
                  pl.BlockSpec((tm, LANES), lambda i, pos: (i, 0)),
                  pl.BlockSpec((tm, d), lambda i, pos: (i, 0)),
                  pl.BlockSpec((1, 6, d), lambda i, pos: (i // tiles_per_seq, 0, 0)),
                  pl.BlockSpec((1, d), lambda i, pos: (0, 0))],
        out_specs=pl.BlockSpec((tm, d), lambda i, pos: (i, 0)),
        scratch_shapes=[pltpu.VMEM((2, 2, tm, d), F32), pltpu.SemaphoreType.DMA((2,))])
    return pl.pallas_call(
        functools.partial(_final_kernel, tm=tm),
        grid_spec=grid_spec,
        out_shape=jax.ShapeDtypeStruct((rows, d), F32),
        compiler_params=pltpu.CompilerParams(
            dimension_semantics=("arbitrary",), vmem_limit_bytes=VMEM_LIMIT),
    )(pos, ys, route, x1, mod, gf)


def _route_plan(route, tm):
    n = route.shape[0]
    flat_e = route[:, 0:2].astype(jnp.int32).reshape(-1)
    n_tiles = (2 * n + N_EXPERTS * (tm - 1)) // tm
    experts = jnp.arange(N_EXPERTS, dtype=jnp.int32)
    onehot = (flat_e[:, None] == experts[None, :]).astype(jnp.int32)
    csum = jnp.cumsum(onehot, axis=0)
    rank = jnp.sum(csum * onehot, axis=1) - 1
    counts = csum[-1]
    tiles_per_e = (counts + tm - 1) // tm
    tile_end = jnp.cumsum(tiles_per_e)
    tile_start = tile_end - tiles_per_e
    n_used = tile_end[-1]
    pos = jnp.sum(onehot * tile_start[None, :], axis=1) * tm + rank
    token = jnp.arange(2 * n, dtype=jnp.int32) // 2
    src = jnp.zeros((n_tiles * tm,), jnp.int32).at[pos].set(token)
    tiles = jnp.arange(n_tiles, dtype=jnp.int32)
    texp = jnp.sum((tiles[:, None] >= tile_end[None, :]).astype(jnp.int32), axis=1)
    texp = jnp.where(tiles < n_used, texp, texp[n_used - 1])
    first = jnp.concatenate([jnp.ones((1,), jnp.int32), (texp[1:] != texp[:-1]).astype(jnp.int32)])
    later_used = jnp.logical_and(experts[None, :] > experts[:, None], (tiles_per_e > 0)[None, :])
    next_used = jnp.min(jnp.where(later_used, experts[None, :], N_EXPERTS), axis=1)
    next_used = jnp.where(next_used == N_EXPERTS, -1, next_used)
    return texp, first, next_used[texp], n_used.reshape(1), src, pos


def _round_up(n, m):
    return (n + m - 1) // m * m


def kernel(x, c, w_ada, b_ada, norm1_g, w_in, mu_shift, w0, w_decay_up, a0, w_aaa_up, w_gate_up,
           k_k, k_a, r_k, ln_x_w, ln_x_b, rel_bias, beta_rwkv, beta_attn, w_out, norm2_g, w_group,
           b_group, w_expert, b_expert, w_gate, w_up, w_down, norm_f_g):
    assert w_ada.shape[0] == 1, "single trunk layer"
    batch, seq, d = x.shape
    r_width = w0.shape[1]
    a_width = beta_attn.shape[1]
    shift_width = mu_shift.shape[1]
    assert shift_width == 3 * r_width + DECAY_LORA + AAA_LORA + GATE_LORA
    x2d = x.reshape(batch * seq, d)
    row = lambda t: t.reshape(1, -1)

    c8 = jnp.pad(c, ((0, 8 - batch), (0, 0)))
    mod = _ada_call(c8, w_ada[0], row(b_ada[0]), tn=1536)[:batch].reshape(batch, 6, d)

    tm = min(512, seq)
    pw = _round_up(shift_width, 3 * LANES)
    h1 = _hmod_call(x2d, row(norm1_g[0]), mod, tm, seq)
    tm_p = min(1024, seq)
    w_in_bf = w_in[0].astype(BF16)
    p_r = _matmul_call(h1, w_in_bf, pw, pw // 3, tm_p, F32)
    p_a = _matmul_call(h1, w_in_bf[:, shift_width:], 3 * a_width, a_width, tm_p, BF16)

    mu = jnp.pad(mu_shift[0], (0, pw - shift_width))
    vecs = [row(mu), row(w0[0]), row(a0[0]), row(k_k[0]), row(k_a[0]), row(r_k[0]), row(ln_x_w[0]),
            row(ln_x_b[0]), row(beta_rwkv[0])]
    y_r = _rwkv_call(p_r, vecs, w_decay_up[0], w_aaa_up[0], w_gate_up[0], batch, seq, r_width,
                     tb=min(256, seq))

    left = LEFT_CHUNKS * CHUNK
    dist = left + CHUNK - 1 - jnp.arange(BAND + CHUNK - 1)
    base = rel_bias[0][:, jnp.clip(dist, -REL_CLIP, REL_CLIP) + REL_CLIP].astype(F32)
    bias = jnp.stack([base[:, CHUNK - 1 - qi:CHUNK - 1 - qi + BAND] for qi in range(CHUNK)], axis=1)
    nb = ATTN_BLOCK_CHUNKS
    bias = jnp.concatenate(
        [jnp.pad(bias, ((0, 0), (0, 0), (cb * CHUNK, (nb - 1 - cb) * CHUNK)), constant_values=NEG_INF)
         for cb in range(nb)], axis=1)
    bias = bias.reshape(a_width // PAIR, 2 * nb * CHUNK, BAND + (nb - 1) * CHUNK)
    y_a = _attn_call(p_a, bias, row(beta_attn[0]), batch, seq, a_width, qb=min(512, seq))

    n_route = N_GROUPS + N_EXPERTS
    wr = jnp.concatenate([w_group[0], w_expert[0].transpose(1, 0, 2).reshape(d, N_EXPERTS)], axis=1)
    wr = jnp.pad(wr, ((0, 0), (0, LANES - n_route)))
    br = jnp.pad(jnp.concatenate([b_group[0], b_expert[0].reshape(-1)]), (0, LANES - n_route))
    x1, h2, route = _out_router_call(y_r, y_a, w_out[0].astype(BF16), x2d, mod, row(norm2_g[0]),
                                     wr, row(br), seq, tm)

    f = w_gate.shape[-1]
    tm_e = 256
    texp, first, nexp, meta, src, pos = _route_plan(route, tm_e)
    ys = _moe_call(texp, first, nexp, meta, src, h2, w_gate[0].reshape(N_EXPERTS, d, f),
                   w_up[0].reshape(N_EXPERTS, d, f), w_down[0].reshape(N_EXPERTS, f, d), tm_e)
    out = _final_call(pos, ys, route, x1, mod, row(norm_f_g), seq, min(256, seq))
    return out.reshape(batch, seq, d)
```

```python
import functools

import jax
import jax.numpy as jnp
from jax import lax
from jax.experimental import pallas as pl
from jax.experimental.pallas import tpu as pltpu

F32 = jnp.float32
BF16 = jnp.bfloat16
HI = lax.Precision.HIGHEST

LANES = 128
HEAD_DIM = 64
PAIR = 2 * HEAD_DIM
CHUNK = 64
LEFT_CHUNKS = 8
BAND = (LEFT_CHUNKS + 1) * CHUNK
ATTN_BLOCK_CHUNKS = 4
PREP_PAIRS = 4
MOE_GATHER_SLOTS = 3
REL_CLIP = 128
DECAY_LORA = 64
AAA_LORA = 64
GATE_LORA = 160
N_GROUPS = 4
EXPERTS_PER_GROUP = 8
N_EXPERTS = N_GROUPS * EXPERTS_PER_GROUP
RMS_EPS = 1e-6
GN_EPS = 64e-5
L2_EPS = 1e-12
NEG_INF = -1e30
VMEM_LIMIT = 56 * 1024 * 1024

SHIFT1, SCALE1, GATE1, SHIFT2, SCALE2, GATE2 = range(6)


def _dot(a, b, precision=None):
    return jnp.dot(a, b, precision=precision, preferred_element_type=F32)


def _dot_nt(a, b, precision=None):
    return lax.dot_general(a, b, (((1,), (1,)), ((), ())), precision=precision,
                           preferred_element_type=F32)


def _sigmoid(x):
    return 1.0 / (1.0 + jnp.exp(-x))


def _rms(x):
    return x * lax.rsqrt(jnp.mean(x * x, axis=-1, keepdims=True) + RMS_EPS)


def _ada_kernel(c_ref, w_ref, b_ref, o_ref):
    c = c_ref[...]
    o_ref[...] = _dot_x3(c * _sigmoid(c), w_ref[...]) + b_ref[...]


def _ada_call(c8, w, b, tn):
    rows, d = c8.shape
    n = w.shape[1]
    return pl.pallas_call(
        _ada_kernel,
        grid=(n // tn,),
        in_specs=[pl.BlockSpec((rows, d), lambda j: (0, 0)),
                  pl.BlockSpec((d, tn), lambda j: (0, j)),
                  pl.BlockSpec((1, tn), lambda j: (0, j))],
        out_specs=pl.BlockSpec((rows, tn), lambda j: (0, j)),
        out_shape=jax.ShapeDtypeStruct((rows, n), F32),
        compiler_params=pltpu.CompilerParams(vmem_limit_bytes=VMEM_LIMIT),
    )(c8, w, b)


def _hmod_kernel(x_ref, g_ref, mod_ref, o_ref):
    m = mod_ref[0]
    h = _rms(x_ref[...]) * g_ref[...] * (1.0 + m[SCALE1:SCALE1 + 1]) + m[SHIFT1:SHIFT1 + 1]
    o_ref[...] = h.astype(BF16)


def _hmod_call(x2d, g, mod, tm, seq):
    rows, d = x2d.shape
    tiles_per_seq = seq // tm
    return pl.pallas_call(
        _hmod_kernel,
        grid=(rows // tm,),
        in_specs=[pl.BlockSpec((tm, d), lambda i: (i, 0)),
                  pl.BlockSpec((1, d), lambda i: (0, 0)),
                  pl.BlockSpec((1, 6, d), lambda i: (i // tiles_per_seq, 0, 0))],
        out_specs=pl.BlockSpec((tm, d), lambda i: (i, 0)),
        out_shape=jax.ShapeDtypeStruct((rows, d), BF16),
        compiler_params=pltpu.CompilerParams(
            dimension_semantics=("parallel",), vmem_limit_bytes=VMEM_LIMIT),
    )(x2d, g, mod)


def _matmul_kernel(a_ref, w_ref, o_ref):
    o_ref[...] = _dot(a_ref[...], w_ref[...]).astype(o_ref.dtype)


def _matmul_call(a, w, n_out, tn, tm, out_dtype):
    rows, d = a.shape
    return pl.pallas_call(
        _matmul_kernel,
        grid=(n_out // tn, rows // tm),
        in_specs=[pl.BlockSpec((tm, d), lambda j, i: (i, 0)),
                  pl.BlockSpec((d, tn), lambda j, i: (0, j))],
        out_specs=pl.BlockSpec((tm, tn), lambda j, i: (i, j)),
        out_shape=jax.ShapeDtypeStruct((rows, n_out), out_dtype),
        compiler_params=pltpu.CompilerParams(
            dimension_semantics=("parallel", "parallel"), vmem_limit_bytes=VMEM_LIMIT),
    )(a, w)


def _split3(t):
    hi = t.astype(BF16)
    rest = t - hi.astype(F32)
    mid = rest.astype(BF16)
    lo = (rest - mid.astype(F32)).astype(BF16)
    return hi, mid, lo


def _dot_hilo_lhs(t, m):
    hi = t.astype(BF16)
    lo = (t - hi.astype(F32)).astype(BF16)
    return _dot(lo, m) + _dot(hi, m)


def _dot_f32_rhs(m, t):
    hi, mid, lo = _split3(t)
    return _dot(m, lo) + _dot(m, mid) + _dot(m, hi)


def _dot_x3(a, b):
    ah = a.astype(BF16)
    al = (a - ah.astype(F32)).astype(BF16)
    bh = b.astype(BF16)
    bl = (b - bh.astype(F32)).astype(BF16)
    return _dot(al, bh) + _dot(ah, bl) + _dot(ah, bh)


def _rwkv_kernel(p_ref, mu_ref, w0_ref, a0_ref, kk_ref, ka_ref, rk_ref, lnw_ref, lnb_ref, beta_ref,
                 wd_ref, wa_ref, wg_ref, o_ref,
                 carry_scr, state_scr, ar_scr, bk_scr, bkh_scr, v_scr, et_scr, y_scr,
                 pm_scr, arb_scr, av_scr, vk_scr, *, width, tb):
    n_pairs = width // PAIR
    n_chunks = tb // CHUNK
    stacked = 2 * CHUNK

    @pl.when(pl.program_id(1) == 0)
    def _():
        carry_scr[...] = jnp.zeros_like(carry_scr)
        state_scr[...] = jnp.zeros_like(state_scr)

    p = p_ref[...]
    row = lax.broadcasted_iota(jnp.int32, (tb, 1), 0)
    prev = jnp.where(row == 0, carry_scr[...], pltpu.roll(p, 1, axis=0))
    carry_scr[...] = p[tb - 1:tb, :]
    ps = p + (prev - p) * mu_ref[...]

    c = width
    r = ps[:, :c]
    k = ps[:, c:2 * c]
    v = ps[:, 2 * c:3 * c]
    o = 3 * c
    xw = ps[:, o:o + DECAY_LORA]
    xa = ps[:, o + DECAY_LORA:o + DECAY_LORA + AAA_LORA]
    xg = ps[:, o + DECAY_LORA + AAA_LORA:o + DECAY_LORA + AAA_LORA + GATE_LORA]

    z = w0_ref[...] + _dot_x3(jnp.tanh(xw), wd_ref[...])
    softplus_neg_z = jnp.maximum(-z, 0.0) + jnp.log(1.0 + jnp.exp(-jnp.abs(z)))
    lw = -jnp.exp(-softplus_neg_z - 0.5)
    a = _sigmoid(a0_ref[...] + _dot(xa.astype(BF16), wa_ref[...].astype(BF16)))
    g = _dot(_sigmoid(xg).astype(BF16), wg_ref[...].astype(BF16))

    li = lax.broadcasted_iota(jnp.int32, (PAIR, PAIR), 0)
    lj = lax.broadcasted_iota(jnp.int32, (PAIR, PAIR), 1)
    wide = 2 * PAIR
    hi_ = lax.broadcasted_iota(jnp.int32, (wide, wide), 0)
    hj_ = lax.broadcasted_iota(jnp.int32, (wide, wide), 1)
    head_ones = ((hi_ // HEAD_DIM) == (hj_ // HEAD_DIM)).astype(BF16)

    def head_sum(t):
        return jnp.concatenate(
            [_dot_hilo_lhs(t[:, q * wide:(q + 1) * wide], head_ones) for q in range(c // wide)],
            axis=1)

    kk = k * kk_ref[...]
    kk = kk / jnp.maximum(jnp.sqrt(head_sum(kk * kk)), L2_EPS)
    k2 = k * (1.0 + (a - 1.0) * ka_ref[...])

    ti = lax.broadcasted_iota(jnp.int32, (tb, tb), 0)
    tj = lax.broadcasted_iota(jnp.int32, (tb, tb), 1)
    tri = jnp.logical_and((ti // CHUNK) == (tj // CHUNK), tj <= ti).astype(BF16)
    cum = _dot_f32_rhs(tri, lw)
    tot = jnp.concatenate(
        [jnp.broadcast_to(cum[(ci + 1) * CHUNK - 1:(ci + 1) * CHUNK, :], (CHUNK, c))
         for ci in range(n_chunks)], axis=0)

    e_neg = jnp.exp(-cum)
    e_rem = jnp.exp(tot - cum)
    kka = kk * a
    lane = lax.broadcasted_iota(jnp.int32, (1, PAIR), 1)
    first_head = lane < HEAD_DIM

    def put(scr, base, val):
        for q in range(n_pairs):
            vq = val[:, q * PAIR:(q + 1) * PAIR]
            h0 = jnp.where(first_head, vq, 0.0).astype(scr.dtype)
            h1 = jnp.where(first_head, 0.0, vq).astype(scr.dtype)
            for ci in range(n_chunks):
                rows = slice(ci * CHUNK, (ci + 1) * CHUNK)
                scr[q, ci, base:base + CHUNK, :] = h0[rows]
                scr[q, ci, base + CHUNK:base + stacked, :] = h1[rows]

    put(ar_scr, 0, -kk * jnp.exp(cum - lw))
    put(ar_scr, stacked, r * jnp.exp(cum))
    put(bk_scr, 0, kka * e_neg)
    put(bk_scr, stacked, k2 * e_neg)
    put(bkh_scr, 0, kka * e_rem)
    put(bkh_scr, stacked, k2 * e_rem)
    put(v_scr, 0, v)
    e_tot = jnp.exp(tot)
    for q in range(n_pairs):
        et_scr[q] = e_tot[:, q * PAIR:(q + 1) * PAIR]

    si = lax.broadcasted_iota(jnp.int32, (2 * stacked, 2 * stacked), 0)
    sj = lax.broadcasted_iota(jnp.int32, (2 * stacked, 2 * stacked), 1)
    keep = jnp.where(si < stacked, si, si - stacked + 1) > (sj % stacked)
    eye = (li == lj).astype(F32)

    chunks = range(n_chunks)
    pairs = range(n_pairs)

    def prepare(step, carry):
        chains = [(step * PREP_PAIRS + dq, ci) for dq in range(PREP_PAIRS) for ci in chunks]
        n = []
        for q, ci in chains:
            sc = jnp.where(keep, _dot_nt(ar_scr[q, ci], bk_scr[q, ci]), 0.0)
            scb = sc.astype(BF16)
            vm = v_scr[q, ci]
            arb_scr[q, ci] = scb[stacked:, :stacked]
            av_scr[q, ci] = _dot(scb[:, stacked:], vm.astype(BF16))
            vk_scr[q, ci] = _dot(vm.T.astype(BF16), bkh_scr[q, ci, stacked:, :])
            n.append(sc[:stacked, :stacked])
        acc = [eye + t for t in n]
        pw = [t.astype(BF16) for t in n]
        pw = [_dot(t, t).astype(BF16) for t in pw]
        for it in range(1, 6):
            if it < 5:
                both = [_dot(jnp.concatenate([p_, a_.astype(BF16)], axis=0), p_)
                        for p_, a_ in zip(pw, acc)]
                pw = [t[:stacked].astype(BF16) for t in both]
                acc = [a_ + t[stacked:] for a_, t in zip(acc, both)]
            else:
                acc = [a_ + _dot(a_.astype(BF16), p_) for p_, a_ in zip(pw, acc)]
        for (q, ci), a_ in zip(chains, acc):
            pm_scr[q, ci] = a_.astype(BF16)
        return carry

    for step in range(n_pairs // PREP_PAIRS):
        prepare(step, 0)

    s = [state_scr[q] for q in pairs]
    for ci in chunks:
        sr = [_dot_nt(ar_scr[q, ci], s[q].astype(BF16)) for q in pairs]
        u = [_dot(pm_scr[q, ci], (sr[q][:stacked] + av_scr[q, ci, :stacked, :]).astype(BF16))
             for q in pairs]
        for q in pairs:
            ym = (sr[q][stacked:] + av_scr[q, ci, stacked:, :]
                  + _dot(arb_scr[q, ci], u[q].astype(BF16)))
            y_scr[q, ci * CHUNK:(ci + 1) * CHUNK, :] = ym[:CHUNK] + ym[CHUNK:]
        s = [s[q] * et_scr[q, ci * CHUNK:ci * CHUNK + 1, :]
             + _dot(u[q].T.astype(BF16), bkh_scr[q, ci, :stacked, :]) + vk_scr[q, ci] for q in pairs]
    for q in pairs:
        state_scr[q] = s[q]

    y = jnp.concatenate([y_scr[q] for q in range(n_pairs)], axis=1)
    mean = head_sum(y) * (1.0 / HEAD_DIM)
    d = y - mean
    var = head_sum(d * d) * (1.0 / HEAD_DIM)
    yn = d * lax.rsqrt(var + GN_EPS) * lnw_ref[...] + lnb_ref[...]
    bonus = head_sum(r * k2 * rk_ref[...]) * v
    o_ref[...] = (yn + bonus) * g * beta_ref[...]


def _rwkv_call(p_r, vecs, wd, wa, wg, batch, seq, width, tb):
    pw = p_r.shape[1]
    n_pairs = width // PAIR
    n_chunks = tb // CHUNK
    tiles = seq // tb
    vec_specs = [pl.BlockSpec((1, v.shape[1]), lambda b, t: (0, 0)) for v in vecs]
    mat_specs = [pl.BlockSpec(m.shape, lambda b, t: (0, 0)) for m in (wd, wa, wg)]
    stacked2 = pltpu.VMEM((n_pairs, n_chunks, 4 * CHUNK, PAIR), BF16)
    return pl.pallas_call(
        functools.partial(_rwkv_kernel, width=width, tb=tb),
        grid=(batch, tiles),
        in_specs=[pl.BlockSpec((tb, pw), lambda b, t: (b * tiles + t, 0))] + vec_specs + mat_specs,
        out_specs=pl.BlockSpec((tb, width), lambda b, t: (b * tiles + t, 0)),
        out_shape=jax.ShapeDtypeStruct((batch * seq, width), F32),
        scratch_shapes=[pltpu.VMEM((1, pw), F32),
                        pltpu.VMEM((n_pairs, PAIR, PAIR), F32),
                        stacked2, stacked2, stacked2,
                        pltpu.VMEM((n_pairs, n_chunks, 2 * CHUNK, PAIR), F32),
                        pltpu.VMEM((n_pairs, tb, PAIR), F32),
                        pltpu.VMEM((n_pairs, tb, PAIR), F32),
                        pltpu.VMEM((n_pairs, n_chunks, 2 * CHUNK, PAIR), BF16),
                        pltpu.VMEM((n_pairs, n_chunks, 2 * CHUNK, PAIR), BF16),
                        pltpu.VMEM((n_pairs, n_chunks, 4 * CHUNK, PAIR), F32),
                        pltpu.VMEM((n_pairs, n_chunks, 2 * CHUNK, PAIR), F32)],
        compiler_params=pltpu.CompilerParams(
            dimension_semantics=("parallel", "arbitrary"), vmem_limit_bytes=VMEM_LIMIT),
    )(p_r, *vecs, wd, wa, wg)


def _attn_kernel(q_ref, kp_ref, kc_ref, vp_ref, vc_ref, bias_ref, beta_ref, o_ref, kwin, vwin,
                 *, width, qb):
    i = pl.program_id(1)
    n_pairs = width // PAIR
    kwin[0:qb, :] = kp_ref[...]
    kwin[qb:2 * qb, :] = kc_ref[...]
    vwin[0:qb, :] = vp_ref[...]
    vwin[qb:2 * qb, :] = vc_ref[...]
    lane = lax.broadcasted_iota(jnp.int32, (1, PAIR), 1)
    first_head = lane < HEAD_DIM
    rows = ATTN_BLOCK_CHUNKS * CHUNK
    win = BAND + rows - CHUNK
    col = lax.broadcasted_iota(jnp.int32, (1, win), 1)
    scale = HEAD_DIM ** -0.5
    left = LEFT_CHUNKS * CHUNK

    for q in range(n_pairs):
        lanes = slice(q * PAIR, (q + 1) * PAIR)
        for blk in range(qb // rows):
            r0 = blk * rows
            qc = q_ref[r0:r0 + rows, lanes]
            zero = jnp.zeros_like(qc)
            qm = jnp.concatenate([jnp.where(first_head, qc, zero), jnp.where(first_head, zero, qc)],
                                 axis=0)
            s = _dot_nt(qm, kwin[r0:r0 + win, lanes]) * scale + bias_ref[q]
            valid = jnp.logical_or(i > 0, col + r0 >= left)
            s = jnp.where(valid, s, NEG_INF)
            e = jnp.exp(s - jnp.max(s, axis=-1, keepdims=True))
            denom = jnp.sum(e, axis=-1, keepdims=True)
            o = _dot(e.astype(BF16), vwin[r0:r0 + win, lanes]) / denom
            oc = jnp.where(first_head, o[:rows], o[rows:])
            o_ref[r0:r0 + rows, lanes] = oc * beta_ref[:, lanes]


def _attn_call(p_a, bias, beta, batch, seq, width, qb):
    assert qb == LEFT_CHUNKS * CHUNK, "key window = previous block + current block"
    tiles = seq // qb
    n_pairs = width // PAIR

    def cur(col):
        return pl.BlockSpec((qb, width), lambda b, t: (b * tiles + t, col))

    def prev(col):
        return pl.BlockSpec((qb, width), lambda b, t: (b * tiles + jnp.maximum(t - 1, 0), col))

    return pl.pallas_call(
        functools.partial(_attn_kernel, width=width, qb=qb),
        grid=(batch, tiles),
        in_specs=[cur(0), prev(1), cur(1), prev(2), cur(2),
                  pl.BlockSpec(bias.shape, lambda b, t: (0, 0, 0)),
                  pl.BlockSpec((1, width), lambda b, t: (0, 0))],
        out_specs=pl.BlockSpec((qb, width), lambda b, t: (b * tiles + t, 0)),
        out_shape=jax.ShapeDtypeStruct((batch * seq, width), F32),
        scratch_shapes=[pltpu.VMEM((2 * qb, width), BF16), pltpu.VMEM((2 * qb, width), BF16)],
        compiler_params=pltpu.CompilerParams(
            dimension_semantics=("parallel", "arbitrary"), vmem_limit_bytes=VMEM_LIMIT),
    )(p_a, p_a, p_a, p_a, p_a, bias, beta)


def _out_router_kernel(yr_ref, ya_ref, wo_ref, x_ref, mod_ref, g2_ref, wr_ref, br_ref,
                       x1_ref, h2_ref, route_ref, *, r_width):
    y = (_dot(yr_ref[...].astype(BF16), wo_ref[0:r_width, :])
         + _dot(ya_ref[...].astype(BF16), wo_ref[r_width:, :]))
    m = mod_ref[0]
    x1 = x_ref[...] + m[GATE1:GATE1 + 1] * y
    x1_ref[...] = x1
    h2 = _rms(x1) * g2_ref[...] * (1.0 + m[SCALE2:SCALE2 + 1]) + m[SHIFT2:SHIFT2 + 1]
    h2_ref[...] = h2

    logits = _dot_x3(h2, wr_ref[...]) + br_ref[...]
    li = lax.broadcasted_iota(jnp.int32, logits.shape, 1).astype(F32)
    none = jnp.float32(-jnp.inf)
    far = jnp.float32(LANES)
    is_group = li < N_GROUPS
    gl = jnp.where(is_group, logits, none)
    gmax = jnp.max(gl, axis=-1, keepdims=True)
    gidx = jnp.min(jnp.where(gl == gmax, li, far), axis=-1, keepdims=True)
    g_w = 1.0 / jnp.sum(jnp.where(is_group, jnp.exp(logits - gmax), 0.0), axis=-1, keepdims=True)
    lo = N_GROUPS + EXPERTS_PER_GROUP * gidx
    el = jnp.where(jnp.logical_and(li >= lo, li < lo + EXPERTS_PER_GROUP), logits, none)
    m1 = jnp.max(el, axis=-1, keepdims=True)
    i1 = jnp.min(jnp.where(el == m1, li, far), axis=-1, keepdims=True)
    el2 = jnp.where(li == i1, none, el)
    m2 = jnp.max(el2, axis=-1, keepdims=True)
    i2 = jnp.min(jnp.where(el2 == m2, li, far), axis=-1, keepdims=True)
    e2 = jnp.exp(m2 - m1)
    w1 = g_w / (1.0 + e2)
    w2 = g_w * e2 / (1.0 + e2)
    route = jnp.where(li == 0.0, i1 - N_GROUPS, 0.0) + jnp.where(li == 1.0, i2 - N_GROUPS, 0.0)
    route_ref[...] = route + jnp.where(li == 2.0, w1, 0.0) + jnp.where(li == 3.0, w2, 0.0)


def _out_router_call(y_r, y_a, wo, x2d, mod, g2, wr, br, seq, tm):
    rows, d = x2d.shape
    r_width = y_r.shape[1]
    a_width = y_a.shape[1]
    tiles_per_seq = seq // tm
    return pl.pallas_call(
        functools.partial(_out_router_kernel, r_width=r_width),
        grid=(rows // tm,),
        in_specs=[pl.BlockSpec((tm, r_width), lambda i: (i, 0)),
                  pl.BlockSpec((tm, a_width), lambda i: (i, 0)),
                  pl.BlockSpec(wo.shape, lambda i: (0, 0)),
                  pl.BlockSpec((tm, d), lambda i: (i, 0)),
                  pl.BlockSpec((1, 6, d), lambda i: (i // tiles_per_seq, 0, 0)),
                  pl.BlockSpec((1, d), lambda i: (0, 0)),
                  pl.BlockSpec((d, LANES), lambda i: (0, 0)),
                  pl.BlockSpec((1, LANES), lambda i: (0, 0))],
        out_specs=[pl.BlockSpec((tm, d), lambda i: (i, 0)),
                   pl.BlockSpec((tm, d), lambda i: (i, 0)),
                   pl.BlockSpec((tm, LANES), lambda i: (i, 0))],
        out_shape=[jax.ShapeDtypeStruct((rows, d), F32),
                   jax.ShapeDtypeStruct((rows, d), F32),
                   jax.ShapeDtypeStruct((rows, LANES), F32)],
        compiler_params=pltpu.CompilerParams(
            dimension_semantics=("parallel",), vmem_limit_bytes=VMEM_LIMIT),
    )(y_r, y_a, wo, x2d, mod, g2, wr, br)


def _moe_kernel(texp_ref, first_ref, nexp_ref, meta_ref, src_ref,
                h2_hbm, wg_hbm, wu_hbm, wd_hbm, o_ref,
                xbuf, xsem, wg_land, wu_land, wd_land, wsem, wg_bf, wu_bf, wd_bf, *, tm):
    t = pl.program_id(0)
    n_used = meta_ref[0]
    slot = lax.rem(t, MOE_GATHER_SLOTS)

    def weight_copies(e):
        return (pltpu.make_async_copy(wg_hbm.at[e], wg_land, wsem.at[0]),
                pltpu.make_async_copy(wu_hbm.at[e], wu_land, wsem.at[1]),
                pltpu.make_async_copy(wd_hbm.at[e], wd_land, wsem.at[2]))

    def gather_rows(tile):
        buf_slot = lax.rem(tile, MOE_GATHER_SLOTS)

        def body(i, carry):
            tok = src_ref[tile * tm + i]
            pltpu.make_async_copy(h2_hbm.at[pl.ds(tok, 1)], xbuf.at[buf_slot, pl.ds(i, 1)],
                                  xsem.at[buf_slot]).start()
            return carry

        lax.fori_loop(0, tm, body, 0, unroll=8)

    @pl.when(t == 0)
    def _():
        for cp in weight_copies(texp_ref[0]):
            cp.start()
        for ahead in range(MOE_GATHER_SLOTS - 1):
            @pl.when(ahead < n_used)
            def _():
                gather_rows(ahead)

    is_first = jnp.logical_and(t < n_used, first_ref[t] == 1)

    @pl.when(is_first)
    def _():
        for cp in weight_copies(texp_ref[t]):
            cp.wait()
        wg_bf[...] = wg_land[...].astype(BF16)
        wu_bf[...] = wu_land[...].astype(BF16)
        wd_bf[...] = wd_land[...].astype(BF16)

    @pl.when(t + MOE_GATHER_SLOTS - 1 < n_used)
    def _():
        gather_rows(t + MOE_GATHER_SLOTS - 1)

    @pl.when(jnp.logical_and(is_first, nexp_ref[t] >= 0))
    def _():
        for cp in weight_copies(nexp_ref[t]):
            cp.start()

    @pl.when(t < n_used)
    def _():
        pltpu.make_async_copy(h2_hbm.at[pl.ds(0, tm)], xbuf.at[slot], xsem.at[slot]).wait()
        x = xbuf[slot].astype(BF16)
        gate = _dot(x, wg_bf[...])
        up = _dot(x, wu_bf[...])
        h = gate * _sigmoid(gate) * up
        o_ref[...] = _dot(h.astype(BF16), wd_bf[...])

    @pl.when(t >= n_used)
    def _():
        o_ref[...] = jnp.zeros_like(o_ref)


def _moe_call(texp, first, nexp, meta, src, h2, wg, wu, wd, tm):
    n_e, d, f = wg.shape
    n_tiles = texp.shape[0]
    grid_spec = pltpu.PrefetchScalarGridSpec(
        num_scalar_prefetch=5,
        grid=(n_tiles,),
        in_specs=[pl.BlockSpec(memory_space=pl.ANY)] * 4,
        out_specs=pl.BlockSpec((tm, d), lambda t, *_: (t, 0)),
        scratch_shapes=[pltpu.VMEM((MOE_GATHER_SLOTS, tm, d), F32),
                        pltpu.SemaphoreType.DMA((MOE_GATHER_SLOTS,)),
                        pltpu.VMEM((d, f), F32), pltpu.VMEM((d, f), F32), pltpu.VMEM((f, d), F32),
                        pltpu.SemaphoreType.DMA((3,)),
                        pltpu.VMEM((d, f), BF16), pltpu.VMEM((d, f), BF16), pltpu.VMEM((f, d), BF16)])
    return pl.pallas_call(
        functools.partial(_moe_kernel, tm=tm),
        grid_spec=grid_spec,
        out_shape=jax.ShapeDtypeStruct((n_tiles * tm, d), F32),
        compiler_params=pltpu.CompilerParams(
            dimension_semantics=("arbitrary",), vmem_limit_bytes=VMEM_LIMIT),
    )(texp, first, nexp, meta, src, h2, wg, wu, wd)


def _final_kernel(pos_ref, ys_hbm, route_ref, x1_ref, mod_ref, gf_ref, o_ref, ybuf, sem, *, tm):
    i = pl.program_id(0)
    n = pl.num_programs(0)
    slot = i % 2

    def gather_rows(tile, buf_slot):
        base = tile * tm * 2

        def body(r, carry):
            for s in range(2):
                pltpu.make_async_copy(ys_hbm.at[pl.ds(pos_ref[base + 2 * r + s], 1)],
                                      ybuf.at[buf_slot, s, pl.ds(r, 1)], sem.at[buf_slot]).start()
            return carry

        lax.fori_loop(0, tm, body, 0, unroll=4)

    @pl.when(i == 0)
    def _():
        gather_rows(0, 0)

    @pl.when(i + 1 < n)
    def _():
        gather_rows(i + 1, 1 - slot)

    for s in range(2):
        pltpu.make_async_copy(ys_hbm.at[pl.ds(0, tm)], ybuf.at[slot, s], sem.at[slot]).wait()
    route = route_ref[...]
    w0 = route[:, 2:3]
    w1 = route[:, 3:4]
    f = w0 * ybuf[slot, 0] + w1 * ybuf[slot, 1]
    x2 = x1_ref[...] + mod_ref[0][GATE2:GATE2 + 1] * f
    o_ref[...] = _rms(x2) * gf_ref[...]


def _final_call(pos, ys, route, x1, mod, gf, seq, tm):
    rows, d = x1.shape
    tiles_per_seq = seq // tm
    grid_spec = pltpu.PrefetchScalarGridSpec(
        num_scalar_prefetch=1,
        grid=(rows // tm,),
        in_specs=[pl.BlockSpec(memory_space=pl.ANY),
                  pl.BlockSpec((tm, LANES), lambda i, pos: (i, 0)),
                  pl.BlockSpec((tm, d), lambda i, pos: (i, 0)),
                  pl.BlockSpec((1, 6, d), lambda i, pos: (i // tiles_per_seq, 0, 0)),
                  pl.BlockSpec((1, d), lambda i, pos: (0, 0))],
        out_specs=pl.BlockSpec((tm, d), lambda i, pos: (i, 0)),
        scratch_shapes=[pltpu.VMEM((2, 2, tm, d), F32), pltpu.SemaphoreType.DMA((2,))])
    return pl.pallas_call(
        functools.partial(_final_kernel, tm=tm),
        grid_spec=grid_spec,
        out_shape=jax.ShapeDtypeStruct((rows, d), F32),
        compiler_params=pltpu.CompilerParams(
            dimension_semantics=("arbitrary",), vmem_limit_bytes=VMEM_LIMIT),
    )(pos, ys, route, x1, mod, gf)


def _route_plan(route, tm):
    n = route.shape[0]
    flat_e = route[:, 0:2].astype(jnp.int32).reshape(-1)
    n_tiles = (2 * n + N_EXPERTS * (tm - 1)) // tm
    experts = jnp.arange(N_EXPERTS, dtype=jnp.int32)
    onehot = (flat_e[:, None] == experts[None, :]).astype(jnp.int32)
    csum = jnp.cumsum(onehot, axis=0)
    rank = jnp.sum(csum * onehot, axis=1) - 1
    counts = csum[-1]
    tiles_per_e = (counts + tm - 1) // tm
    tile_end = jnp.cumsum(tiles_per_e)
    tile_start = tile_end - tiles_per_e
    n_used = tile_end[-1]
    pos = jnp.sum(onehot * tile_start[None, :], axis=1) * tm + rank
    token = jnp.arange(2 * n, dtype=jnp.int32) // 2
    src = jnp.zeros((n_tiles * tm,), jnp.int32).at[pos].set(token)
    tiles = jnp.arange(n_tiles, dtype=jnp.int32)
    texp = jnp.sum((tiles[:, None] >= tile_end[None, :]).astype(jnp.int32), axis=1)
    texp = jnp.where(tiles < n_used, texp, texp[n_used - 1])
    first = jnp.concatenate([jnp.ones((1,), jnp.int32), (texp[1:] != texp[:-1]).astype(jnp.int32)])
    later_used = jnp.logical_and(experts[None, :] > experts[:, None], (tiles_per_e > 0)[None, :])
    next_used = jnp.min(jnp.where(later_used, experts[None, :], N_EXPERTS), axis=1)
    next_used = jnp.where(next_used == N_EXPERTS, -1, next_used)
    return texp, first, next_used[texp], n_used.reshape(1), src, pos


def _round_up(n, m):
    return (n + m - 1) // m * m


def kernel(x, c, w_ada, b_ada, norm1_g, w_in, mu_shift, w0, w_decay_up, a0, w_aaa_up, w_gate_up,
           k_k, k_a, r_k, ln_x_w, ln_x_b, rel_bias, beta_rwkv, beta_attn, w_out, norm2_g, w_group,
           b_group, w_expert, b_expert, w_gate, w_up, w_down, norm_f_g):
    assert w_ada.shape[0] == 1, "single trunk layer"
    batch, seq, d = x.shape
    r_width = w0.shape[1]
    a_width = beta_attn.shape[1]
    shift_width = mu_shift.shape[1]
    assert shift_width == 3 * r_width + DECAY_LORA + AAA_LORA + GATE_LORA
    x2d = x.reshape(batch * seq, d)
    row = lambda t: t.reshape(1, -1)

    c8 = jnp.pad(c, ((0, 8 - batch), (0, 0)))
    mod = _ada_call(c8, w_ada[0], row(b_ada[0]), tn=1536)[:batch].reshape(batch, 6, d)

    tm = min(512, seq)
    pw = _round_up(shift_width, 3 * LANES)
    h1 = _hmod_call(x2d, row(norm1_g[0]), mod, tm, seq)
    tm_p = min(1024, seq)
    w_in_bf = w_in[0].astype(BF16)
    p_r = _matmul_call(h1, w_in_bf, pw, pw // 3, tm_p, F32)
    p_a = _matmul_call(h1, w_in_bf[:, shift_width:], 3 * a_width, a_width, tm_p, BF16)

    mu = jnp.pad(mu_shift[0], (0, pw - shift_width))
    vecs = [row(mu), row(w0[0]), row(a0[0]), row(k_k[0]), row(k_a[0]), row(r_k[0]), row(ln_x_w[0]),
            row(ln_x_b[0]), row(beta_rwkv[0])]
    y_r = _rwkv_call(p_r, vecs, w_decay_up[0], w_aaa_up[0], w_gate_up[0], batch, seq, r_width,
                     tb=min(256, seq))

    left = LEFT_CHUNKS * CHUNK
    dist = left + CHUNK - 1 - jnp.arange(BAND + CHUNK - 1)
    base = rel_bias[0][:, jnp.clip(dist, -REL_CLIP, REL_CLIP) + REL_CLIP].astype(F32)
    bias = jnp.stack([base[:, CHUNK - 1 - qi:CHUNK - 1 - qi + BAND] for qi in range(CHUNK)], axis=1)
    nb = ATTN_BLOCK_CHUNKS
    bias = jnp.concatenate(
        [jnp.pad(bias, ((0, 0), (0, 0), (cb * CHUNK, (nb - 1 - cb) * CHUNK)), constant_values=NEG_INF)
         for cb in range(nb)], axis=1)
    bias = bias.reshape(a_width // PAIR, 2 * nb * CHUNK, BAND + (nb - 1) * CHUNK)
    y_a = _attn_call(p_a, bias, row(beta_attn[0]), batch, seq, a_width, qb=min(512, seq))

    n_route = N_GROUPS + N_EXPERTS
    wr = jnp.concatenate([w_group[0], w_expert[0].transpose(1, 0, 2).reshape(d, N_EXPERTS)], axis=1)
    wr = jnp.pad(wr, ((0, 0), (0, LANES - n_route)))
    br = jnp.pad(jnp.concatenate([b_group[0], b_expert[0].reshape(-1)]), (0, LANES - n_route))
    x1, h2, route = _out_router_call(y_r, y_a, w_out[0].astype(BF16), x2d, mod, row(norm2_g[0]),
                                     wr, row(br), seq, tm)

    f = w_gate.shape[-1]
    tm_e = 256
    texp, first, nexp, meta, src, pos = _route_plan(route, tm_e)
    ys = _moe_call(texp, first, nexp, meta, src, h2, w_gate[0].reshape(N_EXPERTS, d, f),
                   w_up[0].reshape(N_EXPERTS, d, f), w_down[0].reshape(N_EXPERTS, f, d), tm_e)
    out = _final_call(pos, ys, route, x1, mod, row(norm_f_g), seq, min(256, seq))
    return out.reshape(batch, seq, d)
```

```python
import functools

import jax
import jax.numpy as jnp
from jax import lax
from jax.experimental import pallas as pl
from jax.experimental.pallas import tpu as pltpu

F32 = jnp.float32
BF16 = jnp.bfloat16
HI = lax.Precision.HIGHEST

LANES = 128
HEAD_DIM = 64
PAIR = 2 * HEAD_DIM
CHUNK = 64
LEFT_CHUNKS = 8
BAND = (LEFT_CHUNKS + 1) * CHUNK
ATTN_BLOCK_CHUNKS = 4
PREP_PAIRS = 4
MOE_GATHER_SLOTS = 3
W_CHUNKS = 2
REL_CLIP = 128
DECAY_LORA = 64
AAA_LORA = 64
GATE_LORA = 160
N_GROUPS = 4
EXPERTS_PER_GROUP = 8
N_EXPERTS = N_GROUPS * EXPERTS_PER_GROUP
RMS_EPS = 1e-6
GN_EPS = 64e-5
L2_EPS = 1e-12
NEG_INF = -1e30
VMEM_LIMIT = 56 * 1024 * 1024

SHIFT1, SCALE1, GATE1, SHIFT2, SCALE2, GATE2 = range(6)


def _dot(a, b, precision=None):
    return jnp.dot(a, b, precision=precision, preferred_element_type=F32)


def _dot_nt(a, b, precision=None):
    return lax.dot_general(a, b, (((1,), (1,)), ((), ())), precision=precision,
                           preferred_element_type=F32)


def _sigmoid(x):
    return 1.0 / (1.0 + jnp.exp(-x))


def _rms(x):
    return x * lax.rsqrt(jnp.mean(x * x, axis=-1, keepdims=True) + RMS_EPS)


def _ada_kernel(c_ref, w_ref, b_ref, o_ref):
    c = c_ref[...]
    o_ref[...] = _dot_x3(c * _sigmoid(c), w_ref[...]) + b_ref[...]


def _ada_call(c8, w, b, tn):
    rows, d = c8.shape
    n = w.shape[1]
    return pl.pallas_call(
        _ada_kernel,
        grid=(n // tn,),
        in_specs=[pl.BlockSpec((rows, d), lambda j: (0, 0)),
                  pl.BlockSpec((d, tn), lambda j: (0, j)),
                  pl.BlockSpec((1, tn), lambda j: (0, j))],
        out_specs=pl.BlockSpec((rows, tn), lambda j: (0, j)),
        out_shape=jax.ShapeDtypeStruct((rows, n), F32),
        compiler_params=pltpu.CompilerParams(vmem_limit_bytes=VMEM_LIMIT),
    )(c8, w, b)


def _hmod_kernel(x_ref, g_ref, mod_ref, o_ref):
    m = mod_ref[0]
    h = _rms(x_ref[...]) * g_ref[...] * (1.0 + m[SCALE1:SCALE1 + 1]) + m[SHIFT1:SHIFT1 + 1]
    o_ref[...] = h.astype(BF16)


def _hmod_call(x2d, g, mod, tm, seq):
    rows, d = x2d.shape
    tiles_per_seq = seq // tm
    return pl.pallas_call(
        _hmod_kernel,
        grid=(rows // tm,),
        in_specs=[pl.BlockSpec((tm, d), lambda i: (i, 0)),
                  pl.BlockSpec((1, d), lambda i: (0, 0)),
                  pl.BlockSpec((1, 6, d), lambda i: (i // tiles_per_seq, 0, 0))],
        out_specs=pl.BlockSpec((tm, d), lambda i: (i, 0)),
        out_shape=jax.ShapeDtypeStruct((rows, d), BF16),
        compiler_params=pltpu.CompilerParams(
            dimension_semantics=("parallel",), vmem_limit_bytes=VMEM_LIMIT),
    )(x2d, g, mod)


def _matmul_kernel(a_ref, w_ref, o_ref):
    o_ref[...] = _dot(a_ref[...], w_ref[...]).astype(o_ref.dtype)


def _matmul_call(a, w, n_out, tn, tm, out_dtype):
    rows, d = a.shape
    return pl.pallas_call(
        _matmul_kernel,
        grid=(n_out // tn, rows // tm),
        in_specs=[pl.BlockSpec((tm, d), lambda j, i: (i, 0)),
                  pl.BlockSpec((d, tn), lambda j, i: (0, j))],
        out_specs=pl.BlockSpec((tm, tn), lambda j, i: (i, j)),
        out_shape=jax.ShapeDtypeStruct((rows, n_out), out_dtype),
        compiler_params=pltpu.CompilerParams(
            dimension_semantics=("parallel", "parallel"), vmem_limit_bytes=VMEM_LIMIT),
    )(a, w)


def _split3(t):
    hi = t.astype(BF16)
    rest = t - hi.astype(F32)
    mid = rest.astype(BF16)
    lo = (rest - mid.astype(F32)).astype(BF16)
    return hi, mid, lo


def _dot_hilo_lhs(t, m):
    hi = t.astype(BF16)
    lo = (t - hi.astype(F32)).astype(BF16)
    return _dot(lo, m) + _dot(hi, m)


def _dot_f32_rhs(m, t):
    hi, mid, lo = _split3(t)
    return _dot(m, lo) + _dot(m, mid) + _dot(m, hi)


def _dot_x3(a, b):
    ah = a.astype(BF16)
    al = (a - ah.astype(F32)).astype(BF16)
    bh = b.astype(BF16)
    bl = (b - bh.astype(F32)).astype(BF16)
    return _dot(al, bh) + _dot(ah, bl) + _dot(ah, bh)


def _rwkv_kernel(p_ref, mu_ref, w0_ref, a0_ref, kk_ref, ka_ref, rk_ref, lnw_ref, lnb_ref, beta_ref,
                 wd_ref, wa_ref, wg_ref, o_ref,
                 carry_scr, state_scr, ar_scr, bk_scr, bkh_scr, v_scr, et_scr, y_scr,
                 pm_scr, arb_scr, av_scr, vk_scr, *, width, tb):
    n_pairs = width // PAIR
    n_chunks = tb // CHUNK
    stacked = 2 * CHUNK

    @pl.when(pl.program_id(1) == 0)
    def _():
        carry_scr[...] = jnp.zeros_like(carry_scr)
        state_scr[...] = jnp.zeros_like(state_scr)

    p = p_ref[...]
    row = lax.broadcasted_iota(jnp.int32, (tb, 1), 0)
    prev = jnp.where(row == 0, carry_scr[...], pltpu.roll(p, 1, axis=0))
    carry_scr[...] = p[tb - 1:tb, :]
    ps = p + (prev - p) * mu_ref[...]

    c = width
    r = ps[:, :c]
    k = ps[:, c:2 * c]
    v = ps[:, 2 * c:3 * c]
    o = 3 * c
    xw = ps[:, o:o + DECAY_LORA]
    xa = ps[:, o + DECAY_LORA:o + DECAY_LORA + AAA_LORA]
    xg = ps[:, o + DECAY_LORA + AAA_LORA:o + DECAY_LORA + AAA_LORA + GATE_LORA]

    z = w0_ref[...] + _dot_x3(jnp.tanh(xw), wd_ref[...])
    softplus_neg_z = jnp.maximum(-z, 0.0) + jnp.log(1.0 + jnp.exp(-jnp.abs(z)))
    lw = -jnp.exp(-softplus_neg_z - 0.5)
    a = _sigmoid(a0_ref[...] + _dot(xa.astype(BF16), wa_ref[...].astype(BF16)))
    g = _dot(_sigmoid(xg).astype(BF16), wg_ref[...].astype(BF16))

    li = lax.broadcasted_iota(jnp.int32, (PAIR, PAIR), 0)
    lj = lax.broadcasted_iota(jnp.int32, (PAIR, PAIR), 1)
    head_ones = ((li // HEAD_DIM) == (lj // HEAD_DIM)).astype(BF16)

    def head_sum(t):
        return jnp.concatenate(
            [_dot_hilo_lhs(t[:, q * PAIR:(q + 1) * PAIR], head_ones) for q in range(n_pairs)], axis=1)

    kk = k * kk_ref[...]
    kk = kk / jnp.maximum(jnp.sqrt(head_sum(kk * kk)), L2_EPS)
    k2 = k * (1.0 + (a - 1.0) * ka_ref[...])

    ti = lax.broadcasted_iota(jnp.int32, (tb, tb), 0)
    tj = lax.broadcasted_iota(jnp.int32, (tb, tb), 1)
    tri = jnp.logical_and((ti // CHUNK) == (tj // CHUNK), tj <= ti).astype(BF16)
    cum = _dot_f32_rhs(tri, lw)
    tot = jnp.concatenate(
        [jnp.broadcast_to(cum[(ci + 1) * CHUNK - 1:(ci + 1) * CHUNK, :], (CHUNK, c))
         for ci in range(n_chunks)], axis=0)

    e_neg = jnp.exp(-cum)
    e_rem = jnp.exp(tot - cum)
    kka = kk * a
    lane = lax.broadcasted_iota(jnp.int32, (1, PAIR), 1)
    first_head = lane < HEAD_DIM

    def put(scr, base, val):
        for q in range(n_pairs):
            vq = val[:, q * PAIR:(q + 1) * PAIR]
            h0 = jnp.where(first_head, vq, 0.0).astype(scr.dtype)
            h1 = jnp.where(first_head, 0.0, vq).astype(scr.dtype)
            for ci in range(n_chunks):
                rows = slice(ci * CHUNK, (ci + 1) * CHUNK)
                scr[q, ci, base:base + CHUNK, :] = h0[rows]
                scr[q, ci, base + CHUNK:base + stacked, :] = h1[rows]

    put(ar_scr, 0, -kk * jnp.exp(cum - lw))
    put(ar_scr, stacked, r * jnp.exp(cum))
    put(bk_scr, 0, kka * e_neg)
    put(bk_scr, stacked, k2 * e_neg)
    put(bkh_scr, 0, kka * e_rem)
    put(bkh_scr, stacked, k2 * e_rem)
    put(v_scr, 0, v)
    e_tot = jnp.exp(tot)
    for q in range(n_pairs):
        et_scr[q] = e_tot[:, q * PAIR:(q + 1) * PAIR]

    si = lax.broadcasted_iota(jnp.int32, (2 * stacked, 2 * stacked), 0)
    sj = lax.broadcasted_iota(jnp.int32, (2 * stacked, 2 * stacked), 1)
    keep = jnp.where(si < stacked, si, si - stacked + 1) > (sj % stacked)
    eye = (li == lj).astype(F32)

    chunks = range(n_chunks)

    def side(ta, tb):
        return jnp.concatenate([ta, tb], axis=1)

    def blockdiag(ta, tb):
        za = jnp.zeros((ta.shape[0], tb.shape[1]), ta.dtype)
        zb = jnp.zeros((tb.shape[0], ta.shape[1]), ta.dtype)
        return jnp.concatenate([side(ta, za), side(zb, tb)], axis=0)

    def halves(t):
        return t[:, :PAIR], t[:, PAIR:]

    def bf(t):
        return t.astype(BF16)

    def prepare(step):
        couples = [(step * PREP_PAIRS + 2 * j, ci) for j in range(PREP_PAIRS // 2) for ci in chunks]
        acc, pw, rhs = [], [], []
        for qa, ci in couples:
            n, tail, vm = [], [], []
            for q in (qa, qa + 1):
                sc = jnp.where(keep, _dot_nt(ar_scr[q, ci], bk_scr[q, ci]), 0.0)
                scb = bf(sc)
                arb_scr[q, ci] = scb[stacked:, :stacked]
                n.append(sc[:stacked, :stacked])
                tail.append(scb[:, stacked:])
                vm.append(v_scr[q, ci])
            av = _dot(side(*tail), blockdiag(bf(vm[0]), bf(vm[1])))
            vk = _dot(side(bf(vm[0].T), bf(vm[1].T)),
                      blockdiag(bkh_scr[qa, ci, stacked:, :], bkh_scr[qa + 1, ci, stacked:, :]))
            for q, t_av, t_vk in zip((qa, qa + 1), halves(av), halves(vk)):
                av_scr[q, ci] = t_av
                vk_scr[q, ci] = t_vk
            acc.append(side(eye + n[0], eye + n[1]))
            pw.append(side(bf(n[0]), bf(n[1])))
            rhs.append(blockdiag(bf(n[0]), bf(n[1])))
        pw = [bf(_dot(p_, r_)) for p_, r_ in zip(pw, rhs)]
        for it in range(1, 6):
            rhs = [blockdiag(*halves(p_)) for p_ in pw]
            if it < 5:
                both = [_dot(jnp.concatenate([p_, bf(a_)], axis=0), r_)
                        for p_, a_, r_ in zip(pw, acc, rhs)]
                pw = [bf(t[:stacked]) for t in both]
                acc = [a_ + t[stacked:] for a_, t in zip(acc, both)]
            else:
                acc = [a_ + _dot(bf(a_), r_) for a_, r_ in zip(acc, rhs)]
        for (qa, ci), a_ in zip(couples, acc):
            pm_scr[qa, ci], pm_scr[qa + 1, ci] = (bf(t) for t in halves(a_))

    for step in range(n_pairs // PREP_PAIRS):
        prepare(step)

    firsts = range(0, n_pairs, 2)
    s = [side(state_scr[q], state_scr[q + 1]) for q in firsts]
    for ci in chunks:
        rows = slice(ci * CHUNK, (ci + 1) * CHUNK)

        def both(scr, part=slice(None)):
            return [side(scr[q, ci, part, :], scr[q + 1, ci, part, :]) for q in firsts]

        top, bottom = slice(0, stacked), slice(stacked, 2 * stacked)
        sr = [_dot_nt(ar_, blockdiag(*halves(bf(s_)))) for ar_, s_ in zip(both(ar_scr), s)]
        u = [_dot(pm_, blockdiag(*halves(bf(sr_[:stacked] + av_))))
             for pm_, sr_, av_ in zip(both(pm_scr), sr, both(av_scr, top))]
        ym = [sr_[stacked:] + av_ + _dot(arb_, blockdiag(*halves(bf(u_))))
              for sr_, av_, arb_, u_ in zip(sr, both(av_scr, bottom), both(arb_scr), u)]
        for q, ym_ in zip(firsts, ym):
            ya, yb = halves(ym_[:CHUNK] + ym_[CHUNK:])
            y_scr[q, rows, :] = ya
            y_scr[q + 1, rows, :] = yb
        decay = [side(et_scr[q, ci * CHUNK:ci * CHUNK + 1, :], et_scr[q + 1, ci * CHUNK:ci * CHUNK + 1, :])
                 for q in firsts]
        s = [s_ * d_ + _dot(side(*(bf(t.T) for t in halves(u_))), blockdiag(*halves(bh_))) + vk_
             for s_, d_, u_, bh_, vk_ in zip(s, decay, u, both(bkh_scr, top), both(vk_scr))]
    for q, s_ in zip(firsts, s):
        state_scr[q], state_scr[q + 1] = halves(s_)

    y = jnp.concatenate([y_scr[q] for q in range(n_pairs)], axis=1)
    mean = head_sum(y) * (1.0 / HEAD_DIM)
    d = y - mean
    var = head_sum(d * d) * (1.0 / HEAD_DIM)
    yn = d * lax.rsqrt(var + GN_EPS) * lnw_ref[...] + lnb_ref[...]
    bonus = head_sum(r * k2 * rk_ref[...]) * v
    o_ref[...] = (yn + bonus) * g * beta_ref[...]


def _rwkv_call(p_r, vecs, wd, wa, wg, batch, seq, width, tb):
    pw = p_r.shape[1]
    n_pairs = width // PAIR
    n_chunks = tb // CHUNK
    tiles = seq // tb
    vec_specs = [pl.BlockSpec((1, v.shape[1]), lambda b, t: (0, 0)) for v in vecs]
    mat_specs = [pl.BlockSpec(m.shape, lambda b, t: (0, 0)) for m in (wd, wa, wg)]
    stacked2 = pltpu.VMEM((n_pairs, n_chunks, 4 * CHUNK, PAIR), BF16)
    return pl.pallas_call(
        functools.partial(_rwkv_kernel, width=width, tb=tb),
        grid=(batch, tiles),
        in_specs=[pl.BlockSpec((tb, pw), lambda b, t: (b * tiles + t, 0))] + vec_specs + mat_specs,
        out_specs=pl.BlockSpec((tb, width), lambda b, t: (b * tiles + t, 0)),
        out_shape=jax.ShapeDtypeStruct((batch * seq, width), F32),
        scratch_shapes=[pltpu.VMEM((1, pw), F32),
                        pltpu.VMEM((n_pairs, PAIR, PAIR), F32),
                        stacked2, stacked2, stacked2,
                        pltpu.VMEM((n_pairs, n_chunks, 2 * CHUNK, PAIR), F32),
                        pltpu.VMEM((n_pairs, tb, PAIR), F32),
                        pltpu.VMEM((n_pairs, tb, PAIR), F32),
                        pltpu.VMEM((n_pairs, n_chunks, 2 * CHUNK, PAIR), BF16),
                        pltpu.VMEM((n_pairs, n_chunks, 2 * CHUNK, PAIR), BF16),
                        pltpu.VMEM((n_pairs, n_chunks, 4 * CHUNK, PAIR), F32),
                        pltpu.VMEM((n_pairs, n_chunks, 2 * CHUNK, PAIR), F32)],
        compiler_params=pltpu.CompilerParams(
            dimension_semantics=("parallel", "arbitrary"), vmem_limit_bytes=VMEM_LIMIT),
    )(p_r, *vecs, wd, wa, wg)


def _attn_kernel(q_ref, kp_ref, kc_ref, vp_ref, vc_ref, bias_ref, beta_ref, o_ref, kwin, vwin,
                 *, width, qb):
    i = pl.program_id(1)
    n_pairs = width // PAIR
    kwin[0:qb, :] = kp_ref[...]
    kwin[qb:2 * qb, :] = kc_ref[...]
    vwin[0:qb, :] = vp_ref[...]
    vwin[qb:2 * qb, :] = vc_ref[...]
    lane = lax.broadcasted_iota(jnp.int32, (1, PAIR), 1)
    first_head = lane < HEAD_DIM
    rows = ATTN_BLOCK_CHUNKS * CHUNK
    win = BAND + rows - CHUNK
    col = lax.broadcasted_iota(jnp.int32, (1, win), 1)
    scale = HEAD_DIM ** -0.5
    left = LEFT_CHUNKS * CHUNK

    for q in range(n_pairs):
        lanes = slice(q * PAIR, (q + 1) * PAIR)
        for blk in range(qb // rows):
            r0 = blk * rows
            qc = q_ref[r0:r0 + rows, lanes]
            zero = jnp.zeros_like(qc)
            qm = jnp.concatenate([jnp.where(first_head, qc, zero), jnp.where(first_head, zero, qc)],
                                 axis=0)
            s = _dot_nt(qm, kwin[r0:r0 + win, lanes]) * scale + bias_ref[q]
            valid = jnp.logical_or(i > 0, col + r0 >= left)
            s = jnp.where(valid, s, NEG_INF)
            e = jnp.exp(s - jnp.max(s, axis=-1, keepdims=True))
            denom = jnp.sum(e, axis=-1, keepdims=True)
            o = _dot(e.astype(BF16), vwin[r0:r0 + win, lanes]) / denom
            oc = jnp.where(first_head, o[:rows], o[rows:])
            o_ref[r0:r0 + rows, lanes] = oc * beta_ref[:, lanes]


def _attn_call(p_a, bias, beta, batch, seq, width, qb):
    assert qb == LEFT_CHUNKS * CHUNK, "key window = previous block + current block"
    tiles = seq // qb
    n_pairs = width // PAIR

    def cur(col):
        return pl.BlockSpec((qb, width), lambda b, t: (b * tiles + t, col))

    def prev(col):
        return pl.BlockSpec((qb, width), lambda b, t: (b * tiles + jnp.maximum(t - 1, 0), col))

    return pl.pallas_call(
        functools.partial(_attn_kernel, width=width, qb=qb),
        grid=(batch, tiles),
        in_specs=[cur(0), prev(1), cur(1), prev(2), cur(2),
                  pl.BlockSpec(bias.shape, lambda b, t: (0, 0, 0)),
                  pl.BlockSpec((1, width), lambda b, t: (0, 0))],
        out_specs=pl.BlockSpec((qb, width), lambda b, t: (b * tiles + t, 0)),
        out_shape=jax.ShapeDtypeStruct((batch * seq, width), F32),
        scratch_shapes=[pltpu.VMEM((2 * qb, width), BF16), pltpu.VMEM((2 * qb, width), BF16)],
        compiler_params=pltpu.CompilerParams(
            dimension_semantics=("parallel", "arbitrary"), vmem_limit_bytes=VMEM_LIMIT),
    )(p_a, p_a, p_a, p_a, p_a, bias, beta)


def _out_router_kernel(yr_ref, ya_ref, wo_ref, x_ref, mod_ref, g2_ref, wr_ref, br_ref,
                       x1_ref, h2_ref, route_ref, *, r_width):
    y = (_dot(yr_ref[...].astype(BF16), wo_ref[0:r_width, :])
         + _dot(ya_ref[...].astype(BF16), wo_ref[r_width:, :]))
    m = mod_ref[0]
    x1 = x_ref[...] + m[GATE1:GATE1 + 1] * y
    x1_ref[...] = x1
    h2 = _rms(x1) * g2_ref[...] * (1.0 + m[SCALE2:SCALE2 + 1]) + m[SHIFT2:SHIFT2 + 1]
    h2_ref[...] = h2

    logits = _dot_x3(h2, wr_ref[...]) + br_ref[...]
    li = lax.broadcasted_iota(jnp.int32, logits.shape, 1).astype(F32)
    none = jnp.float32(-jnp.inf)
    far = jnp.float32(LANES)
    is_group = li < N_GROUPS
    gl = jnp.where(is_group, logits, none)
    gmax = jnp.max(gl, axis=-1, keepdims=True)
    gidx = jnp.min(jnp.where(gl == gmax, li, far), axis=-1, keepdims=True)
    g_w = 1.0 / jnp.sum(jnp.where(is_group, jnp.exp(logits - gmax), 0.0), axis=-1, keepdims=True)
    lo = N_GROUPS + EXPERTS_PER_GROUP * gidx
    el = jnp.where(jnp.logical_and(li >= lo, li < lo + EXPERTS_PER_GROUP), logits, none)
    m1 = jnp.max(el, axis=-1, keepdims=True)
    i1 = jnp.min(jnp.where(el == m1, li, far), axis=-1, keepdims=True)
    el2 = jnp.where(li == i1, none, el)
    m2 = jnp.max(el2, axis=-1, keepdims=True)
    i2 = jnp.min(jnp.where(el2 == m2, li, far), axis=-1, keepdims=True)
    e2 = jnp.exp(m2 - m1)
    w1 = g_w / (1.0 + e2)
    w2 = g_w * e2 / (1.0 + e2)
    route = jnp.where(li == 0.0, i1 - N_GROUPS, 0.0) + jnp.where(li == 1.0, i2 - N_GROUPS, 0.0)
    route_ref[...] = route + jnp.where(li == 2.0, w1, 0.0) + jnp.where(li == 3.0, w2, 0.0)


def _out_router_call(y_r, y_a, wo, x2d, mod, g2, wr, br, seq, tm):
    rows, d = x2d.shape
    r_width = y_r.shape[1]
    a_width = y_a.shape[1]
    tiles_per_seq = seq // tm
    return pl.pallas_call(
        functools.partial(_out_router_kernel, r_width=r_width),
        grid=(rows // tm,),
        in_specs=[pl.BlockSpec((tm, r_width), lambda i: (i, 0)),
                  pl.BlockSpec((tm, a_width), lambda i: (i, 0)),
                  pl.BlockSpec(wo.shape, lambda i: (0, 0)),
                  pl.BlockSpec((tm, d), lambda i: (i, 0)),
                  pl.BlockSpec((1, 6, d), lambda i: (i // tiles_per_seq, 0, 0)),
                  pl.BlockSpec((1, d), lambda i: (0, 0)),
                  pl.BlockSpec((d, LANES), lambda i: (0, 0)),
                  pl.BlockSpec((1, LANES), lambda i: (0, 0))],
        out_specs=[pl.BlockSpec((tm, d), lambda i: (i, 0)),
                   pl.BlockSpec((tm, d), lambda i: (i, 0)),
                   pl.BlockSpec((tm, LANES), lambda i: (i, 0))],
        out_shape=[jax.ShapeDtypeStruct((rows, d), F32),
                   jax.ShapeDtypeStruct((rows, d), F32),
                   jax.ShapeDtypeStruct((rows, LANES), F32)],
        compiler_params=pltpu.CompilerParams(
            dimension_semantics=("parallel",), vmem_limit_bytes=VMEM_LIMIT),
    )(y_r, y_a, wo, x2d, mod, g2, wr, br)


def _moe_kernel(texp_ref, first_ref, land_ref, ahead1_ref, ahead2_ref, meta_ref, src_ref,
                h2_hbm, wg_hbm, wu_hbm, wd_hbm, o_ref,
                xbuf, xsem, wg_land, wu_land, wd_land, wsem, wg_bf, wu_bf, wd_bf, *, tm):
    t = pl.program_id(0)
    n_used = meta_ref[0]
    slot = lax.rem(t, MOE_GATHER_SLOTS)

    def weight_copies(e, ls):
        cps = []
        for m, (hbm, land) in enumerate(((wg_hbm, wg_land), (wu_hbm, wu_land), (wd_hbm, wd_land))):
            rows = hbm.shape[1] // W_CHUNKS
            for ck in range(W_CHUNKS):
                part = pl.ds(ck * rows, rows)
                cps.append(pltpu.make_async_copy(hbm.at[e, part], land.at[ls, part],
                                                 wsem.at[ls, m, ck]))
        return cps

    def gather_rows(tile):
        buf_slot = lax.rem(tile, MOE_GATHER_SLOTS)

        def body(i, carry):
            tok = src_ref[tile * tm + i]
            pltpu.make_async_copy(h2_hbm.at[pl.ds(tok, 1)], xbuf.at[buf_slot, pl.ds(i, 1)],
                                  xsem.at[buf_slot]).start()
            return carry

        lax.fori_loop(0, tm, body, 0, unroll=8)

    @pl.when(t == 0)
    def _():
        for cp in weight_copies(texp_ref[0], 0):
            cp.start()

        @pl.when(ahead1_ref[0] >= 0)
        def _():
            for cp in weight_copies(ahead1_ref[0], 1):
                cp.start()

        for ahead in range(MOE_GATHER_SLOTS - 1):
            @pl.when(ahead < n_used)
            def _():
                gather_rows(ahead)

    is_first = jnp.logical_and(t < n_used, first_ref[t] == 1)

    @pl.when(is_first)
    def _():
        ls = land_ref[t]
        for cp in weight_copies(texp_ref[t], ls):
            cp.wait()
        wg_bf[...] = wg_land[ls].astype(BF16)
        wu_bf[...] = wu_land[ls].astype(BF16)
        wd_bf[...] = wd_land[ls].astype(BF16)

    @pl.when(t + MOE_GATHER_SLOTS - 1 < n_used)
    def _():
        gather_rows(t + MOE_GATHER_SLOTS - 1)

    @pl.when(jnp.logical_and(is_first, ahead2_ref[t] >= 0))
    def _():
        for cp in weight_copies(ahead2_ref[t], land_ref[t]):
            cp.start()

    @pl.when(t < n_used)
    def _():
        pltpu.make_async_copy(h2_hbm.at[pl.ds(0, tm)], xbuf.at[slot], xsem.at[slot]).wait()
        x = xbuf[slot].astype(BF16)
        gate = _dot(x, wg_bf[...])
        up = _dot(x, wu_bf[...])
        h = gate * _sigmoid(gate) * up
        o_ref[...] = _dot(h.astype(BF16), wd_bf[...])

    @pl.when(t >= n_used)
    def _():
        o_ref[...] = jnp.zeros_like(o_ref)


def _moe_call(plan, src, h2, wg, wu, wd, tm):
    n_e, d, f = wg.shape
    n_tiles = plan[0].shape[0]
    lands = 2
    grid_spec = pltpu.PrefetchScalarGridSpec(
        num_scalar_prefetch=len(plan) + 1,
        grid=(n_tiles,),
        in_specs=[pl.BlockSpec(memory_space=pl.ANY)] * 4,
        out_specs=pl.BlockSpec((tm, d), lambda t, *_: (t, 0)),
        scratch_shapes=[pltpu.VMEM((MOE_GATHER_SLOTS, tm, d), F32),
                        pltpu.SemaphoreType.DMA((MOE_GATHER_SLOTS,)),
                        pltpu.VMEM((lands, d, f), F32), pltpu.VMEM((lands, d, f), F32),
                        pltpu.VMEM((lands, f, d), F32),
                        pltpu.SemaphoreType.DMA((lands, 3, W_CHUNKS)),
                        pltpu.VMEM((d, f), BF16), pltpu.VMEM((d, f), BF16), pltpu.VMEM((f, d), BF16)])
    return pl.pallas_call(
        functools.partial(_moe_kernel, tm=tm),
        grid_spec=grid_spec,
        out_shape=jax.ShapeDtypeStruct((n_tiles * tm, d), F32),
        compiler_params=pltpu.CompilerParams(
            dimension_semantics=("arbitrary",), vmem_limit_bytes=VMEM_LIMIT),
    )(*plan, src, h2, wg, wu, wd)


def _final_kernel(pos_ref, ys_hbm, route_ref, x1_ref, mod_ref, gf_ref, o_ref, ybuf, sem, *, tm):
    i = pl.program_id(0)
    n = pl.num_programs(0)
    slot = i % 2

    def gather_rows(tile, buf_slot):
        base = tile * tm * 2

        def body(r, carry):
            for s in range(2):
                pltpu.make_async_copy(ys_hbm.at[pl.ds(pos_ref[base + 2 * r + s], 1)],
                                      ybuf.at[buf_slot, s, pl.ds(r, 1)], sem.at[buf_slot]).start()
            return carry

        lax.fori_loop(0, tm, body, 0, unroll=4)

    @pl.when(i == 0)
    def _():
        gather_rows(0, 0)

    @pl.when(i + 1 < n)
    def _():
        gather_rows(i + 1, 1 - slot)

    for s in range(2):
        pltpu.make_async_copy(ys_hbm.at[pl.ds(0, tm)], ybuf.at[slot, s], sem.at[slot]).wait()
    route = route_ref[...]
    w0 = route[:, 2:3]
    w1 = route[:, 3:4]
    f = w0 * ybuf[slot, 0] + w1 * ybuf[slot, 1]
    x2 = x1_ref[...] + mod_ref[0][GATE2:GATE2 + 1] * f
    o_ref[...] = _rms(x2) * gf_ref[...]


def _final_call(pos, ys, route, x1, mod, gf, seq, tm):
    rows, d = x1.shape
    tiles_per_seq = seq // tm
    grid_spec = pltpu.PrefetchScalarGridSpec(
        num_scalar_prefetch=1,
        grid=(rows // tm,),
        in_specs=[pl.BlockSpec(memory_space=pl.ANY),
                  pl.BlockSpec((tm, LANES), lambda i, pos: (i, 0)),
                  pl.BlockSpec((tm, d), lambda i, pos: (i, 0)),
                  pl.BlockSpec((1, 6, d), lambda i, pos: (i // tiles_per_seq, 0, 0)),
                  pl.BlockSpec((1, d), lambda i, pos: (0, 0))],
        out_specs=pl.BlockSpec((tm, d), lambda i, pos: (i, 0)),
        scratch_shapes=[pltpu.VMEM((2, 2, tm, d), F32), pltpu.SemaphoreType.DMA((2,))])
    return pl.pallas_call(
        functools.partial(_final_kernel, tm=tm),
        grid_spec=grid_spec,
        out_shape=jax.ShapeDtypeStruct((rows, d), F32),
        compiler_params=pltpu.CompilerParams(
            dimension_semantics=("arbitrary",), vmem_limit_bytes=VMEM_LIMIT),
    )(pos, ys, route, x1, mod, gf)


def _route_plan(route, tm):
    n = route.shape[0]
    flat_e = route[:, 0:2].astype(jnp.int32).reshape(-1)
    n_tiles = (2 * n + N_EXPERTS * (tm - 1)) // tm
    experts = jnp.arange(N_EXPERTS, dtype=jnp.int32)
    onehot = (flat_e[:, None] == experts[None, :]).astype(jnp.int32)
    csum = jnp.cumsum(onehot, axis=0)
    rank = jnp.sum(csum * onehot, axis=1) - 1
    counts = csum[-1]
    tiles_per_e = (counts + tm - 1) // tm
    tile_end = jnp.cumsum(tiles_per_e)
    tile_start = tile_end - tiles_per_e
    n_used = tile_end[-1]
    pos = jnp.sum(onehot * tile_start[None, :], axis=1) * tm + rank
    token = jnp.arange(2 * n, dtype=jnp.int32) // 2
    src = jnp.zeros((n_tiles * tm,), jnp.int32).at[pos].set(token)
    tiles = jnp.arange(n_tiles, dtype=jnp.int32)
    texp = jnp.sum((tiles[:, None] >= tile_end[None, :]).astype(jnp.int32), axis=1)
    texp = jnp.where(tiles < n_used, texp, texp[n_used - 1])
    first = jnp.concatenate([jnp.ones((1,), jnp.int32), (texp[1:] != texp[:-1]).astype(jnp.int32)])
    later_used = jnp.logical_and(experts[None, :] > experts[:, None], (tiles_per_e > 0)[None, :])
    next_used = jnp.min(jnp.where(later_used, experts[None, :], N_EXPERTS), axis=1)
    next_used = jnp.where(next_used == N_EXPERTS, -1, next_used)
    ahead1 = next_used[texp]
    ahead2 = jnp.where(ahead1 >= 0, next_used[jnp.maximum(ahead1, 0)], -1)
    land = (jnp.cumsum((tiles_per_e > 0).astype(jnp.int32)) - 1)[texp] % 2
    return (texp, first, land, ahead1, ahead2, n_used.reshape(1)), src, pos


def _round_up(n, m):
    return (n + m - 1) // m * m


def kernel(x, c, w_ada, b_ada, norm1_g, w_in, mu_shift, w0, w_decay_up, a0, w_aaa_up, w_gate_up,
           k_k, k_a, r_k, ln_x_w, ln_x_b, rel_bias, beta_rwkv, beta_attn, w_out, norm2_g, w_group,
           b_group, w_expert, b_expert, w_gate, w_up, w_down, norm_f_g):
    assert w_ada.shape[0] == 1, "single trunk layer"
    batch, seq, d = x.shape
    r_width = w0.shape[1]
    a_width = beta_attn.shape[1]
    shift_width = mu_shift.shape[1]
    assert shift_width == 3 * r_width + DECAY_LORA + AAA_LORA + GATE_LORA
    x2d = x.reshape(batch * seq, d)
    row = lambda t: t.reshape(1, -1)

    c8 = jnp.pad(c, ((0, 8 - batch), (0, 0)))
    mod = _ada_call(c8, w_ada[0], row(b_ada[0]), tn=1536)[:batch].reshape(batch, 6, d)

    tm = min(512, seq)
    pw = _round_up(shift_width, 3 * LANES)
    h1 = _hmod_call(x2d, row(norm1_g[0]), mod, tm, seq)
    tm_p = min(1024, seq)
    w_in_bf = w_in[0].astype(BF16)
    p_r = _matmul_call(h1, w_in_bf, pw, pw // 3, tm_p, F32)
    p_a = _matmul_call(h1, w_in_bf[:, shift_width:], 3 * a_width, a_width, tm_p, BF16)

    mu = jnp.pad(mu_shift[0], (0, pw - shift_width))
    vecs = [row(mu), row(w0[0]), row(a0[0]), row(k_k[0]), row(k_a[0]), row(r_k[0]), row(ln_x_w[0]),
            row(ln_x_b[0]), row(beta_rwkv[0])]
    y_r = _rwkv_call(p_r, vecs, w_decay_up[0], w_aaa_up[0], w_gate_up[0], batch, seq, r_width,
                     tb=min(256, seq))

    left = LEFT_CHUNKS * CHUNK
    dist = left + CHUNK - 1 - jnp.arange(BAND + CHUNK - 1)
    base = rel_bias[0][:, jnp.clip(dist, -REL_CLIP, REL_CLIP) + REL_CLIP].astype(F32)
    bias = jnp.stack([base[:, CHUNK - 1 - qi:CHUNK - 1 - qi + BAND] for qi in range(CHUNK)], axis=1)
    nb = ATTN_BLOCK_CHUNKS
    bias = jnp.concatenate(
        [jnp.pad(bias, ((0, 0), (0, 0), (cb * CHUNK, (nb - 1 - cb) * CHUNK)), constant_values=NEG_INF)
         for cb in range(nb)], axis=1)
    bias = bias.reshape(a_width // PAIR, 2 * nb * CHUNK, BAND + (nb - 1) * CHUNK)
    y_a = _attn_call(p_a, bias, row(beta_attn[0]), batch, seq, a_width, qb=min(512, seq))

    n_route = N_GROUPS + N_EXPERTS
    wr = jnp.concatenate([w_group[0], w_expert[0].transpose(1, 0, 2).reshape(d, N_EXPERTS)], axis=1)
    wr = jnp.pad(wr, ((0, 0), (0, LANES - n_route)))
    br = jnp.pad(jnp.concatenate([b_group[0], b_expert[0].reshape(-1)]), (0, LANES - n_route))
    x1, h2, route = _out_router_call(y_r, y_a, w_out[0].astype(BF16), x2d, mod, row(norm2_g[0]),
                                     wr, row(br), seq, tm)

    f = w_gate.shape[-1]
    tm_e = 256
    plan, src, pos = _route_plan(route, tm_e)
    ys = _moe_call(plan, src, h2, w_gate[0].reshape(N_EXPERTS, d, f),
                   w_up[0].reshape(N_EXPERTS, d, f), w_down[0].reshape(N_EXPERTS, f, d), tm_e)
    out = _final_call(pos, ys, route, x1, mod, row(norm_f_g), seq, min(256, seq))
    return out.reshape(batch, seq, d)
```

```python
import functools

import jax
import jax.numpy as jnp
from jax import lax
from jax.experimental import pallas as pl
from jax.experimental.pallas import tpu as pltpu

F32 = jnp.float32
BF16 = jnp.bfloat16
HI = lax.Precision.HIGHEST

LANES = 128
HEAD_DIM = 64
PAIR = 2 * HEAD_DIM
CHUNK = 64
LEFT_CHUNKS = 8
BAND = (LEFT_CHUNKS + 1) * CHUNK
ATTN_BLOCK_CHUNKS = 4
PREP_PAIRS = 4
MOE_GATHER_SLOTS = 3
W_CHUNKS = 2
ROW_GROUP = 8
REL_CLIP = 128
DECAY_LORA = 64
AAA_LORA = 64
GATE_LORA = 160
N_GROUPS = 4
EXPERTS_PER_GROUP = 8
N_EXPERTS = N_GROUPS * EXPERTS_PER_GROUP
RMS_EPS = 1e-6
GN_EPS = 64e-5
L2_EPS = 1e-12
NEG_INF = -1e30
VMEM_LIMIT = 56 * 1024 * 1024

SHIFT1, SCALE1, GATE1, SHIFT2, SCALE2, GATE2 = range(6)


def _dot(a, b, precision=None):
    return jnp.dot(a, b, precision=precision, preferred_element_type=F32)


def _dot_nt(a, b, precision=None):
    return lax.dot_general(a, b, (((1,), (1,)), ((), ())), precision=precision,
                           preferred_element_type=F32)


def _sigmoid(x):
    return 1.0 / (1.0 + jnp.exp(-x))


def _rms(x):
    return x * lax.rsqrt(jnp.mean(x * x, axis=-1, keepdims=True) + RMS_EPS)


def _ada_kernel(c_ref, w_ref, b_ref, o_ref):
    c = c_ref[...]
    o_ref[...] = _dot_x3(c * _sigmoid(c), w_ref[...]) + b_ref[...]


def _ada_call(c8, w, b, tn):
    rows, d = c8.shape
    n = w.shape[1]
    return pl.pallas_call(
        _ada_kernel,
        grid=(n // tn,),
        in_specs=[pl.BlockSpec((rows, d), lambda j: (0, 0)),
                  pl.BlockSpec((d, tn), lambda j: (0, j)),
                  pl.BlockSpec((1, tn), lambda j: (0, j))],
        out_specs=pl.BlockSpec((rows, tn), lambda j: (0, j)),
        out_shape=jax.ShapeDtypeStruct((rows, n), F32),
        compiler_params=pltpu.CompilerParams(vmem_limit_bytes=VMEM_LIMIT),
    )(c8, w, b)


def _hmod_kernel(x_ref, g_ref, mod_ref, o_ref):
    m = mod_ref[0]
    h = _rms(x_ref[...]) * g_ref[...] * (1.0 + m[SCALE1:SCALE1 + 1]) + m[SHIFT1:SHIFT1 + 1]
    o_ref[...] = h.astype(BF16)


def _hmod_call(x2d, g, mod, tm, seq):
    rows, d = x2d.shape
    tiles_per_seq = seq // tm
    return pl.pallas_call(
        _hmod_kernel,
        grid=(rows // tm,),
        in_specs=[pl.BlockSpec((tm, d), lambda i: (i, 0)),
                  pl.BlockSpec((1, d), lambda i: (0, 0)),
                  pl.BlockSpec((1, 6, d), lambda i: (i // tiles_per_seq, 0, 0))],
        out_specs=pl.BlockSpec((tm, d), lambda i: (i, 0)),
        out_shape=jax.ShapeDtypeStruct((rows, d), BF16),
        compiler_params=pltpu.CompilerParams(
            dimension_semantics=("parallel",), vmem_limit_bytes=VMEM_LIMIT),
    )(x2d, g, mod)


def _matmul_kernel(a_ref, w_ref, o_ref):
    o_ref[...] = _dot(a_ref[...], w_ref[...]).astype(o_ref.dtype)


def _matmul_call(a, w, n_out, tn, tm, out_dtype):
    rows, d = a.shape
    return pl.pallas_call(
        _matmul_kernel,
        grid=(n_out // tn, rows // tm),
        in_specs=[pl.BlockSpec((tm, d), lambda j, i: (i, 0)),
                  pl.BlockSpec((d, tn), lambda j, i: (0, j))],
        out_specs=pl.BlockSpec((tm, tn), lambda j, i: (i, j)),
        out_shape=jax.ShapeDtypeStruct((rows, n_out), out_dtype),
        compiler_params=pltpu.CompilerParams(
            dimension_semantics=("parallel", "parallel"), vmem_limit_bytes=VMEM_LIMIT),
    )(a, w)


def _split3(t):
    hi = t.astype(BF16)
    rest = t - hi.astype(F32)
    mid = rest.astype(BF16)
    lo = (rest - mid.astype(F32)).astype(BF16)
    return hi, mid, lo


def _dot_hilo_lhs(t, m):
    hi = t.astype(BF16)
    lo = (t - hi.astype(F32)).astype(BF16)
    return _dot(lo, m) + _dot(hi, m)


def _dot_f32_rhs(m, t):
    hi, mid, lo = _split3(t)
    return _dot(m, lo) + _dot(m, mid) + _dot(m, hi)


def _dot_x3(a, b):
    ah = a.astype(BF16)
    al = (a - ah.astype(F32)).astype(BF16)
    bh = b.astype(BF16)
    bl = (b - bh.astype(F32)).astype(BF16)
    return _dot(al, bh) + _dot(ah, bl) + _dot(ah, bh)


def _rwkv_kernel(p_ref, mu_ref, w0_ref, a0_ref, kk_ref, ka_ref, rk_ref, lnw_ref, lnb_ref, beta_ref,
                 wd_ref, wa_ref, wg_ref, o_ref,
                 carry_scr, state_scr, ar_scr, bk_scr, bkh_scr, v_scr, et_scr, y_scr,
                 pm_scr, arb_scr, av_scr, vk_scr, *, width, tb):
    n_pairs = width // PAIR
    n_chunks = tb // CHUNK
    stacked = 2 * CHUNK

    @pl.when(pl.program_id(1) == 0)
    def _():
        carry_scr[...] = jnp.zeros_like(carry_scr)
        state_scr[...] = jnp.zeros_like(state_scr)

    p = p_ref[...]
    row = lax.broadcasted_iota(jnp.int32, (tb, 1), 0)
    prev = jnp.where(row == 0, carry_scr[...], pltpu.roll(p, 1, axis=0))
    carry_scr[...] = p[tb - 1:tb, :]
    ps = p + (prev - p) * mu_ref[...]

    c = width
    r = ps[:, :c]
    k = ps[:, c:2 * c]
    v = ps[:, 2 * c:3 * c]
    o = 3 * c
    xw = ps[:, o:o + DECAY_LORA]
    xa = ps[:, o + DECAY_LORA:o + DECAY_LORA + AAA_LORA]
    xg = ps[:, o + DECAY_LORA + AAA_LORA:o + DECAY_LORA + AAA_LORA + GATE_LORA]

    z = w0_ref[...] + _dot_x3(jnp.tanh(xw), wd_ref[...])
    softplus_neg_z = jnp.maximum(-z, 0.0) + jnp.log(1.0 + jnp.exp(-jnp.abs(z)))
    lw = -jnp.exp(-softplus_neg_z - 0.5)
    a = _sigmoid(a0_ref[...] + _dot(xa.astype(BF16), wa_ref[...].astype(BF16)))
    g = _dot(_sigmoid(xg).astype(BF16), wg_ref[...].astype(BF16))

    li = lax.broadcasted_iota(jnp.int32, (PAIR, PAIR), 0)
    lj = lax.broadcasted_iota(jnp.int32, (PAIR, PAIR), 1)
    head_ones = ((li // HEAD_DIM) == (lj // HEAD_DIM)).astype(BF16)

    def head_sum(t):
        return jnp.concatenate(
            [_dot_hilo_lhs(t[:, q * PAIR:(q + 1) * PAIR], head_ones) for q in range(n_pairs)], axis=1)

    kk = k * kk_ref[...]
    kk = kk / jnp.maximum(jnp.sqrt(head_sum(kk * kk)), L2_EPS)
    k2 = k * (1.0 + (a - 1.0) * ka_ref[...])

    ti = lax.broadcasted_iota(jnp.int32, (tb, tb), 0)
    tj = lax.broadcasted_iota(jnp.int32, (tb, tb), 1)
    tri = jnp.logical_and((ti // CHUNK) == (tj // CHUNK), tj <= ti).astype(BF16)
    cum = _dot_f32_rhs(tri, lw)
    tot = jnp.concatenate(
        [jnp.broadcast_to(cum[(ci + 1) * CHUNK - 1:(ci + 1) * CHUNK, :], (CHUNK, c))
         for ci in range(n_chunks)], axis=0)

    e_neg = jnp.exp(-cum)
    e_rem = jnp.exp(tot - cum)
    kka = kk * a
    lane = lax.broadcasted_iota(jnp.int32, (1, PAIR), 1)
    first_head = lane < HEAD_DIM

    def put(scr, base, val):
        for q in range(n_pairs):
            vq = val[:, q * PAIR:(q + 1) * PAIR]
            h0 = jnp.where(first_head, vq, 0.0).astype(scr.dtype)
            h1 = jnp.where(first_head, 0.0, vq).astype(scr.dtype)
            for ci in range(n_chunks):
                rows = slice(ci * CHUNK, (ci + 1) * CHUNK)
                scr[q, ci, base:base + CHUNK, :] = h0[rows]
                scr[q, ci, base + CHUNK:base + stacked, :] = h1[rows]

    put(ar_scr, 0, -kk * jnp.exp(cum - lw))
    put(ar_scr, stacked, r * jnp.exp(cum))
    put(bk_scr, 0, kka * e_neg)
    put(bk_scr, stacked, k2 * e_neg)
    put(bkh_scr, 0, kka * e_rem)
    put(bkh_scr, stacked, k2 * e_rem)
    put(v_scr, 0, v)
    e_tot = jnp.exp(tot)
    for q in range(n_pairs):
        et_scr[q] = e_tot[:, q * PAIR:(q + 1) * PAIR]

    si = lax.broadcasted_iota(jnp.int32, (2 * stacked, 2 * stacked), 0)
    sj = lax.broadcasted_iota(jnp.int32, (2 * stacked, 2 * stacked), 1)
    keep = jnp.where(si < stacked, si, si - stacked + 1) > (sj % stacked)
    eye = (li == lj).astype(F32)

    chunks = range(n_chunks)

    def side(ta, tb):
        return jnp.concatenate([ta, tb], axis=1)

    def blockdiag(ta, tb):
        za = jnp.zeros((ta.shape[0], tb.shape[1]), ta.dtype)
        zb = jnp.zeros((tb.shape[0], ta.shape[1]), ta.dtype)
        return jnp.concatenate([side(ta, za), side(zb, tb)], axis=0)

    def halves(t):
        return t[:, :PAIR], t[:, PAIR:]

    def bf(t):
        return t.astype(BF16)

    def prepare(step):
        couples = [(step * PREP_PAIRS + 2 * j, ci) for j in range(PREP_PAIRS // 2) for ci in chunks]
        acc, pw, rhs = [], [], []
        for qa, ci in couples:
            n, tail, vm = [], [], []
            for q in (qa, qa + 1):
                sc = jnp.where(keep, _dot_nt(ar_scr[q, ci], bk_scr[q, ci]), 0.0)
                scb = bf(sc)
                arb_scr[q, ci] = scb[stacked:, :stacked]
                n.append(sc[:stacked, :stacked])
                tail.append(scb[:, stacked:])
                vm.append(v_scr[q, ci])
            av = _dot(side(*tail), blockdiag(bf(vm[0]), bf(vm[1])))
            vk = _dot(side(bf(vm[0].T), bf(vm[1].T)),
                      blockdiag(bkh_scr[qa, ci, stacked:, :], bkh_scr[qa + 1, ci, stacked:, :]))
            for q, t_av, t_vk in zip((qa, qa + 1), halves(av), halves(vk)):
                av_scr[q, ci] = t_av
                vk_scr[q, ci] = t_vk
            acc.append(side(eye + n[0], eye + n[1]))
            pw.append(side(bf(n[0]), bf(n[1])))
            rhs.append(blockdiag(bf(n[0]), bf(n[1])))
        pw = [bf(_dot(p_, r_)) for p_, r_ in zip(pw, rhs)]
        for it in range(1, 6):
            rhs = [blockdiag(*halves(p_)) for p_ in pw]
            if it < 5:
                both = [_dot(jnp.concatenate([p_, bf(a_)], axis=0), r_)
                        for p_, a_, r_ in zip(pw, acc, rhs)]
                pw = [bf(t[:stacked]) for t in both]
                acc = [a_ + t[stacked:] for a_, t in zip(acc, both)]
            else:
                acc = [a_ + _dot(bf(a_), r_) for a_, r_ in zip(acc, rhs)]
        for (qa, ci), a_ in zip(couples, acc):
            pm_scr[qa, ci], pm_scr[qa + 1, ci] = (bf(t) for t in halves(a_))

    for step in range(n_pairs // PREP_PAIRS):
        prepare(step)

    firsts = range(0, n_pairs, 2)
    s = [side(state_scr[q], state_scr[q + 1]) for q in firsts]
    for ci in chunks:
        rows = slice(ci * CHUNK, (ci + 1) * CHUNK)

        def both(scr, part=slice(None)):
            return [side(scr[q, ci, part, :], scr[q + 1, ci, part, :]) for q in firsts]

        top, bottom = slice(0, stacked), slice(stacked, 2 * stacked)
        sr = [_dot_nt(ar_, blockdiag(*halves(bf(s_)))) for ar_, s_ in zip(both(ar_scr), s)]
        u = [_dot(pm_, blockdiag(*halves(bf(sr_[:stacked] + av_))))
             for pm_, sr_, av_ in zip(both(pm_scr), sr, both(av_scr, top))]
        ym = [sr_[stacked:] + av_ + _dot(arb_, blockdiag(*halves(bf(u_))))
              for sr_, av_, arb_, u_ in zip(sr, both(av_scr, bottom), both(arb_scr), u)]
        for q, ym_ in zip(firsts, ym):
            ya, yb = halves(ym_[:CHUNK] + ym_[CHUNK:])
            y_scr[q, rows, :] = ya
            y_scr[q + 1, rows, :] = yb
        decay = [side(et_scr[q, ci * CHUNK:ci * CHUNK + 1, :], et_scr[q + 1, ci * CHUNK:ci * CHUNK + 1, :])
                 for q in firsts]
        s = [s_ * d_ + _dot(side(*(bf(t.T) for t in halves(u_))), blockdiag(*halves(bh_))) + vk_
             for s_, d_, u_, bh_, vk_ in zip(s, decay, u, both(bkh_scr, top), both(vk_scr))]
    for q, s_ in zip(firsts, s):
        state_scr[q], state_scr[q + 1] = halves(s_)

    y = jnp.concatenate([y_scr[q] for q in range(n_pairs)], axis=1)
    mean = head_sum(y) * (1.0 / HEAD_DIM)
    d = y - mean
    var = head_sum(d * d) * (1.0 / HEAD_DIM)
    yn = d * lax.rsqrt(var + GN_EPS) * lnw_ref[...] + lnb_ref[...]
    bonus = head_sum(r * k2 * rk_ref[...]) * v
    o_ref[...] = (yn + bonus) * g * beta_ref[...]


def _rwkv_call(p_r, vecs, wd, wa, wg, batch, seq, width, tb):
    pw = p_r.shape[1]
    n_pairs = width // PAIR
    n_chunks = tb // CHUNK
    tiles = seq // tb
    vec_specs = [pl.BlockSpec((1, v.shape[1]), lambda b, t: (0, 0)) for v in vecs]
    mat_specs = [pl.BlockSpec(m.shape, lambda b, t: (0, 0)) for m in (wd, wa, wg)]
    stacked2 = pltpu.VMEM((n_pairs, n_chunks, 4 * CHUNK, PAIR), BF16)
    return pl.pallas_call(
        functools.partial(_rwkv_kernel, width=width, tb=tb),
        grid=(batch, tiles),
        in_specs=[pl.BlockSpec((tb, pw), lambda b, t: (b * tiles + t, 0))] + vec_specs + mat_specs,
        out_specs=pl.BlockSpec((tb, width), lambda b, t: (b * tiles + t, 0)),
        out_shape=jax.ShapeDtypeStruct((batch * seq, width), F32),
        scratch_shapes=[pltpu.VMEM((1, pw), F32),
                        pltpu.VMEM((n_pairs, PAIR, PAIR), F32),
                        stacked2, stacked2, stacked2,
                        pltpu.VMEM((n_pairs, n_chunks, 2 * CHUNK, PAIR), F32),
                        pltpu.VMEM((n_pairs, tb, PAIR), F32),
                        pltpu.VMEM((n_pairs, tb, PAIR), F32),
                        pltpu.VMEM((n_pairs, n_chunks, 2 * CHUNK, PAIR), BF16),
                        pltpu.VMEM((n_pairs, n_chunks, 2 * CHUNK, PAIR), BF16),
                        pltpu.VMEM((n_pairs, n_chunks, 4 * CHUNK, PAIR), F32),
                        pltpu.VMEM((n_pairs, n_chunks, 2 * CHUNK, PAIR), F32)],
        compiler_params=pltpu.CompilerParams(
            dimension_semantics=("parallel", "arbitrary"), vmem_limit_bytes=VMEM_LIMIT),
    )(p_r, *vecs, wd, wa, wg)


def _attn_kernel(q_ref, kp_ref, kc_ref, vp_ref, vc_ref, bias_ref, beta_ref, o_ref, kwin, vwin,
                 *, width, qb):
    i = pl.program_id(1)
    n_pairs = width // PAIR
    kwin[0:qb, :] = kp_ref[...]
    kwin[qb:2 * qb, :] = kc_ref[...]
    vwin[0:qb, :] = vp_ref[...]
    vwin[qb:2 * qb, :] = vc_ref[...]
    lane = lax.broadcasted_iota(jnp.int32, (1, PAIR), 1)
    first_head = lane < HEAD_DIM
    rows = ATTN_BLOCK_CHUNKS * CHUNK
    win = BAND + rows - CHUNK
    col = lax.broadcasted_iota(jnp.int32, (1, win), 1)
    scale = HEAD_DIM ** -0.5
    left = LEFT_CHUNKS * CHUNK

    for q in range(n_pairs):
        lanes = slice(q * PAIR, (q + 1) * PAIR)
        for blk in range(qb // rows):
            r0 = blk * rows
            qc = q_ref[r0:r0 + rows, lanes]
            zero = jnp.zeros_like(qc)
            qm = jnp.concatenate([jnp.where(first_head, qc, zero), jnp.where(first_head, zero, qc)],
                                 axis=0)
            s = _dot_nt(qm, kwin[r0:r0 + win, lanes]) * scale + bias_ref[q]
            valid = jnp.logical_or(i > 0, col + r0 >= left)
            s = jnp.where(valid, s, NEG_INF)
            e = jnp.exp(s - jnp.max(s, axis=-1, keepdims=True))
            denom = jnp.sum(e, axis=-1, keepdims=True)
            o = _dot(e.astype(BF16), vwin[r0:r0 + win, lanes]) / denom
            oc = jnp.where(first_head, o[:rows], o[rows:])
            o_ref[r0:r0 + rows, lanes] = oc * beta_ref[:, lanes]


def _attn_call(p_a, bias, beta, batch, seq, width, qb):
    assert qb == LEFT_CHUNKS * CHUNK, "key window = previous block + current block"
    tiles = seq // qb
    n_pairs = width // PAIR

    def cur(col):
        return pl.BlockSpec((qb, width), lambda b, t: (b * tiles + t, col))

    def prev(col):
        return pl.BlockSpec((qb, width), lambda b, t: (b * tiles + jnp.maximum(t - 1, 0), col))

    return pl.pallas_call(
        functools.partial(_attn_kernel, width=width, qb=qb),
        grid=(batch, tiles),
        in_specs=[cur(0), prev(1), cur(1), prev(2), cur(2),
                  pl.BlockSpec(bias.shape, lambda b, t: (0, 0, 0)),
                  pl.BlockSpec((1, width), lambda b, t: (0, 0))],
        out_specs=pl.BlockSpec((qb, width), lambda b, t: (b * tiles + t, 0)),
        out_shape=jax.ShapeDtypeStruct((batch * seq, width), F32),
        scratch_shapes=[pltpu.VMEM((2 * qb, width), BF16), pltpu.VMEM((2 * qb, width), BF16)],
        compiler_params=pltpu.CompilerParams(
            dimension_semantics=("parallel", "arbitrary"), vmem_limit_bytes=VMEM_LIMIT),
    )(p_a, p_a, p_a, p_a, p_a, bias, beta)


def _out_router_kernel(yr_ref, ya_ref, wo_ref, x_ref, mod_ref, g2_ref, wr_ref, br_ref,
                       x1_ref, h2_ref, route_ref, *, r_width):
    y = (_dot(yr_ref[...].astype(BF16), wo_ref[0:r_width, :])
         + _dot(ya_ref[...].astype(BF16), wo_ref[r_width:, :]))
    m = mod_ref[0]
    x1 = x_ref[...] + m[GATE1:GATE1 + 1] * y
    x1_ref[...] = x1
    h2 = _rms(x1) * g2_ref[...] * (1.0 + m[SCALE2:SCALE2 + 1]) + m[SHIFT2:SHIFT2 + 1]
    h2_ref[...] = h2

    logits = _dot_x3(h2, wr_ref[...]) + br_ref[...]
    li = lax.broadcasted_iota(jnp.int32, logits.shape, 1).astype(F32)
    none = jnp.float32(-jnp.inf)
    far = jnp.float32(LANES)
    is_group = li < N_GROUPS
    gl = jnp.where(is_group, logits, none)
    gmax = jnp.max(gl, axis=-1, keepdims=True)
    gidx = jnp.min(jnp.where(gl == gmax, li, far), axis=-1, keepdims=True)
    g_w = 1.0 / jnp.sum(jnp.where(is_group, jnp.exp(logits - gmax), 0.0), axis=-1, keepdims=True)
    lo = N_GROUPS + EXPERTS_PER_GROUP * gidx
    el = jnp.where(jnp.logical_and(li >= lo, li < lo + EXPERTS_PER_GROUP), logits, none)
    m1 = jnp.max(el, axis=-1, keepdims=True)
    i1 = jnp.min(jnp.where(el == m1, li, far), axis=-1, keepdims=True)
    el2 = jnp.where(li == i1, none, el)
    m2 = jnp.max(el2, axis=-1, keepdims=True)
    i2 = jnp.min(jnp.where(el2 == m2, li, far), axis=-1, keepdims=True)
    e2 = jnp.exp(m2 - m1)
    w1 = g_w / (1.0 + e2)
    w2 = g_w * e2 / (1.0 + e2)
    route = jnp.where(li == 0.0, i1 - N_GROUPS, 0.0) + jnp.where(li == 1.0, i2 - N_GROUPS, 0.0)
    route_ref[...] = route + jnp.where(li == 2.0, w1, 0.0) + jnp.where(li == 3.0, w2, 0.0)


def _out_router_call(y_r, y_a, wo, x2d, mod, g2, wr, br, seq, tm):
    rows, d = x2d.shape
    r_width = y_r.shape[1]
    a_width = y_a.shape[1]
    tiles_per_seq = seq // tm
    return pl.pallas_call(
        functools.partial(_out_router_kernel, r_width=r_width),
        grid=(rows // tm,),
        in_specs=[pl.BlockSpec((tm, r_width), lambda i: (i, 0)),
                  pl.BlockSpec((tm, a_width), lambda i: (i, 0)),
                  pl.BlockSpec(wo.shape, lambda i: (0, 0)),
                  pl.BlockSpec((tm, d), lambda i: (i, 0)),
                  pl.BlockSpec((1, 6, d), lambda i: (i // tiles_per_seq, 0, 0)),
                  pl.BlockSpec((1, d), lambda i: (0, 0)),
                  pl.BlockSpec((d, LANES), lambda i: (0, 0)),
                  pl.BlockSpec((1, LANES), lambda i: (0, 0))],
        out_specs=[pl.BlockSpec((tm, d), lambda i: (i, 0)),
                   pl.BlockSpec((tm, d), lambda i: (i, 0)),
                   pl.BlockSpec((tm, LANES), lambda i: (i, 0))],
        out_shape=[jax.ShapeDtypeStruct((rows, d), F32),
                   jax.ShapeDtypeStruct((rows, d), F32),
                   jax.ShapeDtypeStruct((rows, LANES), F32)],
        compiler_params=pltpu.CompilerParams(
            dimension_semantics=("parallel",), vmem_limit_bytes=VMEM_LIMIT),
    )(y_r, y_a, wo, x2d, mod, g2, wr, br)


def _moe_kernel(texp_ref, first_ref, next_ref, rows_ref, meta_ref, src_ref,
                h2_hbm, wg_hbm, wu_hbm, wd_hbm, o_ref,
                xbuf, xsem, wg_land, wu_land, wd_land, wsem, wg_bf, wu_bf, wd_bf, *, tm):
    t = pl.program_id(0)
    n_used = meta_ref[0]
    slot = lax.rem(t, MOE_GATHER_SLOTS)

    def weight_copies(e):
        cps = []
        for m, (hbm, land) in enumerate(((wg_hbm, wg_land), (wu_hbm, wu_land), (wd_hbm, wd_land))):
            rows = hbm.shape[1] // W_CHUNKS
            for ck in range(W_CHUNKS):
                part = pl.ds(ck * rows, rows)
                cps.append(pltpu.make_async_copy(hbm.at[e, part], land.at[part], wsem.at[m, ck]))
        return cps

    def row_groups(tile):
        return (rows_ref[tile] + ROW_GROUP - 1) // ROW_GROUP

    def gather_rows(tile):
        buf_slot = lax.rem(tile, MOE_GATHER_SLOTS)

        def body(g, carry):
            for k in range(ROW_GROUP):
                i = g * ROW_GROUP + k
                tok = src_ref[tile * tm + i]
                pltpu.make_async_copy(h2_hbm.at[pl.ds(tok, 1)], xbuf.at[buf_slot, pl.ds(i, 1)],
                                      xsem.at[buf_slot]).start()
            return carry

        lax.fori_loop(0, row_groups(tile), body, 0)

    @pl.when(t == 0)
    def _():
        xbuf[...] = jnp.zeros_like(xbuf)
        for cp in weight_copies(texp_ref[0]):
            cp.start()
        for ahead in range(MOE_GATHER_SLOTS - 1):
            @pl.when(ahead < n_used)
            def _():
                gather_rows(ahead)

    is_first = jnp.logical_and(t < n_used, first_ref[t] == 1)

    @pl.when(is_first)
    def _():
        for cp in weight_copies(texp_ref[t]):
            cp.wait()
        wg_bf[...] = wg_land[...].astype(BF16)
        wu_bf[...] = wu_land[...].astype(BF16)
        wd_bf[...] = wd_land[...].astype(BF16)

    @pl.when(t + MOE_GATHER_SLOTS - 1 < n_used)
    def _():
        gather_rows(t + MOE_GATHER_SLOTS - 1)

    @pl.when(jnp.logical_and(is_first, next_ref[t] >= 0))
    def _():
        for cp in weight_copies(next_ref[t]):
            cp.start()

    @pl.when(t < n_used)
    def _():
        def wait_group(g, carry):
            r0 = pl.multiple_of(g * ROW_GROUP, ROW_GROUP)
            pltpu.make_async_copy(h2_hbm.at[pl.ds(0, ROW_GROUP)],
                                  xbuf.at[slot, pl.ds(r0, ROW_GROUP)], xsem.at[slot]).wait()
            return carry

        lax.fori_loop(0, row_groups(t), wait_group, 0)
        x = xbuf[slot].astype(BF16)
        gate = _dot(x, wg_bf[...])
        up = _dot(x, wu_bf[...])
        h = gate * _sigmoid(gate) * up
        o_ref[...] = _dot(h.astype(BF16), wd_bf[...])

    @pl.when(t >= n_used)
    def _():
        o_ref[...] = jnp.zeros_like(o_ref)


def _moe_call(plan, src, h2, wg, wu, wd, tm):
    n_e, d, f = wg.shape
    n_tiles = plan[0].shape[0]
    grid_spec = pltpu.PrefetchScalarGridSpec(
        num_scalar_prefetch=len(plan) + 1,
        grid=(n_tiles,),
        in_specs=[pl.BlockSpec(memory_space=pl.ANY)] * 4,
        out_specs=pl.BlockSpec((tm, d), lambda t, *_: (t, 0)),
        scratch_shapes=[pltpu.VMEM((MOE_GATHER_SLOTS, tm, d), F32),
                        pltpu.SemaphoreType.DMA((MOE_GATHER_SLOTS,)),
                        pltpu.VMEM((d, f), F32), pltpu.VMEM((d, f), F32), pltpu.VMEM((f, d), F32),
                        pltpu.SemaphoreType.DMA((3, W_CHUNKS)),
                        pltpu.VMEM((d, f), BF16), pltpu.VMEM((d, f), BF16), pltpu.VMEM((f, d), BF16)])
    return pl.pallas_call(
        functools.partial(_moe_kernel, tm=tm),
        grid_spec=grid_spec,
        out_shape=jax.ShapeDtypeStruct((n_tiles * tm, d), F32),
        compiler_params=pltpu.CompilerParams(
            dimension_semantics=("arbitrary",), vmem_limit_bytes=VMEM_LIMIT),
    )(*plan, src, h2, wg, wu, wd)


def _final_kernel(pos_ref, ys_hbm, route_ref, x1_ref, mod_ref, gf_ref, o_ref, ybuf, sem, *, tm):
    i = pl.program_id(0)
    n = pl.num_programs(0)
    slot = i % 2

    def gather_rows(tile, buf_slot):
        base = tile * tm * 2

        def body(r, carry):
            for s in range(2):
                pltpu.make_async_copy(ys_hbm.at[pl.ds(pos_ref[base + 2 * r + s], 1)],
                                      ybuf.at[buf_slot, s, pl.ds(r, 1)], sem.at[buf_slot]).start()
            return carry

        lax.fori_loop(0, tm, body, 0, unroll=4)

    @pl.when(i == 0)
    def _():
        gather_rows(0, 0)

    @pl.when(i + 1 < n)
    def _():
        gather_rows(i + 1, 1 - slot)

    for s in range(2):
        pltpu.make_async_copy(ys_hbm.at[pl.ds(0, tm)], ybuf.at[slot, s], sem.at[slot]).wait()
    route = route_ref[...]
    w0 = route[:, 2:3]
    w1 = route[:, 3:4]
    f = w0 * ybuf[slot, 0] + w1 * ybuf[slot, 1]
    x2 = x1_ref[...] + mod_ref[0][GATE2:GATE2 + 1] * f
    o_ref[...] = _rms(x2) * gf_ref[...]


def _final_call(pos, ys, route, x1, mod, gf, seq, tm):
    rows, d = x1.shape
    tiles_per_seq = seq // tm
    grid_spec = pltpu.PrefetchScalarGridSpec(
        num_scalar_prefetch=1,
        grid=(rows // tm,),
        in_specs=[pl.BlockSpec(memory_space=pl.ANY),
                  pl.BlockSpec((tm, LANES), lambda i, pos: (i, 0)),
                  pl.BlockSpec((tm, d), lambda i, pos: (i, 0)),
                  pl.BlockSpec((1, 6, d), lambda i, pos: (i // tiles_per_seq, 0, 0)),
                  pl.BlockSpec((1, d), lambda i, pos: (0, 0))],
        out_specs=pl.BlockSpec((tm, d), lambda i, pos: (i, 0)),
        scratch_shapes=[pltpu.VMEM((2, 2, tm, d), F32), pltpu.SemaphoreType.DMA((2,))])
    return pl.pallas_call(
        functools.partial(_final_kernel, tm=tm),
        grid_spec=grid_spec,
        out_shape=jax.ShapeDtypeStruct((rows, d), F32),
        compiler_params=pltpu.CompilerParams(
            dimension_semantics=("arbitrary",), vmem_limit_bytes=VMEM_LIMIT),
    )(pos, ys, route, x1, mod, gf)


def _route_plan(route, tm):
    n = route.shape[0]
    flat_e = route[:, 0:2].astype(jnp.int32).reshape(-1)
    n_tiles = (2 * n + N_EXPERTS * (tm - 1)) // tm
    experts = jnp.arange(N_EXPERTS, dtype=jnp.int32)
    onehot = (flat_e[:, None] == experts[None, :]).astype(jnp.int32)
    csum = jnp.cumsum(onehot, axis=0)
    rank = jnp.sum(csum * onehot, axis=1) - 1
    counts = csum[-1]
    tiles_per_e = (counts + tm - 1) // tm
    tile_end = jnp.cumsum(tiles_per_e)
    tile_start = tile_end - tiles_per_e
    n_used = tile_end[-1]
    pos = jnp.sum(onehot * tile_start[None, :], axis=1) * tm + rank
    token = jnp.arange(2 * n, dtype=jnp.int32) // 2
    src = jnp.zeros((n_tiles * tm,), jnp.int32).at[pos].set(token)
    tiles = jnp.arange(n_tiles, dtype=jnp.int32)
    texp = jnp.sum((tiles[:, None] >= tile_end[None, :]).astype(jnp.int32), axis=1)
    texp = jnp.where(tiles < n_used, texp, texp[n_used - 1])
    first = jnp.concatenate([jnp.ones((1,), jnp.int32), (texp[1:] != texp[:-1]).astype(jnp.int32)])
    later_used = jnp.logical_and(experts[None, :] > experts[:, None], (tiles_per_e > 0)[None, :])
    next_used = jnp.min(jnp.where(later_used, experts[None, :], N_EXPERTS), axis=1)
    next_used = jnp.where(next_used == N_EXPERTS, -1, next_used)
    tile_rows = jnp.clip(counts[texp] - (tiles - tile_start[texp]) * tm, 0, tm)
    return (texp, first, next_used[texp], tile_rows, n_used.reshape(1)), src, pos


def _round_up(n, m):
    return (n + m - 1) // m * m


def kernel(x, c, w_ada, b_ada, norm1_g, w_in, mu_shift, w0, w_decay_up, a0, w_aaa_up, w_gate_up,
           k_k, k_a, r_k, ln_x_w, ln_x_b, rel_bias, beta_rwkv, beta_attn, w_out, norm2_g, w_group,
           b_group, w_expert, b_expert, w_gate, w_up, w_down, norm_f_g):
    assert w_ada.shape[0] == 1, "single trunk layer"
    batch, seq, d = x.shape
    r_width = w0.shape[1]
    a_width = beta_attn.shape[1]
    shift_width = mu_shift.shape[1]
    assert shift_width == 3 * r_width + DECAY_LORA + AAA_LORA + GATE_LORA
    x2d = x.reshape(batch * seq, d)
    row = lambda t: t.reshape(1, -1)

    c8 = jnp.pad(c, ((0, 8 - batch), (0, 0)))
    mod = _ada_call(c8, w_ada[0], row(b_ada[0]), tn=1536)[:batch].reshape(batch, 6, d)

    tm = min(512, seq)
    pw = _round_up(shift_width, 3 * LANES)
    h1 = _hmod_call(x2d, row(norm1_g[0]), mod, tm, seq)
    tm_p = min(1024, seq)
    w_in_bf = w_in[0].astype(BF16)
    p_r = _matmul_call(h1, w_in_bf, pw, pw // 3, tm_p, F32)
    p_a = _matmul_call(h1, w_in_bf[:, shift_width:], 3 * a_width, a_width, tm_p, BF16)

    mu = jnp.pad(mu_shift[0], (0, pw - shift_width))
    vecs = [row(mu), row(w0[0]), row(a0[0]), row(k_k[0]), row(k_a[0]), row(r_k[0]), row(ln_x_w[0]),
            row(ln_x_b[0]), row(beta_rwkv[0])]
    y_r = _rwkv_call(p_r, vecs, w_decay_up[0], w_aaa_up[0], w_gate_up[0], batch, seq, r_width,
                     tb=min(256, seq))

    left = LEFT_CHUNKS * CHUNK
    dist = left + CHUNK - 1 - jnp.arange(BAND + CHUNK - 1)
    base = rel_bias[0][:, jnp.clip(dist, -REL_CLIP, REL_CLIP) + REL_CLIP].astype(F32)
    bias = jnp.stack([base[:, CHUNK - 1 - qi:CHUNK - 1 - qi + BAND] for qi in range(CHUNK)], axis=1)
    nb = ATTN_BLOCK_CHUNKS
    bias = jnp.concatenate(
        [jnp.pad(bias, ((0, 0), (0, 0), (cb * CHUNK, (nb - 1 - cb) * CHUNK)), constant_values=NEG_INF)
         for cb in range(nb)], axis=1)
    bias = bias.reshape(a_width // PAIR, 2 * nb * CHUNK, BAND + (nb - 1) * CHUNK)
    y_a = _attn_call(p_a, bias, row(beta_attn[0]), batch, seq, a_width, qb=min(512, seq))

    n_route = N_GROUPS + N_EXPERTS
    wr = jnp.concatenate([w_group[0], w_expert[0].transpose(1, 0, 2).reshape(d, N_EXPERTS)], axis=1)
    wr = jnp.pad(wr, ((0, 0), (0, LANES - n_route)))
    br = jnp.pad(jnp.concatenate([b_group[0], b_expert[0].reshape(-1)]), (0, LANES - n_route))
    x1, h2, route = _out_router_call(y_r, y_a, w_out[0].astype(BF16), x2d, mod, row(norm2_g[0]),
                                     wr, row(br), seq, tm)

    f = w_gate.shape[-1]
    tm_e = 256
    plan, src, pos = _route_plan(route, tm_e)
    ys = _moe_call(plan, src, h2, w_gate[0].reshape(N_EXPERTS, d, f),
                   w_up[0].reshape(N_EXPERTS, d, f), w_down[0].reshape(N_EXPERTS, f, d), tm_e)
    out = _final_call(pos, ys, route, x1, mod, row(norm_f_g), seq, min(256, seq))
    return out.reshape(batch, seq, d)
```

```python
import functools

import jax
import jax.numpy as jnp
from jax import lax
from jax.experimental import pallas as pl
from jax.experimental.pallas import tpu as pltpu

F32 = jnp.float32
BF16 = jnp.bfloat16
HI = lax.Precision.HIGHEST

LANES = 128
HEAD_DIM = 64
PAIR = 2 * HEAD_DIM
CHUNK = 64
LEFT_CHUNKS = 8
BAND = (LEFT_CHUNKS + 1) * CHUNK
ATTN_BLOCK_CHUNKS = 4
PREP_PAIRS = 4
MOE_GATHER_SLOTS = 3
W_CHUNKS = 2
ROW_GROUP = 8
REL_CLIP = 128
DECAY_LORA = 64
AAA_LORA = 64
GATE_LORA = 160
N_GROUPS = 4
EXPERTS_PER_GROUP = 8
N_EXPERTS = N_GROUPS * EXPERTS_PER_GROUP
RMS_EPS = 1e-6
GN_EPS = 64e-5
L2_EPS = 1e-12
NEG_INF = -1e30
VMEM_LIMIT = 56 * 1024 * 1024

SHIFT1, SCALE1, GATE1, SHIFT2, SCALE2, GATE2 = range(6)


def _dot(a, b, precision=None):
    return jnp.dot(a, b, precision=precision, preferred_element_type=F32)


def _dot_nt(a, b, precision=None):
    return lax.dot_general(a, b, (((1,), (1,)), ((), ())), precision=precision,
                           preferred_element_type=F32)


def _sigmoid(x):
    return 1.0 / (1.0 + jnp.exp(-x))


def _rms(x):
    return x * lax.rsqrt(jnp.mean(x * x, axis=-1, keepdims=True) + RMS_EPS)


def _ada_kernel(c_ref, w_ref, b_ref, o_ref):
    c = c_ref[...]
    o_ref[...] = _dot_x3(c * _sigmoid(c), w_ref[...]) + b_ref[...]


def _ada_call(c8, w, b, tn):
    rows, d = c8.shape
    n = w.shape[1]
    return pl.pallas_call(
        _ada_kernel,
        grid=(n // tn,),
        in_specs=[pl.BlockSpec((rows, d), lambda j: (0, 0)),
                  pl.BlockSpec((d, tn), lambda j: (0, j)),
                  pl.BlockSpec((1, tn), lambda j: (0, j))],
        out_specs=pl.BlockSpec((rows, tn), lambda j: (0, j)),
        out_shape=jax.ShapeDtypeStruct((rows, n), F32),
        compiler_params=pltpu.CompilerParams(vmem_limit_bytes=VMEM_LIMIT),
    )(c8, w, b)


def _hmod_kernel(x_ref, g_ref, mod_ref, o_ref):
    m = mod_ref[0]
    h = _rms(x_ref[...]) * g_ref[...] * (1.0 + m[SCALE1:SCALE1 + 1]) + m[SHIFT1:SHIFT1 + 1]
    o_ref[...] = h.astype(BF16)


def _hmod_call(x2d, g, mod, tm, seq):
    rows, d = x2d.shape
    tiles_per_seq = seq // tm
    return pl.pallas_call(
        _hmod_kernel,
        grid=(rows // tm,),
        in_specs=[pl.BlockSpec((tm, d), lambda i: (i, 0)),
                  pl.BlockSpec((1, d), lambda i: (0, 0)),
                  pl.BlockSpec((1, 6, d), lambda i: (i // tiles_per_seq, 0, 0))],
        out_specs=pl.BlockSpec((tm, d), lambda i: (i, 0)),
        out_shape=jax.ShapeDtypeStruct((rows, d), BF16),
        compiler_params=pltpu.CompilerParams(
            dimension_semantics=("parallel",), vmem_limit_bytes=VMEM_LIMIT),
    )(x2d, g, mod)


def _matmul_kernel(a_ref, w_ref, o_ref):
    o_ref[...] = _dot(a_ref[...], w_ref[...]).astype(o_ref.dtype)


def _matmul_call(a, w, n_out, tn, tm, out_dtype):
    rows, d = a.shape
    return pl.pallas_call(
        _matmul_kernel,
        grid=(n_out // tn, rows // tm),
        in_specs=[pl.BlockSpec((tm, d), lambda j, i: (i, 0)),
                  pl.BlockSpec((d, tn), lambda j, i: (0, j))],
        out_specs=pl.BlockSpec((tm, tn), lambda j, i: (i, j)),
        out_shape=jax.ShapeDtypeStruct((rows, n_out), out_dtype),
        compiler_params=pltpu.CompilerParams(
            dimension_semantics=("parallel", "parallel"), vmem_limit_bytes=VMEM_LIMIT),
    )(a, w)


def _split3(t):
    hi = t.astype(BF16)
    rest = t - hi.astype(F32)
    mid = rest.astype(BF16)
    lo = (rest - mid.astype(F32)).astype(BF16)
    return hi, mid, lo


def _dot_hilo_lhs(t, m):
    hi = t.astype(BF16)
    lo = (t - hi.astype(F32)).astype(BF16)
    return _dot(lo, m) + _dot(hi, m)


def _dot_f32_rhs(m, t):
    hi, mid, lo = _split3(t)
    return _dot(m, lo) + _dot(m, mid) + _dot(m, hi)


def _dot_x3(a, b):
    ah = a.astype(BF16)
    al = (a - ah.astype(F32)).astype(BF16)
    bh = b.astype(BF16)
    bl = (b - bh.astype(F32)).astype(BF16)
    return _dot(al, bh) + _dot(ah, bl) + _dot(ah, bh)


def _rwkv_kernel(p_ref, mu_ref, w0_ref, a0_ref, kk_ref, ka_ref, rk_ref, lnw_ref, lnb_ref, beta_ref,
                 wd_ref, wa_ref, wg_ref, o_ref,
                 carry_scr, state_scr, ar_scr, bk_scr, bkh_scr, v_scr, et_scr, y_scr,
                 pm_scr, arb_scr, av_scr, vk_scr, *, width, tb):
    n_pairs = width // PAIR
    n_chunks = tb // CHUNK
    stacked = 2 * CHUNK

    @pl.when(pl.program_id(1) == 0)
    def _():
        carry_scr[...] = jnp.zeros_like(carry_scr)
        state_scr[...] = jnp.zeros_like(state_scr)

    p = p_ref[...]
    row = lax.broadcasted_iota(jnp.int32, (tb, 1), 0)
    prev = jnp.where(row == 0, carry_scr[...], pltpu.roll(p, 1, axis=0))
    carry_scr[...] = p[tb - 1:tb, :]
    ps = p + (prev - p) * mu_ref[...]

    c = width
    r = ps[:, :c]
    k = ps[:, c:2 * c]
    v = ps[:, 2 * c:3 * c]
    o = 3 * c
    xw = ps[:, o:o + DECAY_LORA]
    xa = ps[:, o + DECAY_LORA:o + DECAY_LORA + AAA_LORA]
    xg = ps[:, o + DECAY_LORA + AAA_LORA:o + DECAY_LORA + AAA_LORA + GATE_LORA]

    z = w0_ref[...] + _dot_x3(jnp.tanh(xw), wd_ref[...])
    softplus_neg_z = jnp.maximum(-z, 0.0) + jnp.log(1.0 + jnp.exp(-jnp.abs(z)))
    lw = -jnp.exp(-softplus_neg_z - 0.5)
    a = _sigmoid(a0_ref[...] + _dot(xa.astype(BF16), wa_ref[...].astype(BF16)))
    g = _dot(_sigmoid(xg).astype(BF16), wg_ref[...].astype(BF16))

    li = lax.broadcasted_iota(jnp.int32, (PAIR, PAIR), 0)
    lj = lax.broadcasted_iota(jnp.int32, (PAIR, PAIR), 1)
    head_ones = ((li // HEAD_DIM) == (lj // HEAD_DIM)).astype(BF16)

    def head_sum(t):
        return jnp.concatenate(
            [_dot_hilo_lhs(t[:, q * PAIR:(q + 1) * PAIR], head_ones) for q in range(n_pairs)], axis=1)

    kk = k * kk_ref[...]
    kk = kk / jnp.maximum(jnp.sqrt(head_sum(kk * kk)), L2_EPS)
    k2 = k * (1.0 + (a - 1.0) * ka_ref[...])

    ti = lax.broadcasted_iota(jnp.int32, (tb, tb), 0)
    tj = lax.broadcasted_iota(jnp.int32, (tb, tb), 1)
    tri = jnp.logical_and((ti // CHUNK) == (tj // CHUNK), tj <= ti).astype(BF16)
    cum = _dot_f32_rhs(tri, lw)
    tot = jnp.concatenate(
        [jnp.broadcast_to(cum[(ci + 1) * CHUNK - 1:(ci + 1) * CHUNK, :], (CHUNK, c))
         for ci in range(n_chunks)], axis=0)

    e_neg = jnp.exp(-cum)
    e_rem = jnp.exp(tot - cum)
    kka = kk * a
    lane = lax.broadcasted_iota(jnp.int32, (1, PAIR), 1)
    first_head = lane < HEAD_DIM

    def put(scr, base, val):
        for q in range(n_pairs):
            vq = val[:, q * PAIR:(q + 1) * PAIR]
            h0 = jnp.where(first_head, vq, 0.0).astype(scr.dtype)
            h1 = jnp.where(first_head, 0.0, vq).astype(scr.dtype)
            for ci in range(n_chunks):
                rows = slice(ci * CHUNK, (ci + 1) * CHUNK)
                scr[q, ci, base:base + CHUNK, :] = h0[rows]
                scr[q, ci, base + CHUNK:base + stacked, :] = h1[rows]

    put(ar_scr, 0, -kk * jnp.exp(cum - lw))
    put(ar_scr, stacked, r * jnp.exp(cum))
    put(bk_scr, 0, kka * e_neg)
    put(bk_scr, stacked, k2 * e_neg)
    put(bkh_scr, 0, kka * e_rem)
    put(bkh_scr, stacked, k2 * e_rem)
    put(v_scr, 0, v)
    e_tot = jnp.exp(tot)
    for q in range(n_pairs):
        et_scr[q] = e_tot[:, q * PAIR:(q + 1) * PAIR]

    si = lax.broadcasted_iota(jnp.int32, (2 * stacked, 2 * stacked), 0)
    sj = lax.broadcasted_iota(jnp.int32, (2 * stacked, 2 * stacked), 1)
    keep = jnp.where(si < stacked, si, si - stacked + 1) > (sj % stacked)
    eye = (li == lj).astype(F32)

    chunks = range(n_chunks)

    def side(ta, tb):
        return jnp.concatenate([ta, tb], axis=1)

    def blockdiag(ta, tb):
        za = jnp.zeros((ta.shape[0], tb.shape[1]), ta.dtype)
        zb = jnp.zeros((tb.shape[0], ta.shape[1]), ta.dtype)
        return jnp.concatenate([side(ta, za), side(zb, tb)], axis=0)

    def halves(t):
        return t[:, :PAIR], t[:, PAIR:]

    def bf(t):
        return t.astype(BF16)

    hr = lax.broadcasted_iota(jnp.int32, (CHUNK, 4 * CHUNK), 0)
    hl = lax.broadcasted_iota(jnp.int32, (CHUNK, 4 * CHUNK), 1)
    eye4 = (hr == hl % CHUNK).astype(F32)
    zero_bf = jnp.zeros((CHUNK, stacked), BF16)

    def head_blocks(t):
        z = jnp.zeros_like(t)
        return jnp.concatenate([jnp.where(hl // CHUNK == j, t, z) for j in range(4)], axis=0)

    def prepare(step):
        couples = [(step * PREP_PAIRS + 2 * j, ci) for j in range(PREP_PAIRS // 2) for ci in chunks]
        pw = []
        for qa, ci in couples:
            n, tail, vm = [], [], []
            for q in (qa, qa + 1):
                sc = jnp.where(keep, _dot_nt(ar_scr[q, ci], bk_scr[q, ci]), 0.0)
                scb = bf(sc)
                arb_scr[q, ci] = scb[stacked:, :stacked]
                n.append(sc[:stacked, :stacked])
                tail.append(scb[:, stacked:])
                vm.append(v_scr[q, ci])
            av = _dot(side(*tail), blockdiag(bf(vm[0]), bf(vm[1])))
            vk = _dot(side(bf(vm[0].T), bf(vm[1].T)),
                      blockdiag(bkh_scr[qa, ci, stacked:, :], bkh_scr[qa + 1, ci, stacked:, :]))
            for q, t_av, t_vk in zip((qa, qa + 1), halves(av), halves(vk)):
                av_scr[q, ci] = t_av
                vk_scr[q, ci] = t_vk
            pw.append(bf(side(n[0][:CHUNK] + n[0][CHUNK:], n[1][:CHUNK] + n[1][CHUNK:])))
        acc = [eye4 + p_ for p_ in pw]
        pw = [bf(_dot(p_, head_blocks(p_))) for p_ in pw]
        for it in range(1, 6):
            rhs = [head_blocks(p_) for p_ in pw]
            if it < 5:
                both = [_dot(jnp.concatenate([p_, bf(a_)], axis=0), r_)
                        for p_, a_, r_ in zip(pw, acc, rhs)]
                pw = [bf(t[:CHUNK]) for t in both]
                acc = [a_ + t[CHUNK:] for a_, t in zip(acc, both)]
            else:
                acc = [a_ + _dot(bf(a_), r_) for a_, r_ in zip(acc, rhs)]
        for (qa, ci), a_ in zip(couples, acc):
            for q, t in zip((qa, qa + 1), halves(bf(a_))):
                pm_scr[q, ci] = jnp.concatenate(
                    [jnp.where(first_head, t, zero_bf), jnp.where(first_head, zero_bf, t)], axis=0)

    for step in range(n_pairs // PREP_PAIRS):
        prepare(step)

    firsts = range(0, n_pairs, 2)
    s = [side(state_scr[q], state_scr[q + 1]) for q in firsts]
    for ci in chunks:
        rows = slice(ci * CHUNK, (ci + 1) * CHUNK)

        def both(scr, part=slice(None)):
            return [side(scr[q, ci, part, :], scr[q + 1, ci, part, :]) for q in firsts]

        top, bottom = slice(0, stacked), slice(stacked, 2 * stacked)
        sr = [_dot_nt(ar_, blockdiag(*halves(bf(s_)))) for ar_, s_ in zip(both(ar_scr), s)]
        u = [_dot(pm_, blockdiag(*halves(bf(sr_[:stacked] + av_))))
             for pm_, sr_, av_ in zip(both(pm_scr), sr, both(av_scr, top))]
        ym = [sr_[stacked:] + av_ + _dot(arb_, blockdiag(*halves(bf(u_))))
              for sr_, av_, arb_, u_ in zip(sr, both(av_scr, bottom), both(arb_scr), u)]
        for q, ym_ in zip(firsts, ym):
            ya, yb = halves(ym_[:CHUNK] + ym_[CHUNK:])
            y_scr[q, rows, :] = ya
            y_scr[q + 1, rows, :] = yb
        decay = [side(et_scr[q, ci * CHUNK:ci * CHUNK + 1, :], et_scr[q + 1, ci * CHUNK:ci * CHUNK + 1, :])
                 for q in firsts]
        s = [s_ * d_ + _dot(side(*(bf(t.T) for t in halves(u_))), blockdiag(*halves(bh_))) + vk_
             for s_, d_, u_, bh_, vk_ in zip(s, decay, u, both(bkh_scr, top), both(vk_scr))]
    for q, s_ in zip(firsts, s):
        state_scr[q], state_scr[q + 1] = halves(s_)

    y = jnp.concatenate([y_scr[q] for q in range(n_pairs)], axis=1)
    mean = head_sum(y) * (1.0 / HEAD_DIM)
    d = y - mean
    var = head_sum(d * d) * (1.0 / HEAD_DIM)
    yn = d * lax.rsqrt(var + GN_EPS) * lnw_ref[...] + lnb_ref[...]
    bonus = head_sum(r * k2 * rk_ref[...]) * v
    o_ref[...] = (yn + bonus) * g * beta_ref[...]


def _rwkv_call(p_r, vecs, wd, wa, wg, batch, seq, width, tb):
    assert CHUNK == HEAD_DIM, "chunk operands stack two heads' rows against their 2 * HEAD_DIM lanes"
    assert (width // PAIR) % PREP_PAIRS == 0 and PREP_PAIRS % 2 == 0
    pw = p_r.shape[1]
    n_pairs = width // PAIR
    n_chunks = tb // CHUNK
    tiles = seq // tb
    vec_specs = [pl.BlockSpec((1, v.shape[1]), lambda b, t: (0, 0)) for v in vecs]
    mat_specs = [pl.BlockSpec(m.shape, lambda b, t: (0, 0)) for m in (wd, wa, wg)]
    stacked2 = pltpu.VMEM((n_pairs, n_chunks, 4 * CHUNK, PAIR), BF16)
    return pl.pallas_call(
        functools.partial(_rwkv_kernel, width=width, tb=tb),
        grid=(batch, tiles),
        in_specs=[pl.BlockSpec((tb, pw), lambda b, t: (b * tiles + t, 0))] + vec_specs + mat_specs,
        out_specs=pl.BlockSpec((tb, width), lambda b, t: (b * tiles + t, 0)),
        out_shape=jax.ShapeDtypeStruct((batch * seq, width), F32),
        scratch_shapes=[pltpu.VMEM((1, pw), F32),
                        pltpu.VMEM((n_pairs, PAIR, PAIR), F32),
                        stacked2, stacked2, stacked2,
                        pltpu.VMEM((n_pairs, n_chunks, 2 * CHUNK, PAIR), F32),
                        pltpu.VMEM((n_pairs, tb, PAIR), F32),
                        pltpu.VMEM((n_pairs, tb, PAIR), F32),
                        pltpu.VMEM((n_pairs, n_chunks, 2 * CHUNK, PAIR), BF16),
                        pltpu.VMEM((n_pairs, n_chunks, 2 * CHUNK, PAIR), BF16),
                        pltpu.VMEM((n_pairs, n_chunks, 4 * CHUNK, PAIR), F32),
                        pltpu.VMEM((n_pairs, n_chunks, 2 * CHUNK, PAIR), F32)],
        compiler_params=pltpu.CompilerParams(
            dimension_semantics=("parallel", "arbitrary"), vmem_limit_bytes=VMEM_LIMIT),
    )(p_r, *vecs, wd, wa, wg)


def _attn_kernel(q_ref, kp_ref, kc_ref, vp_ref, vc_ref, bias_ref, beta_ref, o_ref, kwin, vwin,
                 *, width, qb):
    i = pl.program_id(1)
    n_pairs = width // PAIR
    kwin[0:qb, :] = kp_ref[...]
    kwin[qb:2 * qb, :] = kc_ref[...]
    vwin[0:qb, :] = vp_ref[...]
    vwin[qb:2 * qb, :] = vc_ref[...]
    lane = lax.broadcasted_iota(jnp.int32, (1, PAIR), 1)
    first_head = lane < HEAD_DIM
    rows = ATTN_BLOCK_CHUNKS * CHUNK
    win = BAND + rows - CHUNK
    col = lax.broadcasted_iota(jnp.int32, (1, win), 1)
    scale = HEAD_DIM ** -0.5
    left = LEFT_CHUNKS * CHUNK

    for q in range(n_pairs):
        lanes = slice(q * PAIR, (q + 1) * PAIR)
        for blk in range(qb // rows):
            r0 = blk * rows
            qc = q_ref[r0:r0 + rows, lanes]
            zero = jnp.zeros_like(qc)
            qm = jnp.concatenate([jnp.where(first_head, qc, zero), jnp.where(first_head, zero, qc)],
                                 axis=0)
            s = _dot_nt(qm, kwin[r0:r0 + win, lanes]) * scale + bias_ref[q]
            valid = jnp.logical_or(i > 0, col + r0 >= left)
            s = jnp.where(valid, s, NEG_INF)
            e = jnp.exp(s - jnp.max(s, axis=-1, keepdims=True))
            denom = jnp.sum(e, axis=-1, keepdims=True)
            o = _dot(e.astype(BF16), vwin[r0:r0 + win, lanes]) / denom
            oc = jnp.where(first_head, o[:rows], o[rows:])
            o_ref[r0:r0 + rows, lanes] = oc * beta_ref[:, lanes]


def _attn_call(p_a, bias, beta, batch, seq, width, qb):
    assert qb == LEFT_CHUNKS * CHUNK, "key window = previous block + current block"
    tiles = seq // qb
    n_pairs = width // PAIR

    def cur(col):
        return pl.BlockSpec((qb, width), lambda b, t: (b * tiles + t, col))

    def prev(col):
        return pl.BlockSpec((qb, width), lambda b, t: (b * tiles + jnp.maximum(t - 1, 0), col))

    return pl.pallas_call(
        functools.partial(_attn_kernel, width=width, qb=qb),
        grid=(batch, tiles),
        in_specs=[cur(0), prev(1), cur(1), prev(2), cur(2),
                  pl.BlockSpec(bias.shape, lambda b, t: (0, 0, 0)),
                  pl.BlockSpec((1, width), lambda b, t: (0, 0))],
        out_specs=pl.BlockSpec((qb, width), lambda b, t: (b * tiles + t, 0)),
        out_shape=jax.ShapeDtypeStruct((batch * seq, width), F32),
        scratch_shapes=[pltpu.VMEM((2 * qb, width), BF16), pltpu.VMEM((2 * qb, width), BF16)],
        compiler_params=pltpu.CompilerParams(
            dimension_semantics=("parallel", "arbitrary"), vmem_limit_bytes=VMEM_LIMIT),
    )(p_a, p_a, p_a, p_a, p_a, bias, beta)


def _out_router_kernel(yr_ref, ya_ref, wo_ref, x_ref, mod_ref, g2_ref, wr_ref, br_ref,
                       x1_ref, h2_ref, route_ref, *, r_width):
    y = (_dot(yr_ref[...].astype(BF16), wo_ref[0:r_width, :])
         + _dot(ya_ref[...].astype(BF16), wo_ref[r_width:, :]))
    m = mod_ref[0]
    x1 = x_ref[...] + m[GATE1:GATE1 + 1] * y
    x1_ref[...] = x1
    h2 = _rms(x1) * g2_ref[...] * (1.0 + m[SCALE2:SCALE2 + 1]) + m[SHIFT2:SHIFT2 + 1]
    h2_ref[...] = h2

    logits = _dot_x3(h2, wr_ref[...]) + br_ref[...]
    li = lax.broadcasted_iota(jnp.int32, logits.shape, 1).astype(F32)
    none = jnp.float32(-jnp.inf)
    far = jnp.float32(LANES)
    is_group = li < N_GROUPS
    gl = jnp.where(is_group, logits, none)
    gmax = jnp.max(gl, axis=-1, keepdims=True)
    gidx = jnp.min(jnp.where(gl == gmax, li, far), axis=-1, keepdims=True)
    g_w = 1.0 / jnp.sum(jnp.where(is_group, jnp.exp(logits - gmax), 0.0), axis=-1, keepdims=True)
    lo = N_GROUPS + EXPERTS_PER_GROUP * gidx
    el = jnp.where(jnp.logical_and(li >= lo, li < lo + EXPERTS_PER_GROUP), logits, none)
    m1 = jnp.max(el, axis=-1, keepdims=True)
    i1 = jnp.min(jnp.where(el == m1, li, far), axis=-1, keepdims=True)
    el2 = jnp.where(li == i1, none, el)
    m2 = jnp.max(el2, axis=-1, keepdims=True)
    i2 = jnp.min(jnp.where(el2 == m2, li, far), axis=-1, keepdims=True)
    e2 = jnp.exp(m2 - m1)
    w1 = g_w / (1.0 + e2)
    w2 = g_w * e2 / (1.0 + e2)
    route = jnp.where(li == 0.0, i1 - N_GROUPS, 0.0) + jnp.where(li == 1.0, i2 - N_GROUPS, 0.0)
    route_ref[...] = route + jnp.where(li == 2.0, w1, 0.0) + jnp.where(li == 3.0, w2, 0.0)


def _out_router_call(y_r, y_a, wo, x2d, mod, g2, wr, br, seq, tm):
    rows, d = x2d.shape
    r_width = y_r.shape[1]
    a_width = y_a.shape[1]
    tiles_per_seq = seq // tm
    return pl.pallas_call(
        functools.partial(_out_router_kernel, r_width=r_width),
        grid=(rows // tm,),
        in_specs=[pl.BlockSpec((tm, r_width), lambda i: (i, 0)),
                  pl.BlockSpec((tm, a_width), lambda i: (i, 0)),
                  pl.BlockSpec(wo.shape, lambda i: (0, 0)),
                  pl.BlockSpec((tm, d), lambda i: (i, 0)),
                  pl.BlockSpec((1, 6, d), lambda i: (i // tiles_per_seq, 0, 0)),
                  pl.BlockSpec((1, d), lambda i: (0, 0)),
                  pl.BlockSpec((d, LANES), lambda i: (0, 0)),
                  pl.BlockSpec((1, LANES), lambda i: (0, 0))],
        out_specs=[pl.BlockSpec((tm, d), lambda i: (i, 0)),
                   pl.BlockSpec((tm, d), lambda i: (i, 0)),
                   pl.BlockSpec((tm, LANES), lambda i: (i, 0))],
        out_shape=[jax.ShapeDtypeStruct((rows, d), F32),
                   jax.ShapeDtypeStruct((rows, d), F32),
                   jax.ShapeDtypeStruct((rows, LANES), F32)],
        compiler_params=pltpu.CompilerParams(
            dimension_semantics=("parallel",), vmem_limit_bytes=VMEM_LIMIT),
    )(y_r, y_a, wo, x2d, mod, g2, wr, br)


def _moe_kernel(texp_ref, first_ref, next_ref, rows_ref, meta_ref, src_ref,
                h2_hbm, wg_hbm, wu_hbm, wd_hbm, o_ref,
                xbuf, xsem, wg_land, wu_land, wd_land, wsem, wg_bf, wu_bf, wd_bf, *, tm):
    t = pl.program_id(0)
    n_used = meta_ref[0]
    slot = lax.rem(t, MOE_GATHER_SLOTS)

    def weight_copies(e):
        cps = []
        for m, (hbm, land) in enumerate(((wg_hbm, wg_land), (wu_hbm, wu_land), (wd_hbm, wd_land))):
            rows = hbm.shape[1] // W_CHUNKS
            for ck in range(W_CHUNKS):
                part = pl.ds(ck * rows, rows)
                cps.append(pltpu.make_async_copy(hbm.at[e, part], land.at[part], wsem.at[m, ck]))
        return cps

    def row_groups(tile):
        return (rows_ref[tile] + ROW_GROUP - 1) // ROW_GROUP

    def gather_rows(tile):
        buf_slot = lax.rem(tile, MOE_GATHER_SLOTS)

        def body(g, carry):
            for k in range(ROW_GROUP):
                i = g * ROW_GROUP + k
                tok = src_ref[tile * tm + i]
                pltpu.make_async_copy(h2_hbm.at[pl.ds(tok, 1)], xbuf.at[buf_slot, pl.ds(i, 1)],
                                      xsem.at[buf_slot]).start()
            return carry

        lax.fori_loop(0, row_groups(tile), body, 0)

    @pl.when(t == 0)
    def _():
        xbuf[...] = jnp.zeros_like(xbuf)
        for cp in weight_copies(texp_ref[0]):
            cp.start()
        for ahead in range(MOE_GATHER_SLOTS - 1):
            @pl.when(ahead < n_used)
            def _():
                gather_rows(ahead)

    is_first = jnp.logical_and(t < n_used, first_ref[t] == 1)

    @pl.when(is_first)
    def _():
        for cp in weight_copies(texp_ref[t]):
            cp.wait()
        wg_bf[...] = wg_land[...].astype(BF16)
        wu_bf[...] = wu_land[...].astype(BF16)
        wd_bf[...] = wd_land[...].astype(BF16)

    @pl.when(t + MOE_GATHER_SLOTS - 1 < n_used)
    def _():
        gather_rows(t + MOE_GATHER_SLOTS - 1)

    @pl.when(jnp.logical_and(is_first, next_ref[t] >= 0))
    def _():
        for cp in weight_copies(next_ref[t]):
            cp.start()

    @pl.when(t < n_used)
    def _():
        def wait_group(g, carry):
            r0 = pl.multiple_of(g * ROW_GROUP, ROW_GROUP)
            pltpu.make_async_copy(h2_hbm.at[pl.ds(0, ROW_GROUP)],
                                  xbuf.at[slot, pl.ds(r0, ROW_GROUP)], xsem.at[slot]).wait()
            return carry

        lax.fori_loop(0, row_groups(t), wait_group, 0)
        x = xbuf[slot].astype(BF16)
        gate = _dot(x, wg_bf[...])
        up = _dot(x, wu_bf[...])
        h = gate * _sigmoid(gate) * up
        o_ref[...] = _dot(h.astype(BF16), wd_bf[...])

    @pl.when(t >= n_used)
    def _():
        o_ref[...] = jnp.zeros_like(o_ref)


def _moe_call(plan, src, h2, wg, wu, wd, tm):
    n_e, d, f = wg.shape
    n_tiles = plan[0].shape[0]
    grid_spec = pltpu.PrefetchScalarGridSpec(
        num_scalar_prefetch=len(plan) + 1,
        grid=(n_tiles,),
        in_specs=[pl.BlockSpec(memory_space=pl.ANY)] * 4,
        out_specs=pl.BlockSpec((tm, d), lambda t, *_: (t, 0)),
        scratch_shapes=[pltpu.VMEM((MOE_GATHER_SLOTS, tm, d), F32),
                        pltpu.SemaphoreType.DMA((MOE_GATHER_SLOTS,)),
                        pltpu.VMEM((d, f), F32), pltpu.VMEM((d, f), F32), pltpu.VMEM((f, d), F32),
                        pltpu.SemaphoreType.DMA((3, W_CHUNKS)),
                        pltpu.VMEM((d, f), BF16), pltpu.VMEM((d, f), BF16), pltpu.VMEM((f, d), BF16)])
    return pl.pallas_call(
        functools.partial(_moe_kernel, tm=tm),
        grid_spec=grid_spec,
        out_shape=jax.ShapeDtypeStruct((n_tiles * tm, d), F32),
        compiler_params=pltpu.CompilerParams(
            dimension_semantics=("arbitrary",), vmem_limit_bytes=VMEM_LIMIT),
    )(*plan, src, h2, wg, wu, wd)


def _final_kernel(pos_ref, ys_hbm, route_ref, x1_ref, mod_ref, gf_ref, o_ref, ybuf, sem, *, tm):
    i = pl.program_id(0)
    n = pl.num_programs(0)
    slot = i % 2

    def gather_rows(tile, buf_slot):
        base = tile * tm * 2

        def body(r, carry):
            for s in range(2):
                pltpu.make_async_copy(ys_hbm.at[pl.ds(pos_ref[base + 2 * r + s], 1)],
                                      ybuf.at[buf_slot, s, pl.ds(r, 1)], sem.at[buf_slot]).start()
            return carry

        lax.fori_loop(0, tm, body, 0, unroll=4)

    @pl.when(i == 0)
    def _():
        gather_rows(0, 0)

    @pl.when(i + 1 < n)
    def _():
        gather_rows(i + 1, 1 - slot)

    for s in range(2):
        pltpu.make_async_copy(ys_hbm.at[pl.ds(0, tm)], ybuf.at[slot, s], sem.at[slot]).wait()
    route = route_ref[...]
    w0 = route[:, 2:3]
    w1 = route[:, 3:4]
    f = w0 * ybuf[slot, 0] + w1 * ybuf[slot, 1]
    x2 = x1_ref[...] + mod_ref[0][GATE2:GATE2 + 1] * f
    o_ref[...] = _rms(x2) * gf_ref[...]


def _final_call(pos, ys, route, x1, mod, gf, seq, tm):
    rows, d = x1.shape
    tiles_per_seq = seq // tm
    grid_spec = pltpu.PrefetchScalarGridSpec(
        num_scalar_prefetch=1,
        grid=(rows // tm,),
        in_specs=[pl.BlockSpec(memory_space=pl.ANY),
                  pl.BlockSpec((tm, LANES), lambda i, pos: (i, 0)),
                  pl.BlockSpec((tm, d), lambda i, pos: (i, 0)),
                  pl.BlockSpec((1, 6, d), lambda i, pos: (i // tiles_per_seq, 0, 0)),
                  pl.BlockSpec((1, d), lambda i, pos: (0, 0))],
        out_specs=pl.BlockSpec((tm, d), lambda i, pos: (i, 0)),
        scratch_shapes=[pltpu.VMEM((2, 2, tm, d), F32), pltpu.SemaphoreType.DMA((2,))])
    return pl.pallas_call(
        functools.partial(_final_kernel, tm=tm),
        grid_spec=grid_spec,
        out_shape=jax.ShapeDtypeStruct((rows, d), F32),
        compiler_params=pltpu.CompilerParams(
            dimension_semantics=("arbitrary",), vmem_limit_bytes=VMEM_LIMIT),
    )(pos, ys, route, x1, mod, gf)


def _route_plan(route, tm):
    n = route.shape[0]
    flat_e = route[:, 0:2].astype(jnp.int32).reshape(-1)
    n_tiles = (2 * n + N_EXPERTS * (tm - 1)) // tm
    experts = jnp.arange(N_EXPERTS, dtype=jnp.int32)
    onehot = (flat_e[:, None] == experts[None, :]).astype(jnp.int32)
    csum = jnp.cumsum(onehot, axis=0)
    rank = jnp.sum(csum * onehot, axis=1) - 1
    counts = csum[-1]
    tiles_per_e = (counts + tm - 1) // tm
    tile_end = jnp.cumsum(tiles_per_e)
    tile_start = tile_end - tiles_per_e
    n_used = tile_end[-1]
    pos = jnp.sum(onehot * tile_start[None, :], axis=1) * tm + rank
    token = jnp.arange(2 * n, dtype=jnp.int32) // 2
    src = jnp.zeros((n_tiles * tm,), jnp.int32).at[pos].set(token)
    tiles = jnp.arange(n_tiles, dtype=jnp.int32)
    texp = jnp.sum((tiles[:, None] >= tile_end[None, :]).astype(jnp.int32), axis=1)
    texp = jnp.where(tiles < n_used, texp, texp[n_used - 1])
    first = jnp.concatenate([jnp.ones((1,), jnp.int32), (texp[1:] != texp[:-1]).astype(jnp.int32)])
    later_used = jnp.logical_and(experts[None, :] > experts[:, None], (tiles_per_e > 0)[None, :])
    next_used = jnp.min(jnp.where(later_used, experts[None, :], N_EXPERTS), axis=1)
    next_used = jnp.where(next_used == N_EXPERTS, -1, next_used)
    tile_rows = jnp.clip(counts[texp] - (tiles - tile_start[texp]) * tm, 0, tm)
    return (texp, first, next_used[texp], tile_rows, n_used.reshape(1)), src, pos


def _round_up(n, m):
    return (n + m - 1) // m * m


def kernel(x, c, w_ada, b_ada, norm1_g, w_in, mu_shift, w0, w_decay_up, a0, w_aaa_up, w_gate_up,
           k_k, k_a, r_k, ln_x_w, ln_x_b, rel_bias, beta_rwkv, beta_attn, w_out, norm2_g, w_group,
           b_group, w_expert, b_expert, w_gate, w_up, w_down, norm_f_g):
    assert w_ada.shape[0] == 1, "single trunk layer"
    batch, seq, d = x.shape
    r_width = w0.shape[1]
    a_width = beta_attn.shape[1]
    shift_width = mu_shift.shape[1]
    assert shift_width == 3 * r_width + DECAY_LORA + AAA_LORA + GATE_LORA
    x2d = x.reshape(batch * seq, d)
    row = lambda t: t.reshape(1, -1)

    c8 = jnp.pad(c, ((0, 8 - batch), (0, 0)))
    mod = _ada_call(c8, w_ada[0], row(b_ada[0]), tn=1536)[:batch].reshape(batch, 6, d)

    tm = min(512, seq)
    pw = _round_up(shift_width, 3 * LANES)
    h1 = _hmod_call(x2d, row(norm1_g[0]), mod, tm, seq)
    tm_p = min(1024, seq)
    w_in_bf = w_in[0].astype(BF16)
    p_r = _matmul_call(h1, w_in_bf, pw, pw // 3, tm_p, F32)
    p_a = _matmul_call(h1, w_in_bf[:, shift_width:], 3 * a_width, a_width, tm_p, BF16)

    mu = jnp.pad(mu_shift[0], (0, pw - shift_width))
    vecs = [row(mu), row(w0[0]), row(a0[0]), row(k_k[0]), row(k_a[0]), row(r_k[0]), row(ln_x_w[0]),
            row(ln_x_b[0]), row(beta_rwkv[0])]
    y_r = _rwkv_call(p_r, vecs, w_decay_up[0], w_aaa_up[0], w_gate_up[0], batch, seq, r_width,
                     tb=min(256, seq))

    left = LEFT_CHUNKS * CHUNK
    dist = left + CHUNK - 1 - jnp.arange(BAND + CHUNK - 1)
    base = rel_bias[0][:, jnp.clip(dist, -REL_CLIP, REL_CLIP) + REL_CLIP].astype(F32)
    bias = jnp.stack([base[:, CHUNK - 1 - qi:CHUNK - 1 - qi + BAND] for qi in range(CHUNK)], axis=1)
    nb = ATTN_BLOCK_CHUNKS
    bias = jnp.concatenate(
        [jnp.pad(bias, ((0, 0), (0, 0), (cb * CHUNK, (nb - 1 - cb) * CHUNK)), constant_values=NEG_INF)
         for cb in range(nb)], axis=1)
    bias = bias.reshape(a_width // PAIR, 2 * nb * CHUNK, BAND + (nb - 1) * CHUNK)
    y_a = _attn_call(p_a, bias, row(beta_attn[0]), batch, seq, a_width, qb=min(512, seq))

    n_route = N_GROUPS + N_EXPERTS
    wr = jnp.concatenate([w_group[0], w_expert[0].transpose(1, 0, 2).reshape(d, N_EXPERTS)], axis=1)
    wr = jnp.pad(wr, ((0, 0), (0, LANES - n_route)))
    br = jnp.pad(jnp.concatenate([b_group[0], b_expert[0].reshape(-1)]), (0, LANES - n_route))
    x1, h2, route = _out_router_call(y_r, y_a, w_out[0].astype(BF16), x2d, mod, row(norm2_g[0]),
                                     wr, row(br), seq, tm)

    f = w_gate.shape[-1]
    tm_e = 256
    plan, src, pos = _route_plan(route, tm_e)
    ys = _moe_call(plan, src, h2, w_gate[0].reshape(N_EXPERTS, d, f),
                   w_up[0].reshape(N_EXPERTS, d, f), w_down[0].reshape(N_EXPERTS, f, d), tm_e)
    out = _final_call(pos, ys, route, x1, mod, row(norm_f_g), seq, min(256, seq))
    return out.reshape(batch, seq, d)
```

```python
import functools

import jax
import jax.numpy as jnp
from jax import lax
from jax.experimental import pallas as pl
from jax.experimental.pallas import tpu as pltpu

F32 = jnp.float32
BF16 = jnp.bfloat16
HI = lax.Precision.HIGHEST

LANES = 128
HEAD_DIM = 64
PAIR = 2 * HEAD_DIM
CHUNK = 64
LEFT_CHUNKS = 8
BAND = (LEFT_CHUNKS + 1) * CHUNK
ATTN_BLOCK_CHUNKS = 4
PREP_PAIRS = 4
MOE_GATHER_SLOTS = 3
W_CHUNKS = 2
ROW_GROUP = 8
REL_CLIP = 128
DECAY_LORA = 64
AAA_LORA = 64
GATE_LORA = 160
N_GROUPS = 4
EXPERTS_PER_GROUP = 8
N_EXPERTS = N_GROUPS * EXPERTS_PER_GROUP
RMS_EPS = 1e-6
GN_EPS = 64e-5
L2_EPS = 1e-12
NEG_INF = -1e30
VMEM_LIMIT = 56 * 1024 * 1024

SHIFT1, SCALE1, GATE1, SHIFT2, SCALE2, GATE2 = range(6)


def _dot(a, b, precision=None):
    return jnp.dot(a, b, precision=precision, preferred_element_type=F32)


def _dot_nt(a, b, precision=None):
    return lax.dot_general(a, b, (((1,), (1,)), ((), ())), precision=precision,
                           preferred_element_type=F32)


def _sigmoid(x):
    return 1.0 / (1.0 + jnp.exp(-x))


def _rms(x):
    return x * lax.rsqrt(jnp.mean(x * x, axis=-1, keepdims=True) + RMS_EPS)


def _ada_kernel(c_ref, w_ref, b_ref, o_ref):
    c = c_ref[...]
    o_ref[...] = _dot_x3(c * _sigmoid(c), w_ref[...]) + b_ref[...]


def _ada_call(c8, w, b, tn):
    rows, d = c8.shape
    n = w.shape[1]
    return pl.pallas_call(
        _ada_kernel,
        grid=(n // tn,),
        in_specs=[pl.BlockSpec((rows, d), lambda j: (0, 0)),
                  pl.BlockSpec((d, tn), lambda j: (0, j)),
                  pl.BlockSpec((1, tn), lambda j: (0, j))],
        out_specs=pl.BlockSpec((rows, tn), lambda j: (0, j)),
        out_shape=jax.ShapeDtypeStruct((rows, n), F32),
        compiler_params=pltpu.CompilerParams(vmem_limit_bytes=VMEM_LIMIT),
    )(c8, w, b)


def _hmod_kernel(x_ref, g_ref, mod_ref, o_ref):
    m = mod_ref[0]
    h = _rms(x_ref[...]) * g_ref[...] * (1.0 + m[SCALE1:SCALE1 + 1]) + m[SHIFT1:SHIFT1 + 1]
    o_ref[...] = h.astype(BF16)


def _hmod_call(x2d, g, mod, tm, seq):
    rows, d = x2d.shape
    tiles_per_seq = seq // tm
    return pl.pallas_call(
        _hmod_kernel,
        grid=(rows // tm,),
        in_specs=[pl.BlockSpec((tm, d), lambda i: (i, 0)),
                  pl.BlockSpec((1, d), lambda i: (0, 0)),
                  pl.BlockSpec((1, 6, d), lambda i: (i // tiles_per_seq, 0, 0))],
        out_specs=pl.BlockSpec((tm, d), lambda i: (i, 0)),
        out_shape=jax.ShapeDtypeStruct((rows, d), BF16),
        compiler_params=pltpu.CompilerParams(
            dimension_semantics=("parallel",), vmem_limit_bytes=VMEM_LIMIT),
    )(x2d, g, mod)


def _matmul_kernel(a_ref, w_ref, o_ref, w_bf):
    @pl.when(pl.program_id(1) == 0)
    def _():
        w_bf[...] = w_ref[...].astype(BF16)

    o_ref[...] = _dot(a_ref[...], w_bf[...]).astype(o_ref.dtype)


def _matmul_call(a, w, n_out, tn, tm, out_dtype):
    rows, d = a.shape
    return pl.pallas_call(
        _matmul_kernel,
        grid=(n_out // tn, rows // tm),
        in_specs=[pl.BlockSpec((tm, d), lambda j, i: (i, 0)),
                  pl.BlockSpec((d, tn), lambda j, i: (0, j))],
        out_specs=pl.BlockSpec((tm, tn), lambda j, i: (i, j)),
        out_shape=jax.ShapeDtypeStruct((rows, n_out), out_dtype),
        scratch_shapes=[pltpu.VMEM((d, tn), BF16)],
        compiler_params=pltpu.CompilerParams(
            dimension_semantics=("parallel", "arbitrary"), vmem_limit_bytes=VMEM_LIMIT),
    )(a, w)


def _split3(t):
    hi = t.astype(BF16)
    rest = t - hi.astype(F32)
    mid = rest.astype(BF16)
    lo = (rest - mid.astype(F32)).astype(BF16)
    return hi, mid, lo


def _dot_hilo_lhs(t, m):
    hi = t.astype(BF16)
    lo = (t - hi.astype(F32)).astype(BF16)
    return _dot(lo, m) + _dot(hi, m)


def _dot_f32_rhs(m, t):
    hi, mid, lo = _split3(t)
    return _dot(m, lo) + _dot(m, mid) + _dot(m, hi)


def _dot_x3(a, b):
    ah = a.astype(BF16)
    al = (a - ah.astype(F32)).astype(BF16)
    bh = b.astype(BF16)
    bl = (b - bh.astype(F32)).astype(BF16)
    return _dot(al, bh) + _dot(ah, bl) + _dot(ah, bh)


def _rwkv_kernel(p_ref, mu_ref, w0_ref, a0_ref, kk_ref, ka_ref, rk_ref, lnw_ref, lnb_ref, beta_ref,
                 wd_ref, wa_ref, wg_ref, o_ref,
                 carry_scr, state_scr, ar_scr, bk_scr, bkh_scr, v_scr, et_scr, y_scr,
                 pm_scr, arb_scr, av_scr, vk_scr, *, width, tb):
    n_pairs = width // PAIR
    n_chunks = tb // CHUNK
    stacked = 2 * CHUNK

    @pl.when(pl.program_id(1) == 0)
    def _():
        carry_scr[...] = jnp.zeros_like(carry_scr)
        state_scr[...] = jnp.zeros_like(state_scr)

    p = p_ref[...]
    row = lax.broadcasted_iota(jnp.int32, (tb, 1), 0)
    prev = jnp.where(row == 0, carry_scr[...], pltpu.roll(p, 1, axis=0))
    carry_scr[...] = p[tb - 1:tb, :]
    ps = p + (prev - p) * mu_ref[...]

    c = width
    r = ps[:, :c]
    k = ps[:, c:2 * c]
    v = ps[:, 2 * c:3 * c]
    o = 3 * c
    xw = ps[:, o:o + DECAY_LORA]
    xa = ps[:, o + DECAY_LORA:o + DECAY_LORA + AAA_LORA]
    xg = ps[:, o + DECAY_LORA + AAA_LORA:o + DECAY_LORA + AAA_LORA + GATE_LORA]

    z = w0_ref[...] + _dot_x3(jnp.tanh(xw), wd_ref[...])
    softplus_neg_z = jnp.maximum(-z, 0.0) + jnp.log(1.0 + jnp.exp(-jnp.abs(z)))
    lw = -jnp.exp(-softplus_neg_z - 0.5)
    a = _sigmoid(a0_ref[...] + _dot(xa.astype(BF16), wa_ref[...].astype(BF16)))
    g = _dot(_sigmoid(xg).astype(BF16), wg_ref[...].astype(BF16))

    li = lax.broadcasted_iota(jnp.int32, (PAIR, PAIR), 0)
    lj = lax.broadcasted_iota(jnp.int32, (PAIR, PAIR), 1)
    head_ones = ((li // HEAD_DIM) == (lj // HEAD_DIM)).astype(BF16)

    def head_sum(t):
        return jnp.concatenate(
            [_dot_hilo_lhs(t[:, q * PAIR:(q + 1) * PAIR], head_ones) for q in range(n_pairs)], axis=1)

    kk = k * kk_ref[...]
    kk = kk / jnp.maximum(jnp.sqrt(head_sum(kk * kk)), L2_EPS)
    k2 = k * (1.0 + (a - 1.0) * ka_ref[...])

    ti = lax.broadcasted_iota(jnp.int32, (tb, tb), 0)
    tj = lax.broadcasted_iota(jnp.int32, (tb, tb), 1)
    tri = jnp.logical_and((ti // CHUNK) == (tj // CHUNK), tj <= ti).astype(BF16)
    cum = _dot_f32_rhs(tri, lw)
    tot = jnp.concatenate(
        [jnp.broadcast_to(cum[(ci + 1) * CHUNK - 1:(ci + 1) * CHUNK, :], (CHUNK, c))
         for ci in range(n_chunks)], axis=0)

    e_neg = jnp.exp(-cum)
    e_rem = jnp.exp(tot - cum)
    kka = kk * a
    lane = lax.broadcasted_iota(jnp.int32, (1, PAIR), 1)
    first_head = lane < HEAD_DIM

    def put(scr, base, val):
        for q in range(n_pairs):
            vq = val[:, q * PAIR:(q + 1) * PAIR]
            h0 = jnp.where(first_head, vq, 0.0).astype(scr.dtype)
            h1 = jnp.where(first_head, 0.0, vq).astype(scr.dtype)
            for ci in range(n_chunks):
                rows = slice(ci * CHUNK, (ci + 1) * CHUNK)
                scr[q, ci, base:base + CHUNK, :] = h0[rows]
                scr[q, ci, base + CHUNK:base + stacked, :] = h1[rows]

    put(ar_scr, 0, -kk * jnp.exp(cum - lw))
    put(ar_scr, stacked, r * jnp.exp(cum))
    put(bk_scr, 0, kka * e_neg)
    put(bk_scr, stacked, k2 * e_neg)
    put(bkh_scr, 0, kka * e_rem)
    put(bkh_scr, stacked, k2 * e_rem)
    put(v_scr, 0, v)
    e_tot = jnp.exp(tot)
    for q in range(n_pairs):
        et_scr[q] = e_tot[:, q * PAIR:(q + 1) * PAIR]

    si = lax.broadcasted_iota(jnp.int32, (2 * stacked, 2 * stacked), 0)
    sj = lax.broadcasted_iota(jnp.int32, (2 * stacked, 2 * stacked), 1)
    keep = jnp.where(si < stacked, si, si - stacked + 1) > (sj % stacked)
    eye = (li == lj).astype(F32)

    chunks = range(n_chunks)

    def side(ta, tb):
        return jnp.concatenate([ta, tb], axis=1)

    def blockdiag(ta, tb):
        za = jnp.zeros((ta.shape[0], tb.shape[1]), ta.dtype)
        zb = jnp.zeros((tb.shape[0], ta.shape[1]), ta.dtype)
        return jnp.concatenate([side(ta, za), side(zb, tb)], axis=0)

    def halves(t):
        return t[:, :PAIR], t[:, PAIR:]

    def bf(t):
        return t.astype(BF16)

    hr = lax.broadcasted_iota(jnp.int32, (CHUNK, 4 * CHUNK), 0)
    hl = lax.broadcasted_iota(jnp.int32, (CHUNK, 4 * CHUNK), 1)
    eye4 = (hr == hl % CHUNK).astype(F32)
    zero_bf = jnp.zeros((CHUNK, stacked), BF16)

    def head_blocks(t):
        z = jnp.zeros_like(t)
        return jnp.concatenate([jnp.where(hl // CHUNK == j, t, z) for j in range(4)], axis=0)

    def prepare(step):
        couples = [(step * PREP_PAIRS + 2 * j, ci) for j in range(PREP_PAIRS // 2) for ci in chunks]
        pw = []
        for qa, ci in couples:
            n, tail, vm = [], [], []
            for q in (qa, qa + 1):
                sc = jnp.where(keep, _dot_nt(ar_scr[q, ci], bk_scr[q, ci]), 0.0)
                scb = bf(sc)
                arb_scr[q, ci] = scb[stacked:, :stacked]
                n.append(sc[:stacked, :stacked])
                tail.append(scb[:, stacked:])
                vm.append(v_scr[q, ci])
            av = _dot(side(*tail), blockdiag(bf(vm[0]), bf(vm[1])))
            vk = _dot(side(bf(vm[0].T), bf(vm[1].T)),
                      blockdiag(bkh_scr[qa, ci, stacked:, :], bkh_scr[qa + 1, ci, stacked:, :]))
            for q, t_av, t_vk in zip((qa, qa + 1), halves(av), halves(vk)):
                av_scr[q, ci] = t_av
                vk_scr[q, ci] = t_vk
            pw.append(bf(side(n[0][:CHUNK] + n[0][CHUNK:], n[1][:CHUNK] + n[1][CHUNK:])))
        acc = [eye4 + p_ for p_ in pw]
        pw = [bf(_dot(p_, head_blocks(p_))) for p_ in pw]
        for it in range(1, 6):
            rhs = [head_blocks(p_) for p_ in pw]
            if it < 5:
                both = [_dot(jnp.concatenate([p_, bf(a_)], axis=0), r_)
                        for p_, a_, r_ in zip(pw, acc, rhs)]
                pw = [bf(t[:CHUNK]) for t in both]
                acc = [a_ + t[CHUNK:] for a_, t in zip(acc, both)]
            else:
                acc = [a_ + _dot(bf(a_), r_) for a_, r_ in zip(acc, rhs)]
        for (qa, ci), a_ in zip(couples, acc):
            for q, t in zip((qa, qa + 1), halves(bf(a_))):
                pm_scr[q, ci] = jnp.concatenate(
                    [jnp.where(first_head, t, zero_bf), jnp.where(first_head, zero_bf, t)], axis=0)

    for step in range(n_pairs // PREP_PAIRS):
        prepare(step)

    firsts = range(0, n_pairs, 2)
    s = [side(state_scr[q], state_scr[q + 1]) for q in firsts]
    for ci in chunks:
        rows = slice(ci * CHUNK, (ci + 1) * CHUNK)

        def both(scr, part=slice(None)):
            return [side(scr[q, ci, part, :], scr[q + 1, ci, part, :]) for q in firsts]

        top, bottom = slice(0, stacked), slice(stacked, 2 * stacked)
        sr = [_dot_nt(ar_, blockdiag(*halves(bf(s_)))) for ar_, s_ in zip(both(ar_scr), s)]
        u = [_dot(pm_, blockdiag(*halves(bf(sr_[:stacked] + av_))))
             for pm_, sr_, av_ in zip(both(pm_scr), sr, both(av_scr, top))]
        ym = [sr_[stacked:] + av_ + _dot(arb_, blockdiag(*halves(bf(u_))))
              for sr_, av_, arb_, u_ in zip(sr, both(av_scr, bottom), both(arb_scr), u)]
        for q, ym_ in zip(firsts, ym):
            ya, yb = halves(ym_[:CHUNK] + ym_[CHUNK:])
            y_scr[q, rows, :] = ya
            y_scr[q + 1, rows, :] = yb
        decay = [side(et_scr[q, ci * CHUNK:ci * CHUNK + 1, :], et_scr[q + 1, ci * CHUNK:ci * CHUNK + 1, :])
                 for q in firsts]
        s = [s_ * d_ + _dot(side(*(bf(t.T) for t in halves(u_))), blockdiag(*halves(bh_))) + vk_
             for s_, d_, u_, bh_, vk_ in zip(s, decay, u, both(bkh_scr, top), both(vk_scr))]
    for q, s_ in zip(firsts, s):
        state_scr[q], state_scr[q + 1] = halves(s_)

    y = jnp.concatenate([y_scr[q] for q in range(n_pairs)], axis=1)
    mean = head_sum(y) * (1.0 / HEAD_DIM)
    d = y - mean
    var = head_sum(d * d) * (1.0 / HEAD_DIM)
    yn = d * lax.rsqrt(var + GN_EPS) * lnw_ref[...] + lnb_ref[...]
    bonus = head_sum(r * k2 * rk_ref[...]) * v
    o_ref[...] = (yn + bonus) * g * beta_ref[...]


def _rwkv_call(p_r, vecs, wd, wa, wg, batch, seq, width, tb):
    assert CHUNK == HEAD_DIM, "chunk operands stack two heads' rows against their 2 * HEAD_DIM lanes"
    assert (width // PAIR) % PREP_PAIRS == 0 and PREP_PAIRS % 2 == 0
    pw = p_r.shape[1]
    n_pairs = width // PAIR
    n_chunks = tb // CHUNK
    tiles = seq // tb
    vec_specs = [pl.BlockSpec((1, v.shape[1]), lambda b, t: (0, 0)) for v in vecs]
    mat_specs = [pl.BlockSpec(m.shape, lambda b, t: (0, 0)) for m in (wd, wa, wg)]
    stacked2 = pltpu.VMEM((n_pairs, n_chunks, 4 * CHUNK, PAIR), BF16)
    return pl.pallas_call(
        functools.partial(_rwkv_kernel, width=width, tb=tb),
        grid=(batch, tiles),
        in_specs=[pl.BlockSpec((tb, pw), lambda b, t: (b * tiles + t, 0))] + vec_specs + mat_specs,
        out_specs=pl.BlockSpec((tb, width), lambda b, t: (b * tiles + t, 0)),
        out_shape=jax.ShapeDtypeStruct((batch * seq, width), F32),
        scratch_shapes=[pltpu.VMEM((1, pw), F32),
                        pltpu.VMEM((n_pairs, PAIR, PAIR), F32),
                        stacked2, stacked2, stacked2,
                        pltpu.VMEM((n_pairs, n_chunks, 2 * CHUNK, PAIR), F32),
                        pltpu.VMEM((n_pairs, tb, PAIR), F32),
                        pltpu.VMEM((n_pairs, tb, PAIR), F32),
                        pltpu.VMEM((n_pairs, n_chunks, 2 * CHUNK, PAIR), BF16),
                        pltpu.VMEM((n_pairs, n_chunks, 2 * CHUNK, PAIR), BF16),
                        pltpu.VMEM((n_pairs, n_chunks, 4 * CHUNK, PAIR), F32),
                        pltpu.VMEM((n_pairs, n_chunks, 2 * CHUNK, PAIR), F32)],
        compiler_params=pltpu.CompilerParams(
            dimension_semantics=("parallel", "arbitrary"), vmem_limit_bytes=VMEM_LIMIT),
    )(p_r, *vecs, wd, wa, wg)


def _attn_kernel(q_ref, kp_ref, kc_ref, vp_ref, vc_ref, bias_ref, beta_ref, o_ref, kwin, vwin,
                 *, width, qb):
    i = pl.program_id(1)
    n_pairs = width // PAIR
    kwin[0:qb, :] = kp_ref[...]
    kwin[qb:2 * qb, :] = kc_ref[...]
    vwin[0:qb, :] = vp_ref[...]
    vwin[qb:2 * qb, :] = vc_ref[...]
    lane = lax.broadcasted_iota(jnp.int32, (1, PAIR), 1)
    first_head = lane < HEAD_DIM
    rows = ATTN_BLOCK_CHUNKS * CHUNK
    win = BAND + rows - CHUNK
    col = lax.broadcasted_iota(jnp.int32, (1, win), 1)
    scale = HEAD_DIM ** -0.5
    left = LEFT_CHUNKS * CHUNK

    for q in range(n_pairs):
        lanes = slice(q * PAIR, (q + 1) * PAIR)
        for blk in range(qb // rows):
            r0 = blk * rows
            qc = q_ref[r0:r0 + rows, lanes]
            zero = jnp.zeros_like(qc)
            qm = jnp.concatenate([jnp.where(first_head, qc, zero), jnp.where(first_head, zero, qc)],
                                 axis=0)
            s = _dot_nt(qm, kwin[r0:r0 + win, lanes]) * scale + bias_ref[q]
            valid = jnp.logical_or(i > 0, col + r0 >= left)
            s = jnp.where(valid, s, NEG_INF)
            e = jnp.exp(s - jnp.max(s, axis=-1, keepdims=True))
            denom = jnp.sum(e, axis=-1, keepdims=True)
            o = _dot(e.astype(BF16), vwin[r0:r0 + win, lanes]) / denom
            oc = jnp.where(first_head, o[:rows], o[rows:])
            o_ref[r0:r0 + rows, lanes] = oc * beta_ref[:, lanes]


def _attn_call(p_a, bias, beta, batch, seq, width, qb):
    assert qb == LEFT_CHUNKS * CHUNK, "key window = previous block + current block"
    tiles = seq // qb
    n_pairs = width // PAIR

    def cur(col):
        return pl.BlockSpec((qb, width), lambda b, t: (b * tiles + t, col))

    def prev(col):
        return pl.BlockSpec((qb, width), lambda b, t: (b * tiles + jnp.maximum(t - 1, 0), col))

    return pl.pallas_call(
        functools.partial(_attn_kernel, width=width, qb=qb),
        grid=(batch, tiles),
        in_specs=[cur(0), prev(1), cur(1), prev(2), cur(2),
                  pl.BlockSpec(bias.shape, lambda b, t: (0, 0, 0)),
                  pl.BlockSpec((1, width), lambda b, t: (0, 0))],
        out_specs=pl.BlockSpec((qb, width), lambda b, t: (b * tiles + t, 0)),
        out_shape=jax.ShapeDtypeStruct((batch * seq, width), F32),
        scratch_shapes=[pltpu.VMEM((2 * qb, width), BF16), pltpu.VMEM((2 * qb, width), BF16)],
        compiler_params=pltpu.CompilerParams(
            dimension_semantics=("parallel", "arbitrary"), vmem_limit_bytes=VMEM_LIMIT),
    )(p_a, p_a, p_a, p_a, p_a, bias, beta)


def _out_router_kernel(yr_ref, ya_ref, wo_ref, x_ref, mod_ref, g2_ref, wr_ref, br_ref,
                       x1_ref, h2_ref, route_ref, *, r_width):
    y = (_dot(yr_ref[...].astype(BF16), wo_ref[0:r_width, :])
         + _dot(ya_ref[...].astype(BF16), wo_ref[r_width:, :]))
    m = mod_ref[0]
    x1 = x_ref[...] + m[GATE1:GATE1 + 1] * y
    x1_ref[...] = x1
    h2 = _rms(x1) * g2_ref[...] * (1.0 + m[SCALE2:SCALE2 + 1]) + m[SHIFT2:SHIFT2 + 1]
    h2_ref[...] = h2

    logits = _dot_x3(h2, wr_ref[...]) + br_ref[...]
    li = lax.broadcasted_iota(jnp.int32, logits.shape, 1).astype(F32)
    none = jnp.float32(-jnp.inf)
    far = jnp.float32(LANES)
    is_group = li < N_GROUPS
    gl = jnp.where(is_group, logits, none)
    gmax = jnp.max(gl, axis=-1, keepdims=True)
    gidx = jnp.min(jnp.where(gl == gmax, li, far), axis=-1, keepdims=True)
    g_w = 1.0 / jnp.sum(jnp.where(is_group, jnp.exp(logits - gmax), 0.0), axis=-1, keepdims=True)
    lo = N_GROUPS + EXPERTS_PER_GROUP * gidx
    el = jnp.where(jnp.logical_and(li >= lo, li < lo + EXPERTS_PER_GROUP), logits, none)
    m1 = jnp.max(el, axis=-1, keepdims=True)
    i1 = jnp.min(jnp.where(el == m1, li, far), axis=-1, keepdims=True)
    el2 = jnp.where(li == i1, none, el)
    m2 = jnp.max(el2, axis=-1, keepdims=True)
    i2 = jnp.min(jnp.where(el2 == m2, li, far), axis=-1, keepdims=True)
    e2 = jnp.exp(m2 - m1)
    w1 = g_w / (1.0 + e2)
    w2 = g_w * e2 / (1.0 + e2)
    route = jnp.where(li == 0.0, i1 - N_GROUPS, 0.0) + jnp.where(li == 1.0, i2 - N_GROUPS, 0.0)
    route_ref[...] = route + jnp.where(li == 2.0, w1, 0.0) + jnp.where(li == 3.0, w2, 0.0)


def _out_router_call(y_r, y_a, wo, x2d, mod, g2, wr, br, seq, tm):
    rows, d = x2d.shape
    r_width = y_r.shape[1]
    a_width = y_a.shape[1]
    tiles_per_seq = seq // tm
    return pl.pallas_call(
        functools.partial(_out_router_kernel, r_width=r_width),
        grid=(rows // tm,),
        in_specs=[pl.BlockSpec((tm, r_width), lambda i: (i, 0)),
                  pl.BlockSpec((tm, a_width), lambda i: (i, 0)),
                  pl.BlockSpec(wo.shape, lambda i: (0, 0)),
                  pl.BlockSpec((tm, d), lambda i: (i, 0)),
                  pl.BlockSpec((1, 6, d), lambda i: (i // tiles_per_seq, 0, 0)),
                  pl.BlockSpec((1, d), lambda i: (0, 0)),
                  pl.BlockSpec((d, LANES), lambda i: (0, 0)),
                  pl.BlockSpec((1, LANES), lambda i: (0, 0))],
        out_specs=[pl.BlockSpec((tm, d), lambda i: (i, 0)),
                   pl.BlockSpec((tm, d), lambda i: (i, 0)),
                   pl.BlockSpec((tm, LANES), lambda i: (i, 0))],
        out_shape=[jax.ShapeDtypeStruct((rows, d), F32),
                   jax.ShapeDtypeStruct((rows, d), F32),
                   jax.ShapeDtypeStruct((rows, LANES), F32)],
        compiler_params=pltpu.CompilerParams(
            dimension_semantics=("parallel",), vmem_limit_bytes=VMEM_LIMIT),
    )(y_r, y_a, wo, x2d, mod, g2, wr, br)


def _moe_kernel(texp_ref, first_ref, next_ref, rows_ref, meta_ref, src_ref,
                h2_hbm, wg_hbm, wu_hbm, wd_hbm, o_ref,
                xbuf, xsem, wg_land, wu_land, wd_land, wsem, wg_bf, wu_bf, wd_bf, *, tm):
    t = pl.program_id(0)
    n_used = meta_ref[0]
    slot = lax.rem(t, MOE_GATHER_SLOTS)

    def weight_copies(e):
        cps = []
        for m, (hbm, land) in enumerate(((wg_hbm, wg_land), (wu_hbm, wu_land), (wd_hbm, wd_land))):
            rows = hbm.shape[1] // W_CHUNKS
            for ck in range(W_CHUNKS):
                part = pl.ds(ck * rows, rows)
                cps.append(pltpu.make_async_copy(hbm.at[e, part], land.at[part], wsem.at[m, ck]))
        return cps

    def row_groups(tile):
        return (rows_ref[tile] + ROW_GROUP - 1) // ROW_GROUP

    def gather_rows(tile, unrolled=False):
        buf_slot = lax.rem(tile, MOE_GATHER_SLOTS)
        n_groups = row_groups(tile)

        def body(g, carry):
            for k in range(ROW_GROUP):
                i = g * ROW_GROUP + k
                tok = src_ref[tile * tm + i]
                pltpu.make_async_copy(h2_hbm.at[pl.ds(tok, 1)], xbuf.at[buf_slot, pl.ds(i, 1)],
                                      xsem.at[buf_slot]).start()
            return carry

        if unrolled:
            for g in range(tm // ROW_GROUP):
                @pl.when(g < n_groups)
                def _(g=g):
                    body(g, 0)
        else:
            lax.fori_loop(0, n_groups, body, 0)

    @pl.when(t == 0)
    def _():
        xbuf[...] = jnp.zeros_like(xbuf)
        for cp in weight_copies(texp_ref[0]):
            cp.start()
        for ahead in range(MOE_GATHER_SLOTS - 1):
            @pl.when(ahead < n_used)
            def _():
                gather_rows(ahead)

    is_first = jnp.logical_and(t < n_used, first_ref[t] == 1)

    @pl.when(is_first)
    def _():
        for cp in weight_copies(texp_ref[t]):
            cp.wait()
        wg_bf[...] = wg_land[...].astype(BF16)
        wu_bf[...] = wu_land[...].astype(BF16)
        wd_bf[...] = wd_land[...].astype(BF16)

    @pl.when(t + MOE_GATHER_SLOTS - 1 < n_used)
    def _():
        gather_rows(t + MOE_GATHER_SLOTS - 1, unrolled=True)

    @pl.when(jnp.logical_and(is_first, next_ref[t] >= 0))
    def _():
        for cp in weight_copies(next_ref[t]):
            cp.start()

    @pl.when(t < n_used)
    def _():
        def wait_group(g, carry):
            r0 = pl.multiple_of(g * ROW_GROUP, ROW_GROUP)
            pltpu.make_async_copy(h2_hbm.at[pl.ds(0, ROW_GROUP)],
                                  xbuf.at[slot, pl.ds(r0, ROW_GROUP)], xsem.at[slot]).wait()
            return carry

        lax.fori_loop(0, row_groups(t), wait_group, 0)
        x = xbuf[slot].astype(BF16)
        gate = _dot(x, wg_bf[...])
        up = _dot(x, wu_bf[...])
        h = gate * _sigmoid(gate) * up
        o_ref[...] = _dot(h.astype(BF16), wd_bf[...])

    @pl.when(t >= n_used)
    def _():
        o_ref[...] = jnp.zeros_like(o_ref)


def _moe_call(plan, src, h2, wg, wu, wd, tm):
    n_e, d, f = wg.shape
    n_tiles = plan[0].shape[0]
    grid_spec = pltpu.PrefetchScalarGridSpec(
        num_scalar_prefetch=len(plan) + 1,
        grid=(n_tiles,),
        in_specs=[pl.BlockSpec(memory_space=pl.ANY)] * 4,
        out_specs=pl.BlockSpec((tm, d), lambda t, *_: (t, 0)),
        scratch_shapes=[pltpu.VMEM((MOE_GATHER_SLOTS, tm, d), F32),
                        pltpu.SemaphoreType.DMA((MOE_GATHER_SLOTS,)),
                        pltpu.VMEM((d, f), F32), pltpu.VMEM((d, f), F32), pltpu.VMEM((f, d), F32),
                        pltpu.SemaphoreType.DMA((3, W_CHUNKS)),
                        pltpu.VMEM((d, f), BF16), pltpu.VMEM((d, f), BF16), pltpu.VMEM((f, d), BF16)])
    return pl.pallas_call(
        functools.partial(_moe_kernel, tm=tm),
        grid_spec=grid_spec,
        out_shape=jax.ShapeDtypeStruct((n_tiles * tm, d), F32),
        compiler_params=pltpu.CompilerParams(
            dimension_semantics=("arbitrary",), vmem_limit_bytes=VMEM_LIMIT),
    )(*plan, src, h2, wg, wu, wd)


def _final_kernel(pos_ref, ys_hbm, route_ref, x1_ref, mod_ref, gf_ref, o_ref, ybuf, sem, *, tm):
    i = pl.program_id(0)
    n = pl.num_programs(0)
    slot = i % 2

    def gather_rows(tile, buf_slot, unrolled):
        base = tile * tm * 2

        def body(r, carry):
            for s in range(2):
                pltpu.make_async_copy(ys_hbm.at[pl.ds(pos_ref[base + 2 * r + s], 1)],
                                      ybuf.at[buf_slot, s, pl.ds(r, 1)], sem.at[buf_slot]).start()
            return carry

        if unrolled:
            for r in range(tm):
                body(r, 0)
        else:
            lax.fori_loop(0, tm, body, 0, unroll=4)

    @pl.when(i == 0)
    def _():
        gather_rows(0, 0, unrolled=False)

    @pl.when(i + 1 < n)
    def _():
        gather_rows(i + 1, 1 - slot, unrolled=True)

    for s in range(2):
        pltpu.make_async_copy(ys_hbm.at[pl.ds(0, tm)], ybuf.at[slot, s], sem.at[slot]).wait()
    route = route_ref[...]
    w0 = route[:, 2:3]
    w1 = route[:, 3:4]
    f = w0 * ybuf[slot, 0] + w1 * ybuf[slot, 1]
    x2 = x1_ref[...] + mod_ref[0][GATE2:GATE2 + 1] * f
    o_ref[...] = _rms(x2) * gf_ref[...]


def _final_call(pos, ys, route, x1, mod, gf, seq, tm):
    rows, d = x1.shape
    tiles_per_seq = seq // tm
    grid_spec = pltpu.PrefetchScalarGridSpec(
        num_scalar_prefetch=1,
        grid=(rows // tm,),
        in_specs=[pl.BlockSpec(memory_space=pl.ANY),
                  pl.BlockSpec((tm, LANES), lambda i, pos: (i, 0)),
                  pl.BlockSpec((tm, d), lambda i, pos: (i, 0)),
                  pl.BlockSpec((1, 6, d), lambda i, pos: (i // tiles_per_seq, 0, 0)),
                  pl.BlockSpec((1, d), lambda i, pos: (0, 0))],
        out_specs=pl.BlockSpec((tm, d), lambda i, pos: (i, 0)),
        scratch_shapes=[pltpu.VMEM((2, 2, tm, d), F32), pltpu.SemaphoreType.DMA((2,))])
    return pl.pallas_call(
        functools.partial(_final_kernel, tm=tm),
        grid_spec=grid_spec,
        out_shape=jax.ShapeDtypeStruct((rows, d), F32),
        compiler_params=pltpu.CompilerParams(
            dimension_semantics=("arbitrary",), vmem_limit_bytes=VMEM_LIMIT),
    )(pos, ys, route, x1, mod, gf)


def _route_plan(route, tm):
    n = route.shape[0]
    flat_e = route[:, 0:2].astype(jnp.int32).reshape(-1)
    n_tiles = (2 * n + N_EXPERTS * (tm - 1)) // tm
    experts = jnp.arange(N_EXPERTS, dtype=jnp.int32)
    onehot = (flat_e[:, None] == experts[None, :]).astype(jnp.int32)
    csum = jnp.cumsum(onehot, axis=0)
    rank = jnp.sum(csum * onehot, axis=1) - 1
    counts = csum[-1]
    tiles_per_e = (counts + tm - 1) // tm
    tile_end = jnp.cumsum(tiles_per_e)
    tile_start = tile_end - tiles_per_e
    n_used = tile_end[-1]
    pos = jnp.sum(onehot * tile_start[None, :], axis=1) * tm + rank
    token = jnp.arange(2 * n, dtype=jnp.int32) // 2
    src = jnp.zeros((n_tiles * tm,), jnp.int32).at[pos].set(token)
    tiles = jnp.arange(n_tiles, dtype=jnp.int32)
    texp = jnp.sum((tiles[:, None] >= tile_end[None, :]).astype(jnp.int32), axis=1)
    texp = jnp.where(tiles < n_used, texp, texp[n_used - 1])
    first = jnp.concatenate([jnp.ones((1,), jnp.int32), (texp[1:] != texp[:-1]).astype(jnp.int32)])
    later_used = jnp.logical_and(experts[None, :] > experts[:, None], (tiles_per_e > 0)[None, :])
    next_used = jnp.min(jnp.where(later_used, experts[None, :], N_EXPERTS), axis=1)
    next_used = jnp.where(next_used == N_EXPERTS, -1, next_used)
    tile_rows = jnp.clip(counts[texp] - (tiles - tile_start[texp]) * tm, 0, tm)
    return (texp, first, next_used[texp], tile_rows, n_used.reshape(1)), src, pos


def _round_up(n, m):
    return (n + m - 1) // m * m


def kernel(x, c, w_ada, b_ada, norm1_g, w_in, mu_shift, w0, w_decay_up, a0, w_aaa_up, w_gate_up,
           k_k, k_a, r_k, ln_x_w, ln_x_b, rel_bias, beta_rwkv, beta_attn, w_out, norm2_g, w_group,
           b_group, w_expert, b_expert, w_gate, w_up, w_down, norm_f_g):
    assert w_ada.shape[0] == 1, "single trunk layer"
    batch, seq, d = x.shape
    r_width = w0.shape[1]
    a_width = beta_attn.shape[1]
    shift_width = mu_shift.shape[1]
    assert shift_width == 3 * r_width + DECAY_LORA + AAA_LORA + GATE_LORA
    x2d = x.reshape(batch * seq, d)
    row = lambda t: t.reshape(1, -1)

    c8 = jnp.pad(c, ((0, 8 - batch), (0, 0)))
    mod = _ada_call(c8, w_ada[0], row(b_ada[0]), tn=1536)[:batch].reshape(batch, 6, d)

    tm = min(512, seq)
    pw = _round_up(shift_width, 3 * LANES)
    h1 = _hmod_call(x2d, row(norm1_g[0]), mod, tm, seq)
    tm_p = min(1024, seq)
    p_r = _matmul_call(h1, w_in[0], pw, pw // 3, tm_p, F32)
    p_a = _matmul_call(h1, w_in[0][:, shift_width:], 3 * a_width, a_width, tm_p, BF16)

    mu = jnp.pad(mu_shift[0], (0, pw - shift_width))
    vecs = [row(mu), row(w0[0]), row(a0[0]), row(k_k[0]), row(k_a[0]), row(r_k[0]), row(ln_x_w[0]),
            row(ln_x_b[0]), row(beta_rwkv[0])]
    y_r = _rwkv_call(p_r, vecs, w_decay_up[0], w_aaa_up[0], w_gate_up[0], batch, seq, r_width,
                     tb=min(256, seq))

    left = LEFT_CHUNKS * CHUNK
    dist = left + CHUNK - 1 - jnp.arange(BAND + CHUNK - 1)
    base = rel_bias[0][:, jnp.clip(dist, -REL_CLIP, REL_CLIP) + REL_CLIP].astype(F32)
    bias = jnp.stack([base[:, CHUNK - 1 - qi:CHUNK - 1 - qi + BAND] for qi in range(CHUNK)], axis=1)
    nb = ATTN_BLOCK_CHUNKS
    bias = jnp.concatenate(
        [jnp.pad(bias, ((0, 0), (0, 0), (cb * CHUNK, (nb - 1 - cb) * CHUNK)), constant_values=NEG_INF)
         for cb in range(nb)], axis=1)
    bias = bias.reshape(a_width // PAIR, 2 * nb * CHUNK, BAND + (nb - 1) * CHUNK)
    y_a = _attn_call(p_a, bias, row(beta_attn[0]), batch, seq, a_width, qb=min(512, seq))

    n_route = N_GROUPS + N_EXPERTS
    wr = jnp.concatenate([w_group[0], w_expert[0].transpose(1, 0, 2).reshape(d, N_EXPERTS)], axis=1)
    wr = jnp.pad(wr, ((0, 0), (0, LANES - n_route)))
    br = jnp.pad(jnp.concatenate([b_group[0], b_expert[0].reshape(-1)]), (0, LANES - n_route))
    x1, h2, route = _out_router_call(y_r, y_a, w_out[0].astype(BF16), x2d, mod, row(norm2_g[0]),
                                     wr, row(br), seq, tm)

    f = w_gate.shape[-1]
    tm_e = 256
    plan, src, pos = _route_plan(route, tm_e)
    ys = _moe_call(plan, src, h2, w_gate[0].reshape(N_EXPERTS, d, f),
                   w_up[0].reshape(N_EXPERTS, d, f), w_down[0].reshape(N_EXPERTS, f, d), tm_e)
    out = _final_call(pos, ys, route, x1, mod, row(norm_f_g), seq, min(256, seq))
    return out.reshape(batch, seq, d)
```

```python
import functools

import jax
import jax.numpy as jnp
from jax import lax
from jax.experimental import pallas as pl
from jax.experimental.pallas import tpu as pltpu

F32 = jnp.float32
BF16 = jnp.bfloat16
HI = lax.Precision.HIGHEST

LANES = 128
HEAD_DIM = 64
PAIR = 2 * HEAD_DIM
CHUNK = 64
LEFT_CHUNKS = 8
BAND = (LEFT_CHUNKS + 1) * CHUNK
ATTN_BLOCK_CHUNKS = 4
PREP_PAIRS = 4
MOE_GATHER_SLOTS = 3
W_CHUNKS = 2
ROW_GROUP = 8
REL_CLIP = 128
DECAY_LORA = 64
AAA_LORA = 64
GATE_LORA = 160
N_GROUPS = 4
EXPERTS_PER_GROUP = 8
N_EXPERTS = N_GROUPS * EXPERTS_PER_GROUP
RMS_EPS = 1e-6
GN_EPS = 64e-5
L2_EPS = 1e-12
NEG_INF = -1e30
VMEM_LIMIT = 56 * 1024 * 1024

SHIFT1, SCALE1, GATE1, SHIFT2, SCALE2, GATE2 = range(6)


def _dot(a, b, precision=None):
    return jnp.dot(a, b, precision=precision, preferred_element_type=F32)


def _dot_nt(a, b, precision=None):
    return lax.dot_general(a, b, (((1,), (1,)), ((), ())), precision=precision,
                           preferred_element_type=F32)


def _sigmoid(x):
    return 1.0 / (1.0 + jnp.exp(-x))


def _rms(x):
    return x * lax.rsqrt(jnp.mean(x * x, axis=-1, keepdims=True) + RMS_EPS)


def _ada_kernel(c_ref, w_ref, b_ref, o_ref):
    c = c_ref[...]
    o_ref[...] = _dot_x3(c * _sigmoid(c), w_ref[...]) + b_ref[...]


def _ada_call(c8, w, b, tn):
    rows, d = c8.shape
    n = w.shape[1]
    return pl.pallas_call(
        _ada_kernel,
        grid=(n // tn,),
        in_specs=[pl.BlockSpec((rows, d), lambda j: (0, 0)),
                  pl.BlockSpec((d, tn), lambda j: (0, j)),
                  pl.BlockSpec((1, tn), lambda j: (0, j))],
        out_specs=pl.BlockSpec((rows, tn), lambda j: (0, j)),
        out_shape=jax.ShapeDtypeStruct((rows, n), F32),
        compiler_params=pltpu.CompilerParams(vmem_limit_bytes=VMEM_LIMIT),
    )(c8, w, b)


def _hmod_kernel(x_ref, g_ref, mod_ref, o_ref):
    m = mod_ref[0]
    h = _rms(x_ref[...]) * g_ref[...] * (1.0 + m[SCALE1:SCALE1 + 1]) + m[SHIFT1:SHIFT1 + 1]
    o_ref[...] = h.astype(BF16)


def _hmod_call(x2d, g, mod, tm, seq):
    rows, d = x2d.shape
    tiles_per_seq = seq // tm
    return pl.pallas_call(
        _hmod_kernel,
        grid=(rows // tm,),
        in_specs=[pl.BlockSpec((tm, d), lambda i: (i, 0)),
                  pl.BlockSpec((1, d), lambda i: (0, 0)),
                  pl.BlockSpec((1, 6, d), lambda i: (i // tiles_per_seq, 0, 0))],
        out_specs=pl.BlockSpec((tm, d), lambda i: (i, 0)),
        out_shape=jax.ShapeDtypeStruct((rows, d), BF16),
        compiler_params=pltpu.CompilerParams(
            dimension_semantics=("parallel",), vmem_limit_bytes=VMEM_LIMIT),
    )(x2d, g, mod)


def _matmul_kernel(a_ref, wt_ref, o_ref, w_bf):
    @pl.when(pl.program_id(1) == 0)
    def _():
        w_bf[...] = wt_ref[...].astype(BF16)

    o_ref[...] = _dot_nt(a_ref[...], w_bf[...]).astype(o_ref.dtype)


def _matmul_call(a, wt, n_out, tn, tm, out_dtype):
    rows, d = a.shape
    return pl.pallas_call(
        _matmul_kernel,
        grid=(n_out // tn, rows // tm),
        in_specs=[pl.BlockSpec((tm, d), lambda j, i: (i, 0)),
                  pl.BlockSpec((tn, d), lambda j, i: (j, 0))],
        out_specs=pl.BlockSpec((tm, tn), lambda j, i: (i, j)),
        out_shape=jax.ShapeDtypeStruct((rows, n_out), out_dtype),
        scratch_shapes=[pltpu.VMEM((tn, d), BF16)],
        compiler_params=pltpu.CompilerParams(
            dimension_semantics=("parallel", "arbitrary"), vmem_limit_bytes=VMEM_LIMIT),
    )(a, wt)


def _split3(t):
    hi = t.astype(BF16)
    rest = t - hi.astype(F32)
    mid = rest.astype(BF16)
    lo = (rest - mid.astype(F32)).astype(BF16)
    return hi, mid, lo


def _dot_hilo_lhs(t, m):
    hi = t.astype(BF16)
    lo = (t - hi.astype(F32)).astype(BF16)
    return _dot(lo, m) + _dot(hi, m)


def _dot_f32_rhs(m, t):
    hi, mid, lo = _split3(t)
    return _dot(m, lo) + _dot(m, mid) + _dot(m, hi)


def _dot_x3(a, b):
    ah = a.astype(BF16)
    al = (a - ah.astype(F32)).astype(BF16)
    bh = b.astype(BF16)
    bl = (b - bh.astype(F32)).astype(BF16)
    return _dot(al, bh) + _dot(ah, bl) + _dot(ah, bh)


def _rwkv_kernel(p_ref, mu_ref, w0_ref, a0_ref, kk_ref, ka_ref, rk_ref, lnw_ref, lnb_ref, beta_ref,
                 wd_ref, wa_ref, wg_ref, o_ref,
                 carry_scr, state_scr, ar_scr, bk_scr, bkh_scr, v_scr, et_scr, y_scr,
                 pm_scr, arb_scr, av_scr, vk_scr, *, width, tb):
    n_pairs = width // PAIR
    n_chunks = tb // CHUNK
    stacked = 2 * CHUNK

    @pl.when(pl.program_id(1) == 0)
    def _():
        carry_scr[...] = jnp.zeros_like(carry_scr)
        state_scr[...] = jnp.zeros_like(state_scr)

    p = p_ref[...]
    row = lax.broadcasted_iota(jnp.int32, (tb, 1), 0)
    prev = jnp.where(row == 0, carry_scr[...], pltpu.roll(p, 1, axis=0))
    carry_scr[...] = p[tb - 1:tb, :]
    ps = p + (prev - p) * mu_ref[...]

    c = width
    r = ps[:, :c]
    k = ps[:, c:2 * c]
    v = ps[:, 2 * c:3 * c]
    o = 3 * c
    xw = ps[:, o:o + DECAY_LORA]
    xa = ps[:, o + DECAY_LORA:o + DECAY_LORA + AAA_LORA]
    xg = ps[:, o + DECAY_LORA + AAA_LORA:o + DECAY_LORA + AAA_LORA + GATE_LORA]

    z = w0_ref[...] + _dot_x3(jnp.tanh(xw), wd_ref[...])
    softplus_neg_z = jnp.maximum(-z, 0.0) + jnp.log(1.0 + jnp.exp(-jnp.abs(z)))
    lw = -jnp.exp(-softplus_neg_z - 0.5)
    a = _sigmoid(a0_ref[...] + _dot(xa.astype(BF16), wa_ref[...].astype(BF16)))
    g = _dot(_sigmoid(xg).astype(BF16), wg_ref[...].astype(BF16))

    li = lax.broadcasted_iota(jnp.int32, (PAIR, PAIR), 0)
    lj = lax.broadcasted_iota(jnp.int32, (PAIR, PAIR), 1)
    head_ones = ((li // HEAD_DIM) == (lj // HEAD_DIM)).astype(BF16)

    def head_sum(t):
        return jnp.concatenate(
            [_dot_hilo_lhs(t[:, q * PAIR:(q + 1) * PAIR], head_ones) for q in range(n_pairs)], axis=1)

    kk = k * kk_ref[...]
    kk = kk / jnp.maximum(jnp.sqrt(head_sum(kk * kk)), L2_EPS)
    k2 = k * (1.0 + (a - 1.0) * ka_ref[...])

    ti = lax.broadcasted_iota(jnp.int32, (tb, tb), 0)
    tj = lax.broadcasted_iota(jnp.int32, (tb, tb), 1)
    tri = jnp.logical_and((ti // CHUNK) == (tj // CHUNK), tj <= ti).astype(BF16)
    cum = _dot_f32_rhs(tri, lw)
    tot = jnp.concatenate(
        [jnp.broadcast_to(cum[(ci + 1) * CHUNK - 1:(ci + 1) * CHUNK, :], (CHUNK, c))
         for ci in range(n_chunks)], axis=0)

    e_neg = jnp.exp(-cum)
    e_rem = jnp.exp(tot - cum)
    kka = kk * a
    lane = lax.broadcasted_iota(jnp.int32, (1, PAIR), 1)
    first_head = lane < HEAD_DIM

    def put(scr, base, val):
        for q in range(n_pairs):
            vq = val[:, q * PAIR:(q + 1) * PAIR]
            h0 = jnp.where(first_head, vq, 0.0).astype(scr.dtype)
            h1 = jnp.where(first_head, 0.0, vq).astype(scr.dtype)
            for ci in range(n_chunks):
                rows = slice(ci * CHUNK, (ci + 1) * CHUNK)
                scr[q, ci, base:base + CHUNK, :] = h0[rows]
                scr[q, ci, base + CHUNK:base + stacked, :] = h1[rows]

    put(ar_scr, 0, -kk * jnp.exp(cum - lw))
    put(ar_scr, stacked, r * jnp.exp(cum))
    put(bk_scr, 0, kka * e_neg)
    put(bk_scr, stacked, k2 * e_neg)
    put(bkh_scr, 0, kka * e_rem)
    put(bkh_scr, stacked, k2 * e_rem)
    put(v_scr, 0, v)
    e_tot = jnp.exp(tot)
    for q in range(n_pairs):
        et_scr[q] = e_tot[:, q * PAIR:(q + 1) * PAIR]

    si = lax.broadcasted_iota(jnp.int32, (2 * stacked, 2 * stacked), 0)
    sj = lax.broadcasted_iota(jnp.int32, (2 * stacked, 2 * stacked), 1)
    keep = jnp.where(si < stacked, si, si - stacked + 1) > (sj % stacked)
    eye = (li == lj).astype(F32)

    chunks = range(n_chunks)

    def side(ta, tb):
        return jnp.concatenate([ta, tb], axis=1)

    def blockdiag(ta, tb):
        za = jnp.zeros((ta.shape[0], tb.shape[1]), ta.dtype)
        zb = jnp.zeros((tb.shape[0], ta.shape[1]), ta.dtype)
        return jnp.concatenate([side(ta, za), side(zb, tb)], axis=0)

    def halves(t):
        return t[:, :PAIR], t[:, PAIR:]

    def bf(t):
        return t.astype(BF16)

    hr = lax.broadcasted_iota(jnp.int32, (CHUNK, 4 * CHUNK), 0)
    hl = lax.broadcasted_iota(jnp.int32, (CHUNK, 4 * CHUNK), 1)
    eye4 = (hr == hl % CHUNK).astype(F32)
    zero_bf = jnp.zeros((CHUNK, stacked), BF16)

    def head_blocks(t):
        z = jnp.zeros_like(t)
        return jnp.concatenate([jnp.where(hl // CHUNK == j, t, z) for j in range(4)], axis=0)

    def prepare(step):
        couples = [(step * PREP_PAIRS + 2 * j, ci) for j in range(PREP_PAIRS // 2) for ci in chunks]
        pw = []
        for qa, ci in couples:
            n, tail, vm = [], [], []
            for q in (qa, qa + 1):
                sc = jnp.where(keep, _dot_nt(ar_scr[q, ci], bk_scr[q, ci]), 0.0)
                scb = bf(sc)
                arb_scr[q, ci] = scb[stacked:, :stacked]
                n.append(sc[:stacked, :stacked])
                tail.append(scb[:, stacked:])
                vm.append(v_scr[q, ci])
            av = _dot(side(*tail), blockdiag(bf(vm[0]), bf(vm[1])))
            vk = _dot(side(bf(vm[0].T), bf(vm[1].T)),
                      blockdiag(bkh_scr[qa, ci, stacked:, :], bkh_scr[qa + 1, ci, stacked:, :]))
            for q, t_av, t_vk in zip((qa, qa + 1), halves(av), halves(vk)):
                av_scr[q, ci] = t_av
                vk_scr[q, ci] = t_vk
            pw.append(bf(side(n[0][:CHUNK] + n[0][CHUNK:], n[1][:CHUNK] + n[1][CHUNK:])))
        acc = [eye4 + p_ for p_ in pw]
        pw = [bf(_dot(p_, head_blocks(p_))) for p_ in pw]
        for it in range(1, 6):
            rhs = [head_blocks(p_) for p_ in pw]
            if it < 5:
                both = [_dot(jnp.concatenate([p_, bf(a_)], axis=0), r_)
                        for p_, a_, r_ in zip(pw, acc, rhs)]
                pw = [bf(t[:CHUNK]) for t in both]
                acc = [a_ + t[CHUNK:] for a_, t in zip(acc, both)]
            else:
                acc = [a_ + _dot(bf(a_), r_) for a_, r_ in zip(acc, rhs)]
        for (qa, ci), a_ in zip(couples, acc):
            for q, t in zip((qa, qa + 1), halves(bf(a_))):
                pm_scr[q, ci] = jnp.concatenate(
                    [jnp.where(first_head, t, zero_bf), jnp.where(first_head, zero_bf, t)], axis=0)

    for step in range(n_pairs // PREP_PAIRS):
        prepare(step)

    firsts = range(0, n_pairs, 2)
    s = [side(state_scr[q], state_scr[q + 1]) for q in firsts]
    for ci in chunks:
        rows = slice(ci * CHUNK, (ci + 1) * CHUNK)

        def both(scr, part=slice(None)):
            return [side(scr[q, ci, part, :], scr[q + 1, ci, part, :]) for q in firsts]

        top, bottom = slice(0, stacked), slice(stacked, 2 * stacked)
        sr = [_dot_nt(ar_, blockdiag(*halves(bf(s_)))) for ar_, s_ in zip(both(ar_scr), s)]
        u = [_dot(pm_, blockdiag(*halves(bf(sr_[:stacked] + av_))))
             for pm_, sr_, av_ in zip(both(pm_scr), sr, both(av_scr, top))]
        ym = [sr_[stacked:] + av_ + _dot(arb_, blockdiag(*halves(bf(u_))))
              for sr_, av_, arb_, u_ in zip(sr, both(av_scr, bottom), both(arb_scr), u)]
        for q, ym_ in zip(firsts, ym):
            ya, yb = halves(ym_[:CHUNK] + ym_[CHUNK:])
            y_scr[q, rows, :] = ya
            y_scr[q + 1, rows, :] = yb
        decay = [side(et_scr[q, ci * CHUNK:ci * CHUNK + 1, :], et_scr[q + 1, ci * CHUNK:ci * CHUNK + 1, :])
                 for q in firsts]
        s = [s_ * d_ + _dot(side(*(bf(t.T) for t in halves(u_))), blockdiag(*halves(bh_))) + vk_
             for s_, d_, u_, bh_, vk_ in zip(s, decay, u, both(bkh_scr, top), both(vk_scr))]
    for q, s_ in zip(firsts, s):
        state_scr[q], state_scr[q + 1] = halves(s_)

    y = jnp.concatenate([y_scr[q] for q in range(n_pairs)], axis=1)
    mean = head_sum(y) * (1.0 / HEAD_DIM)
    d = y - mean
    var = head_sum(d * d) * (1.0 / HEAD_DIM)
    yn = d * lax.rsqrt(var + GN_EPS) * lnw_ref[...] + lnb_ref[...]
    bonus = head_sum(r * k2 * rk_ref[...]) * v
    o_ref[...] = (yn + bonus) * g * beta_ref[...]


def _rwkv_call(p_r, vecs, wd, wa, wg, batch, seq, width, tb):
    assert CHUNK == HEAD_DIM, "chunk operands stack two heads' rows against their 2 * HEAD_DIM lanes"
    assert (width // PAIR) % PREP_PAIRS == 0 and PREP_PAIRS % 2 == 0
    pw = p_r.shape[1]
    n_pairs = width // PAIR
    n_chunks = tb // CHUNK
    tiles = seq // tb
    vec_specs = [pl.BlockSpec((1, v.shape[1]), lambda b, t: (0, 0)) for v in vecs]
    mat_specs = [pl.BlockSpec(m.shape, lambda b, t: (0, 0)) for m in (wd, wa, wg)]
    stacked2 = pltpu.VMEM((n_pairs, n_chunks, 4 * CHUNK, PAIR), BF16)
    return pl.pallas_call(
        functools.partial(_rwkv_kernel, width=width, tb=tb),
        grid=(batch, tiles),
        in_specs=[pl.BlockSpec((tb, pw), lambda b, t: (b * tiles + t, 0))] + vec_specs + mat_specs,
        out_specs=pl.BlockSpec((tb, width), lambda b, t: (b * tiles + t, 0)),
        out_shape=jax.ShapeDtypeStruct((batch * seq, width), F32),
        scratch_shapes=[pltpu.VMEM((1, pw), F32),
                        pltpu.VMEM((n_pairs, PAIR, PAIR), F32),
                        stacked2, stacked2, stacked2,
                        pltpu.VMEM((n_pairs, n_chunks, 2 * CHUNK, PAIR), F32),
                        pltpu.VMEM((n_pairs, tb, PAIR), F32),
                        pltpu.VMEM((n_pairs, tb, PAIR), F32),
                        pltpu.VMEM((n_pairs, n_chunks, 2 * CHUNK, PAIR), BF16),
                        pltpu.VMEM((n_pairs, n_chunks, 2 * CHUNK, PAIR), BF16),
                        pltpu.VMEM((n_pairs, n_chunks, 4 * CHUNK, PAIR), F32),
                        pltpu.VMEM((n_pairs, n_chunks, 2 * CHUNK, PAIR), F32)],
        compiler_params=pltpu.CompilerParams(
            dimension_semantics=("parallel", "arbitrary"), vmem_limit_bytes=VMEM_LIMIT),
    )(p_r, *vecs, wd, wa, wg)


def _attn_kernel(q_ref, kp_ref, kc_ref, vp_ref, vc_ref, bias_ref, beta_ref, o_ref, kwin, vwin,
                 *, width, qb):
    i = pl.program_id(1)
    n_pairs = width // PAIR
    kwin[0:qb, :] = kp_ref[...]
    kwin[qb:2 * qb, :] = kc_ref[...]
    vwin[0:qb, :] = vp_ref[...]
    vwin[qb:2 * qb, :] = vc_ref[...]
    lane = lax.broadcasted_iota(jnp.int32, (1, PAIR), 1)
    first_head = lane < HEAD_DIM
    rows = ATTN_BLOCK_CHUNKS * CHUNK
    win = BAND + rows - CHUNK
    col = lax.broadcasted_iota(jnp.int32, (1, win), 1)
    scale = HEAD_DIM ** -0.5
    left = LEFT_CHUNKS * CHUNK

    for q in range(n_pairs):
        lanes = slice(q * PAIR, (q + 1) * PAIR)
        for blk in range(qb // rows):
            r0 = blk * rows
            qc = q_ref[r0:r0 + rows, lanes]
            zero = jnp.zeros_like(qc)
            qm = jnp.concatenate([jnp.where(first_head, qc, zero), jnp.where(first_head, zero, qc)],
                                 axis=0)
            s = _dot_nt(qm, kwin[r0:r0 + win, lanes]) * scale + bias_ref[q]
            valid = jnp.logical_or(i > 0, col + r0 >= left)
            s = jnp.where(valid, s, NEG_INF)
            e = jnp.exp(s - jnp.max(s, axis=-1, keepdims=True))
            denom = jnp.sum(e, axis=-1, keepdims=True)
            o = _dot(e.astype(BF16), vwin[r0:r0 + win, lanes]) / denom
            oc = jnp.where(first_head, o[:rows], o[rows:])
            o_ref[r0:r0 + rows, lanes] = oc * beta_ref[:, lanes]


def _attn_call(p_a, bias, beta, batch, seq, width, qb):
    assert qb == LEFT_CHUNKS * CHUNK, "key window = previous block + current block"
    tiles = seq // qb
    n_pairs = width // PAIR

    def cur(col):
        return pl.BlockSpec((qb, width), lambda b, t: (b * tiles + t, col))

    def prev(col):
        return pl.BlockSpec((qb, width), lambda b, t: (b * tiles + jnp.maximum(t - 1, 0), col))

    return pl.pallas_call(
        functools.partial(_attn_kernel, width=width, qb=qb),
        grid=(batch, tiles),
        in_specs=[cur(0), prev(1), cur(1), prev(2), cur(2),
                  pl.BlockSpec(bias.shape, lambda b, t: (0, 0, 0)),
                  pl.BlockSpec((1, width), lambda b, t: (0, 0))],
        out_specs=pl.BlockSpec((qb, width), lambda b, t: (b * tiles + t, 0)),
        out_shape=jax.ShapeDtypeStruct((batch * seq, width), F32),
        scratch_shapes=[pltpu.VMEM((2 * qb, width), BF16), pltpu.VMEM((2 * qb, width), BF16)],
        compiler_params=pltpu.CompilerParams(
            dimension_semantics=("parallel", "arbitrary"), vmem_limit_bytes=VMEM_LIMIT),
    )(p_a, p_a, p_a, p_a, p_a, bias, beta)


def _out_router_kernel(yr_ref, ya_ref, wo_ref, x_ref, mod_ref, g2_ref, wr_ref, br_ref,
                       x1_ref, h2_ref, route_ref, *, r_width):
    y = (_dot(yr_ref[...].astype(BF16), wo_ref[0:r_width, :])
         + _dot(ya_ref[...].astype(BF16), wo_ref[r_width:, :]))
    m = mod_ref[0]
    x1 = x_ref[...] + m[GATE1:GATE1 + 1] * y
    x1_ref[...] = x1
    h2 = _rms(x1) * g2_ref[...] * (1.0 + m[SCALE2:SCALE2 + 1]) + m[SHIFT2:SHIFT2 + 1]
    h2_ref[...] = h2

    logits = _dot_x3(h2, wr_ref[...]) + br_ref[...]
    li = lax.broadcasted_iota(jnp.int32, logits.shape, 1).astype(F32)
    none = jnp.float32(-jnp.inf)
    far = jnp.float32(LANES)
    is_group = li < N_GROUPS
    gl = jnp.where(is_group, logits, none)
    gmax = jnp.max(gl, axis=-1, keepdims=True)
    gidx = jnp.min(jnp.where(gl == gmax, li, far), axis=-1, keepdims=True)
    g_w = 1.0 / jnp.sum(jnp.where(is_group, jnp.exp(logits - gmax), 0.0), axis=-1, keepdims=True)
    lo = N_GROUPS + EXPERTS_PER_GROUP * gidx
    el = jnp.where(jnp.logical_and(li >= lo, li < lo + EXPERTS_PER_GROUP), logits, none)
    m1 = jnp.max(el, axis=-1, keepdims=True)
    i1 = jnp.min(jnp.where(el == m1, li, far), axis=-1, keepdims=True)
    el2 = jnp.where(li == i1, none, el)
    m2 = jnp.max(el2, axis=-1, keepdims=True)
    i2 = jnp.min(jnp.where(el2 == m2, li, far), axis=-1, keepdims=True)
    e2 = jnp.exp(m2 - m1)
    w1 = g_w / (1.0 + e2)
    w2 = g_w * e2 / (1.0 + e2)
    route = jnp.where(li == 0.0, i1 - N_GROUPS, 0.0) + jnp.where(li == 1.0, i2 - N_GROUPS, 0.0)
    route_ref[...] = route + jnp.where(li == 2.0, w1, 0.0) + jnp.where(li == 3.0, w2, 0.0)


def _out_router_call(y_r, y_a, wo, x2d, mod, g2, wr, br, seq, tm):
    rows, d = x2d.shape
    r_width = y_r.shape[1]
    a_width = y_a.shape[1]
    tiles_per_seq = seq // tm
    return pl.pallas_call(
        functools.partial(_out_router_kernel, r_width=r_width),
        grid=(rows // tm,),
        in_specs=[pl.BlockSpec((tm, r_width), lambda i: (i, 0)),
                  pl.BlockSpec((tm, a_width), lambda i: (i, 0)),
                  pl.BlockSpec(wo.shape, lambda i: (0, 0)),
                  pl.BlockSpec((tm, d), lambda i: (i, 0)),
                  pl.BlockSpec((1, 6, d), lambda i: (i // tiles_per_seq, 0, 0)),
                  pl.BlockSpec((1, d), lambda i: (0, 0)),
                  pl.BlockSpec((d, LANES), lambda i: (0, 0)),
                  pl.BlockSpec((1, LANES), lambda i: (0, 0))],
        out_specs=[pl.BlockSpec((tm, d), lambda i: (i, 0)),
                   pl.BlockSpec((tm, d), lambda i: (i, 0)),
                   pl.BlockSpec((tm, LANES), lambda i: (i, 0))],
        out_shape=[jax.ShapeDtypeStruct((rows, d), F32),
                   jax.ShapeDtypeStruct((rows, d), F32),
                   jax.ShapeDtypeStruct((rows, LANES), F32)],
        compiler_params=pltpu.CompilerParams(
            dimension_semantics=("parallel",), vmem_limit_bytes=VMEM_LIMIT),
    )(y_r, y_a, wo, x2d, mod, g2, wr, br)


def _moe_kernel(texp_ref, first_ref, next_ref, rows_ref, meta_ref, src_ref,
                h2_hbm, wg_hbm, wu_hbm, wd_hbm, o_ref,
                xbuf, xsem, wg_land, wu_land, wd_land, wsem, wg_bf, wu_bf, wd_bf, *, tm):
    t = pl.program_id(0)
    n_used = meta_ref[0]
    slot = lax.rem(t, MOE_GATHER_SLOTS)

    def weight_copies(e):
        cps = []
        for m, (hbm, land) in enumerate(((wg_hbm, wg_land), (wu_hbm, wu_land), (wd_hbm, wd_land))):
            rows = hbm.shape[1] // W_CHUNKS
            for ck in range(W_CHUNKS):
                part = pl.ds(ck * rows, rows)
                cps.append(pltpu.make_async_copy(hbm.at[e, part], land.at[part], wsem.at[m, ck]))
        return cps

    def row_groups(tile):
        return (rows_ref[tile] + ROW_GROUP - 1) // ROW_GROUP

    def gather_rows(tile, unrolled=False):
        buf_slot = lax.rem(tile, MOE_GATHER_SLOTS)
        n_groups = row_groups(tile)

        def body(g, carry):
            for k in range(ROW_GROUP):
                i = g * ROW_GROUP + k
                tok = src_ref[tile * tm + i]
                pltpu.make_async_copy(h2_hbm.at[pl.ds(tok, 1)], xbuf.at[buf_slot, pl.ds(i, 1)],
                                      xsem.at[buf_slot]).start()
            return carry

        if unrolled:
            for g in range(tm // ROW_GROUP):
                @pl.when(g < n_groups)
                def _(g=g):
                    body(g, 0)
        else:
            lax.fori_loop(0, n_groups, body, 0)

    @pl.when(t == 0)
    def _():
        xbuf[...] = jnp.zeros_like(xbuf)
        for cp in weight_copies(texp_ref[0]):
            cp.start()
        for ahead in range(MOE_GATHER_SLOTS - 1):
            @pl.when(ahead < n_used)
            def _():
                gather_rows(ahead)

    is_first = jnp.logical_and(t < n_used, first_ref[t] == 1)

    @pl.when(is_first)
    def _():
        for cp in weight_copies(texp_ref[t]):
            cp.wait()
        wg_bf[...] = wg_land[...].astype(BF16)
        wu_bf[...] = wu_land[...].astype(BF16)
        wd_bf[...] = wd_land[...].astype(BF16)

    @pl.when(t + MOE_GATHER_SLOTS - 1 < n_used)
    def _():
        gather_rows(t + MOE_GATHER_SLOTS - 1, unrolled=True)

    @pl.when(jnp.logical_and(is_first, next_ref[t] >= 0))
    def _():
        for cp in weight_copies(next_ref[t]):
            cp.start()

    @pl.when(t < n_used)
    def _():
        def wait_group(g, carry):
            r0 = pl.multiple_of(g * ROW_GROUP, ROW_GROUP)
            pltpu.make_async_copy(h2_hbm.at[pl.ds(0, ROW_GROUP)],
                                  xbuf.at[slot, pl.ds(r0, ROW_GROUP)], xsem.at[slot]).wait()
            return carry

        lax.fori_loop(0, row_groups(t), wait_group, 0)
        x = xbuf[slot].astype(BF16)
        gate = _dot(x, wg_bf[...])
        up = _dot(x, wu_bf[...])
        h = gate * _sigmoid(gate) * up
        o_ref[...] = _dot(h.astype(BF16), wd_bf[...])

    @pl.when(t >= n_used)
    def _():
        o_ref[...] = jnp.zeros_like(o_ref)


def _moe_call(plan, src, h2, wg, wu, wd, tm):
    n_e, d, f = wg.shape
    n_tiles = plan[0].shape[0]
    grid_spec = pltpu.PrefetchScalarGridSpec(
        num_scalar_prefetch=len(plan) + 1,
        grid=(n_tiles,),
        in_specs=[pl.BlockSpec(memory_space=pl.ANY)] * 4,
        out_specs=pl.BlockSpec((tm, d), lambda t, *_: (t, 0)),
        scratch_shapes=[pltpu.VMEM((MOE_GATHER_SLOTS, tm, d), F32),
                        pltpu.SemaphoreType.DMA((MOE_GATHER_SLOTS,)),
                        pltpu.VMEM((d, f), F32), pltpu.VMEM((d, f), F32), pltpu.VMEM((f, d), F32),
                        pltpu.SemaphoreType.DMA((3, W_CHUNKS)),
                        pltpu.VMEM((d, f), BF16), pltpu.VMEM((d, f), BF16), pltpu.VMEM((f, d), BF16)])
    return pl.pallas_call(
        functools.partial(_moe_kernel, tm=tm),
        grid_spec=grid_spec,
        out_shape=jax.ShapeDtypeStruct((n_tiles * tm, d), F32),
        compiler_params=pltpu.CompilerParams(
            dimension_semantics=("arbitrary",), vmem_limit_bytes=VMEM_LIMIT),
    )(*plan, src, h2, wg, wu, wd)


def _final_kernel(pos_ref, ys_hbm, route_ref, x1_ref, mod_ref, gf_ref, o_ref, ybuf, sem, *, tm):
    i = pl.program_id(0)
    n = pl.num_programs(0)
    slot = i % 2

    def gather_rows(tile, buf_slot, unrolled):
        base = tile * tm * 2

        def body(r, carry):
            for s in range(2):
                pltpu.make_async_copy(ys_hbm.at[pl.ds(pos_ref[base + 2 * r + s], 1)],
                                      ybuf.at[buf_slot, s, pl.ds(r, 1)], sem.at[buf_slot]).start()
            return carry

        if unrolled:
            for r in range(tm):
                body(r, 0)
        else:
            lax.fori_loop(0, tm, body, 0, unroll=4)

    @pl.when(i == 0)
    def _():
        gather_rows(0, 0, unrolled=False)

    @pl.when(i + 1 < n)
    def _():
        gather_rows(i + 1, 1 - slot, unrolled=True)

    for s in range(2):
        pltpu.make_async_copy(ys_hbm.at[pl.ds(0, tm)], ybuf.at[slot, s], sem.at[slot]).wait()
    route = route_ref[...]
    w0 = route[:, 2:3]
    w1 = route[:, 3:4]
    f = w0 * ybuf[slot, 0] + w1 * ybuf[slot, 1]
    x2 = x1_ref[...] + mod_ref[0][GATE2:GATE2 + 1] * f
    o_ref[...] = _rms(x2) * gf_ref[...]


def _final_call(pos, ys, route, x1, mod, gf, seq, tm):
    rows, d = x1.shape
    tiles_per_seq = seq // tm
    grid_spec = pltpu.PrefetchScalarGridSpec(
        num_scalar_prefetch=1,
        grid=(rows // tm,),
        in_specs=[pl.BlockSpec(memory_space=pl.ANY),
                  pl.BlockSpec((tm, LANES), lambda i, pos: (i, 0)),
                  pl.BlockSpec((tm, d), lambda i, pos: (i, 0)),
                  pl.BlockSpec((1, 6, d), lambda i, pos: (i // tiles_per_seq, 0, 0)),
                  pl.BlockSpec((1, d), lambda i, pos: (0, 0))],
        out_specs=pl.BlockSpec((tm, d), lambda i, pos: (i, 0)),
        scratch_shapes=[pltpu.VMEM((2, 2, tm, d), F32), pltpu.SemaphoreType.DMA((2,))])
    return pl.pallas_call(
        functools.partial(_final_kernel, tm=tm),
        grid_spec=grid_spec,
        out_shape=jax.ShapeDtypeStruct((rows, d), F32),
        compiler_params=pltpu.CompilerParams(
            dimension_semantics=("arbitrary",), vmem_limit_bytes=VMEM_LIMIT),
    )(pos, ys, route, x1, mod, gf)


def _route_plan(route, tm):
    n = route.shape[0]
    flat_e = route[:, 0:2].astype(jnp.int32).reshape(-1)
    n_tiles = (2 * n + N_EXPERTS * (tm - 1)) // tm
    experts = jnp.arange(N_EXPERTS, dtype=jnp.int32)
    onehot = (flat_e[:, None] == experts[None, :]).astype(jnp.int32)
    csum = jnp.cumsum(onehot, axis=0)
    rank = jnp.sum(csum * onehot, axis=1) - 1
    counts = csum[-1]
    tiles_per_e = (counts + tm - 1) // tm
    tile_end = jnp.cumsum(tiles_per_e)
    tile_start = tile_end - tiles_per_e
    n_used = tile_end[-1]
    pos = jnp.sum(onehot * tile_start[None, :], axis=1) * tm + rank
    token = jnp.arange(2 * n, dtype=jnp.int32) // 2
    src = jnp.zeros((n_tiles * tm,), jnp.int32).at[pos].set(token)
    tiles = jnp.arange(n_tiles, dtype=jnp.int32)
    texp = jnp.sum((tiles[:, None] >= tile_end[None, :]).astype(jnp.int32), axis=1)
    texp = jnp.where(tiles < n_used, texp, texp[n_used - 1])
    first = jnp.concatenate([jnp.ones((1,), jnp.int32), (texp[1:] != texp[:-1]).astype(jnp.int32)])
    later_used = jnp.logical_and(experts[None, :] > experts[:, None], (tiles_per_e > 0)[None, :])
    next_used = jnp.min(jnp.where(later_used, experts[None, :], N_EXPERTS), axis=1)
    next_used = jnp.where(next_used == N_EXPERTS, -1, next_used)
    tile_rows = jnp.clip(counts[texp] - (tiles - tile_start[texp]) * tm, 0, tm)
    return (texp, first, next_used[texp], tile_rows, n_used.reshape(1)), src, pos


def _round_up(n, m):
    return (n + m - 1) // m * m


def kernel(x, c, w_ada, b_ada, norm1_g, w_in, mu_shift, w0, w_decay_up, a0, w_aaa_up, w_gate_up,
           k_k, k_a, r_k, ln_x_w, ln_x_b, rel_bias, beta_rwkv, beta_attn, w_out, norm2_g, w_group,
           b_group, w_expert, b_expert, w_gate, w_up, w_down, norm_f_g):
    assert w_ada.shape[0] == 1, "single trunk layer"
    batch, seq, d = x.shape
    r_width = w0.shape[1]
    a_width = beta_attn.shape[1]
    shift_width = mu_shift.shape[1]
    assert shift_width == 3 * r_width + DECAY_LORA + AAA_LORA + GATE_LORA
    x2d = x.reshape(batch * seq, d)
    row = lambda t: t.reshape(1, -1)

    c8 = jnp.pad(c, ((0, 8 - batch), (0, 0)))
    mod = _ada_call(c8, w_ada[0], row(b_ada[0]), tn=1536)[:batch].reshape(batch, 6, d)

    tm = min(512, seq)
    pw = _round_up(shift_width, 3 * LANES)
    h1 = _hmod_call(x2d, row(norm1_g[0]), mod, tm, seq)
    tm_p = min(1024, seq)
    w_in_t = jnp.swapaxes(w_in[0], 0, 1)
    p_r = _matmul_call(h1, w_in_t, pw, pw // 3, tm_p, F32)
    p_a = _matmul_call(h1, w_in_t[shift_width:], 3 * a_width, a_width, tm_p, BF16)

    mu = jnp.pad(mu_shift[0], (0, pw - shift_width))
    vecs = [row(mu), row(w0[0]), row(a0[0]), row(k_k[0]), row(k_a[0]), row(r_k[0]), row(ln_x_w[0]),
            row(ln_x_b[0]), row(beta_rwkv[0])]
    y_r = _rwkv_call(p_r, vecs, w_decay_up[0], w_aaa_up[0], w_gate_up[0], batch, seq, r_width,
                     tb=min(256, seq))

    left = LEFT_CHUNKS * CHUNK
    dist = left + CHUNK - 1 - jnp.arange(BAND + CHUNK - 1)
    base = rel_bias[0][:, jnp.clip(dist, -REL_CLIP, REL_CLIP) + REL_CLIP].astype(F32)
    bias = jnp.stack([base[:, CHUNK - 1 - qi:CHUNK - 1 - qi + BAND] for qi in range(CHUNK)], axis=1)
    nb = ATTN_BLOCK_CHUNKS
    bias = jnp.concatenate(
        [jnp.pad(bias, ((0, 0), (0, 0), (cb * CHUNK, (nb - 1 - cb) * CHUNK)), constant_values=NEG_INF)
         for cb in range(nb)], axis=1)
    bias = bias.reshape(a_width // PAIR, 2 * nb * CHUNK, BAND + (nb - 1) * CHUNK)
    y_a = _attn_call(p_a, bias, row(beta_attn[0]), batch, seq, a_width, qb=min(512, seq))

    n_route = N_GROUPS + N_EXPERTS
    wr = jnp.concatenate([w_group[0], w_expert[0].transpose(1, 0, 2).reshape(d, N_EXPERTS)], axis=1)
    wr = jnp.pad(wr, ((0, 0), (0, LANES - n_route)))
    br = jnp.pad(jnp.concatenate([b_group[0], b_expert[0].reshape(-1)]), (0, LANES - n_route))
    x1, h2, route = _out_router_call(y_r, y_a, w_out[0].astype(BF16), x2d, mod, row(norm2_g[0]),
                                     wr, row(br), seq, tm)

    f = w_gate.shape[-1]
    tm_e = 256
    plan, src, pos = _route_plan(route, tm_e)
    ys = _moe_call(plan, src, h2, w_gate[0].reshape(N_EXPERTS, d, f),
                   w_up[0].reshape(N_EXPERTS, d, f), w_down[0].reshape(N_EXPERTS, f, d), tm_e)
    out = _final_call(pos, ys, route, x1, mod, row(norm_f_g), seq, min(256, seq))
    return out.reshape(batch, seq, d)
```

```python
import functools

import jax
import jax.numpy as jnp
from jax import lax
from jax.experimental import pallas as pl
from jax.experimental.pallas import tpu as pltpu

F32 = jnp.float32
BF16 = jnp.bfloat16
HI = lax.Precision.HIGHEST

LANES = 128
SUBLANES = 8
HEAD_DIM = 64
PAIR = 2 * HEAD_DIM
CHUNK = 64
LEFT_CHUNKS = 8
BAND = (LEFT_CHUNKS + 1) * CHUNK
ATTN_BLOCK_CHUNKS = 4
PREP_PAIRS = 4
MOE_GATHER_SLOTS = 3
W_CHUNKS = 2
ROW_GROUP = 8
REL_CLIP = 128
DECAY_LORA = 64
AAA_LORA = 64
GATE_LORA = 160
N_GROUPS = 4
EXPERTS_PER_GROUP = 8
N_EXPERTS = N_GROUPS * EXPERTS_PER_GROUP
RMS_EPS = 1e-6
GN_EPS = 64e-5
L2_EPS = 1e-12
NEG_INF = -1e30
VMEM_LIMIT = 56 * 1024 * 1024

SHIFT1, SCALE1, GATE1, SHIFT2, SCALE2, GATE2 = range(6)


def _dot(a, b, precision=None):
    return jnp.dot(a, b, precision=precision, preferred_element_type=F32)


def _dot_nt(a, b, precision=None):
    return lax.dot_general(a, b, (((1,), (1,)), ((), ())), precision=precision,
                           preferred_element_type=F32)


def _sigmoid(x):
    return 1.0 / (1.0 + jnp.exp(-x))


def _rms(x):
    return x * lax.rsqrt(jnp.mean(x * x, axis=-1, keepdims=True) + RMS_EPS)


def _ada_kernel(c_ref, w_ref, b_ref, o_ref):
    c = c_ref[...]
    o_ref[...] = _dot_x3(c * _sigmoid(c), w_ref[...]) + b_ref[...]


def _ada_call(c8, w, b, tn):
    rows, d = c8.shape
    n = w.shape[1]
    return pl.pallas_call(
        _ada_kernel,
        grid=(n // tn,),
        in_specs=[pl.BlockSpec((rows, d), lambda j: (0, 0)),
                  pl.BlockSpec((d, tn), lambda j: (0, j)),
                  pl.BlockSpec((1, tn), lambda j: (0, j))],
        out_specs=pl.BlockSpec((rows, tn), lambda j: (0, j)),
        out_shape=jax.ShapeDtypeStruct((rows, n), F32),
        compiler_params=pltpu.CompilerParams(vmem_limit_bytes=VMEM_LIMIT),
    )(c8, w, b)


def _hmod_kernel(x_ref, g_ref, mod_ref, o_ref):
    m = mod_ref[0]
    h = _rms(x_ref[...]) * g_ref[...] * (1.0 + m[SCALE1:SCALE1 + 1]) + m[SHIFT1:SHIFT1 + 1]
    o_ref[...] = h.astype(BF16)


def _hmod_call(x2d, g, mod, tm, seq):
    rows, d = x2d.shape
    tiles_per_seq = seq // tm
    return pl.pallas_call(
        _hmod_kernel,
        grid=(rows // tm,),
        in_specs=[pl.BlockSpec((tm, d), lambda i: (i, 0)),
                  pl.BlockSpec((1, d), lambda i: (0, 0)),
                  pl.BlockSpec((1, 6, d), lambda i: (i // tiles_per_seq, 0, 0))],
        out_specs=pl.BlockSpec((tm, d), lambda i: (i, 0)),
        out_shape=jax.ShapeDtypeStruct((rows, d), BF16),
        compiler_params=pltpu.CompilerParams(
            dimension_semantics=("parallel",), vmem_limit_bytes=VMEM_LIMIT),
    )(x2d, g, mod)


def _matmul_kernel(a_ref, wt_ref, o_ref, w_bf):
    @pl.when(pl.program_id(1) == 0)
    def _():
        w_bf[...] = wt_ref[...].astype(BF16)

    o_ref[...] = _dot_nt(a_ref[...], w_bf[...]).astype(o_ref.dtype)


def _matmul_call(a, wt, row0, n_out, tn, tm, out_dtype):
    rows, d = a.shape
    return pl.pallas_call(
        _matmul_kernel,
        grid=(n_out // tn, rows // tm),
        in_specs=[pl.BlockSpec((tm, d), lambda j, i: (i, 0)),
                  pl.BlockSpec((pl.Element(tn), pl.Element(d)),
                               lambda j, i: (pl.multiple_of(row0 + j * tn, SUBLANES), 0))],
        out_specs=pl.BlockSpec((tm, tn), lambda j, i: (i, j)),
        out_shape=jax.ShapeDtypeStruct((rows, n_out), out_dtype),
        scratch_shapes=[pltpu.VMEM((tn, d), BF16)],
        compiler_params=pltpu.CompilerParams(
            dimension_semantics=("parallel", "arbitrary"), vmem_limit_bytes=VMEM_LIMIT),
    )(a, wt)


def _split3(t):
    hi = t.astype(BF16)
    rest = t - hi.astype(F32)
    mid = rest.astype(BF16)
    lo = (rest - mid.astype(F32)).astype(BF16)
    return hi, mid, lo


def _dot_hilo_lhs(t, m):
    hi = t.astype(BF16)
    lo = (t - hi.astype(F32)).astype(BF16)
    return _dot(lo, m) + _dot(hi, m)


def _dot_f32_rhs(m, t):
    hi, mid, lo = _split3(t)
    return _dot(m, lo) + _dot(m, mid) + _dot(m, hi)


def _dot_x3(a, b):
    ah = a.astype(BF16)
    al = (a - ah.astype(F32)).astype(BF16)
    bh = b.astype(BF16)
    bl = (b - bh.astype(F32)).astype(BF16)
    return _dot(al, bh) + _dot(ah, bl) + _dot(ah, bh)


def _rwkv_kernel(p_ref, mu_ref, w0_ref, a0_ref, kk_ref, ka_ref, rk_ref, lnw_ref, lnb_ref, beta_ref,
                 wd_ref, wa_ref, wg_ref, o_ref,
                 carry_scr, state_scr, ar_scr, bk_scr, bkh_scr, v_scr, et_scr, y_scr,
                 pm_scr, arb_scr, av_scr, vk_scr, *, width, tb):
    n_pairs = width // PAIR
    n_chunks = tb // CHUNK
    stacked = 2 * CHUNK

    @pl.when(pl.program_id(1) == 0)
    def _():
        carry_scr[...] = jnp.zeros_like(carry_scr)
        state_scr[...] = jnp.zeros_like(state_scr)

    p = p_ref[...]
    row = lax.broadcasted_iota(jnp.int32, (tb, 1), 0)
    prev = jnp.where(row == 0, carry_scr[...], pltpu.roll(p, 1, axis=0))
    carry_scr[...] = p[tb - 1:tb, :]
    ps = p + (prev - p) * mu_ref[...]

    c = width
    r = ps[:, :c]
    k = ps[:, c:2 * c]
    v = ps[:, 2 * c:3 * c]
    o = 3 * c
    xw = ps[:, o:o + DECAY_LORA]
    xa = ps[:, o + DECAY_LORA:o + DECAY_LORA + AAA_LORA]
    xg = ps[:, o + DECAY_LORA + AAA_LORA:o + DECAY_LORA + AAA_LORA + GATE_LORA]

    z = w0_ref[...] + _dot_x3(jnp.tanh(xw), wd_ref[...])
    softplus_neg_z = jnp.maximum(-z, 0.0) + jnp.log(1.0 + jnp.exp(-jnp.abs(z)))
    lw = -jnp.exp(-softplus_neg_z - 0.5)
    a = _sigmoid(a0_ref[...] + _dot(xa.astype(BF16), wa_ref[...].astype(BF16)))
    g = _dot(_sigmoid(xg).astype(BF16), wg_ref[...].astype(BF16))

    li = lax.broadcasted_iota(jnp.int32, (PAIR, PAIR), 0)
    lj = lax.broadcasted_iota(jnp.int32, (PAIR, PAIR), 1)
    head_ones = ((li // HEAD_DIM) == (lj // HEAD_DIM)).astype(BF16)

    def head_sum(t):
        return jnp.concatenate(
            [_dot_hilo_lhs(t[:, q * PAIR:(q + 1) * PAIR], head_ones) for q in range(n_pairs)], axis=1)

    kk = k * kk_ref[...]
    kk = kk / jnp.maximum(jnp.sqrt(head_sum(kk * kk)), L2_EPS)
    k2 = k * (1.0 + (a - 1.0) * ka_ref[...])

    ti = lax.broadcasted_iota(jnp.int32, (tb, tb), 0)
    tj = lax.broadcasted_iota(jnp.int32, (tb, tb), 1)
    tri = jnp.logical_and((ti // CHUNK) == (tj // CHUNK), tj <= ti).astype(BF16)
    cum = _dot_f32_rhs(tri, lw)
    tot = jnp.concatenate(
        [jnp.broadcast_to(cum[(ci + 1) * CHUNK - 1:(ci + 1) * CHUNK, :], (CHUNK, c))
         for ci in range(n_chunks)], axis=0)

    e_neg = jnp.exp(-cum)
    e_rem = jnp.exp(tot - cum)
    kka = kk * a
    lane = lax.broadcasted_iota(jnp.int32, (1, PAIR), 1)
    first_head = lane < HEAD_DIM

    def put(scr, base, val):
        for q in range(n_pairs):
            vq = val[:, q * PAIR:(q + 1) * PAIR]
            h0 = jnp.where(first_head, vq, 0.0).astype(scr.dtype)
            h1 = jnp.where(first_head, 0.0, vq).astype(scr.dtype)
            for ci in range(n_chunks):
                rows = slice(ci * CHUNK, (ci + 1) * CHUNK)
                scr[q, ci, base:base + CHUNK, :] = h0[rows]
                scr[q, ci, base + CHUNK:base + stacked, :] = h1[rows]

    put(ar_scr, 0, -kk * jnp.exp(cum - lw))
    put(ar_scr, stacked, r * jnp.exp(cum))
    put(bk_scr, 0, kka * e_neg)
    put(bk_scr, stacked, k2 * e_neg)
    put(bkh_scr, 0, kka * e_rem)
    put(bkh_scr, stacked, k2 * e_rem)
    put(v_scr, 0, v)
    e_tot = jnp.exp(tot)
    for q in range(n_pairs):
        et_scr[q] = e_tot[:, q * PAIR:(q + 1) * PAIR]

    si = lax.broadcasted_iota(jnp.int32, (2 * stacked, 2 * stacked), 0)
    sj = lax.broadcasted_iota(jnp.int32, (2 * stacked, 2 * stacked), 1)
    keep = jnp.where(si < stacked, si, si - stacked + 1) > (sj % stacked)
    eye = (li == lj).astype(F32)

    chunks = range(n_chunks)

    def side(ta, tb):
        return jnp.concatenate([ta, tb], axis=1)

    def blockdiag(ta, tb):
        za = jnp.zeros((ta.shape[0], tb.shape[1]), ta.dtype)
        zb = jnp.zeros((tb.shape[0], ta.shape[1]), ta.dtype)
        return jnp.concatenate([side(ta, za), side(zb, tb)], axis=0)

    def halves(t):
        return t[:, :PAIR], t[:, PAIR:]

    def bf(t):
        return t.astype(BF16)

    hr = lax.broadcasted_iota(jnp.int32, (CHUNK, 4 * CHUNK), 0)
    hl = lax.broadcasted_iota(jnp.int32, (CHUNK, 4 * CHUNK), 1)
    eye4 = (hr == hl % CHUNK).astype(F32)
    zero_bf = jnp.zeros((CHUNK, stacked), BF16)

    def head_blocks(t):
        z = jnp.zeros_like(t)
        return jnp.concatenate([jnp.where(hl // CHUNK == j, t, z) for j in range(4)], axis=0)

    def prepare(step):
        couples = [(step * PREP_PAIRS + 2 * j, ci) for j in range(PREP_PAIRS // 2) for ci in chunks]
        pw = []
        for qa, ci in couples:
            n, tail, vm = [], [], []
            for q in (qa, qa + 1):
                sc = jnp.where(keep, _dot_nt(ar_scr[q, ci], bk_scr[q, ci]), 0.0)
                scb = bf(sc)
                arb_scr[q, ci] = scb[stacked:, :stacked]
                n.append(sc[:stacked, :stacked])
                tail.append(scb[:, stacked:])
                vm.append(v_scr[q, ci])
            av = _dot(side(*tail), blockdiag(bf(vm[0]), bf(vm[1])))
            vk = _dot(side(bf(vm[0].T), bf(vm[1].T)),
                      blockdiag(bkh_scr[qa, ci, stacked:, :], bkh_scr[qa + 1, ci, stacked:, :]))
            for q, t_av, t_vk in zip((qa, qa + 1), halves(av), halves(vk)):
                av_scr[q, ci] = t_av
                vk_scr[q, ci] = t_vk
            pw.append(bf(side(n[0][:CHUNK] + n[0][CHUNK:], n[1][:CHUNK] + n[1][CHUNK:])))
        acc = [eye4 + p_ for p_ in pw]
        pw = [bf(_dot(p_, head_blocks(p_))) for p_ in pw]
        for it in range(1, 6):
            rhs = [head_blocks(p_) for p_ in pw]
            if it < 5:
                both = [_dot(jnp.concatenate([p_, bf(a_)], axis=0), r_)
                        for p_, a_, r_ in zip(pw, acc, rhs)]
                pw = [bf(t[:CHUNK]) for t in both]
                acc = [a_ + t[CHUNK:] for a_, t in zip(acc, both)]
            else:
                acc = [a_ + _dot(bf(a_), r_) for a_, r_ in zip(acc, rhs)]
        for (qa, ci), a_ in zip(couples, acc):
            for q, t in zip((qa, qa + 1), halves(bf(a_))):
                pm_scr[q, ci] = jnp.concatenate(
                    [jnp.where(first_head, t, zero_bf), jnp.where(first_head, zero_bf, t)], axis=0)

    for step in range(n_pairs // PREP_PAIRS):
        prepare(step)

    firsts = range(0, n_pairs, 2)
    s = [side(state_scr[q], state_scr[q + 1]) for q in firsts]
    for ci in chunks:
        rows = slice(ci * CHUNK, (ci + 1) * CHUNK)

        def both(scr, part=slice(None)):
            return [side(scr[q, ci, part, :], scr[q + 1, ci, part, :]) for q in firsts]

        top, bottom = slice(0, stacked), slice(stacked, 2 * stacked)
        sr = [_dot_nt(ar_, blockdiag(*halves(bf(s_)))) for ar_, s_ in zip(both(ar_scr), s)]
        u = [_dot(pm_, blockdiag(*halves(bf(sr_[:stacked] + av_))))
             for pm_, sr_, av_ in zip(both(pm_scr), sr, both(av_scr, top))]
        ym = [sr_[stacked:] + av_ + _dot(arb_, blockdiag(*halves(bf(u_))))
              for sr_, av_, arb_, u_ in zip(sr, both(av_scr, bottom), both(arb_scr), u)]
        for q, ym_ in zip(firsts, ym):
            ya, yb = halves(ym_[:CHUNK] + ym_[CHUNK:])
            y_scr[q, rows, :] = ya
            y_scr[q + 1, rows, :] = yb
        decay = [side(et_scr[q, ci * CHUNK:ci * CHUNK + 1, :], et_scr[q + 1, ci * CHUNK:ci * CHUNK + 1, :])
                 for q in firsts]
        s = [s_ * d_ + _dot(side(*(bf(t.T) for t in halves(u_))), blockdiag(*halves(bh_))) + vk_
             for s_, d_, u_, bh_, vk_ in zip(s, decay, u, both(bkh_scr, top), both(vk_scr))]
    for q, s_ in zip(firsts, s):
        state_scr[q], state_scr[q + 1] = halves(s_)

    y = jnp.concatenate([y_scr[q] for q in range(n_pairs)], axis=1)
    mean = head_sum(y) * (1.0 / HEAD_DIM)
    d = y - mean
    var = head_sum(d * d) * (1.0 / HEAD_DIM)
    yn = d * lax.rsqrt(var + GN_EPS) * lnw_ref[...] + lnb_ref[...]
    bonus = head_sum(r * k2 * rk_ref[...]) * v
    o_ref[...] = (yn + bonus) * g * beta_ref[...]


def _rwkv_call(p_r, vecs, wd, wa, wg, batch, seq, width, tb):
    assert CHUNK == HEAD_DIM, "chunk operands stack two heads' rows against their 2 * HEAD_DIM lanes"
    assert (width // PAIR) % PREP_PAIRS == 0 and PREP_PAIRS % 2 == 0
    pw = p_r.shape[1]
    n_pairs = width // PAIR
    n_chunks = tb // CHUNK
    tiles = seq // tb
    vec_specs = [pl.BlockSpec((1, v.shape[1]), lambda b, t: (0, 0)) for v in vecs]
    mat_specs = [pl.BlockSpec(m.shape, lambda b, t: (0, 0)) for m in (wd, wa, wg)]
    stacked2 = pltpu.VMEM((n_pairs, n_chunks, 4 * CHUNK, PAIR), BF16)
    return pl.pallas_call(
        functools.partial(_rwkv_kernel, width=width, tb=tb),
        grid=(batch, tiles),
        in_specs=[pl.BlockSpec((tb, pw), lambda b, t: (b * tiles + t, 0))] + vec_specs + mat_specs,
        out_specs=pl.BlockSpec((tb, width), lambda b, t: (b * tiles + t, 0)),
        out_shape=jax.ShapeDtypeStruct((batch * seq, width), F32),
        scratch_shapes=[pltpu.VMEM((1, pw), F32),
                        pltpu.VMEM((n_pairs, PAIR, PAIR), F32),
                        stacked2, stacked2, stacked2,
                        pltpu.VMEM((n_pairs, n_chunks, 2 * CHUNK, PAIR), F32),
                        pltpu.VMEM((n_pairs, tb, PAIR), F32),
                        pltpu.VMEM((n_pairs, tb, PAIR), F32),
                        pltpu.VMEM((n_pairs, n_chunks, 2 * CHUNK, PAIR), BF16),
                        pltpu.VMEM((n_pairs, n_chunks, 2 * CHUNK, PAIR), BF16),
                        pltpu.VMEM((n_pairs, n_chunks, 4 * CHUNK, PAIR), F32),
                        pltpu.VMEM((n_pairs, n_chunks, 2 * CHUNK, PAIR), F32)],
        compiler_params=pltpu.CompilerParams(
            dimension_semantics=("parallel", "arbitrary"), vmem_limit_bytes=VMEM_LIMIT),
    )(p_r, *vecs, wd, wa, wg)


def _attn_kernel(q_ref, kp_ref, kc_ref, vp_ref, vc_ref, bias_ref, beta_ref, o_ref, kwin, vwin,
                 *, width, qb):
    i = pl.program_id(1)
    n_pairs = width // PAIR
    kwin[0:qb, :] = kp_ref[...]
    kwin[qb:2 * qb, :] = kc_ref[...]
    vwin[0:qb, :] = vp_ref[...]
    vwin[qb:2 * qb, :] = vc_ref[...]
    lane = lax.broadcasted_iota(jnp.int32, (1, PAIR), 1)
    first_head = lane < HEAD_DIM
    rows = ATTN_BLOCK_CHUNKS * CHUNK
    win = BAND + rows - CHUNK
    col = lax.broadcasted_iota(jnp.int32, (1, win), 1)
    scale = HEAD_DIM ** -0.5
    left = LEFT_CHUNKS * CHUNK

    def attend(first_block):
        for q in range(n_pairs):
            lanes = slice(q * PAIR, (q + 1) * PAIR)
            for blk in range(qb // rows):
                r0 = blk * rows
                qc = q_ref[r0:r0 + rows, lanes]
                zero = jnp.zeros_like(qc)
                qm = jnp.concatenate(
                    [jnp.where(first_head, qc, zero), jnp.where(first_head, zero, qc)], axis=0)
                s = _dot_nt(qm, kwin[r0:r0 + win, lanes]) * scale + bias_ref[q]
                if first_block:
                    s = jnp.where(col + r0 >= left, s, NEG_INF)
                e = jnp.exp(s - jnp.max(s, axis=-1, keepdims=True))
                denom = jnp.sum(e, axis=-1, keepdims=True)
                o = _dot(e.astype(BF16), vwin[r0:r0 + win, lanes]) / denom
                oc = jnp.where(first_head, o[:rows], o[rows:])
                o_ref[r0:r0 + rows, lanes] = oc * beta_ref[:, lanes]

    pl.when(i == 0)(lambda: attend(True))
    pl.when(i > 0)(lambda: attend(False))


def _attn_call(p_a, bias, beta, batch, seq, width, qb):
    assert qb == LEFT_CHUNKS * CHUNK, "key window = previous block + current block"
    tiles = seq // qb
    n_pairs = width // PAIR

    def cur(col):
        return pl.BlockSpec((qb, width), lambda b, t: (b * tiles + t, col))

    def prev(col):
        return pl.BlockSpec((qb, width), lambda b, t: (b * tiles + jnp.maximum(t - 1, 0), col))

    return pl.pallas_call(
        functools.partial(_attn_kernel, width=width, qb=qb),
        grid=(batch, tiles),
        in_specs=[cur(0), prev(1), cur(1), prev(2), cur(2),
                  pl.BlockSpec(bias.shape, lambda b, t: (0, 0, 0)),
                  pl.BlockSpec((1, width), lambda b, t: (0, 0))],
        out_specs=pl.BlockSpec((qb, width), lambda b, t: (b * tiles + t, 0)),
        out_shape=jax.ShapeDtypeStruct((batch * seq, width), F32),
        scratch_shapes=[pltpu.VMEM((2 * qb, width), BF16), pltpu.VMEM((2 * qb, width), BF16)],
        compiler_params=pltpu.CompilerParams(
            dimension_semantics=("parallel", "arbitrary"), vmem_limit_bytes=VMEM_LIMIT),
    )(p_a, p_a, p_a, p_a, p_a, bias, beta)


def _out_router_kernel(yr_ref, ya_ref, wo_ref, x_ref, mod_ref, g2_ref, wr_ref, br_ref,
                       x1_ref, h2_ref, route_ref, *, r_width):
    y = (_dot(yr_ref[...].astype(BF16), wo_ref[0:r_width, :])
         + _dot(ya_ref[...].astype(BF16), wo_ref[r_width:, :]))
    m = mod_ref[0]
    x1 = x_ref[...] + m[GATE1:GATE1 + 1] * y
    x1_ref[...] = x1
    h2 = _rms(x1) * g2_ref[...] * (1.0 + m[SCALE2:SCALE2 + 1]) + m[SHIFT2:SHIFT2 + 1]
    h2_ref[...] = h2

    logits = _dot_x3(h2, wr_ref[...]) + br_ref[...]
    li = lax.broadcasted_iota(jnp.int32, logits.shape, 1).astype(F32)
    none = jnp.float32(-jnp.inf)
    far = jnp.float32(LANES)
    is_group = li < N_GROUPS
    gl = jnp.where(is_group, logits, none)
    gmax = jnp.max(gl, axis=-1, keepdims=True)
    gidx = jnp.min(jnp.where(gl == gmax, li, far), axis=-1, keepdims=True)
    g_w = 1.0 / jnp.sum(jnp.where(is_group, jnp.exp(logits - gmax), 0.0), axis=-1, keepdims=True)
    lo = N_GROUPS + EXPERTS_PER_GROUP * gidx
    el = jnp.where(jnp.logical_and(li >= lo, li < lo + EXPERTS_PER_GROUP), logits, none)
    m1 = jnp.max(el, axis=-1, keepdims=True)
    i1 = jnp.min(jnp.where(el == m1, li, far), axis=-1, keepdims=True)
    el2 = jnp.where(li == i1, none, el)
    m2 = jnp.max(el2, axis=-1, keepdims=True)
    i2 = jnp.min(jnp.where(el2 == m2, li, far), axis=-1, keepdims=True)
    e2 = jnp.exp(m2 - m1)
    w1 = g_w / (1.0 + e2)
    w2 = g_w * e2 / (1.0 + e2)
    route = jnp.where(li == 0.0, i1 - N_GROUPS, 0.0) + jnp.where(li == 1.0, i2 - N_GROUPS, 0.0)
    route_ref[...] = route + jnp.where(li == 2.0, w1, 0.0) + jnp.where(li == 3.0, w2, 0.0)


def _out_router_call(y_r, y_a, wo, x2d, mod, g2, wr, br, seq, tm):
    rows, d = x2d.shape
    r_width = y_r.shape[1]
    a_width = y_a.shape[1]
    tiles_per_seq = seq // tm
    return pl.pallas_call(
        functools.partial(_out_router_kernel, r_width=r_width),
        grid=(rows // tm,),
        in_specs=[pl.BlockSpec((tm, r_width), lambda i: (i, 0)),
                  pl.BlockSpec((tm, a_width), lambda i: (i, 0)),
                  pl.BlockSpec(wo.shape, lambda i: (0, 0)),
                  pl.BlockSpec((tm, d), lambda i: (i, 0)),
                  pl.BlockSpec((1, 6, d), lambda i: (i // tiles_per_seq, 0, 0)),
                  pl.BlockSpec((1, d), lambda i: (0, 0)),
                  pl.BlockSpec((d, LANES), lambda i: (0, 0)),
                  pl.BlockSpec((1, LANES), lambda i: (0, 0))],
        out_specs=[pl.BlockSpec((tm, d), lambda i: (i, 0)),
                   pl.BlockSpec((tm, d), lambda i: (i, 0)),
                   pl.BlockSpec((tm, LANES), lambda i: (i, 0))],
        out_shape=[jax.ShapeDtypeStruct((rows, d), F32),
                   jax.ShapeDtypeStruct((rows, d), F32),
                   jax.ShapeDtypeStruct((rows, LANES), F32)],
        compiler_params=pltpu.CompilerParams(
            dimension_semantics=("parallel",), vmem_limit_bytes=VMEM_LIMIT),
    )(y_r, y_a, wo, x2d, mod, g2, wr, br)


def _moe_kernel(texp_ref, first_ref, next_ref, rows_ref, meta_ref, src_ref,
                h2_hbm, wg_hbm, wu_hbm, wd_hbm, o_ref,
                xbuf, xsem, wg_land, wu_land, wd_land, wsem, wg_bf, wu_bf, wd_bf, *, tm):
    t = pl.program_id(0)
    n_used = meta_ref[0]
    slot = lax.rem(t, MOE_GATHER_SLOTS)

    def weight_copies(e):
        cps = []
        for m, (hbm, land) in enumerate(((wg_hbm, wg_land), (wu_hbm, wu_land), (wd_hbm, wd_land))):
            rows = hbm.shape[1] // W_CHUNKS
            for ck in range(W_CHUNKS):
                part = pl.ds(ck * rows, rows)
                cps.append(pltpu.make_async_copy(hbm.at[e, part], land.at[part], wsem.at[m, ck]))
        return cps

    def row_groups(tile):
        return (rows_ref[tile] + ROW_GROUP - 1) // ROW_GROUP

    def gather_rows(tile, unrolled=False):
        buf_slot = lax.rem(tile, MOE_GATHER_SLOTS)
        n_groups = row_groups(tile)

        def body(g, carry):
            for k in range(ROW_GROUP):
                i = g * ROW_GROUP + k
                tok = src_ref[tile * tm + i]
                pltpu.make_async_copy(h2_hbm.at[pl.ds(tok, 1)], xbuf.at[buf_slot, pl.ds(i, 1)],
                                      xsem.at[buf_slot]).start()
            return carry

        if unrolled:
            for g in range(tm // ROW_GROUP):
                @pl.when(g < n_groups)
                def _(g=g):
                    body(g, 0)
        else:
            lax.fori_loop(0, n_groups, body, 0)

    @pl.when(t == 0)
    def _():
        xbuf[...] = jnp.zeros_like(xbuf)
        for cp in weight_copies(texp_ref[0]):
            cp.start()
        for ahead in range(MOE_GATHER_SLOTS - 1):
            @pl.when(ahead < n_used)
            def _():
                gather_rows(ahead)

    is_first = jnp.logical_and(t < n_used, first_ref[t] == 1)

    @pl.when(is_first)
    def _():
        for cp in weight_copies(texp_ref[t]):
            cp.wait()
        wg_bf[...] = wg_land[...].astype(BF16)
        wu_bf[...] = wu_land[...].astype(BF16)
        wd_bf[...] = wd_land[...].astype(BF16)

    @pl.when(t + MOE_GATHER_SLOTS - 1 < n_used)
    def _():
        gather_rows(t + MOE_GATHER_SLOTS - 1, unrolled=True)

    @pl.when(jnp.logical_and(is_first, next_ref[t] >= 0))
    def _():
        for cp in weight_copies(next_ref[t]):
            cp.start()

    @pl.when(t < n_used)
    def _():
        def wait_group(g, carry):
            r0 = pl.multiple_of(g * ROW_GROUP, ROW_GROUP)
            pltpu.make_async_copy(h2_hbm.at[pl.ds(0, ROW_GROUP)],
                                  xbuf.at[slot, pl.ds(r0, ROW_GROUP)], xsem.at[slot]).wait()
            return carry

        lax.fori_loop(0, row_groups(t), wait_group, 0)
        x = xbuf[slot].astype(BF16)
        gate = _dot(x, wg_bf[...])
        up = _dot(x, wu_bf[...])
        h = gate * _sigmoid(gate) * up
        o_ref[...] = _dot(h.astype(BF16), wd_bf[...])

    @pl.when(t >= n_used)
    def _():
        o_ref[...] = jnp.zeros_like(o_ref)


def _moe_call(plan, src, h2, wg, wu, wd, tm):
    n_e, d, f = wg.shape
    n_tiles = plan[0].shape[0]
    grid_spec = pltpu.PrefetchScalarGridSpec(
        num_scalar_prefetch=len(plan) + 1,
        grid=(n_tiles,),
        in_specs=[pl.BlockSpec(memory_space=pl.ANY)] * 4,
        out_specs=pl.BlockSpec((tm, d), lambda t, *_: (t, 0)),
        scratch_shapes=[pltpu.VMEM((MOE_GATHER_SLOTS, tm, d), F32),
                        pltpu.SemaphoreType.DMA((MOE_GATHER_SLOTS,)),
                        pltpu.VMEM((d, f), F32), pltpu.VMEM((d, f), F32), pltpu.VMEM((f, d), F32),
                        pltpu.SemaphoreType.DMA((3, W_CHUNKS)),
                        pltpu.VMEM((d, f), BF16), pltpu.VMEM((d, f), BF16), pltpu.VMEM((f, d), BF16)])
    return pl.pallas_call(
        functools.partial(_moe_kernel, tm=tm),
        grid_spec=grid_spec,
        out_shape=jax.ShapeDtypeStruct((n_tiles * tm, d), F32),
        compiler_params=pltpu.CompilerParams(
            dimension_semantics=("arbitrary",), vmem_limit_bytes=VMEM_LIMIT),
    )(*plan, src, h2, wg, wu, wd)


def _final_kernel(pos_ref, ys_hbm, route_ref, x1_ref, mod_ref, gf_ref, o_ref, ybuf, sem, *, tm):
    i = pl.program_id(0)
    n = pl.num_programs(0)
    slot = i % 2

    def gather_rows(tile, buf_slot, unrolled):
        base = tile * tm * 2

        def body(r, carry):
            for s in range(2):
                pltpu.make_async_copy(ys_hbm.at[pl.ds(pos_ref[base + 2 * r + s], 1)],
                                      ybuf.at[buf_slot, s, pl.ds(r, 1)], sem.at[buf_slot]).start()
            return carry

        if unrolled:
            for r in range(tm):
                body(r, 0)
        else:
            lax.fori_loop(0, tm, body, 0, unroll=4)

    @pl.when(i == 0)
    def _():
        gather_rows(0, 0, unrolled=False)

    @pl.when(i + 1 < n)
    def _():
        gather_rows(i + 1, 1 - slot, unrolled=True)

    for s in range(2):
        pltpu.make_async_copy(ys_hbm.at[pl.ds(0, tm)], ybuf.at[slot, s], sem.at[slot]).wait()
    route = route_ref[...]
    w0 = route[:, 2:3]
    w1 = route[:, 3:4]
    f = w0 * ybuf[slot, 0] + w1 * ybuf[slot, 1]
    x2 = x1_ref[...] + mod_ref[0][GATE2:GATE2 + 1] * f
    o_ref[...] = _rms(x2) * gf_ref[...]


def _final_call(pos, ys, route, x1, mod, gf, seq, tm):
    rows, d = x1.shape
    tiles_per_seq = seq // tm
    grid_spec = pltpu.PrefetchScalarGridSpec(
        num_scalar_prefetch=1,
        grid=(rows // tm,),
        in_specs=[pl.BlockSpec(memory_space=pl.ANY),
                  pl.BlockSpec((tm, LANES), lambda i, pos: (i, 0)),
                  pl.BlockSpec((tm, d), lambda i, pos: (i, 0)),
                  pl.BlockSpec((1, 6, d), lambda i, pos: (i // tiles_per_seq, 0, 0)),
                  pl.BlockSpec((1, d), lambda i, pos: (0, 0))],
        out_specs=pl.BlockSpec((tm, d), lambda i, pos: (i, 0)),
        scratch_shapes=[pltpu.VMEM((2, 2, tm, d), F32), pltpu.SemaphoreType.DMA((2,))])
    return pl.pallas_call(
        functools.partial(_final_kernel, tm=tm),
        grid_spec=grid_spec,
        out_shape=jax.ShapeDtypeStruct((rows, d), F32),
        compiler_params=pltpu.CompilerParams(
            dimension_semantics=("arbitrary",), vmem_limit_bytes=VMEM_LIMIT),
    )(pos, ys, route, x1, mod, gf)


def _route_plan(route, tm):
    n = route.shape[0]
    flat_e = route[:, 0:2].astype(jnp.int32).reshape(-1)
    n_tiles = (2 * n + N_EXPERTS * (tm - 1)) // tm
    experts = jnp.arange(N_EXPERTS, dtype=jnp.int32)
    onehot = (flat_e[:, None] == experts[None, :]).astype(jnp.int32)
    csum = jnp.cumsum(onehot, axis=0)
    rank = jnp.sum(csum * onehot, axis=1) - 1
    counts = csum[-1]
    tiles_per_e = (counts + tm - 1) // tm
    tile_end = jnp.cumsum(tiles_per_e)
    tile_start = tile_end - tiles_per_e
    n_used = tile_end[-1]
    pos = jnp.sum(onehot * tile_start[None, :], axis=1) * tm + rank
    token = jnp.arange(2 * n, dtype=jnp.int32) // 2
    src = jnp.zeros((n_tiles * tm,), jnp.int32).at[pos].set(token)
    tiles = jnp.arange(n_tiles, dtype=jnp.int32)
    texp = jnp.sum((tiles[:, None] >= tile_end[None, :]).astype(jnp.int32), axis=1)
    texp = jnp.where(tiles < n_used, texp, texp[n_used - 1])
    first = jnp.concatenate([jnp.ones((1,), jnp.int32), (texp[1:] != texp[:-1]).astype(jnp.int32)])
    later_used = jnp.logical_and(experts[None, :] > experts[:, None], (tiles_per_e > 0)[None, :])
    next_used = jnp.min(jnp.where(later_used, experts[None, :], N_EXPERTS), axis=1)
    next_used = jnp.where(next_used == N_EXPERTS, -1, next_used)
    tile_rows = jnp.clip(counts[texp] - (tiles - tile_start[texp]) * tm, 0, tm)
    return (texp, first, next_used[texp], tile_rows, n_used.reshape(1)), src, pos


def _round_up(n, m):
    return (n + m - 1) // m * m


def kernel(x, c, w_ada, b_ada, norm1_g, w_in, mu_shift, w0, w_decay_up, a0, w_aaa_up, w_gate_up,
           k_k, k_a, r_k, ln_x_w, ln_x_b, rel_bias, beta_rwkv, beta_attn, w_out, norm2_g, w_group,
           b_group, w_expert, b_expert, w_gate, w_up, w_down, norm_f_g):
    assert w_ada.shape[0] == 1, "single trunk layer"
    batch, seq, d = x.shape
    r_width = w0.shape[1]
    a_width = beta_attn.shape[1]
    shift_width = mu_shift.shape[1]
    assert shift_width == 3 * r_width + DECAY_LORA + AAA_LORA + GATE_LORA
    x2d = x.reshape(batch * seq, d)
    row = lambda t: t.reshape(1, -1)

    c8 = jnp.pad(c, ((0, 8 - batch), (0, 0)))
    mod = _ada_call(c8, w_ada[0], row(b_ada[0]), tn=1536)[:batch].reshape(batch, 6, d)

    tm = min(512, seq)
    pw = _round_up(shift_width, 3 * LANES)
    h1 = _hmod_call(x2d, row(norm1_g[0]), mod, tm, seq)
    tm_p = min(1024, seq)
    w_in_t = jnp.swapaxes(w_in[0], 0, 1)
    p_r = _matmul_call(h1, w_in_t, 0, pw, pw // 3, tm_p, F32)
    p_a = _matmul_call(h1, w_in_t, shift_width, 3 * a_width, a_width, tm_p, BF16)

    mu = jnp.pad(mu_shift[0], (0, pw - shift_width))
    vecs = [row(mu), row(w0[0]), row(a0[0]), row(k_k[0]), row(k_a[0]), row(r_k[0]), row(ln_x_w[0]),
            row(ln_x_b[0]), row(beta_rwkv[0])]
    y_r = _rwkv_call(p_r, vecs, w_decay_up[0], w_aaa_up[0], w_gate_up[0], batch, seq, r_width,
                     tb=min(256, seq))

    left = LEFT_CHUNKS * CHUNK
    dist = left + CHUNK - 1 - jnp.arange(BAND + CHUNK - 1)
    base = rel_bias[0][:, jnp.clip(dist, -REL_CLIP, REL_CLIP) + REL_CLIP].astype(F32)
    bias = jnp.stack([base[:, CHUNK - 1 - qi:CHUNK - 1 - qi + BAND] for qi in range(CHUNK)], axis=1)
    nb = ATTN_BLOCK_CHUNKS
    bias = jnp.concatenate(
        [jnp.pad(bias, ((0, 0), (0, 0), (cb * CHUNK, (nb - 1 - cb) * CHUNK)), constant_values=NEG_INF)
         for cb in range(nb)], axis=1)
    bias = bias.reshape(a_width // PAIR, 2 * nb * CHUNK, BAND + (nb - 1) * CHUNK)
    y_a = _attn_call(p_a, bias, row(beta_attn[0]), batch, seq, a_width, qb=min(512, seq))

    n_route = N_GROUPS + N_EXPERTS
    wr = jnp.concatenate([w_group[0], w_expert[0].transpose(1, 0, 2).reshape(d, N_EXPERTS)], axis=1)
    wr = jnp.pad(wr, ((0, 0), (0, LANES - n_route)))
    br = jnp.pad(jnp.concatenate([b_group[0], b_expert[0].reshape(-1)]), (0, LANES - n_route))
    x1, h2, route = _out_router_call(y_r, y_a, w_out[0].astype(BF16), x2d, mod, row(norm2_g[0]),
                                     wr, row(br), seq, tm)

    f = w_gate.shape[-1]
    tm_e = 256
    plan, src, pos = _route_plan(route, tm_e)
    ys = _moe_call(plan, src, h2, w_gate[0].reshape(N_EXPERTS, d, f),
                   w_up[0].reshape(N_EXPERTS, d, f), w_down[0].reshape(N_EXPERTS, f, d), tm_e)
    out = _final_call(pos, ys, route, x1, mod, row(norm_f_g), seq, min(256, seq))
    return out.reshape(batch, seq, d)
```

```python
import functools

import jax
import jax.numpy as jnp
from jax import lax
from jax.experimental import pallas as pl
from jax.experimental.pallas import tpu as pltpu

F32 = jnp.float32
BF16 = jnp.bfloat16

LANES = 128
SUBLANES = 8
HEAD_DIM = 64
PAIR = 2 * HEAD_DIM
CHUNK = 64
LEFT_CHUNKS = 8
BAND = (LEFT_CHUNKS + 1) * CHUNK
ATTN_BLOCK_CHUNKS = 4
PREP_PAIRS = 4
MOE_GATHER_SLOTS = 3
W_CHUNKS = 2
ROW_GROUP = 8
REL_CLIP = 128
DECAY_LORA = 64
AAA_LORA = 64
GATE_LORA = 160
N_GROUPS = 4
EXPERTS_PER_GROUP = 8
N_EXPERTS = N_GROUPS * EXPERTS_PER_GROUP
RMS_EPS = 1e-6
GN_EPS = 64e-5
L2_EPS = 1e-12
NEG_INF = -1e30
VMEM_LIMIT = 56 * 1024 * 1024

SHIFT1, SCALE1, GATE1, SHIFT2, SCALE2, GATE2 = range(6)


def _dot(a, b):
    return jnp.dot(a, b, preferred_element_type=F32)


def _dot_nt(a, b):
    return lax.dot_general(a, b, (((1,), (1,)), ((), ())), preferred_element_type=F32)


def _sigmoid(x):
    return 1.0 / (1.0 + jnp.exp(-x))


def _rms(x):
    return x * lax.rsqrt(jnp.mean(x * x, axis=-1, keepdims=True) + RMS_EPS)


def _ada_kernel(c_ref, w_ref, b_ref, o_ref):
    c = c_ref[...]
    o_ref[...] = _dot_x3(c * _sigmoid(c), w_ref[...]) + b_ref[...]


def _ada_call(c8, w, b, tn):
    rows, d = c8.shape
    n = w.shape[1]
    return pl.pallas_call(
        _ada_kernel,
        grid=(n // tn,),
        in_specs=[pl.BlockSpec((rows, d), lambda j: (0, 0)),
                  pl.BlockSpec((d, tn), lambda j: (0, j)),
                  pl.BlockSpec((1, tn), lambda j: (0, j))],
        out_specs=pl.BlockSpec((rows, tn), lambda j: (0, j)),
        out_shape=jax.ShapeDtypeStruct((rows, n), F32),
        compiler_params=pltpu.CompilerParams(vmem_limit_bytes=VMEM_LIMIT),
    )(c8, w, b)


def _hmod_kernel(x_ref, g_ref, mod_ref, o_ref):
    m = mod_ref[0]
    h = _rms(x_ref[...]) * g_ref[...] * (1.0 + m[SCALE1:SCALE1 + 1]) + m[SHIFT1:SHIFT1 + 1]
    o_ref[...] = h.astype(BF16)


def _hmod_call(x2d, g, mod, tm, seq):
    rows, d = x2d.shape
    tiles_per_seq = seq // tm
    return pl.pallas_call(
        _hmod_kernel,
        grid=(rows // tm,),
        in_specs=[pl.BlockSpec((tm, d), lambda i: (i, 0)),
                  pl.BlockSpec((1, d), lambda i: (0, 0)),
                  pl.BlockSpec((1, 6, d), lambda i: (i // tiles_per_seq, 0, 0))],
        out_specs=pl.BlockSpec((tm, d), lambda i: (i, 0)),
        out_shape=jax.ShapeDtypeStruct((rows, d), BF16),
        compiler_params=pltpu.CompilerParams(
            dimension_semantics=("parallel",), vmem_limit_bytes=VMEM_LIMIT),
    )(x2d, g, mod)


def _matmul_kernel(a_ref, wt_ref, o_ref, w_bf):
    @pl.when(pl.program_id(1) == 0)
    def _():
        w_bf[...] = wt_ref[...].astype(BF16)

    o_ref[...] = _dot_nt(a_ref[...], w_bf[...]).astype(o_ref.dtype)


def _matmul_call(a, wt, row0, n_out, tn, tm, out_dtype):
    rows, d = a.shape
    return pl.pallas_call(
        _matmul_kernel,
        grid=(n_out // tn, rows // tm),
        in_specs=[pl.BlockSpec((tm, d), lambda j, i: (i, 0)),
                  pl.BlockSpec((pl.Element(tn), pl.Element(d)),
                               lambda j, i: (pl.multiple_of(row0 + j * tn, SUBLANES), 0))],
        out_specs=pl.BlockSpec((tm, tn), lambda j, i: (i, j)),
        out_shape=jax.ShapeDtypeStruct((rows, n_out), out_dtype),
        scratch_shapes=[pltpu.VMEM((tn, d), BF16)],
        compiler_params=pltpu.CompilerParams(
            dimension_semantics=("parallel", "arbitrary"), vmem_limit_bytes=VMEM_LIMIT),
    )(a, wt)


def _split3(t):
    hi = t.astype(BF16)
    rest = t - hi.astype(F32)
    mid = rest.astype(BF16)
    lo = (rest - mid.astype(F32)).astype(BF16)
    return hi, mid, lo


def _dot_hilo_lhs(t, m):
    hi = t.astype(BF16)
    lo = (t - hi.astype(F32)).astype(BF16)
    return _dot(lo, m) + _dot(hi, m)


def _dot_f32_rhs(m, t):
    hi, mid, lo = _split3(t)
    return _dot(m, lo) + _dot(m, mid) + _dot(m, hi)


def _dot_x3(a, b):
    ah = a.astype(BF16)
    al = (a - ah.astype(F32)).astype(BF16)
    bh = b.astype(BF16)
    bl = (b - bh.astype(F32)).astype(BF16)
    return _dot(al, bh) + _dot(ah, bl) + _dot(ah, bh)


def _rwkv_kernel(p_ref, mu_ref, w0_ref, a0_ref, kk_ref, ka_ref, rk_ref, lnw_ref, lnb_ref, beta_ref,
                 wd_ref, wa_ref, wg_ref, o_ref,
                 carry_scr, state_scr, ar_scr, bk_scr, bkh_scr, v_scr, et_scr, y_scr,
                 pm_scr, arb_scr, av_scr, vk_scr, *, width, tb):
    n_pairs = width // PAIR
    n_chunks = tb // CHUNK
    stacked = 2 * CHUNK

    @pl.when(pl.program_id(1) == 0)
    def _():
        carry_scr[...] = jnp.zeros_like(carry_scr)
        state_scr[...] = jnp.zeros_like(state_scr)

    p = p_ref[...]
    row = lax.broadcasted_iota(jnp.int32, (tb, 1), 0)
    prev = jnp.where(row == 0, carry_scr[...], pltpu.roll(p, 1, axis=0))
    carry_scr[...] = p[tb - 1:tb, :]
    ps = p + (prev - p) * mu_ref[...]

    c = width
    r = ps[:, :c]
    k = ps[:, c:2 * c]
    v = ps[:, 2 * c:3 * c]
    o = 3 * c
    xw = ps[:, o:o + DECAY_LORA]
    xa = ps[:, o + DECAY_LORA:o + DECAY_LORA + AAA_LORA]
    xg = ps[:, o + DECAY_LORA + AAA_LORA:o + DECAY_LORA + AAA_LORA + GATE_LORA]

    z = w0_ref[...] + _dot_x3(jnp.tanh(xw), wd_ref[...])
    softplus_neg_z = jnp.maximum(-z, 0.0) + jnp.log(1.0 + jnp.exp(-jnp.abs(z)))
    lw = -jnp.exp(-softplus_neg_z - 0.5)
    a = _sigmoid(a0_ref[...] + _dot(xa.astype(BF16), wa_ref[...].astype(BF16)))
    g = _dot(_sigmoid(xg).astype(BF16), wg_ref[...].astype(BF16))

    li = lax.broadcasted_iota(jnp.int32, (PAIR, PAIR), 0)
    lj = lax.broadcasted_iota(jnp.int32, (PAIR, PAIR), 1)
    head_ones = ((li // HEAD_DIM) == (lj // HEAD_DIM)).astype(BF16)

    def head_sum(t):
        return jnp.concatenate(
            [_dot_hilo_lhs(t[:, q * PAIR:(q + 1) * PAIR], head_ones) for q in range(n_pairs)], axis=1)

    kk = k * kk_ref[...]
    kk = kk / jnp.maximum(jnp.sqrt(head_sum(kk * kk)), L2_EPS)
    k2 = k * (1.0 + (a - 1.0) * ka_ref[...])

    ti = lax.broadcasted_iota(jnp.int32, (tb, tb), 0)
    tj = lax.broadcasted_iota(jnp.int32, (tb, tb), 1)
    tri = jnp.logical_and((ti // CHUNK) == (tj // CHUNK), tj <= ti).astype(BF16)
    cum = _dot_f32_rhs(tri, lw)
    tot = jnp.concatenate(
        [jnp.broadcast_to(cum[(ci + 1) * CHUNK - 1:(ci + 1) * CHUNK, :], (CHUNK, c))
         for ci in range(n_chunks)], axis=0)

    e_neg = jnp.exp(-cum)
    e_rem = jnp.exp(tot - cum)
    kka = kk * a
    lane = lax.broadcasted_iota(jnp.int32, (1, PAIR), 1)
    first_head = lane < HEAD_DIM

    def put(scr, base, val):
        for q in range(n_pairs):
            vq = val[:, q * PAIR:(q + 1) * PAIR]
            h0 = jnp.where(first_head, vq, 0.0).astype(scr.dtype)
            h1 = jnp.where(first_head, 0.0, vq).astype(scr.dtype)
            for ci in range(n_chunks):
                rows = slice(ci * CHUNK, (ci + 1) * CHUNK)
                scr[q, ci, base:base + CHUNK, :] = h0[rows]
                scr[q, ci, base + CHUNK:base + stacked, :] = h1[rows]

    put(ar_scr, 0, -kk * jnp.exp(cum - lw))
    put(ar_scr, stacked, r * jnp.exp(cum))
    put(bk_scr, 0, kka * e_neg)
    put(bk_scr, stacked, k2 * e_neg)
    put(bkh_scr, 0, kka * e_rem)
    put(bkh_scr, stacked, k2 * e_rem)
    put(v_scr, 0, v)
    e_tot = jnp.exp(tot)
    for q in range(n_pairs):
        et_scr[q] = e_tot[:, q * PAIR:(q + 1) * PAIR]

    si = lax.broadcasted_iota(jnp.int32, (2 * stacked, 2 * stacked), 0)
    sj = lax.broadcasted_iota(jnp.int32, (2 * stacked, 2 * stacked), 1)
    keep = jnp.where(si < stacked, si, si - stacked + 1) > (sj % stacked)
    eye = (li == lj).astype(F32)

    chunks = range(n_chunks)

    def side(ta, tb):
        return jnp.concatenate([ta, tb], axis=1)

    def blockdiag(ta, tb):
        za = jnp.zeros((ta.shape[0], tb.shape[1]), ta.dtype)
        zb = jnp.zeros((tb.shape[0], ta.shape[1]), ta.dtype)
        return jnp.concatenate([side(ta, za), side(zb, tb)], axis=0)

    def halves(t):
        return t[:, :PAIR], t[:, PAIR:]

    def bf(t):
        return t.astype(BF16)

    hr = lax.broadcasted_iota(jnp.int32, (CHUNK, 4 * CHUNK), 0)
    hl = lax.broadcasted_iota(jnp.int32, (CHUNK, 4 * CHUNK), 1)
    eye4 = (hr == hl % CHUNK).astype(F32)
    zero_bf = jnp.zeros((CHUNK, stacked), BF16)

    def head_blocks(t):
        z = jnp.zeros_like(t)
        return jnp.concatenate([jnp.where(hl // CHUNK == j, t, z) for j in range(4)], axis=0)

    def prepare(step):
        couples = [(step * PREP_PAIRS + 2 * j, ci) for j in range(PREP_PAIRS // 2) for ci in chunks]
        pw = []
        for qa, ci in couples:
            n, tail, vm = [], [], []
            for q in (qa, qa + 1):
                sc = jnp.where(keep, _dot_nt(ar_scr[q, ci], bk_scr[q, ci]), 0.0)
                scb = bf(sc)
                arb_scr[q, ci] = scb[stacked:, :stacked]
                n.append(sc[:stacked, :stacked])
                tail.append(scb[:, stacked:])
                vm.append(v_scr[q, ci])
            av = _dot(side(*tail), blockdiag(bf(vm[0]), bf(vm[1])))
            vk = _dot(side(bf(vm[0].T), bf(vm[1].T)),
                      blockdiag(bkh_scr[qa, ci, stacked:, :], bkh_scr[qa + 1, ci, stacked:, :]))
            for q, t_av, t_vk in zip((qa, qa + 1), halves(av), halves(vk)):
                av_scr[q, ci] = t_av
                vk_scr[q, ci] = t_vk
            pw.append(bf(side(n[0][:CHUNK] + n[0][CHUNK:], n[1][:CHUNK] + n[1][CHUNK:])))
        acc = [eye4 + p_ for p_ in pw]
        pw = [bf(_dot(p_, head_blocks(p_))) for p_ in pw]
        for it in range(1, 6):
            rhs = [head_blocks(p_) for p_ in pw]
            if it < 5:
                both = [_dot(jnp.concatenate([p_, bf(a_)], axis=0), r_)
                        for p_, a_, r_ in zip(pw, acc, rhs)]
                pw = [bf(t[:CHUNK]) for t in both]
                acc = [a_ + t[CHUNK:] for a_, t in zip(acc, both)]
            else:
                acc = [a_ + _dot(bf(a_), r_) for a_, r_ in zip(acc, rhs)]
        for (qa, ci), a_ in zip(couples, acc):
            for q, t in zip((qa, qa + 1), halves(bf(a_))):
                pm_scr[q, ci] = jnp.concatenate(
                    [jnp.where(first_head, t, zero_bf), jnp.where(first_head, zero_bf, t)], axis=0)

    for step in range(n_pairs // PREP_PAIRS):
        prepare(step)

    firsts = range(0, n_pairs, 2)
    s = [side(state_scr[q], state_scr[q + 1]) for q in firsts]
    for ci in chunks:
        rows = slice(ci * CHUNK, (ci + 1) * CHUNK)

        def both(scr, part=slice(None)):
            return [side(scr[q, ci, part, :], scr[q + 1, ci, part, :]) for q in firsts]

        top, bottom = slice(0, stacked), slice(stacked, 2 * stacked)
        sr = [_dot_nt(ar_, blockdiag(*halves(bf(s_)))) for ar_, s_ in zip(both(ar_scr), s)]
        u = [_dot(pm_, blockdiag(*halves(bf(sr_[:stacked] + av_))))
             for pm_, sr_, av_ in zip(both(pm_scr), sr, both(av_scr, top))]
        ym = [sr_[stacked:] + av_ + _dot(arb_, blockdiag(*halves(bf(u_))))
              for sr_, av_, arb_, u_ in zip(sr, both(av_scr, bottom), both(arb_scr), u)]
        for q, ym_ in zip(firsts, ym):
            ya, yb = halves(ym_[:CHUNK] + ym_[CHUNK:])
            y_scr[q, rows, :] = ya
            y_scr[q + 1, rows, :] = yb
        decay = [side(et_scr[q, ci * CHUNK:ci * CHUNK + 1, :], et_scr[q + 1, ci * CHUNK:ci * CHUNK + 1, :])
                 for q in firsts]
        s = [s_ * d_ + _dot(side(*(bf(t.T) for t in halves(u_))), blockdiag(*halves(bh_))) + vk_
             for s_, d_, u_, bh_, vk_ in zip(s, decay, u, both(bkh_scr, top), both(vk_scr))]
    for q, s_ in zip(firsts, s):
        state_scr[q], state_scr[q + 1] = halves(s_)

    y = jnp.concatenate([y_scr[q] for q in range(n_pairs)], axis=1)
    mean = head_sum(y) * (1.0 / HEAD_DIM)
    d = y - mean
    var = head_sum(d * d) * (1.0 / HEAD_DIM)
    yn = d * lax.rsqrt(var + GN_EPS) * lnw_ref[...] + lnb_ref[...]
    bonus = head_sum(r * k2 * rk_ref[...]) * v
    o_ref[...] = (yn + bonus) * g * beta_ref[...]


def _rwkv_call(p_r, vecs, wd, wa, wg, batch, seq, width, tb):
    assert CHUNK == HEAD_DIM, "chunk operands stack two heads' rows against their 2 * HEAD_DIM lanes"
    assert (width // PAIR) % PREP_PAIRS == 0 and PREP_PAIRS % 2 == 0
    pw = p_r.shape[1]
    n_pairs = width // PAIR
    n_chunks = tb // CHUNK
    tiles = seq // tb
    vec_specs = [pl.BlockSpec((1, v.shape[1]), lambda b, t: (0, 0)) for v in vecs]
    mat_specs = [pl.BlockSpec(m.shape, lambda b, t: (0, 0)) for m in (wd, wa, wg)]
    stacked2 = pltpu.VMEM((n_pairs, n_chunks, 4 * CHUNK, PAIR), BF16)
    return pl.pallas_call(
        functools.partial(_rwkv_kernel, width=width, tb=tb),
        grid=(batch, tiles),
        in_specs=[pl.BlockSpec((tb, pw), lambda b, t: (b * tiles + t, 0))] + vec_specs + mat_specs,
        out_specs=pl.BlockSpec((tb, width), lambda b, t: (b * tiles + t, 0)),
        out_shape=jax.ShapeDtypeStruct((batch * seq, width), F32),
        scratch_shapes=[pltpu.VMEM((1, pw), F32),
                        pltpu.VMEM((n_pairs, PAIR, PAIR), F32),
                        stacked2, stacked2, stacked2,
                        pltpu.VMEM((n_pairs, n_chunks, 2 * CHUNK, PAIR), F32),
                        pltpu.VMEM((n_pairs, tb, PAIR), F32),
                        pltpu.VMEM((n_pairs, tb, PAIR), F32),
                        pltpu.VMEM((n_pairs, n_chunks, 2 * CHUNK, PAIR), BF16),
                        pltpu.VMEM((n_pairs, n_chunks, 2 * CHUNK, PAIR), BF16),
                        pltpu.VMEM((n_pairs, n_chunks, 4 * CHUNK, PAIR), F32),
                        pltpu.VMEM((n_pairs, n_chunks, 2 * CHUNK, PAIR), F32)],
        compiler_params=pltpu.CompilerParams(
            dimension_semantics=("parallel", "arbitrary"), vmem_limit_bytes=VMEM_LIMIT),
    )(p_r, *vecs, wd, wa, wg)


def _attn_kernel(q_ref, kp_ref, kc_ref, vp_ref, vc_ref, bias_ref, beta_ref, o_ref, kwin, vwin,
                 *, width, qb):
    i = pl.program_id(1)
    n_pairs = width // PAIR
    kwin[0:qb, :] = kp_ref[...]
    kwin[qb:2 * qb, :] = kc_ref[...]
    vwin[0:qb, :] = vp_ref[...]
    vwin[qb:2 * qb, :] = vc_ref[...]
    lane = lax.broadcasted_iota(jnp.int32, (1, PAIR), 1)
    first_head = lane < HEAD_DIM
    rows = ATTN_BLOCK_CHUNKS * CHUNK
    win = BAND + rows - CHUNK
    col = lax.broadcasted_iota(jnp.int32, (1, win), 1)
    scale = HEAD_DIM ** -0.5
    left = LEFT_CHUNKS * CHUNK

    def attend(first_block):
        for q in range(n_pairs):
            lanes = slice(q * PAIR, (q + 1) * PAIR)
            for blk in range(qb // rows):
                r0 = blk * rows
                qc = q_ref[r0:r0 + rows, lanes]
                zero = jnp.zeros_like(qc)
                qm = jnp.concatenate(
                    [jnp.where(first_head, qc, zero), jnp.where(first_head, zero, qc)], axis=0)
                s = _dot_nt(qm, kwin[r0:r0 + win, lanes]) * scale + bias_ref[q]
                if first_block:
                    s = jnp.where(col + r0 >= left, s, NEG_INF)
                e = jnp.exp(s - jnp.max(s, axis=-1, keepdims=True))
                denom = jnp.sum(e, axis=-1, keepdims=True)
                o = _dot(e.astype(BF16), vwin[r0:r0 + win, lanes]) / denom
                oc = jnp.where(first_head, o[:rows], o[rows:])
                o_ref[r0:r0 + rows, lanes] = oc * beta_ref[:, lanes]

    pl.when(i == 0)(lambda: attend(True))
    pl.when(i > 0)(lambda: attend(False))


def _attn_call(p_a, bias, beta, batch, seq, width, qb):
    assert qb == LEFT_CHUNKS * CHUNK, "key window = previous block + current block"
    tiles = seq // qb
    n_pairs = width // PAIR

    def cur(col):
        return pl.BlockSpec((qb, width), lambda b, t: (b * tiles + t, col))

    def prev(col):
        return pl.BlockSpec((qb, width), lambda b, t: (b * tiles + jnp.maximum(t - 1, 0), col))

    return pl.pallas_call(
        functools.partial(_attn_kernel, width=width, qb=qb),
        grid=(batch, tiles),
        in_specs=[cur(0), prev(1), cur(1), prev(2), cur(2),
                  pl.BlockSpec(bias.shape, lambda b, t: (0, 0, 0)),
                  pl.BlockSpec((1, width), lambda b, t: (0, 0))],
        out_specs=pl.BlockSpec((qb, width), lambda b, t: (b * tiles + t, 0)),
        out_shape=jax.ShapeDtypeStruct((batch * seq, width), F32),
        scratch_shapes=[pltpu.VMEM((2 * qb, width), BF16), pltpu.VMEM((2 * qb, width), BF16)],
        compiler_params=pltpu.CompilerParams(
            dimension_semantics=("parallel", "arbitrary"), vmem_limit_bytes=VMEM_LIMIT),
    )(p_a, p_a, p_a, p_a, p_a, bias, beta)


def _out_router_kernel(yr_ref, ya_ref, wo_ref, x_ref, mod_ref, g2_ref, wr_ref, br_ref,
                       x1_ref, h2_ref, route_ref, *, r_width):
    y = (_dot(yr_ref[...].astype(BF16), wo_ref[0:r_width, :])
         + _dot(ya_ref[...].astype(BF16), wo_ref[r_width:, :]))
    m = mod_ref[0]
    x1 = x_ref[...] + m[GATE1:GATE1 + 1] * y
    x1_ref[...] = x1
    h2 = _rms(x1) * g2_ref[...] * (1.0 + m[SCALE2:SCALE2 + 1]) + m[SHIFT2:SHIFT2 + 1]
    h2_ref[...] = h2

    wr = wr_ref[...]
    wr_hi = wr.astype(BF16)
    wr_lo = (wr - wr_hi.astype(F32)).astype(BF16)
    h2_hi = h2.astype(BF16)
    h2_lo = (h2 - h2_hi.astype(F32)).astype(BF16)
    hi_both = _dot(h2_hi, jnp.concatenate([wr_hi, wr_lo], axis=1))
    logits = _dot(h2_lo, wr_hi) + hi_both[:, LANES:] + hi_both[:, :LANES] + br_ref[...]
    li = lax.broadcasted_iota(jnp.int32, logits.shape, 1).astype(F32)
    none = jnp.float32(-jnp.inf)
    far = jnp.float32(LANES)
    is_group = li < N_GROUPS
    gl = jnp.where(is_group, logits, none)
    gmax = jnp.max(gl, axis=-1, keepdims=True)
    gidx = jnp.min(jnp.where(gl == gmax, li, far), axis=-1, keepdims=True)
    g_w = 1.0 / jnp.sum(jnp.where(is_group, jnp.exp(logits - gmax), 0.0), axis=-1, keepdims=True)
    lo = N_GROUPS + EXPERTS_PER_GROUP * gidx
    el = jnp.where(jnp.logical_and(li >= lo, li < lo + EXPERTS_PER_GROUP), logits, none)
    m1 = jnp.max(el, axis=-1, keepdims=True)
    i1 = jnp.min(jnp.where(el == m1, li, far), axis=-1, keepdims=True)
    el2 = jnp.where(li == i1, none, el)
    m2 = jnp.max(el2, axis=-1, keepdims=True)
    i2 = jnp.min(jnp.where(el2 == m2, li, far), axis=-1, keepdims=True)
    e2 = jnp.exp(m2 - m1)
    w1 = g_w / (1.0 + e2)
    w2 = g_w * e2 / (1.0 + e2)
    route = jnp.where(li == 0.0, i1 - N_GROUPS, 0.0) + jnp.where(li == 1.0, i2 - N_GROUPS, 0.0)
    route_ref[...] = route + jnp.where(li == 2.0, w1, 0.0) + jnp.where(li == 3.0, w2, 0.0)


def _out_router_call(y_r, y_a, wo, x2d, mod, g2, wr, br, seq, tm):
    rows, d = x2d.shape
    r_width = y_r.shape[1]
    a_width = y_a.shape[1]
    tiles_per_seq = seq // tm
    return pl.pallas_call(
        functools.partial(_out_router_kernel, r_width=r_width),
        grid=(rows // tm,),
        in_specs=[pl.BlockSpec((tm, r_width), lambda i: (i, 0)),
                  pl.BlockSpec((tm, a_width), lambda i: (i, 0)),
                  pl.BlockSpec(wo.shape, lambda i: (0, 0)),
                  pl.BlockSpec((tm, d), lambda i: (i, 0)),
                  pl.BlockSpec((1, 6, d), lambda i: (i // tiles_per_seq, 0, 0)),
                  pl.BlockSpec((1, d), lambda i: (0, 0)),
                  pl.BlockSpec((d, LANES), lambda i: (0, 0)),
                  pl.BlockSpec((1, LANES), lambda i: (0, 0))],
        out_specs=[pl.BlockSpec((tm, d), lambda i: (i, 0)),
                   pl.BlockSpec((tm, d), lambda i: (i, 0)),
                   pl.BlockSpec((tm, LANES), lambda i: (i, 0))],
        out_shape=[jax.ShapeDtypeStruct((rows, d), F32),
                   jax.ShapeDtypeStruct((rows, d), F32),
                   jax.ShapeDtypeStruct((rows, LANES), F32)],
        compiler_params=pltpu.CompilerParams(
            dimension_semantics=("parallel",), vmem_limit_bytes=VMEM_LIMIT),
    )(y_r, y_a, wo, x2d, mod, g2, wr, br)


def _moe_kernel(texp_ref, first_ref, next_ref, rows_ref, meta_ref, src_ref,
                h2_hbm, wg_hbm, wu_hbm, wd_hbm, o_ref,
                xbuf, xsem, wg_land, wu_land, wd_land, wsem, wg_bf, wu_bf, wd_bf, *, tm):
    t = pl.program_id(0)
    n_used = meta_ref[0]
    slot = lax.rem(t, MOE_GATHER_SLOTS)

    def weight_copies(e):
        cps = []
        for m, (hbm, land) in enumerate(((wg_hbm, wg_land), (wu_hbm, wu_land), (wd_hbm, wd_land))):
            rows = hbm.shape[1] // W_CHUNKS
            for ck in range(W_CHUNKS):
                part = pl.ds(ck * rows, rows)
                cps.append(pltpu.make_async_copy(hbm.at[e, part], land.at[part], wsem.at[m, ck]))
        return cps

    def row_groups(tile):
        return (rows_ref[tile] + ROW_GROUP - 1) // ROW_GROUP

    def gather_rows(tile, unrolled=False):
        buf_slot = lax.rem(tile, MOE_GATHER_SLOTS)
        n_groups = row_groups(tile)

        def body(g, carry):
            for k in range(ROW_GROUP):
                i = g * ROW_GROUP + k
                tok = src_ref[tile * tm + i]
                pltpu.make_async_copy(h2_hbm.at[pl.ds(tok, 1)], xbuf.at[buf_slot, pl.ds(i, 1)],
                                      xsem.at[buf_slot]).start()
            return carry

        if unrolled:
            for g in range(tm // ROW_GROUP):
                @pl.when(g < n_groups)
                def _(g=g):
                    body(g, 0)
        else:
            lax.fori_loop(0, n_groups, body, 0)

    @pl.when(t == 0)
    def _():
        xbuf[...] = jnp.zeros_like(xbuf)
        for cp in weight_copies(texp_ref[0]):
            cp.start()
        for ahead in range(MOE_GATHER_SLOTS - 1):
            @pl.when(ahead < n_used)
            def _():
                gather_rows(ahead)

    is_first = jnp.logical_and(t < n_used, first_ref[t] == 1)

    @pl.when(is_first)
    def _():
        for cp in weight_copies(texp_ref[t]):
            cp.wait()
        wg_bf[...] = wg_land[...].astype(BF16)
        wu_bf[...] = wu_land[...].astype(BF16)
        wd_bf[...] = wd_land[...].astype(BF16)

    @pl.when(t + MOE_GATHER_SLOTS - 1 < n_used)
    def _():
        gather_rows(t + MOE_GATHER_SLOTS - 1, unrolled=True)

    @pl.when(jnp.logical_and(is_first, next_ref[t] >= 0))
    def _():
        for cp in weight_copies(next_ref[t]):
            cp.start()

    @pl.when(t < n_used)
    def _():
        def wait_group(g, carry):
            r0 = pl.multiple_of(g * ROW_GROUP, ROW_GROUP)
            pltpu.make_async_copy(h2_hbm.at[pl.ds(0, ROW_GROUP)],
                                  xbuf.at[slot, pl.ds(r0, ROW_GROUP)], xsem.at[slot]).wait()
            return carry

        lax.fori_loop(0, row_groups(t), wait_group, 0)
        x = xbuf[slot].astype(BF16)
        gate = _dot(x, wg_bf[...])
        up = _dot(x, wu_bf[...])
        h = gate * _sigmoid(gate) * up
        o_ref[...] = _dot(h.astype(BF16), wd_bf[...])

    @pl.when(t >= n_used)
    def _():
        o_ref[...] = jnp.zeros_like(o_ref)


def _moe_call(plan, src, h2, wg, wu, wd, tm):
    n_e, d, f = wg.shape
    n_tiles = plan[0].shape[0]
    grid_spec = pltpu.PrefetchScalarGridSpec(
        num_scalar_prefetch=len(plan) + 1,
        grid=(n_tiles,),
        in_specs=[pl.BlockSpec(memory_space=pl.ANY)] * 4,
        out_specs=pl.BlockSpec((tm, d), lambda t, *_: (t, 0)),
        scratch_shapes=[pltpu.VMEM((MOE_GATHER_SLOTS, tm, d), F32),
                        pltpu.SemaphoreType.DMA((MOE_GATHER_SLOTS,)),
                        pltpu.VMEM((d, f), F32), pltpu.VMEM((d, f), F32), pltpu.VMEM((f, d), F32),
                        pltpu.SemaphoreType.DMA((3, W_CHUNKS)),
                        pltpu.VMEM((d, f), BF16), pltpu.VMEM((d, f), BF16), pltpu.VMEM((f, d), BF16)])
    return pl.pallas_call(
        functools.partial(_moe_kernel, tm=tm),
        grid_spec=grid_spec,
        out_shape=jax.ShapeDtypeStruct((n_tiles * tm, d), F32),
        compiler_params=pltpu.CompilerParams(
            dimension_semantics=("arbitrary",), vmem_limit_bytes=VMEM_LIMIT),
    )(*plan, src, h2, wg, wu, wd)


def _final_kernel(pos_ref, ys_hbm, route_ref, x1_ref, mod_ref, gf_ref, o_ref, ybuf, sem, *, tm):
    i = pl.program_id(0)
    n = pl.num_programs(0)
    slot = i % 2

    def gather_rows(tile, buf_slot, unrolled):
        base = tile * tm * 2

        def body(r, carry):
            for s in range(2):
                pltpu.make_async_copy(ys_hbm.at[pl.ds(pos_ref[base + 2 * r + s], 1)],
                                      ybuf.at[buf_slot, s, pl.ds(r, 1)], sem.at[buf_slot]).start()
            return carry

        if unrolled:
            for r in range(tm):
                body(r, 0)
        else:
            lax.fori_loop(0, tm, body, 0, unroll=4)

    @pl.when(i == 0)
    def _():
        gather_rows(0, 0, unrolled=False)

    @pl.when(i + 1 < n)
    def _():
        gather_rows(i + 1, 1 - slot, unrolled=True)

    for s in range(2):
        pltpu.make_async_copy(ys_hbm.at[pl.ds(0, tm)], ybuf.at[slot, s], sem.at[slot]).wait()
    route = route_ref[...]
    w0 = route[:, 2:3]
    w1 = route[:, 3:4]
    f = w0 * ybuf[slot, 0] + w1 * ybuf[slot, 1]
    x2 = x1_ref[...] + mod_ref[0][GATE2:GATE2 + 1] * f
    o_ref[...] = _rms(x2) * gf_ref[...]


def _final_call(pos, ys, route, x1, mod, gf, seq, tm):
    rows, d = x1.shape
    tiles_per_seq = seq // tm
    grid_spec = pltpu.PrefetchScalarGridSpec(
        num_scalar_prefetch=1,
        grid=(rows // tm,),
        in_specs=[pl.BlockSpec(memory_space=pl.ANY),
                  pl.BlockSpec((tm, LANES), lambda i, pos: (i, 0)),
                  pl.BlockSpec((tm, d), lambda i, pos: (i, 0)),
                  pl.BlockSpec((1, 6, d), lambda i, pos: (i // tiles_per_seq, 0, 0)),
                  pl.BlockSpec((1, d), lambda i, pos: (0, 0))],
        out_specs=pl.BlockSpec((tm, d), lambda i, pos: (i, 0)),
        scratch_shapes=[pltpu.VMEM((2, 2, tm, d), F32), pltpu.SemaphoreType.DMA((2,))])
    return pl.pallas_call(
        functools.partial(_final_kernel, tm=tm),
        grid_spec=grid_spec,
        out_shape=jax.ShapeDtypeStruct((rows, d), F32),
        compiler_params=pltpu.CompilerParams(
            dimension_semantics=("arbitrary",), vmem_limit_bytes=VMEM_LIMIT),
    )(pos, ys, route, x1, mod, gf)


def _route_plan(route, tm):
    n = route.shape[0]
    flat_e = route[:, 0:2].astype(jnp.int32).reshape(-1)
    n_tiles = (2 * n + N_EXPERTS * (tm - 1)) // tm
    experts = jnp.arange(N_EXPERTS, dtype=jnp.int32)
    onehot = (flat_e[:, None] == experts[None, :]).astype(jnp.int32)
    csum = jnp.cumsum(onehot, axis=0)
    rank = jnp.sum(csum * onehot, axis=1) - 1
    counts = csum[-1]
    tiles_per_e = (counts + tm - 1) // tm
    tile_end = jnp.cumsum(tiles_per_e)
    tile_start = tile_end - tiles_per_e
    n_used = tile_end[-1]
    pos = jnp.sum(onehot * tile_start[None, :], axis=1) * tm + rank
    token = jnp.arange(2 * n, dtype=jnp.int32) // 2
    src = jnp.zeros((n_tiles * tm,), jnp.int32).at[pos].set(token)
    tiles = jnp.arange(n_tiles, dtype=jnp.int32)
    texp = jnp.sum((tiles[:, None] >= tile_end[None, :]).astype(jnp.int32), axis=1)
    texp = jnp.where(tiles < n_used, texp, texp[n_used - 1])
    first = jnp.concatenate([jnp.ones((1,), jnp.int32), (texp[1:] != texp[:-1]).astype(jnp.int32)])
    later_used = jnp.logical_and(experts[None, :] > experts[:, None], (tiles_per_e > 0)[None, :])
    next_used = jnp.min(jnp.where(later_used, experts[None, :], N_EXPERTS), axis=1)
    next_used = jnp.where(next_used == N_EXPERTS, -1, next_used)
    tile_rows = jnp.clip(counts[texp] - (tiles - tile_start[texp]) * tm, 0, tm)
    return (texp, first, next_used[texp], tile_rows, n_used.reshape(1)), src, pos


def _round_up(n, m):
    return (n + m - 1) // m * m


def _tile_sizes(seq):
    return dict(ada_cols=1536, rows=min(512, seq), proj_rows=min(1024, seq),
                rwkv_rows=min(4 * CHUNK, seq), attn_rows=min(LEFT_CHUNKS * CHUNK, seq),
                expert_rows=256, final_rows=min(256, seq))


def kernel(x, c, w_ada, b_ada, norm1_g, w_in, mu_shift, w0, w_decay_up, a0, w_aaa_up, w_gate_up,
           k_k, k_a, r_k, ln_x_w, ln_x_b, rel_bias, beta_rwkv, beta_attn, w_out, norm2_g, w_group,
           b_group, w_expert, b_expert, w_gate, w_up, w_down, norm_f_g):
    assert w_ada.shape[0] == 1, "single trunk layer"
    batch, seq, d = x.shape
    r_width = w0.shape[1]
    a_width = beta_attn.shape[1]
    shift_width = mu_shift.shape[1]
    assert shift_width == 3 * r_width + DECAY_LORA + AAA_LORA + GATE_LORA
    x2d = x.reshape(batch * seq, d)
    row = lambda t: t.reshape(1, -1)

    tiles = _tile_sizes(seq)
    tm = tiles["rows"]

    c8 = jnp.pad(c, ((0, SUBLANES - batch), (0, 0)))
    mod = _ada_call(c8, w_ada[0], row(b_ada[0]), tn=tiles["ada_cols"])[:batch].reshape(batch, 6, d)

    pw = _round_up(shift_width, 3 * LANES)
    h1 = _hmod_call(x2d, row(norm1_g[0]), mod, tm, seq)
    w_in_t = jnp.swapaxes(w_in[0], 0, 1)
    p_r = _matmul_call(h1, w_in_t, 0, pw, pw // 3, tiles["proj_rows"], F32)
    p_a = _matmul_call(h1, w_in_t, shift_width, 3 * a_width, a_width, tiles["proj_rows"], BF16)

    mu = jnp.pad(mu_shift[0], (0, pw - shift_width))
    vecs = [row(mu), row(w0[0]), row(a0[0]), row(k_k[0]), row(k_a[0]), row(r_k[0]), row(ln_x_w[0]),
            row(ln_x_b[0]), row(beta_rwkv[0])]
    y_r = _rwkv_call(p_r, vecs, w_decay_up[0], w_aaa_up[0], w_gate_up[0], batch, seq, r_width,
                     tb=tiles["rwkv_rows"])

    left = LEFT_CHUNKS * CHUNK
    dist = left + CHUNK - 1 - jnp.arange(BAND + CHUNK - 1)
    base = rel_bias[0][:, jnp.clip(dist, -REL_CLIP, REL_CLIP) + REL_CLIP].astype(F32)
    bias = jnp.stack([base[:, CHUNK - 1 - qi:CHUNK - 1 - qi + BAND] for qi in range(CHUNK)], axis=1)
    nb = ATTN_BLOCK_CHUNKS
    bias = jnp.concatenate(
        [jnp.pad(bias, ((0, 0), (0, 0), (cb * CHUNK, (nb - 1 - cb) * CHUNK)), constant_values=NEG_INF)
         for cb in range(nb)], axis=1)
    bias = bias.reshape(a_width // PAIR, 2 * nb * CHUNK, BAND + (nb - 1) * CHUNK)
    y_a = _attn_call(p_a, bias, row(beta_attn[0]), batch, seq, a_width, qb=tiles["attn_rows"])

    n_route = N_GROUPS + N_EXPERTS
    wr = jnp.concatenate([w_group[0], w_expert[0].transpose(1, 0, 2).reshape(d, N_EXPERTS)], axis=1)
    wr = jnp.pad(wr, ((0, 0), (0, LANES - n_route)))
    br = jnp.pad(jnp.concatenate([b_group[0], b_expert[0].reshape(-1)]), (0, LANES - n_route))
    x1, h2, route = _out_router_call(y_r, y_a, w_out[0].astype(BF16), x2d, mod, row(norm2_g[0]),
                                     wr, row(br), seq, tm)

    f = w_gate.shape[-1]
    tm_e = tiles["expert_rows"]
    plan, src, pos = _route_plan(route, tm_e)
    ys = _moe_call(plan, src, h2, w_gate[0].reshape(N_EXPERTS, d, f),
                   w_up[0].reshape(N_EXPERTS, d, f), w_down[0].reshape(N_EXPERTS, f, d), tm_e)
    out = _final_call(pos, ys, route, x1, mod, row(norm_f_g), seq, tiles["final_rows"])
    return out.reshape(batch, seq, d)
```

```python
import functools

import jax
import jax.numpy as jnp
from jax import lax
from jax.experimental import pallas as pl
from jax.experimental.pallas import tpu as pltpu

F32 = jnp.float32
BF16 = jnp.bfloat16

LANES = 128
SUBLANES = 8
HEAD_DIM = 64
PAIR = 2 * HEAD_DIM
CHUNK = 64
LEFT_CHUNKS = 8
BAND = (LEFT_CHUNKS + 1) * CHUNK
ATTN_BLOCK_CHUNKS = 4
PREP_PAIRS = 4
MOE_GATHER_SLOTS = 3
W_CHUNKS = 2
ROW_GROUP = 8
REL_CLIP = 128
DECAY_LORA = 64
AAA_LORA = 64
GATE_LORA = 160
N_GROUPS = 4
EXPERTS_PER_GROUP = 8
N_EXPERTS = N_GROUPS * EXPERTS_PER_GROUP
RMS_EPS = 1e-6
GN_EPS = 64e-5
L2_EPS = 1e-12
NEG_INF = -1e30
VMEM_LIMIT = 56 * 1024 * 1024

SHIFT1, SCALE1, GATE1, SHIFT2, SCALE2, GATE2 = range(6)


def _dot(a, b):
    return jnp.dot(a, b, preferred_element_type=F32)


def _dot_nt(a, b):
    return lax.dot_general(a, b, (((1,), (1,)), ((), ())), preferred_element_type=F32)


def _sigmoid(x):
    return 1.0 / (1.0 + jnp.exp(-x))


def _rms(x):
    return x * lax.rsqrt(jnp.mean(x * x, axis=-1, keepdims=True) + RMS_EPS)


def _ada_kernel(c_ref, w_ref, b_ref, o_ref):
    c = c_ref[...]
    o_ref[...] = _dot_x3(c * _sigmoid(c), w_ref[...]) + b_ref[...]


def _ada_call(c8, w, b, tn):
    rows, d = c8.shape
    n = w.shape[1]
    return pl.pallas_call(
        _ada_kernel,
        grid=(n // tn,),
        in_specs=[pl.BlockSpec((rows, d), lambda j: (0, 0)),
                  pl.BlockSpec((d, tn), lambda j: (0, j)),
                  pl.BlockSpec((1, tn), lambda j: (0, j))],
        out_specs=pl.BlockSpec((rows, tn), lambda j: (0, j)),
        out_shape=jax.ShapeDtypeStruct((rows, n), F32),
        compiler_params=pltpu.CompilerParams(vmem_limit_bytes=VMEM_LIMIT),
    )(c8, w, b)


def _hmod_kernel(x_ref, g_ref, mod_ref, o_ref):
    m = mod_ref[0]
    h = _rms(x_ref[...]) * g_ref[...] * (1.0 + m[SCALE1:SCALE1 + 1]) + m[SHIFT1:SHIFT1 + 1]
    o_ref[...] = h.astype(BF16)


def _hmod_call(x2d, g, mod, tm, seq):
    rows, d = x2d.shape
    tiles_per_seq = seq // tm
    return pl.pallas_call(
        _hmod_kernel,
        grid=(rows // tm,),
        in_specs=[pl.BlockSpec((tm, d), lambda i: (i, 0)),
                  pl.BlockSpec((1, d), lambda i: (0, 0)),
                  pl.BlockSpec((1, 6, d), lambda i: (i // tiles_per_seq, 0, 0))],
        out_specs=pl.BlockSpec((tm, d), lambda i: (i, 0)),
        out_shape=jax.ShapeDtypeStruct((rows, d), BF16),
        compiler_params=pltpu.CompilerParams(
            dimension_semantics=("parallel",), vmem_limit_bytes=VMEM_LIMIT),
    )(x2d, g, mod)


def _matmul_kernel(a_ref, wt_ref, o_ref, w_bf):
    @pl.when(pl.program_id(1) == 0)
    def _():
        w_bf[...] = wt_ref[...].astype(BF16)

    o_ref[...] = _dot_nt(a_ref[...], w_bf[...]).astype(o_ref.dtype)


def _matmul_call(a, wt, row0, n_out, tn, tm, out_dtype):
    rows, d = a.shape
    return pl.pallas_call(
        _matmul_kernel,
        grid=(n_out // tn, rows // tm),
        in_specs=[pl.BlockSpec((tm, d), lambda j, i: (i, 0)),
                  pl.BlockSpec((pl.Element(tn), pl.Element(d)),
                               lambda j, i: (pl.multiple_of(row0 + j * tn, SUBLANES), 0))],
        out_specs=pl.BlockSpec((tm, tn), lambda j, i: (i, j)),
        out_shape=jax.ShapeDtypeStruct((rows, n_out), out_dtype),
        scratch_shapes=[pltpu.VMEM((tn, d), BF16)],
        compiler_params=pltpu.CompilerParams(
            dimension_semantics=("parallel", "arbitrary"), vmem_limit_bytes=VMEM_LIMIT),
    )(a, wt)


def _split3(t):
    hi = t.astype(BF16)
    rest = t - hi.astype(F32)
    mid = rest.astype(BF16)
    lo = (rest - mid.astype(F32)).astype(BF16)
    return hi, mid, lo


def _dot_hilo_lhs(t, m):
    hi = t.astype(BF16)
    lo = (t - hi.astype(F32)).astype(BF16)
    return _dot(lo, m) + _dot(hi, m)


def _dot_f32_rhs(m, t):
    hi, mid, lo = _split3(t)
    return _dot(m, lo) + _dot(m, mid) + _dot(m, hi)


def _dot_x3(a, b):
    ah = a.astype(BF16)
    al = (a - ah.astype(F32)).astype(BF16)
    bh = b.astype(BF16)
    bl = (b - bh.astype(F32)).astype(BF16)
    return _dot(al, bh) + _dot(ah, bl) + _dot(ah, bh)


def _rwkv_kernel(p_ref, mu_ref, w0_ref, a0_ref, kk_ref, ka_ref, rk_ref, lnw_ref, lnb_ref, beta_ref,
                 wd_ref, wa_ref, wg_ref, o_ref,
                 carry_scr, state_scr, ar_scr, bk_scr, bkh_scr, v_scr, et_scr, y_scr,
                 pm_scr, arb_scr, av_scr, vk_scr, *, width, tb):
    n_pairs = width // PAIR
    n_chunks = tb // CHUNK
    stacked = 2 * CHUNK

    @pl.when(pl.program_id(1) == 0)
    def _():
        carry_scr[...] = jnp.zeros_like(carry_scr)
        state_scr[...] = jnp.zeros_like(state_scr)

    p = p_ref[...]
    row = lax.broadcasted_iota(jnp.int32, (tb, 1), 0)
    prev = jnp.where(row == 0, carry_scr[...], pltpu.roll(p, 1, axis=0))
    carry_scr[...] = p[tb - 1:tb, :]
    ps = p + (prev - p) * mu_ref[...]

    c = width
    r = ps[:, :c]
    k = ps[:, c:2 * c]
    v = ps[:, 2 * c:3 * c]
    o = 3 * c
    xw = ps[:, o:o + DECAY_LORA]
    xa = ps[:, o + DECAY_LORA:o + DECAY_LORA + AAA_LORA]
    xg = ps[:, o + DECAY_LORA + AAA_LORA:o + DECAY_LORA + AAA_LORA + GATE_LORA]

    z = w0_ref[...] + _dot_x3(jnp.tanh(xw), wd_ref[...])
    softplus_neg_z = jnp.maximum(-z, 0.0) + jnp.log(1.0 + jnp.exp(-jnp.abs(z)))
    lw = -jnp.exp(-softplus_neg_z - 0.5)
    a = _sigmoid(a0_ref[...] + _dot(xa.astype(BF16), wa_ref[...].astype(BF16)))
    g = _dot(_sigmoid(xg).astype(BF16), wg_ref[...].astype(BF16))

    li = lax.broadcasted_iota(jnp.int32, (PAIR, PAIR), 0)
    lj = lax.broadcasted_iota(jnp.int32, (PAIR, PAIR), 1)
    head_ones = ((li // HEAD_DIM) == (lj // HEAD_DIM)).astype(BF16)

    def head_sum(t):
        return jnp.concatenate(
            [_dot_hilo_lhs(t[:, q * PAIR:(q + 1) * PAIR], head_ones) for q in range(n_pairs)], axis=1)

    kk = k * kk_ref[...]
    kk = kk / jnp.maximum(jnp.sqrt(head_sum(kk * kk)), L2_EPS)
    k2 = k * (1.0 + (a - 1.0) * ka_ref[...])

    ti = lax.broadcasted_iota(jnp.int32, (tb, tb), 0)
    tj = lax.broadcasted_iota(jnp.int32, (tb, tb), 1)
    tri = jnp.logical_and((ti // CHUNK) == (tj // CHUNK), tj <= ti).astype(BF16)
    cum = _dot_f32_rhs(tri, lw)
    tot = jnp.concatenate(
        [jnp.broadcast_to(cum[(ci + 1) * CHUNK - 1:(ci + 1) * CHUNK, :], (CHUNK, c))
         for ci in range(n_chunks)], axis=0)

    e_neg = jnp.exp(-cum)
    e_rem = jnp.exp(tot - cum)
    kka = kk * a
    lane = lax.broadcasted_iota(jnp.int32, (1, PAIR), 1)
    first_head = lane < HEAD_DIM

    def put(scr, base, val):
        for q in range(n_pairs):
            vq = val[:, q * PAIR:(q + 1) * PAIR]
            h0 = jnp.where(first_head, vq, 0.0).astype(scr.dtype)
            h1 = jnp.where(first_head, 0.0, vq).astype(scr.dtype)
            for ci in range(n_chunks):
                rows = slice(ci * CHUNK, (ci + 1) * CHUNK)
                scr[q, ci, base:base + CHUNK, :] = h0[rows]
                scr[q, ci, base + CHUNK:base + stacked, :] = h1[rows]

    put(ar_scr, 0, -kk * jnp.exp(cum - lw))
    put(ar_scr, stacked, r * jnp.exp(cum))
    put(bk_scr, 0, kka * e_neg)
    put(bk_scr, stacked, k2 * e_neg)
    put(bkh_scr, 0, kka * e_rem)
    put(bkh_scr, stacked, k2 * e_rem)
    put(v_scr, 0, v)
    e_tot = jnp.exp(tot)
    for q in range(n_pairs):
        et_scr[q] = e_tot[:, q * PAIR:(q + 1) * PAIR]

    si = lax.broadcasted_iota(jnp.int32, (2 * stacked, 2 * stacked), 0)
    sj = lax.broadcasted_iota(jnp.int32, (2 * stacked, 2 * stacked), 1)
    keep = jnp.where(si < stacked, si, si - stacked + 1) > (sj % stacked)
    eye = (li == lj).astype(F32)

    chunks = range(n_chunks)

    def side(ta, tb):
        return jnp.concatenate([ta, tb], axis=1)

    def blockdiag(ta, tb):
        za = jnp.zeros((ta.shape[0], tb.shape[1]), ta.dtype)
        zb = jnp.zeros((tb.shape[0], ta.shape[1]), ta.dtype)
        return jnp.concatenate([side(ta, za), side(zb, tb)], axis=0)

    def halves(t):
        return t[:, :PAIR], t[:, PAIR:]

    def bf(t):
        return t.astype(BF16)

    hr = lax.broadcasted_iota(jnp.int32, (CHUNK, 4 * CHUNK), 0)
    hl = lax.broadcasted_iota(jnp.int32, (CHUNK, 4 * CHUNK), 1)
    eye4 = (hr == hl % CHUNK).astype(F32)
    zero_bf = jnp.zeros((CHUNK, stacked), BF16)

    def head_blocks(t):
        z = jnp.zeros_like(t)
        return jnp.concatenate([jnp.where(hl // CHUNK == j, t, z) for j in range(4)], axis=0)

    def prepare(step):
        couples = [(step * PREP_PAIRS + 2 * j, ci) for j in range(PREP_PAIRS // 2) for ci in chunks]
        pw = []
        for qa, ci in couples:
            n, tail, vm = [], [], []
            for q in (qa, qa + 1):
                sc = jnp.where(keep, _dot_nt(ar_scr[q, ci], bk_scr[q, ci]), 0.0)
                scb = bf(sc)
                arb_scr[q, ci] = scb[stacked:, :stacked]
                n.append(sc[:stacked, :stacked])
                tail.append(scb[:, stacked:])
                vm.append(v_scr[q, ci])
            av = _dot(side(*tail), blockdiag(bf(vm[0]), bf(vm[1])))
            vk = _dot(side(bf(vm[0].T), bf(vm[1].T)),
                      blockdiag(bkh_scr[qa, ci, stacked:, :], bkh_scr[qa + 1, ci, stacked:, :]))
            for q, t_av, t_vk in zip((qa, qa + 1), halves(av), halves(vk)):
                av_scr[q, ci] = t_av
                vk_scr[q, ci] = t_vk
            pw.append(bf(side(n[0][:CHUNK] + n[0][CHUNK:], n[1][:CHUNK] + n[1][CHUNK:])))
        acc = [eye4 + p_ for p_ in pw]
        pw = [bf(_dot(p_, head_blocks(p_))) for p_ in pw]
        for it in range(1, 6):
            rhs = [head_blocks(p_) for p_ in pw]
            if it < 5:
                both = [_dot(jnp.concatenate([p_, bf(a_)], axis=0), r_)
                        for p_, a_, r_ in zip(pw, acc, rhs)]
                pw = [bf(t[:CHUNK]) for t in both]
                acc = [a_ + t[CHUNK:] for a_, t in zip(acc, both)]
            else:
                acc = [a_ + _dot(bf(a_), r_) for a_, r_ in zip(acc, rhs)]
        for (qa, ci), a_ in zip(couples, acc):
            for q, t in zip((qa, qa + 1), halves(bf(a_))):
                pm_scr[q, ci] = jnp.concatenate(
                    [jnp.where(first_head, t, zero_bf), jnp.where(first_head, zero_bf, t)], axis=0)

    for step in range(n_pairs // PREP_PAIRS):
        prepare(step)

    firsts = range(0, n_pairs, 2)
    s = [side(state_scr[q], state_scr[q + 1]) for q in firsts]
    for ci in chunks:
        rows = slice(ci * CHUNK, (ci + 1) * CHUNK)

        def both(scr, part=slice(None)):
            return [side(scr[q, ci, part, :], scr[q + 1, ci, part, :]) for q in firsts]

        top, bottom = slice(0, stacked), slice(stacked, 2 * stacked)
        sr = [_dot_nt(ar_, blockdiag(*halves(bf(s_)))) for ar_, s_ in zip(both(ar_scr), s)]
        u = [_dot(pm_, blockdiag(*halves(bf(sr_[:stacked] + av_))))
             for pm_, sr_, av_ in zip(both(pm_scr), sr, both(av_scr, top))]
        ym = [sr_[stacked:] + av_ + _dot(arb_, blockdiag(*halves(bf(u_))))
              for sr_, av_, arb_, u_ in zip(sr, both(av_scr, bottom), both(arb_scr), u)]
        for q, ym_ in zip(firsts, ym):
            ya, yb = halves(ym_[:CHUNK] + ym_[CHUNK:])
            y_scr[q, rows, :] = ya
            y_scr[q + 1, rows, :] = yb
        decay = [side(et_scr[q, ci * CHUNK:ci * CHUNK + 1, :], et_scr[q + 1, ci * CHUNK:ci * CHUNK + 1, :])
                 for q in firsts]
        s = [s_ * d_ + _dot(side(*(bf(t.T) for t in halves(u_))), blockdiag(*halves(bh_))) + vk_
             for s_, d_, u_, bh_, vk_ in zip(s, decay, u, both(bkh_scr, top), both(vk_scr))]
    for q, s_ in zip(firsts, s):
        state_scr[q], state_scr[q + 1] = halves(s_)

    y = jnp.concatenate([y_scr[q] for q in range(n_pairs)], axis=1)
    mean = head_sum(y) * (1.0 / HEAD_DIM)
    d = y - mean
    var = head_sum(d * d) * (1.0 / HEAD_DIM)
    yn = d * lax.rsqrt(var + GN_EPS) * lnw_ref[...] + lnb_ref[...]
    bonus = head_sum(r * k2 * rk_ref[...]) * v
    o_ref[...] = (yn + bonus) * g * beta_ref[...]


def _rwkv_call(p_r, vecs, wd, wa, wg, batch, seq, width, tb):
    assert CHUNK == HEAD_DIM, "chunk operands stack two heads' rows against their 2 * HEAD_DIM lanes"
    assert (width // PAIR) % PREP_PAIRS == 0 and PREP_PAIRS % 2 == 0
    pw = p_r.shape[1]
    n_pairs = width // PAIR
    n_chunks = tb // CHUNK
    tiles = seq // tb
    vec_specs = [pl.BlockSpec((1, v.shape[1]), lambda b, t: (0, 0)) for v in vecs]
    mat_specs = [pl.BlockSpec(m.shape, lambda b, t: (0, 0)) for m in (wd, wa, wg)]
    stacked2 = pltpu.VMEM((n_pairs, n_chunks, 4 * CHUNK, PAIR), BF16)
    return pl.pallas_call(
        functools.partial(_rwkv_kernel, width=width, tb=tb),
        grid=(batch, tiles),
        in_specs=[pl.BlockSpec((tb, pw), lambda b, t: (b * tiles + t, 0))] + vec_specs + mat_specs,
        out_specs=pl.BlockSpec((tb, width), lambda b, t: (b * tiles + t, 0)),
        out_shape=jax.ShapeDtypeStruct((batch * seq, width), F32),
        scratch_shapes=[pltpu.VMEM((1, pw), F32),
                        pltpu.VMEM((n_pairs, PAIR, PAIR), F32),
                        stacked2, stacked2, stacked2,
                        pltpu.VMEM((n_pairs, n_chunks, 2 * CHUNK, PAIR), F32),
                        pltpu.VMEM((n_pairs, tb, PAIR), F32),
                        pltpu.VMEM((n_pairs, tb, PAIR), F32),
                        pltpu.VMEM((n_pairs, n_chunks, 2 * CHUNK, PAIR), BF16),
                        pltpu.VMEM((n_pairs, n_chunks, 2 * CHUNK, PAIR), BF16),
                        pltpu.VMEM((n_pairs, n_chunks, 4 * CHUNK, PAIR), F32),
                        pltpu.VMEM((n_pairs, n_chunks, 2 * CHUNK, PAIR), F32)],
        compiler_params=pltpu.CompilerParams(
            dimension_semantics=("parallel", "arbitrary"), vmem_limit_bytes=VMEM_LIMIT),
    )(p_r, *vecs, wd, wa, wg)


def _attn_kernel(q_ref, kp_ref, kc_ref, vp_ref, vc_ref, bias_ref, beta_ref, o_ref, kwin, vwin,
                 *, width, qb):
    i = pl.program_id(1)
    n_pairs = width // PAIR
    kwin[0:qb, :] = kp_ref[...]
    kwin[qb:2 * qb, :] = kc_ref[...]
    vwin[0:qb, :] = vp_ref[...]
    vwin[qb:2 * qb, :] = vc_ref[...]
    lane = lax.broadcasted_iota(jnp.int32, (1, PAIR), 1)
    first_head = lane < HEAD_DIM
    rows = ATTN_BLOCK_CHUNKS * CHUNK
    win = BAND + rows - CHUNK
    col = lax.broadcasted_iota(jnp.int32, (1, win), 1)
    scale = HEAD_DIM ** -0.5
    left = LEFT_CHUNKS * CHUNK

    def attend(first_block):
        for q in range(n_pairs):
            lanes = slice(q * PAIR, (q + 1) * PAIR)
            for blk in range(qb // rows):
                r0 = blk * rows
                qc = q_ref[r0:r0 + rows, lanes]
                zero = jnp.zeros_like(qc)
                qm = jnp.concatenate(
                    [jnp.where(first_head, qc, zero), jnp.where(first_head, zero, qc)], axis=0)
                s = _dot_nt(qm, kwin[r0:r0 + win, lanes]) * scale + bias_ref[q]
                if first_block:
                    s = jnp.where(col + r0 >= left, s, NEG_INF)
                e = jnp.exp(s - jnp.max(s, axis=-1, keepdims=True))
                denom = jnp.sum(e, axis=-1, keepdims=True)
                o = _dot(e.astype(BF16), vwin[r0:r0 + win, lanes]) / denom
                oc = jnp.where(first_head, o[:rows], o[rows:])
                o_ref[r0:r0 + rows, lanes] = oc * beta_ref[:, lanes]

    pl.when(i == 0)(lambda: attend(True))
    pl.when(i > 0)(lambda: attend(False))


def _attn_call(p_a, bias, beta, batch, seq, width, qb):
    assert qb == LEFT_CHUNKS * CHUNK, "key window = previous block + current block"
    tiles = seq // qb
    n_pairs = width // PAIR

    def cur(col):
        return pl.BlockSpec((qb, width), lambda b, t: (b * tiles + t, col))

    def prev(col):
        return pl.BlockSpec((qb, width), lambda b, t: (b * tiles + jnp.maximum(t - 1, 0), col))

    return pl.pallas_call(
        functools.partial(_attn_kernel, width=width, qb=qb),
        grid=(batch, tiles),
        in_specs=[cur(0), prev(1), cur(1), prev(2), cur(2),
                  pl.BlockSpec(bias.shape, lambda b, t: (0, 0, 0)),
                  pl.BlockSpec((1, width), lambda b, t: (0, 0))],
        out_specs=pl.BlockSpec((qb, width), lambda b, t: (b * tiles + t, 0)),
        out_shape=jax.ShapeDtypeStruct((batch * seq, width), F32),
        scratch_shapes=[pltpu.VMEM((2 * qb, width), BF16), pltpu.VMEM((2 * qb, width), BF16)],
        compiler_params=pltpu.CompilerParams(
            dimension_semantics=("parallel", "arbitrary"), vmem_limit_bytes=VMEM_LIMIT),
    )(p_a, p_a, p_a, p_a, p_a, bias, beta)


def _out_router_kernel(yr_ref, ya_ref, wo_ref, x_ref, mod_ref, g2_ref, wr_ref, br_ref,
                       x1_ref, h2_ref, route_ref, *, r_width):
    y = (_dot(yr_ref[...].astype(BF16), wo_ref[0:r_width, :])
         + _dot(ya_ref[...].astype(BF16), wo_ref[r_width:, :]))
    m = mod_ref[0]
    x1 = x_ref[...] + m[GATE1:GATE1 + 1] * y
    x1_ref[...] = x1
    h2 = _rms(x1) * g2_ref[...] * (1.0 + m[SCALE2:SCALE2 + 1]) + m[SHIFT2:SHIFT2 + 1]
    h2_ref[...] = h2

    wr = wr_ref[...]
    wr_hi = wr.astype(BF16)
    wr_lo = (wr - wr_hi.astype(F32)).astype(BF16)
    h2_hi = h2.astype(BF16)
    h2_lo = (h2 - h2_hi.astype(F32)).astype(BF16)
    hi_both = _dot(h2_hi, jnp.concatenate([wr_hi, wr_lo], axis=1))
    logits = _dot(h2_lo, wr_hi) + hi_both[:, LANES:] + hi_both[:, :LANES] + br_ref[...]
    li = lax.broadcasted_iota(jnp.int32, logits.shape, 1).astype(F32)
    none = jnp.float32(-jnp.inf)
    far = jnp.float32(LANES)
    is_group = li < N_GROUPS
    gl = jnp.where(is_group, logits, none)
    gmax = jnp.max(gl, axis=-1, keepdims=True)
    gidx = jnp.min(jnp.where(gl == gmax, li, far), axis=-1, keepdims=True)
    g_w = 1.0 / jnp.sum(jnp.where(is_group, jnp.exp(logits - gmax), 0.0), axis=-1, keepdims=True)
    lo = N_GROUPS + EXPERTS_PER_GROUP * gidx
    el = jnp.where(jnp.logical_and(li >= lo, li < lo + EXPERTS_PER_GROUP), logits, none)
    m1 = jnp.max(el, axis=-1, keepdims=True)
    i1 = jnp.min(jnp.where(el == m1, li, far), axis=-1, keepdims=True)
    el2 = jnp.where(li == i1, none, el)
    m2 = jnp.max(el2, axis=-1, keepdims=True)
    i2 = jnp.min(jnp.where(el2 == m2, li, far), axis=-1, keepdims=True)
    e2 = jnp.exp(m2 - m1)
    w1 = g_w / (1.0 + e2)
    w2 = g_w * e2 / (1.0 + e2)
    route = jnp.where(li == 0.0, i1 - N_GROUPS, 0.0) + jnp.where(li == 1.0, i2 - N_GROUPS, 0.0)
    route_ref[...] = route + jnp.where(li == 2.0, w1, 0.0) + jnp.where(li == 3.0, w2, 0.0)


def _out_router_call(y_r, y_a, wo, x2d, mod, g2, wr, br, seq, tm):
    rows, d = x2d.shape
    r_width = y_r.shape[1]
    a_width = y_a.shape[1]
    tiles_per_seq = seq // tm
    return pl.pallas_call(
        functools.partial(_out_router_kernel, r_width=r_width),
        grid=(rows // tm,),
        in_specs=[pl.BlockSpec((tm, r_width), lambda i: (i, 0)),
                  pl.BlockSpec((tm, a_width), lambda i: (i, 0)),
                  pl.BlockSpec(wo.shape, lambda i: (0, 0)),
                  pl.BlockSpec((tm, d), lambda i: (i, 0)),
                  pl.BlockSpec((1, 6, d), lambda i: (i // tiles_per_seq, 0, 0)),
                  pl.BlockSpec((1, d), lambda i: (0, 0)),
                  pl.BlockSpec((d, LANES), lambda i: (0, 0)),
                  pl.BlockSpec((1, LANES), lambda i: (0, 0))],
        out_specs=[pl.BlockSpec((tm, d), lambda i: (i, 0)),
                   pl.BlockSpec((tm, d), lambda i: (i, 0)),
                   pl.BlockSpec((tm, LANES), lambda i: (i, 0))],
        out_shape=[jax.ShapeDtypeStruct((rows, d), F32),
                   jax.ShapeDtypeStruct((rows, d), F32),
                   jax.ShapeDtypeStruct((rows, LANES), F32)],
        compiler_params=pltpu.CompilerParams(
            dimension_semantics=("parallel",), vmem_limit_bytes=VMEM_LIMIT),
    )(y_r, y_a, wo, x2d, mod, g2, wr, br)


def _moe_kernel(texp_ref, first_ref, next_ref, rows_ref, meta_ref, src_ref,
                h2_hbm, wg_hbm, wu_hbm, wd_hbm, o_ref,
                xbuf, xsem, wg_land, wu_land, wd_land, wsem, wg_bf, wu_bf, wd_bf, *, tm):
    t = pl.program_id(0)
    n_used = meta_ref[0]
    slot = lax.rem(t, MOE_GATHER_SLOTS)

    def weight_copies(e):
        cps = []
        for m, (hbm, land) in enumerate(((wg_hbm, wg_land), (wu_hbm, wu_land), (wd_hbm, wd_land))):
            rows = hbm.shape[1] // W_CHUNKS
            for ck in range(W_CHUNKS):
                part = pl.ds(ck * rows, rows)
                cps.append(pltpu.make_async_copy(hbm.at[e, part], land.at[part], wsem.at[m, ck]))
        return cps

    def row_groups(tile):
        return (rows_ref[tile] + ROW_GROUP - 1) // ROW_GROUP

    def gather_rows(tile, unrolled=False):
        buf_slot = lax.rem(tile, MOE_GATHER_SLOTS)
        n_groups = row_groups(tile)

        def body(g, carry):
            for k in range(ROW_GROUP):
                i = g * ROW_GROUP + k
                tok = src_ref[tile * tm + i]
                pltpu.make_async_copy(h2_hbm.at[pl.ds(tok, 1)], xbuf.at[buf_slot, pl.ds(i, 1)],
                                      xsem.at[buf_slot]).start()
            return carry

        if unrolled:
            for g in range(tm // ROW_GROUP):
                @pl.when(g < n_groups)
                def _(g=g):
                    body(g, 0)
        else:
            lax.fori_loop(0, n_groups, body, 0)

    @pl.when(t == 0)
    def _():
        xbuf[...] = jnp.zeros_like(xbuf)
        for cp in weight_copies(texp_ref[0]):
            cp.start()
        for ahead in range(MOE_GATHER_SLOTS - 1):
            @pl.when(ahead < n_used)
            def _():
                gather_rows(ahead)

    is_first = jnp.logical_and(t < n_used, first_ref[t] == 1)

    @pl.when(is_first)
    def _():
        for cp in weight_copies(texp_ref[t]):
            cp.wait()
        wg_bf[...] = wg_land[...].astype(BF16)
        wu_bf[...] = wu_land[...].astype(BF16)
        wd_bf[...] = wd_land[...].astype(BF16)

    @pl.when(t + MOE_GATHER_SLOTS - 1 < n_used)
    def _():
        gather_rows(t + MOE_GATHER_SLOTS - 1, unrolled=True)

    @pl.when(jnp.logical_and(is_first, next_ref[t] >= 0))
    def _():
        for cp in weight_copies(next_ref[t]):
            cp.start()

    @pl.when(t < n_used)
    def _():
        def wait_group(g, carry):
            r0 = pl.multiple_of(g * ROW_GROUP, ROW_GROUP)
            pltpu.make_async_copy(h2_hbm.at[pl.ds(0, ROW_GROUP)],
                                  xbuf.at[slot, pl.ds(r0, ROW_GROUP)], xsem.at[slot]).wait()
            return carry

        lax.fori_loop(0, row_groups(t), wait_group, 0)
        x = xbuf[slot].astype(BF16)
        gate = _dot(x, wg_bf[...])
        up = _dot(x, wu_bf[...])
        h = gate * _sigmoid(gate) * up
        o_ref[...] = _dot(h.astype(BF16), wd_bf[...])

    @pl.when(t >= n_used)
    def _():
        o_ref[...] = jnp.zeros_like(o_ref)


def _moe_call(plan, src, h2, wg, wu, wd, tm):
    n_e, d, f = wg.shape
    n_tiles = plan[0].shape[0]
    grid_spec = pltpu.PrefetchScalarGridSpec(
        num_scalar_prefetch=len(plan) + 1,
        grid=(n_tiles,),
        in_specs=[pl.BlockSpec(memory_space=pl.ANY)] * 4,
        out_specs=pl.BlockSpec((tm, d), lambda t, *_: (t, 0)),
        scratch_shapes=[pltpu.VMEM((MOE_GATHER_SLOTS, tm, d), F32),
                        pltpu.SemaphoreType.DMA((MOE_GATHER_SLOTS,)),
                        pltpu.VMEM((d, f), F32), pltpu.VMEM((d, f), F32), pltpu.VMEM((f, d), F32),
                        pltpu.SemaphoreType.DMA((3, W_CHUNKS)),
                        pltpu.VMEM((d, f), BF16), pltpu.VMEM((d, f), BF16), pltpu.VMEM((f, d), BF16)])
    return pl.pallas_call(
        functools.partial(_moe_kernel, tm=tm),
        grid_spec=grid_spec,
        out_shape=jax.ShapeDtypeStruct((n_tiles * tm, d), F32),
        compiler_params=pltpu.CompilerParams(
            dimension_semantics=("arbitrary",), vmem_limit_bytes=VMEM_LIMIT),
    )(*plan, src, h2, wg, wu, wd)


def _final_kernel(pos_ref, ys_hbm, route_ref, x1_ref, mod_ref, gf_ref, o_ref, ybuf, sem, *, tm):
    i = pl.program_id(0)
    n = pl.num_programs(0)
    slot = i % 2

    def gather_rows(tile, buf_slot, unrolled):
        base = tile * tm * 2

        def body(r, carry):
            for s in range(2):
                pltpu.make_async_copy(ys_hbm.at[pl.ds(pos_ref[base + 2 * r + s], 1)],
                                      ybuf.at[buf_slot, s, pl.ds(r, 1)], sem.at[buf_slot]).start()
            return carry

        if unrolled:
            for r in range(tm):
                body(r, 0)
        else:
            lax.fori_loop(0, tm, body, 0, unroll=4)

    @pl.when(i == 0)
    def _():
        gather_rows(0, 0, unrolled=False)

    @pl.when(i + 1 < n)
    def _():
        gather_rows(i + 1, 1 - slot, unrolled=True)

    for s in range(2):
        pltpu.make_async_copy(ys_hbm.at[pl.ds(0, tm)], ybuf.at[slot, s], sem.at[slot]).wait()
    route = route_ref[...]
    w0 = route[:, 2:3]
    w1 = route[:, 3:4]
    f = w0 * ybuf[slot, 0] + w1 * ybuf[slot, 1]
    x2 = x1_ref[...] + mod_ref[0][GATE2:GATE2 + 1] * f
    o_ref[...] = _rms(x2) * gf_ref[...]


def _final_call(pos, ys, route, x1, mod, gf, seq, tm):
    rows, d = x1.shape
    tiles_per_seq = seq // tm
    grid_spec = pltpu.PrefetchScalarGridSpec(
        num_scalar_prefetch=1,
        grid=(rows // tm,),
        in_specs=[pl.BlockSpec(memory_space=pl.ANY),
                  pl.BlockSpec((tm, LANES), lambda i, pos: (i, 0)),
                  pl.BlockSpec((tm, d), lambda i, pos: (i, 0)),
                  pl.BlockSpec((1, 6, d), lambda i, pos: (i // tiles_per_seq, 0, 0)),
                  pl.BlockSpec((1, d), lambda i, pos: (0, 0))],
        out_specs=pl.BlockSpec((tm, d), lambda i, pos: (i, 0)),
        scratch_shapes=[pltpu.VMEM((2, 2, tm, d), F32), pltpu.SemaphoreType.DMA((2,))])
    return pl.pallas_call(
        functools.partial(_final_kernel, tm=tm),
        grid_spec=grid_spec,
        out_shape=jax.ShapeDtypeStruct((rows, d), F32),
        compiler_params=pltpu.CompilerParams(
            dimension_semantics=("arbitrary",), vmem_limit_bytes=VMEM_LIMIT),
    )(pos, ys, route, x1, mod, gf)


def _route_plan(route, tm):
    n = route.shape[0]
    flat_e = route[:, 0:2].astype(jnp.int32).reshape(-1)
    n_tiles = (2 * n + N_EXPERTS * (tm - 1)) // tm
    experts = jnp.arange(N_EXPERTS, dtype=jnp.int32)
    onehot = (flat_e[:, None] == experts[None, :]).astype(jnp.int32)
    csum = jnp.cumsum(onehot, axis=0)
    rank = jnp.sum(csum * onehot, axis=1) - 1
    counts = csum[-1]
    tiles_per_e = (counts + tm - 1) // tm
    tile_end = jnp.cumsum(tiles_per_e)
    tile_start = tile_end - tiles_per_e
    n_used = tile_end[-1]
    pos = jnp.sum(onehot * tile_start[None, :], axis=1) * tm + rank
    token = jnp.arange(2 * n, dtype=jnp.int32) // 2
    src = jnp.zeros((n_tiles * tm,), jnp.int32).at[pos].set(token)
    tiles = jnp.arange(n_tiles, dtype=jnp.int32)
    texp = jnp.sum((tiles[:, None] >= tile_end[None, :]).astype(jnp.int32), axis=1)
    texp = jnp.where(tiles < n_used, texp, texp[n_used - 1])
    first = jnp.concatenate([jnp.ones((1,), jnp.int32), (texp[1:] != texp[:-1]).astype(jnp.int32)])
    later_used = jnp.logical_and(experts[None, :] > experts[:, None], (tiles_per_e > 0)[None, :])
    next_used = jnp.min(jnp.where(later_used, experts[None, :], N_EXPERTS), axis=1)
    next_used = jnp.where(next_used == N_EXPERTS, -1, next_used)
    tile_rows = jnp.clip(counts[texp] - (tiles - tile_start[texp]) * tm, 0, tm)
    return (texp, first, next_used[texp], tile_rows, n_used.reshape(1)), src, pos


def _round_up(n, m):
    return (n + m - 1) // m * m


def _tile_sizes(seq):
    return dict(ada_cols=2048, rows=min(512, seq), proj_rows=min(1024, seq),
                rwkv_rows=min(4 * CHUNK, seq), attn_rows=min(LEFT_CHUNKS * CHUNK, seq),
                expert_rows=256, final_rows=min(256, seq))


def kernel(x, c, w_ada, b_ada, norm1_g, w_in, mu_shift, w0, w_decay_up, a0, w_aaa_up, w_gate_up,
           k_k, k_a, r_k, ln_x_w, ln_x_b, rel_bias, beta_rwkv, beta_attn, w_out, norm2_g, w_group,
           b_group, w_expert, b_expert, w_gate, w_up, w_down, norm_f_g):
    assert w_ada.shape[0] == 1, "single trunk layer"
    batch, seq, d = x.shape
    r_width = w0.shape[1]
    a_width = beta_attn.shape[1]
    shift_width = mu_shift.shape[1]
    assert shift_width == 3 * r_width + DECAY_LORA + AAA_LORA + GATE_LORA
    x2d = x.reshape(batch * seq, d)
    row = lambda t: t.reshape(1, -1)

    tiles = _tile_sizes(seq)
    tm = tiles["rows"]

    c8 = jnp.pad(c, ((0, SUBLANES - batch), (0, 0)))
    mod = _ada_call(c8, w_ada[0], row(b_ada[0]), tn=tiles["ada_cols"])[:batch].reshape(batch, 6, d)

    pw = _round_up(shift_width, 3 * LANES)
    h1 = _hmod_call(x2d, row(norm1_g[0]), mod, tm, seq)
    w_in_t = jnp.swapaxes(w_in[0], 0, 1)
    p_r = _matmul_call(h1, w_in_t, 0, pw, pw // 3, tiles["proj_rows"], F32)
    p_a = _matmul_call(h1, w_in_t, shift_width, 3 * a_width, a_width, tiles["proj_rows"], BF16)

    mu = jnp.pad(mu_shift[0], (0, pw - shift_width))
    vecs = [row(mu), row(w0[0]), row(a0[0]), row(k_k[0]), row(k_a[0]), row(r_k[0]), row(ln_x_w[0]),
            row(ln_x_b[0]), row(beta_rwkv[0])]
    y_r = _rwkv_call(p_r, vecs, w_decay_up[0], w_aaa_up[0], w_gate_up[0], batch, seq, r_width,
                     tb=tiles["rwkv_rows"])

    left = LEFT_CHUNKS * CHUNK
    dist = left + CHUNK - 1 - jnp.arange(BAND + CHUNK - 1)
    base = rel_bias[0][:, jnp.clip(dist, -REL_CLIP, REL_CLIP) + REL_CLIP].astype(F32)
    bias = jnp.stack([base[:, CHUNK - 1 - qi:CHUNK - 1 - qi + BAND] for qi in range(CHUNK)], axis=1)
    nb = ATTN_BLOCK_CHUNKS
    bias = jnp.concatenate(
        [jnp.pad(bias, ((0, 0), (0, 0), (cb * CHUNK, (nb - 1 - cb) * CHUNK)), constant_values=NEG_INF)
         for cb in range(nb)], axis=1)
    bias = bias.reshape(a_width // PAIR, 2 * nb * CHUNK, BAND + (nb - 1) * CHUNK)
    y_a = _attn_call(p_a, bias, row(beta_attn[0]), batch, seq, a_width, qb=tiles["attn_rows"])

    n_route = N_GROUPS + N_EXPERTS
    wr = jnp.concatenate([w_group[0], w_expert[0].transpose(1, 0, 2).reshape(d, N_EXPERTS)], axis=1)
    wr = jnp.pad(wr, ((0, 0), (0, LANES - n_route)))
    br = jnp.pad(jnp.concatenate([b_group[0], b_expert[0].reshape(-1)]), (0, LANES - n_route))
    x1, h2, route = _out_router_call(y_r, y_a, w_out[0].astype(BF16), x2d, mod, row(norm2_g[0]),
                                     wr, row(br), seq, tm)

    f = w_gate.shape[-1]
    tm_e = tiles["expert_rows"]
    plan, src, pos = _route_plan(route, tm_e)
    ys = _moe_call(plan, src, h2, w_gate[0].reshape(N_EXPERTS, d, f),
                   w_up[0].reshape(N_EXPERTS, d, f), w_down[0].reshape(N_EXPERTS, f, d), tm_e)
    out = _final_call(pos, ys, route, x1, mod, row(norm_f_g), seq, tiles["final_rows"])
    return out.reshape(batch, seq, d)
```

```python
import functools

import jax
import jax.numpy as jnp
from jax import lax
from jax.experimental import pallas as pl
from jax.experimental.pallas import tpu as pltpu

F32 = jnp.float32
BF16 = jnp.bfloat16

LANES = 128
SUBLANES = 8
HEAD_DIM = 64
PAIR = 2 * HEAD_DIM
CHUNK = 64
LEFT_CHUNKS = 8
BAND = (LEFT_CHUNKS + 1) * CHUNK
ATTN_BLOCK_CHUNKS = 4
PREP_PAIRS = 8
MOE_GATHER_SLOTS = 3
W_CHUNKS = 2
ROW_GROUP = 8
REL_CLIP = 128
DECAY_LORA = 64
AAA_LORA = 64
GATE_LORA = 160
N_GROUPS = 4
EXPERTS_PER_GROUP = 8
N_EXPERTS = N_GROUPS * EXPERTS_PER_GROUP
RMS_EPS = 1e-6
GN_EPS = 64e-5
L2_EPS = 1e-12
NEG_INF = -1e30
VMEM_LIMIT = 56 * 1024 * 1024

SHIFT1, SCALE1, GATE1, SHIFT2, SCALE2, GATE2 = range(6)


def _dot(a, b):
    return jnp.dot(a, b, preferred_element_type=F32)


def _dot_nt(a, b):
    return lax.dot_general(a, b, (((1,), (1,)), ((), ())), preferred_element_type=F32)


def _sigmoid(x):
    return 1.0 / (1.0 + jnp.exp(-x))


def _rms(x):
    return x * lax.rsqrt(jnp.mean(x * x, axis=-1, keepdims=True) + RMS_EPS)


def _ada_kernel(c_ref, w_ref, b_ref, o_ref):
    c = c_ref[...]
    o_ref[...] = _dot_x3(c * _sigmoid(c), w_ref[...]) + b_ref[...]


def _ada_call(c8, w, b, tn):
    rows, d = c8.shape
    n = w.shape[1]
    return pl.pallas_call(
        _ada_kernel,
        grid=(n // tn,),
        in_specs=[pl.BlockSpec((rows, d), lambda j: (0, 0)),
                  pl.BlockSpec((d, tn), lambda j: (0, j)),
                  pl.BlockSpec((1, tn), lambda j: (0, j))],
        out_specs=pl.BlockSpec((rows, tn), lambda j: (0, j)),
        out_shape=jax.ShapeDtypeStruct((rows, n), F32),
        compiler_params=pltpu.CompilerParams(vmem_limit_bytes=VMEM_LIMIT),
    )(c8, w, b)


def _hmod_kernel(x_ref, g_ref, mod_ref, o_ref):
    m = mod_ref[0]
    h = _rms(x_ref[...]) * g_ref[...] * (1.0 + m[SCALE1:SCALE1 + 1]) + m[SHIFT1:SHIFT1 + 1]
    o_ref[...] = h.astype(BF16)


def _hmod_call(x2d, g, mod, tm, seq):
    rows, d = x2d.shape
    tiles_per_seq = seq // tm
    return pl.pallas_call(
        _hmod_kernel,
        grid=(rows // tm,),
        in_specs=[pl.BlockSpec((tm, d), lambda i: (i, 0)),
                  pl.BlockSpec((1, d), lambda i: (0, 0)),
                  pl.BlockSpec((1, 6, d), lambda i: (i // tiles_per_seq, 0, 0))],
        out_specs=pl.BlockSpec((tm, d), lambda i: (i, 0)),
        out_shape=jax.ShapeDtypeStruct((rows, d), BF16),
        compiler_params=pltpu.CompilerParams(
            dimension_semantics=("parallel",), vmem_limit_bytes=VMEM_LIMIT),
    )(x2d, g, mod)


def _matmul_kernel(a_ref, wt_ref, o_ref, w_bf):
    @pl.when(pl.program_id(1) == 0)
    def _():
        w_bf[...] = wt_ref[...].astype(BF16)

    o_ref[...] = _dot_nt(a_ref[...], w_bf[...]).astype(o_ref.dtype)


def _matmul_call(a, wt, row0, n_out, tn, tm, out_dtype):
    rows, d = a.shape
    return pl.pallas_call(
        _matmul_kernel,
        grid=(n_out // tn, rows // tm),
        in_specs=[pl.BlockSpec((tm, d), lambda j, i: (i, 0)),
                  pl.BlockSpec((pl.Element(tn), pl.Element(d)),
                               lambda j, i: (pl.multiple_of(row0 + j * tn, SUBLANES), 0))],
        out_specs=pl.BlockSpec((tm, tn), lambda j, i: (i, j)),
        out_shape=jax.ShapeDtypeStruct((rows, n_out), out_dtype),
        scratch_shapes=[pltpu.VMEM((tn, d), BF16)],
        compiler_params=pltpu.CompilerParams(
            dimension_semantics=("parallel", "arbitrary"), vmem_limit_bytes=VMEM_LIMIT),
    )(a, wt)


def _split3(t):
    hi = t.astype(BF16)
    rest = t - hi.astype(F32)
    mid = rest.astype(BF16)
    lo = (rest - mid.astype(F32)).astype(BF16)
    return hi, mid, lo


def _dot_hilo_lhs(t, m):
    hi = t.astype(BF16)
    lo = (t - hi.astype(F32)).astype(BF16)
    return _dot(lo, m) + _dot(hi, m)


def _dot_f32_rhs(m, t):
    hi, mid, lo = _split3(t)
    return _dot(m, lo) + _dot(m, mid) + _dot(m, hi)


def _dot_x3(a, b):
    ah = a.astype(BF16)
    al = (a - ah.astype(F32)).astype(BF16)
    bh = b.astype(BF16)
    bl = (b - bh.astype(F32)).astype(BF16)
    return _dot(al, bh) + _dot(ah, bl) + _dot(ah, bh)


def _rwkv_kernel(p_ref, mu_ref, w0_ref, a0_ref, kk_ref, ka_ref, rk_ref, lnw_ref, lnb_ref, beta_ref,
                 wd_ref, wa_ref, wg_ref, o_ref,
                 carry_scr, state_scr, ar_scr, bk_scr, bkh_scr, v_scr, et_scr, y_scr,
                 pm_scr, arb_scr, av_scr, vk_scr, *, width, tb):
    n_pairs = width // PAIR
    n_chunks = tb // CHUNK
    stacked = 2 * CHUNK

    @pl.when(pl.program_id(1) == 0)
    def _():
        carry_scr[...] = jnp.zeros_like(carry_scr)
        state_scr[...] = jnp.zeros_like(state_scr)

    p = p_ref[...]
    row = lax.broadcasted_iota(jnp.int32, (tb, 1), 0)
    prev = jnp.where(row == 0, carry_scr[...], pltpu.roll(p, 1, axis=0))
    carry_scr[...] = p[tb - 1:tb, :]
    ps = p + (prev - p) * mu_ref[...]

    c = width
    r = ps[:, :c]
    k = ps[:, c:2 * c]
    v = ps[:, 2 * c:3 * c]
    o = 3 * c
    xw = ps[:, o:o + DECAY_LORA]
    xa = ps[:, o + DECAY_LORA:o + DECAY_LORA + AAA_LORA]
    xg = ps[:, o + DECAY_LORA + AAA_LORA:o + DECAY_LORA + AAA_LORA + GATE_LORA]

    z = w0_ref[...] + _dot_x3(jnp.tanh(xw), wd_ref[...])
    softplus_neg_z = jnp.maximum(-z, 0.0) + jnp.log(1.0 + jnp.exp(-jnp.abs(z)))
    lw = -jnp.exp(-softplus_neg_z - 0.5)
    a = _sigmoid(a0_ref[...] + _dot(xa.astype(BF16), wa_ref[...].astype(BF16)))
    g = _dot(_sigmoid(xg).astype(BF16), wg_ref[...].astype(BF16))

    li = lax.broadcasted_iota(jnp.int32, (PAIR, PAIR), 0)
    lj = lax.broadcasted_iota(jnp.int32, (PAIR, PAIR), 1)
    head_ones = ((li // HEAD_DIM) == (lj // HEAD_DIM)).astype(BF16)

    def head_sum(t):
        return jnp.concatenate(
            [_dot_hilo_lhs(t[:, q * PAIR:(q + 1) * PAIR], head_ones) for q in range(n_pairs)], axis=1)

    kk = k * kk_ref[...]
    kk = kk / jnp.maximum(jnp.sqrt(head_sum(kk * kk)), L2_EPS)
    k2 = k * (1.0 + (a - 1.0) * ka_ref[...])

    ti = lax.broadcasted_iota(jnp.int32, (tb, tb), 0)
    tj = lax.broadcasted_iota(jnp.int32, (tb, tb), 1)
    tri = jnp.logical_and((ti // CHUNK) == (tj // CHUNK), tj <= ti).astype(BF16)
    cum = _dot_f32_rhs(tri, lw)
    tot = jnp.concatenate(
        [jnp.broadcast_to(cum[(ci + 1) * CHUNK - 1:(ci + 1) * CHUNK, :], (CHUNK, c))
         for ci in range(n_chunks)], axis=0)

    e_neg = jnp.exp(-cum)
    e_rem = jnp.exp(tot - cum)
    kka = kk * a
    lane = lax.broadcasted_iota(jnp.int32, (1, PAIR), 1)
    first_head = lane < HEAD_DIM

    def put(scr, base, val):
        for q in range(n_pairs):
            vq = val[:, q * PAIR:(q + 1) * PAIR]
            h0 = jnp.where(first_head, vq, 0.0).astype(scr.dtype)
            h1 = jnp.where(first_head, 0.0, vq).astype(scr.dtype)
            for ci in range(n_chunks):
                rows = slice(ci * CHUNK, (ci + 1) * CHUNK)
                scr[q, ci, base:base + CHUNK, :] = h0[rows]
                scr[q, ci, base + CHUNK:base + stacked, :] = h1[rows]

    put(ar_scr, 0, -kk * jnp.exp(cum - lw))
    put(ar_scr, stacked, r * jnp.exp(cum))
    put(bk_scr, 0, kka * e_neg)
    put(bk_scr, stacked, k2 * e_neg)
    put(bkh_scr, 0, kka * e_rem)
    put(bkh_scr, stacked, k2 * e_rem)
    put(v_scr, 0, v)
    e_tot = jnp.exp(tot)
    for q in range(n_pairs):
        et_scr[q] = e_tot[:, q * PAIR:(q + 1) * PAIR]

    si = lax.broadcasted_iota(jnp.int32, (2 * stacked, 2 * stacked), 0)
    sj = lax.broadcasted_iota(jnp.int32, (2 * stacked, 2 * stacked), 1)
    keep = jnp.where(si < stacked, si, si - stacked + 1) > (sj % stacked)
    eye = (li == lj).astype(F32)

    chunks = range(n_chunks)

    def side(ta, tb):
        return jnp.concatenate([ta, tb], axis=1)

    def blockdiag(ta, tb):
        za = jnp.zeros((ta.shape[0], tb.shape[1]), ta.dtype)
        zb = jnp.zeros((tb.shape[0], ta.shape[1]), ta.dtype)
        return jnp.concatenate([side(ta, za), side(zb, tb)], axis=0)

    def halves(t):
        return t[:, :PAIR], t[:, PAIR:]

    def bf(t):
        return t.astype(BF16)

    hr = lax.broadcasted_iota(jnp.int32, (CHUNK, 4 * CHUNK), 0)
    hl = lax.broadcasted_iota(jnp.int32, (CHUNK, 4 * CHUNK), 1)
    eye4 = (hr == hl % CHUNK).astype(F32)
    zero_bf = jnp.zeros((CHUNK, stacked), BF16)

    def head_blocks(t):
        z = jnp.zeros_like(t)
        return jnp.concatenate([jnp.where(hl // CHUNK == j, t, z) for j in range(4)], axis=0)

    def prepare(step):
        couples = [(step * PREP_PAIRS + 2 * j, ci) for j in range(PREP_PAIRS // 2) for ci in chunks]
        pw = []
        for qa, ci in couples:
            n, tail, vm = [], [], []
            for q in (qa, qa + 1):
                sc = jnp.where(keep, _dot_nt(ar_scr[q, ci], bk_scr[q, ci]), 0.0)
                scb = bf(sc)
                arb_scr[q, ci] = scb[stacked:, :stacked]
                n.append(sc[:stacked, :stacked])
                tail.append(scb[:, stacked:])
                vm.append(v_scr[q, ci])
            av = _dot(side(*tail), blockdiag(bf(vm[0]), bf(vm[1])))
            vk = _dot(side(bf(vm[0].T), bf(vm[1].T)),
                      blockdiag(bkh_scr[qa, ci, stacked:, :], bkh_scr[qa + 1, ci, stacked:, :]))
            for q, t_av, t_vk in zip((qa, qa + 1), halves(av), halves(vk)):
                av_scr[q, ci] = t_av
                vk_scr[q, ci] = t_vk
            pw.append(bf(side(n[0][:CHUNK] + n[0][CHUNK:], n[1][:CHUNK] + n[1][CHUNK:])))
        acc = [eye4 + p_ for p_ in pw]
        pw = [bf(_dot(p_, head_blocks(p_))) for p_ in pw]
        for it in range(1, 6):
            rhs = [head_blocks(p_) for p_ in pw]
            if it < 5:
                both = [_dot(jnp.concatenate([p_, bf(a_)], axis=0), r_)
                        for p_, a_, r_ in zip(pw, acc, rhs)]
                pw = [bf(t[:CHUNK]) for t in both]
                acc = [a_ + t[CHUNK:] for a_, t in zip(acc, both)]
            else:
                acc = [a_ + _dot(bf(a_), r_) for a_, r_ in zip(acc, rhs)]
        for (qa, ci), a_ in zip(couples, acc):
            for q, t in zip((qa, qa + 1), halves(bf(a_))):
                pm_scr[q, ci] = jnp.concatenate(
                    [jnp.where(first_head, t, zero_bf), jnp.where(first_head, zero_bf, t)], axis=0)

    for step in range(n_pairs // PREP_PAIRS):
        prepare(step)

    firsts = range(0, n_pairs, 2)
    s = [side(state_scr[q], state_scr[q + 1]) for q in firsts]
    for ci in chunks:
        rows = slice(ci * CHUNK, (ci + 1) * CHUNK)

        def both(scr, part=slice(None)):
            return [side(scr[q, ci, part, :], scr[q + 1, ci, part, :]) for q in firsts]

        top, bottom = slice(0, stacked), slice(stacked, 2 * stacked)
        sr = [_dot_nt(ar_, blockdiag(*halves(bf(s_)))) for ar_, s_ in zip(both(ar_scr), s)]
        u = [_dot(pm_, blockdiag(*halves(bf(sr_[:stacked] + av_))))
             for pm_, sr_, av_ in zip(both(pm_scr), sr, both(av_scr, top))]
        ym = [sr_[stacked:] + av_ + _dot(arb_, blockdiag(*halves(bf(u_))))
              for sr_, av_, arb_, u_ in zip(sr, both(av_scr, bottom), both(arb_scr), u)]
        for q, ym_ in zip(firsts, ym):
            ya, yb = halves(ym_[:CHUNK] + ym_[CHUNK:])
            y_scr[q, rows, :] = ya
            y_scr[q + 1, rows, :] = yb
        decay = [side(et_scr[q, ci * CHUNK:ci * CHUNK + 1, :], et_scr[q + 1, ci * CHUNK:ci * CHUNK + 1, :])
                 for q in firsts]
        s = [s_ * d_ + _dot(side(*(bf(t.T) for t in halves(u_))), blockdiag(*halves(bh_))) + vk_
             for s_, d_, u_, bh_, vk_ in zip(s, decay, u, both(bkh_scr, top), both(vk_scr))]
    for q, s_ in zip(firsts, s):
        state_scr[q], state_scr[q + 1] = halves(s_)

    y = jnp.concatenate([y_scr[q] for q in range(n_pairs)], axis=1)
    mean = head_sum(y) * (1.0 / HEAD_DIM)
    d = y - mean
    var = head_sum(d * d) * (1.0 / HEAD_DIM)
    yn = d * lax.rsqrt(var + GN_EPS) * lnw_ref[...] + lnb_ref[...]
    bonus = head_sum(r * k2 * rk_ref[...]) * v
    o_ref[...] = (yn + bonus) * g * beta_ref[...]


def _rwkv_call(p_r, vecs, wd, wa, wg, batch, seq, width, tb):
    assert CHUNK == HEAD_DIM, "chunk operands stack two heads' rows against their 2 * HEAD_DIM lanes"
    assert (width // PAIR) % PREP_PAIRS == 0 and PREP_PAIRS % 2 == 0
    pw = p_r.shape[1]
    n_pairs = width // PAIR
    n_chunks = tb // CHUNK
    tiles = seq // tb
    vec_specs = [pl.BlockSpec((1, v.shape[1]), lambda b, t: (0, 0)) for v in vecs]
    mat_specs = [pl.BlockSpec(m.shape, lambda b, t: (0, 0)) for m in (wd, wa, wg)]
    stacked2 = pltpu.VMEM((n_pairs, n_chunks, 4 * CHUNK, PAIR), BF16)
    return pl.pallas_call(
        functools.partial(_rwkv_kernel, width=width, tb=tb),
        grid=(batch, tiles),
        in_specs=[pl.BlockSpec((tb, pw), lambda b, t: (b * tiles + t, 0))] + vec_specs + mat_specs,
        out_specs=pl.BlockSpec((tb, width), lambda b, t: (b * tiles + t, 0)),
        out_shape=jax.ShapeDtypeStruct((batch * seq, width), F32),
        scratch_shapes=[pltpu.VMEM((1, pw), F32),
                        pltpu.VMEM((n_pairs, PAIR, PAIR), F32),
                        stacked2, stacked2, stacked2,
                        pltpu.VMEM((n_pairs, n_chunks, 2 * CHUNK, PAIR), F32),
                        pltpu.VMEM((n_pairs, tb, PAIR), F32),
                        pltpu.VMEM((n_pairs, tb, PAIR), F32),
                        pltpu.VMEM((n_pairs, n_chunks, 2 * CHUNK, PAIR), BF16),
                        pltpu.VMEM((n_pairs, n_chunks, 2 * CHUNK, PAIR), BF16),
                        pltpu.VMEM((n_pairs, n_chunks, 4 * CHUNK, PAIR), F32),
                        pltpu.VMEM((n_pairs, n_chunks, 2 * CHUNK, PAIR), F32)],
        compiler_params=pltpu.CompilerParams(
            dimension_semantics=("parallel", "arbitrary"), vmem_limit_bytes=VMEM_LIMIT),
    )(p_r, *vecs, wd, wa, wg)


def _attn_kernel(q_ref, kp_ref, kc_ref, vp_ref, vc_ref, bias_ref, beta_ref, o_ref, kwin, vwin,
                 *, width, qb):
    i = pl.program_id(1)
    n_pairs = width // PAIR
    kwin[0:qb, :] = kp_ref[...]
    kwin[qb:2 * qb, :] = kc_ref[...]
    vwin[0:qb, :] = vp_ref[...]
    vwin[qb:2 * qb, :] = vc_ref[...]
    lane = lax.broadcasted_iota(jnp.int32, (1, PAIR), 1)
    first_head = lane < HEAD_DIM
    rows = ATTN_BLOCK_CHUNKS * CHUNK
    win = BAND + rows - CHUNK
    col = lax.broadcasted_iota(jnp.int32, (1, win), 1)
    scale = HEAD_DIM ** -0.5
    left = LEFT_CHUNKS * CHUNK

    def attend(first_block):
        for q in range(n_pairs):
            lanes = slice(q * PAIR, (q + 1) * PAIR)
            for blk in range(qb // rows):
                r0 = blk * rows
                qc = q_ref[r0:r0 + rows, lanes]
                zero = jnp.zeros_like(qc)
                qm = jnp.concatenate(
                    [jnp.where(first_head, qc, zero), jnp.where(first_head, zero, qc)], axis=0)
                s = _dot_nt(qm, kwin[r0:r0 + win, lanes]) * scale + bias_ref[q]
                if first_block:
                    s = jnp.where(col + r0 >= left, s, NEG_INF)
                e = jnp.exp(s - jnp.max(s, axis=-1, keepdims=True))
                denom = jnp.sum(e, axis=-1, keepdims=True)
                o = _dot(e.astype(BF16), vwin[r0:r0 + win, lanes]) / denom
                oc = jnp.where(first_head, o[:rows], o[rows:])
                o_ref[r0:r0 + rows, lanes] = oc * beta_ref[:, lanes]

    pl.when(i == 0)(lambda: attend(True))
    pl.when(i > 0)(lambda: attend(False))


def _attn_call(p_a, bias, beta, batch, seq, width, qb):
    assert qb == LEFT_CHUNKS * CHUNK, "key window = previous block + current block"
    tiles = seq // qb
    n_pairs = width // PAIR

    def cur(col):
        return pl.BlockSpec((qb, width), lambda b, t: (b * tiles + t, col))

    def prev(col):
        return pl.BlockSpec((qb, width), lambda b, t: (b * tiles + jnp.maximum(t - 1, 0), col))

    return pl.pallas_call(
        functools.partial(_attn_kernel, width=width, qb=qb),
        grid=(batch, tiles),
        in_specs=[cur(0), prev(1), cur(1), prev(2), cur(2),
                  pl.BlockSpec(bias.shape, lambda b, t: (0, 0, 0)),
                  pl.BlockSpec((1, width), lambda b, t: (0, 0))],
        out_specs=pl.BlockSpec((qb, width), lambda b, t: (b * tiles + t, 0)),
        out_shape=jax.ShapeDtypeStruct((batch * seq, width), F32),
        scratch_shapes=[pltpu.VMEM((2 * qb, width), BF16), pltpu.VMEM((2 * qb, width), BF16)],
        compiler_params=pltpu.CompilerParams(
            dimension_semantics=("parallel", "arbitrary"), vmem_limit_bytes=VMEM_LIMIT),
    )(p_a, p_a, p_a, p_a, p_a, bias, beta)


def _out_router_kernel(yr_ref, ya_ref, wo_ref, x_ref, mod_ref, g2_ref, wr_ref, br_ref,
                       x1_ref, h2_ref, route_ref, *, r_width):
    y = (_dot(yr_ref[...].astype(BF16), wo_ref[0:r_width, :])
         + _dot(ya_ref[...].astype(BF16), wo_ref[r_width:, :]))
    m = mod_ref[0]
    x1 = x_ref[...] + m[GATE1:GATE1 + 1] * y
    x1_ref[...] = x1
    h2 = _rms(x1) * g2_ref[...] * (1.0 + m[SCALE2:SCALE2 + 1]) + m[SHIFT2:SHIFT2 + 1]
    h2_ref[...] = h2

    wr = wr_ref[...]
    wr_hi = wr.astype(BF16)
    wr_lo = (wr - wr_hi.astype(F32)).astype(BF16)
    h2_hi = h2.astype(BF16)
    h2_lo = (h2 - h2_hi.astype(F32)).astype(BF16)
    hi_both = _dot(h2_hi, jnp.concatenate([wr_hi, wr_lo], axis=1))
    logits = _dot(h2_lo, wr_hi) + hi_both[:, LANES:] + hi_both[:, :LANES] + br_ref[...]
    li = lax.broadcasted_iota(jnp.int32, logits.shape, 1).astype(F32)
    none = jnp.float32(-jnp.inf)
    far = jnp.float32(LANES)
    is_group = li < N_GROUPS
    gl = jnp.where(is_group, logits, none)
    gmax = jnp.max(gl, axis=-1, keepdims=True)
    gidx = jnp.min(jnp.where(gl == gmax, li, far), axis=-1, keepdims=True)
    g_w = 1.0 / jnp.sum(jnp.where(is_group, jnp.exp(logits - gmax), 0.0), axis=-1, keepdims=True)
    lo = N_GROUPS + EXPERTS_PER_GROUP * gidx
    el = jnp.where(jnp.logical_and(li >= lo, li < lo + EXPERTS_PER_GROUP), logits, none)
    m1 = jnp.max(el, axis=-1, keepdims=True)
    i1 = jnp.min(jnp.where(el == m1, li, far), axis=-1, keepdims=True)
    el2 = jnp.where(li == i1, none, el)
    m2 = jnp.max(el2, axis=-1, keepdims=True)
    i2 = jnp.min(jnp.where(el2 == m2, li, far), axis=-1, keepdims=True)
    e2 = jnp.exp(m2 - m1)
    w1 = g_w / (1.0 + e2)
    w2 = g_w * e2 / (1.0 + e2)
    route = jnp.where(li == 0.0, i1 - N_GROUPS, 0.0) + jnp.where(li == 1.0, i2 - N_GROUPS, 0.0)
    route_ref[...] = route + jnp.where(li == 2.0, w1, 0.0) + jnp.where(li == 3.0, w2, 0.0)


def _out_router_call(y_r, y_a, wo, x2d, mod, g2, wr, br, seq, tm):
    rows, d = x2d.shape
    r_width = y_r.shape[1]
    a_width = y_a.shape[1]
    tiles_per_seq = seq // tm
    return pl.pallas_call(
        functools.partial(_out_router_kernel, r_width=r_width),
        grid=(rows // tm,),
        in_specs=[pl.BlockSpec((tm, r_width), lambda i: (i, 0)),
                  pl.BlockSpec((tm, a_width), lambda i: (i, 0)),
                  pl.BlockSpec(wo.shape, lambda i: (0, 0)),
                  pl.BlockSpec((tm, d), lambda i: (i, 0)),
                  pl.BlockSpec((1, 6, d), lambda i: (i // tiles_per_seq, 0, 0)),
                  pl.BlockSpec((1, d), lambda i: (0, 0)),
                  pl.BlockSpec((d, LANES), lambda i: (0, 0)),
                  pl.BlockSpec((1, LANES), lambda i: (0, 0))],
        out_specs=[pl.BlockSpec((tm, d), lambda i: (i, 0)),
                   pl.BlockSpec((tm, d), lambda i: (i, 0)),
                   pl.BlockSpec((tm, LANES), lambda i: (i, 0))],
        out_shape=[jax.ShapeDtypeStruct((rows, d), F32),
                   jax.ShapeDtypeStruct((rows, d), F32),
                   jax.ShapeDtypeStruct((rows, LANES), F32)],
        compiler_params=pltpu.CompilerParams(
            dimension_semantics=("parallel",), vmem_limit_bytes=VMEM_LIMIT),
    )(y_r, y_a, wo, x2d, mod, g2, wr, br)


def _moe_kernel(texp_ref, first_ref, next_ref, rows_ref, meta_ref, src_ref,
                h2_hbm, wg_hbm, wu_hbm, wd_hbm, o_ref,
                xbuf, xsem, wg_land, wu_land, wd_land, wsem, wg_bf, wu_bf, wd_bf, *, tm):
    t = pl.program_id(0)
    n_used = meta_ref[0]
    slot = lax.rem(t, MOE_GATHER_SLOTS)

    def weight_copies(e):
        cps = []
        for m, (hbm, land) in enumerate(((wg_hbm, wg_land), (wu_hbm, wu_land), (wd_hbm, wd_land))):
            rows = hbm.shape[1] // W_CHUNKS
            for ck in range(W_CHUNKS):
                part = pl.ds(ck * rows, rows)
                cps.append(pltpu.make_async_copy(hbm.at[e, part], land.at[part], wsem.at[m, ck]))
        return cps

    def row_groups(tile):
        return (rows_ref[tile] + ROW_GROUP - 1) // ROW_GROUP

    def gather_rows(tile, unrolled=False):
        buf_slot = lax.rem(tile, MOE_GATHER_SLOTS)
        n_groups = row_groups(tile)

        def body(g, carry):
            for k in range(ROW_GROUP):
                i = g * ROW_GROUP + k
                tok = src_ref[tile * tm + i]
                pltpu.make_async_copy(h2_hbm.at[pl.ds(tok, 1)], xbuf.at[buf_slot, pl.ds(i, 1)],
                                      xsem.at[buf_slot]).start()
            return carry

        if unrolled:
            for g in range(tm // ROW_GROUP):
                @pl.when(g < n_groups)
                def _(g=g):
                    body(g, 0)
        else:
            lax.fori_loop(0, n_groups, body, 0)

    @pl.when(t == 0)
    def _():
        xbuf[...] = jnp.zeros_like(xbuf)
        for cp in weight_copies(texp_ref[0]):
            cp.start()
        for ahead in range(MOE_GATHER_SLOTS - 1):
            @pl.when(ahead < n_used)
            def _():
                gather_rows(ahead)

    is_first = jnp.logical_and(t < n_used, first_ref[t] == 1)

    @pl.when(is_first)
    def _():
        for cp in weight_copies(texp_ref[t]):
            cp.wait()
        wg_bf[...] = wg_land[...].astype(BF16)
        wu_bf[...] = wu_land[...].astype(BF16)
        wd_bf[...] = wd_land[...].astype(BF16)

    @pl.when(t + MOE_GATHER_SLOTS - 1 < n_used)
    def _():
        gather_rows(t + MOE_GATHER_SLOTS - 1, unrolled=True)

    @pl.when(jnp.logical_and(is_first, next_ref[t] >= 0))
    def _():
        for cp in weight_copies(next_ref[t]):
            cp.start()

    @pl.when(t < n_used)
    def _():
        def wait_group(g, carry):
            r0 = pl.multiple_of(g * ROW_GROUP, ROW_GROUP)
            pltpu.make_async_copy(h2_hbm.at[pl.ds(0, ROW_GROUP)],
                                  xbuf.at[slot, pl.ds(r0, ROW_GROUP)], xsem.at[slot]).wait()
            return carry

        lax.fori_loop(0, row_groups(t), wait_group, 0)
        x = xbuf[slot].astype(BF16)
        gate = _dot(x, wg_bf[...])
        up = _dot(x, wu_bf[...])
        h = gate * _sigmoid(gate) * up
        o_ref[...] = _dot(h.astype(BF16), wd_bf[...])

    @pl.when(t >= n_used)
    def _():
        o_ref[...] = jnp.zeros_like(o_ref)


def _moe_call(plan, src, h2, wg, wu, wd, tm):
    n_e, d, f = wg.shape
    n_tiles = plan[0].shape[0]
    grid_spec = pltpu.PrefetchScalarGridSpec(
        num_scalar_prefetch=len(plan) + 1,
        grid=(n_tiles,),
        in_specs=[pl.BlockSpec(memory_space=pl.ANY)] * 4,
        out_specs=pl.BlockSpec((tm, d), lambda t, *_: (t, 0)),
        scratch_shapes=[pltpu.VMEM((MOE_GATHER_SLOTS, tm, d), F32),
                        pltpu.SemaphoreType.DMA((MOE_GATHER_SLOTS,)),
                        pltpu.VMEM((d, f), F32), pltpu.VMEM((d, f), F32), pltpu.VMEM((f, d), F32),
                        pltpu.SemaphoreType.DMA((3, W_CHUNKS)),
                        pltpu.VMEM((d, f), BF16), pltpu.VMEM((d, f), BF16), pltpu.VMEM((f, d), BF16)])
    return pl.pallas_call(
        functools.partial(_moe_kernel, tm=tm),
        grid_spec=grid_spec,
        out_shape=jax.ShapeDtypeStruct((n_tiles * tm, d), F32),
        compiler_params=pltpu.CompilerParams(
            dimension_semantics=("arbitrary",), vmem_limit_bytes=VMEM_LIMIT),
    )(*plan, src, h2, wg, wu, wd)


def _final_kernel(pos_ref, ys_hbm, route_ref, x1_ref, mod_ref, gf_ref, o_ref, ybuf, sem, *, tm):
    i = pl.program_id(0)
    n = pl.num_programs(0)
    slot = i % 2

    def gather_rows(tile, buf_slot, unrolled):
        base = tile * tm * 2

        def body(r, carry):
            for s in range(2):
                pltpu.make_async_copy(ys_hbm.at[pl.ds(pos_ref[base + 2 * r + s], 1)],
                                      ybuf.at[buf_slot, s, pl.ds(r, 1)], sem.at[buf_slot]).start()
            return carry

        if unrolled:
            for r in range(tm):
                body(r, 0)
        else:
            lax.fori_loop(0, tm, body, 0, unroll=4)

    @pl.when(i == 0)
    def _():
        gather_rows(0, 0, unrolled=False)

    @pl.when(i + 1 < n)
    def _():
        gather_rows(i + 1, 1 - slot, unrolled=True)

    for s in range(2):
        pltpu.make_async_copy(ys_hbm.at[pl.ds(0, tm)], ybuf.at[slot, s], sem.at[slot]).wait()
    route = route_ref[...]
    w0 = route[:, 2:3]
    w1 = route[:, 3:4]
    f = w0 * ybuf[slot, 0] + w1 * ybuf[slot, 1]
    x2 = x1_ref[...] + mod_ref[0][GATE2:GATE2 + 1] * f
    o_ref[...] = _rms(x2) * gf_ref[...]


def _final_call(pos, ys, route, x1, mod, gf, seq, tm):
    rows, d = x1.shape
    tiles_per_seq = seq // tm
    grid_spec = pltpu.PrefetchScalarGridSpec(
        num_scalar_prefetch=1,
        grid=(rows // tm,),
        in_specs=[pl.BlockSpec(memory_space=pl.ANY),
                  pl.BlockSpec((tm, LANES), lambda i, pos: (i, 0)),
                  pl.BlockSpec((tm, d), lambda i, pos: (i, 0)),
                  pl.BlockSpec((1, 6, d), lambda i, pos: (i // tiles_per_seq, 0, 0)),
                  pl.BlockSpec((1, d), lambda i, pos: (0, 0))],
        out_specs=pl.BlockSpec((tm, d), lambda i, pos: (i, 0)),
        scratch_shapes=[pltpu.VMEM((2, 2, tm, d), F32), pltpu.SemaphoreType.DMA((2,))])
    return pl.pallas_call(
        functools.partial(_final_kernel, tm=tm),
        grid_spec=grid_spec,
        out_shape=jax.ShapeDtypeStruct((rows, d), F32),
        compiler_params=pltpu.CompilerParams(
            dimension_semantics=("arbitrary",), vmem_limit_bytes=VMEM_LIMIT),
    )(pos, ys, route, x1, mod, gf)


def _route_plan(route, tm):
    n = route.shape[0]
    flat_e = route[:, 0:2].astype(jnp.int32).reshape(-1)
    n_tiles = (2 * n + N_EXPERTS * (tm - 1)) // tm
    experts = jnp.arange(N_EXPERTS, dtype=jnp.int32)
    onehot = (flat_e[:, None] == experts[None, :]).astype(jnp.int32)
    csum = jnp.cumsum(onehot, axis=0)
    rank = jnp.sum(csum * onehot, axis=1) - 1
    counts = csum[-1]
    tiles_per_e = (counts + tm - 1) // tm
    tile_end = jnp.cumsum(tiles_per_e)
    tile_start = tile_end - tiles_per_e
    n_used = tile_end[-1]
    pos = jnp.sum(onehot * tile_start[None, :], axis=1) * tm + rank
    token = jnp.arange(2 * n, dtype=jnp.int32) // 2
    src = jnp.zeros((n_tiles * tm,), jnp.int32).at[pos].set(token)
    tiles = jnp.arange(n_tiles, dtype=jnp.int32)
    texp = jnp.sum((tiles[:, None] >= tile_end[None, :]).astype(jnp.int32), axis=1)
    texp = jnp.where(tiles < n_used, texp, texp[n_used - 1])
    first = jnp.concatenate([jnp.ones((1,), jnp.int32), (texp[1:] != texp[:-1]).astype(jnp.int32)])
    later_used = jnp.logical_and(experts[None, :] > experts[:, None], (tiles_per_e > 0)[None, :])
    next_used = jnp.min(jnp.where(later_used, experts[None, :], N_EXPERTS), axis=1)
    next_used = jnp.where(next_used == N_EXPERTS, -1, next_used)
    tile_rows = jnp.clip(counts[texp] - (tiles - tile_start[texp]) * tm, 0, tm)
    return (texp, first, next_used[texp], tile_rows, n_used.reshape(1)), src, pos


def _round_up(n, m):
    return (n + m - 1) // m * m


def _tile_sizes(seq):
    return dict(ada_cols=1536, rows=min(512, seq), proj_rows=min(1024, seq),
                rwkv_rows=min(4 * CHUNK, seq), attn_rows=min(LEFT_CHUNKS * CHUNK, seq),
                expert_rows=256, final_rows=min(256, seq))


def kernel(x, c, w_ada, b_ada, norm1_g, w_in, mu_shift, w0, w_decay_up, a0, w_aaa_up, w_gate_up,
           k_k, k_a, r_k, ln_x_w, ln_x_b, rel_bias, beta_rwkv, beta_attn, w_out, norm2_g, w_group,
           b_group, w_expert, b_expert, w_gate, w_up, w_down, norm_f_g):
    assert w_ada.shape[0] == 1, "single trunk layer"
    batch, seq, d = x.shape
    r_width = w0.shape[1]
    a_width = beta_attn.shape[1]
    shift_width = mu_shift.shape[1]
    assert shift_width == 3 * r_width + DECAY_LORA + AAA_LORA + GATE_LORA
    x2d = x.reshape(batch * seq, d)
    row = lambda t: t.reshape(1, -1)

    tiles = _tile_sizes(seq)
    tm = tiles["rows"]

    c8 = jnp.pad(c, ((0, SUBLANES - batch), (0, 0)))
    mod = _ada_call(c8, w_ada[0], row(b_ada[0]), tn=tiles["ada_cols"])[:batch].reshape(batch, 6, d)

    pw = _round_up(shift_width, 3 * LANES)
    h1 = _hmod_call(x2d, row(norm1_g[0]), mod, tm, seq)
    w_in_t = jnp.swapaxes(w_in[0], 0, 1)
    p_r = _matmul_call(h1, w_in_t, 0, pw, pw // 3, tiles["proj_rows"], F32)
    p_a = _matmul_call(h1, w_in_t, shift_width, 3 * a_width, a_width, tiles["proj_rows"], BF16)

    mu = jnp.pad(mu_shift[0], (0, pw - shift_width))
    vecs = [row(mu), row(w0[0]), row(a0[0]), row(k_k[0]), row(k_a[0]), row(r_k[0]), row(ln_x_w[0]),
            row(ln_x_b[0]), row(beta_rwkv[0])]
    y_r = _rwkv_call(p_r, vecs, w_decay_up[0], w_aaa_up[0], w_gate_up[0], batch, seq, r_width,
                     tb=tiles["rwkv_rows"])

    left = LEFT_CHUNKS * CHUNK
    dist = left + CHUNK - 1 - jnp.arange(BAND + CHUNK - 1)
    base = rel_bias[0][:, jnp.clip(dist, -REL_CLIP, REL_CLIP) + REL_CLIP].astype(F32)
    bias = jnp.stack([base[:, CHUNK - 1 - qi:CHUNK - 1 - qi + BAND] for qi in range(CHUNK)], axis=1)
    nb = ATTN_BLOCK_CHUNKS
    bias = jnp.concatenate(
        [jnp.pad(bias, ((0, 0), (0, 0), (cb * CHUNK, (nb - 1 - cb) * CHUNK)), constant_values=NEG_INF)
         for cb in range(nb)], axis=1)
    bias = bias.reshape(a_width // PAIR, 2 * nb * CHUNK, BAND + (nb - 1) * CHUNK)
    y_a = _attn_call(p_a, bias, row(beta_attn[0]), batch, seq, a_width, qb=tiles["attn_rows"])

    n_route = N_GROUPS + N_EXPERTS
    wr = jnp.concatenate([w_group[0], w_expert[0].transpose(1, 0, 2).reshape(d, N_EXPERTS)], axis=1)
    wr = jnp.pad(wr, ((0, 0), (0, LANES - n_route)))
    br = jnp.pad(jnp.concatenate([b_group[0], b_expert[0].reshape(-1)]), (0, LANES - n_route))
    x1, h2, route = _out_router_call(y_r, y_a, w_out[0].astype(BF16), x2d, mod, row(norm2_g[0]),
                                     wr, row(br), seq, tm)

    f = w_gate.shape[-1]
    tm_e = tiles["expert_rows"]
    plan, src, pos = _route_plan(route, tm_e)
    ys = _moe_call(plan, src, h2, w_gate[0].reshape(N_EXPERTS, d, f),
                   w_up[0].reshape(N_EXPERTS, d, f), w_down[0].reshape(N_EXPERTS, f, d), tm_e)
    out = _final_call(pos, ys, route, x1, mod, row(norm_f_g), seq, tiles["final_rows"])
    return out.reshape(batch, seq, d)
```

```python
import functools

import jax
import jax.numpy as jnp
from jax import lax
from jax.experimental import pallas as pl
from jax.experimental.pallas import tpu as pltpu

F32 = jnp.float32
BF16 = jnp.bfloat16

LANES = 128
SUBLANES = 8
HEAD_DIM = 64
PAIR = 2 * HEAD_DIM
CHUNK = 64
LEFT_CHUNKS = 8
BAND = (LEFT_CHUNKS + 1) * CHUNK
ATTN_BLOCK_CHUNKS = 4
PREP_PAIRS = 8
MOE_GATHER_SLOTS = 3
W_CHUNKS = 2
ROW_GROUP = 8
REL_CLIP = 128
DECAY_LORA = 64
AAA_LORA = 64
GATE_LORA = 160
N_GROUPS = 4
EXPERTS_PER_GROUP = 8
N_EXPERTS = N_GROUPS * EXPERTS_PER_GROUP
RMS_EPS = 1e-6
GN_EPS = 64e-5
L2_EPS = 1e-12
NEG_INF = -1e30
VMEM_LIMIT = 56 * 1024 * 1024

SHIFT1, SCALE1, GATE1, SHIFT2, SCALE2, GATE2 = range(6)


def _dot(a, b):
    return jnp.dot(a, b, preferred_element_type=F32)


def _dot_nt(a, b):
    return lax.dot_general(a, b, (((1,), (1,)), ((), ())), preferred_element_type=F32)


def _sigmoid(x):
    return 1.0 / (1.0 + jnp.exp(-x))


def _rms(x):
    return x * lax.rsqrt(jnp.mean(x * x, axis=-1, keepdims=True) + RMS_EPS)


def _ada_kernel(c_ref, w_ref, b_ref, o_ref):
    c = c_ref[...]
    o_ref[...] = _dot_x3(c * _sigmoid(c), w_ref[...]) + b_ref[...]


def _ada_call(c8, w, b, tn):
    rows, d = c8.shape
    n = w.shape[1]
    return pl.pallas_call(
        _ada_kernel,
        grid=(n // tn,),
        in_specs=[pl.BlockSpec((rows, d), lambda j: (0, 0)),
                  pl.BlockSpec((d, tn), lambda j: (0, j)),
                  pl.BlockSpec((1, tn), lambda j: (0, j))],
        out_specs=pl.BlockSpec((rows, tn), lambda j: (0, j)),
        out_shape=jax.ShapeDtypeStruct((rows, n), F32),
        compiler_params=pltpu.CompilerParams(vmem_limit_bytes=VMEM_LIMIT),
    )(c8, w, b)


def _hmod_kernel(x_ref, g_ref, mod_ref, o_ref):
    m = mod_ref[0]
    h = _rms(x_ref[...]) * g_ref[...] * (1.0 + m[SCALE1:SCALE1 + 1]) + m[SHIFT1:SHIFT1 + 1]
    o_ref[...] = h.astype(BF16)


def _hmod_call(x2d, g, mod, tm, seq):
    rows, d = x2d.shape
    tiles_per_seq = seq // tm
    return pl.pallas_call(
        _hmod_kernel,
        grid=(rows // tm,),
        in_specs=[pl.BlockSpec((tm, d), lambda i: (i, 0)),
                  pl.BlockSpec((1, d), lambda i: (0, 0)),
                  pl.BlockSpec((1, 6, d), lambda i: (i // tiles_per_seq, 0, 0))],
        out_specs=pl.BlockSpec((tm, d), lambda i: (i, 0)),
        out_shape=jax.ShapeDtypeStruct((rows, d), BF16),
        compiler_params=pltpu.CompilerParams(
            dimension_semantics=("parallel",), vmem_limit_bytes=VMEM_LIMIT),
    )(x2d, g, mod)


def _matmul_kernel(a_ref, wt_ref, o_ref, w_bf):
    @pl.when(pl.program_id(1) == 0)
    def _():
        w_bf[...] = wt_ref[...].astype(BF16)

    o_ref[...] = _dot_nt(a_ref[...], w_bf[...]).astype(o_ref.dtype)


def _matmul_call(a, wt, row0, n_out, tn, tm, out_dtype):
    rows, d = a.shape
    return pl.pallas_call(
        _matmul_kernel,
        grid=(n_out // tn, rows // tm),
        in_specs=[pl.BlockSpec((tm, d), lambda j, i: (i, 0)),
                  pl.BlockSpec((pl.Element(tn), pl.Element(d)),
                               lambda j, i: (pl.multiple_of(row0 + j * tn, SUBLANES), 0))],
        out_specs=pl.BlockSpec((tm, tn), lambda j, i: (i, j)),
        out_shape=jax.ShapeDtypeStruct((rows, n_out), out_dtype),
        scratch_shapes=[pltpu.VMEM((tn, d), BF16)],
        compiler_params=pltpu.CompilerParams(
            dimension_semantics=("parallel", "arbitrary"), vmem_limit_bytes=VMEM_LIMIT),
    )(a, wt)


def _split3(t):
    hi = t.astype(BF16)
    rest = t - hi.astype(F32)
    mid = rest.astype(BF16)
    lo = (rest - mid.astype(F32)).astype(BF16)
    return hi, mid, lo


def _dot_hilo_lhs(t, m):
    hi = t.astype(BF16)
    lo = (t - hi.astype(F32)).astype(BF16)
    return _dot(lo, m) + _dot(hi, m)


def _dot_f32_rhs(m, t):
    hi, mid, lo = _split3(t)
    return _dot(m, lo) + _dot(m, mid) + _dot(m, hi)


def _dot_x3(a, b):
    ah = a.astype(BF16)
    al = (a - ah.astype(F32)).astype(BF16)
    bh = b.astype(BF16)
    bl = (b - bh.astype(F32)).astype(BF16)
    return _dot(al, bh) + _dot(ah, bl) + _dot(ah, bh)


def _rwkv_kernel(p_ref, mu_ref, w0_ref, a0_ref, kk_ref, ka_ref, rk_ref, lnw_ref, lnb_ref, beta_ref,
                 wd_ref, wa_ref, wg_ref, o_ref,
                 carry_scr, state_scr, ar_scr, bk_scr, bkh_scr, v_scr, et_scr, y_scr,
                 pm_scr, arb_scr, av_scr, vk_scr, *, width, tb):
    n_pairs = width // PAIR
    n_chunks = tb // CHUNK
    stacked = 2 * CHUNK

    @pl.when(pl.program_id(1) == 0)
    def _():
        carry_scr[...] = jnp.zeros_like(carry_scr)
        state_scr[...] = jnp.zeros_like(state_scr)

    p = p_ref[...]
    row = lax.broadcasted_iota(jnp.int32, (tb, 1), 0)
    prev = jnp.where(row == 0, carry_scr[...], pltpu.roll(p, 1, axis=0))
    carry_scr[...] = p[tb - 1:tb, :]
    ps = p + (prev - p) * mu_ref[...]

    c = width
    r = ps[:, :c]
    k = ps[:, c:2 * c]
    v = ps[:, 2 * c:3 * c]
    o = 3 * c
    xw = ps[:, o:o + DECAY_LORA]
    xa = ps[:, o + DECAY_LORA:o + DECAY_LORA + AAA_LORA]
    xg = ps[:, o + DECAY_LORA + AAA_LORA:o + DECAY_LORA + AAA_LORA + GATE_LORA]

    z = w0_ref[...] + _dot_x3(jnp.tanh(xw), wd_ref[...])
    softplus_neg_z = jnp.maximum(-z, 0.0) + jnp.log(1.0 + jnp.exp(-jnp.abs(z)))
    lw = -jnp.exp(-softplus_neg_z - 0.5)
    a = _sigmoid(a0_ref[...] + _dot(xa.astype(BF16), wa_ref[...].astype(BF16)))
    g = _dot(_sigmoid(xg).astype(BF16), wg_ref[...].astype(BF16))

    li = lax.broadcasted_iota(jnp.int32, (PAIR, PAIR), 0)
    lj = lax.broadcasted_iota(jnp.int32, (PAIR, PAIR), 1)
    head_ones = ((li // HEAD_DIM) == (lj // HEAD_DIM)).astype(BF16)

    def head_sum(t):
        return jnp.concatenate(
            [_dot_hilo_lhs(t[:, q * PAIR:(q + 1) * PAIR], head_ones) for q in range(n_pairs)], axis=1)

    kk = k * kk_ref[...]
    kk = kk / jnp.maximum(jnp.sqrt(head_sum(kk * kk)), L2_EPS)
    k2 = k * (1.0 + (a - 1.0) * ka_ref[...])

    ti = lax.broadcasted_iota(jnp.int32, (tb, tb), 0)
    tj = lax.broadcasted_iota(jnp.int32, (tb, tb), 1)
    tri = jnp.logical_and((ti // CHUNK) == (tj // CHUNK), tj <= ti).astype(BF16)
    cum = _dot_f32_rhs(tri, lw)
    tot = jnp.concatenate(
        [jnp.broadcast_to(cum[(ci + 1) * CHUNK - 1:(ci + 1) * CHUNK, :], (CHUNK, c))
         for ci in range(n_chunks)], axis=0)

    e_neg = jnp.exp(-cum)
    e_rem = jnp.exp(tot - cum)
    kka = kk * a
    lane = lax.broadcasted_iota(jnp.int32, (1, PAIR), 1)
    first_head = lane < HEAD_DIM

    def put(scr, base, val):
        for q in range(n_pairs):
            vq = val[:, q * PAIR:(q + 1) * PAIR]
            h0 = jnp.where(first_head, vq, 0.0).astype(scr.dtype)
            h1 = jnp.where(first_head, 0.0, vq).astype(scr.dtype)
            for ci in range(n_chunks):
                rows = slice(ci * CHUNK, (ci + 1) * CHUNK)
                scr[q, ci, base:base + CHUNK, :] = h0[rows]
                scr[q, ci, base + CHUNK:base + stacked, :] = h1[rows]

    put(ar_scr, 0, -kk * jnp.exp(cum - lw))
    put(ar_scr, stacked, r * jnp.exp(cum))
    put(bk_scr, 0, kka * e_neg)
    put(bk_scr, stacked, k2 * e_neg)
    put(bkh_scr, 0, kka * e_rem)
    put(bkh_scr, stacked, k2 * e_rem)
    put(v_scr, 0, v)
    e_tot = jnp.exp(tot)
    for q in range(n_pairs):
        et_scr[q] = e_tot[:, q * PAIR:(q + 1) * PAIR]

    si = lax.broadcasted_iota(jnp.int32, (2 * stacked, 2 * stacked), 0)
    sj = lax.broadcasted_iota(jnp.int32, (2 * stacked, 2 * stacked), 1)
    keep = jnp.where(si < stacked, si, si - stacked + 1) > (sj % stacked)
    eye = (li == lj).astype(F32)

    chunks = range(n_chunks)

    def side(ta, tb):
        return jnp.concatenate([ta, tb], axis=1)

    def blockdiag(ta, tb):
        za = jnp.zeros((ta.shape[0], tb.shape[1]), ta.dtype)
        zb = jnp.zeros((tb.shape[0], ta.shape[1]), ta.dtype)
        return jnp.concatenate([side(ta, za), side(zb, tb)], axis=0)

    def halves(t):
        return t[:, :PAIR], t[:, PAIR:]

    def bf(t):
        return t.astype(BF16)

    hr = lax.broadcasted_iota(jnp.int32, (CHUNK, 4 * CHUNK), 0)
    hl = lax.broadcasted_iota(jnp.int32, (CHUNK, 4 * CHUNK), 1)
    eye4 = (hr == hl % CHUNK).astype(F32)
    zero_bf = jnp.zeros((CHUNK, stacked), BF16)

    def head_blocks(t):
        z = jnp.zeros_like(t)
        return jnp.concatenate([jnp.where(hl // CHUNK == j, t, z) for j in range(4)], axis=0)

    def prepare(step):
        couples = [(step * PREP_PAIRS + 2 * j, ci) for j in range(PREP_PAIRS // 2) for ci in chunks]
        pw = []
        for qa, ci in couples:
            n, tail, vm = [], [], []
            for q in (qa, qa + 1):
                sc = jnp.where(keep, _dot_nt(ar_scr[q, ci], bk_scr[q, ci]), 0.0)
                scb = bf(sc)
                arb_scr[q, ci] = scb[stacked:, :stacked]
                n.append(sc[:stacked, :stacked])
                tail.append(scb[:, stacked:])
                vm.append(v_scr[q, ci])
            av = _dot(side(*tail), blockdiag(bf(vm[0]), bf(vm[1])))
            vk = _dot(side(bf(vm[0].T), bf(vm[1].T)),
                      blockdiag(bkh_scr[qa, ci, stacked:, :], bkh_scr[qa + 1, ci, stacked:, :]))
            for q, t_av, t_vk in zip((qa, qa + 1), halves(av), halves(vk)):
                av_scr[q, ci] = t_av
                vk_scr[q, ci] = t_vk
            pw.append(bf(side(n[0][:CHUNK] + n[0][CHUNK:], n[1][:CHUNK] + n[1][CHUNK:])))
        acc = [eye4 + p_ for p_ in pw]
        pw = [bf(_dot(p_, head_blocks(p_))) for p_ in pw]
        for it in range(1, 6):
            rhs = [head_blocks(p_) for p_ in pw]
            if it < 5:
                both = [_dot(jnp.concatenate([p_, bf(a_)], axis=0), r_)
                        for p_, a_, r_ in zip(pw, acc, rhs)]
                pw = [bf(t[:CHUNK]) for t in both]
                acc = [a_ + t[CHUNK:] for a_, t in zip(acc, both)]
            else:
                acc = [a_ + _dot(bf(a_), r_) for a_, r_ in zip(acc, rhs)]
        for (qa, ci), a_ in zip(couples, acc):
            for q, t in zip((qa, qa + 1), halves(bf(a_))):
                pm_scr[q, ci] = jnp.concatenate(
                    [jnp.where(first_head, t, zero_bf), jnp.where(first_head, zero_bf, t)], axis=0)

    for step in range(n_pairs // PREP_PAIRS):
        prepare(step)

    firsts = range(0, n_pairs, 2)
    s = [side(state_scr[q], state_scr[q + 1]) for q in firsts]
    for ci in chunks:
        rows = slice(ci * CHUNK, (ci + 1) * CHUNK)

        def both(scr, part=slice(None)):
            return [side(scr[q, ci, part, :], scr[q + 1, ci, part, :]) for q in firsts]

        top, bottom = slice(0, stacked), slice(stacked, 2 * stacked)
        sr = [_dot_nt(ar_, blockdiag(*halves(bf(s_)))) for ar_, s_ in zip(both(ar_scr), s)]
        u = [_dot(pm_, blockdiag(*halves(bf(sr_[:stacked] + av_))))
             for pm_, sr_, av_ in zip(both(pm_scr), sr, both(av_scr, top))]
        ym = [sr_[stacked:] + av_ + _dot(arb_, blockdiag(*halves(bf(u_))))
              for sr_, av_, arb_, u_ in zip(sr, both(av_scr, bottom), both(arb_scr), u)]
        for q, ym_ in zip(firsts, ym):
            ya, yb = halves(ym_[:CHUNK] + ym_[CHUNK:])
            y_scr[q, rows, :] = ya
            y_scr[q + 1, rows, :] = yb
        decay = [side(et_scr[q, ci * CHUNK:ci * CHUNK + 1, :], et_scr[q + 1, ci * CHUNK:ci * CHUNK + 1, :])
                 for q in firsts]
        s = [s_ * d_ + _dot(side(*(bf(t.T) for t in halves(u_))), blockdiag(*halves(bh_))) + vk_
             for s_, d_, u_, bh_, vk_ in zip(s, decay, u, both(bkh_scr, top), both(vk_scr))]
    for q, s_ in zip(firsts, s):
        state_scr[q], state_scr[q + 1] = halves(s_)

    y = jnp.concatenate([y_scr[q] for q in range(n_pairs)], axis=1)
    mean = head_sum(y) * (1.0 / HEAD_DIM)
    d = y - mean
    var = head_sum(d * d) * (1.0 / HEAD_DIM)
    yn = d * lax.rsqrt(var + GN_EPS) * lnw_ref[...] + lnb_ref[...]
    bonus = head_sum(r * k2 * rk_ref[...]) * v
    o_ref[...] = (yn + bonus) * g * beta_ref[...]


def _rwkv_call(p_r, vecs, wd, wa, wg, batch, seq, width, tb):
    assert CHUNK == HEAD_DIM, "chunk operands stack two heads' rows against their 2 * HEAD_DIM lanes"
    assert (width // PAIR) % PREP_PAIRS == 0 and PREP_PAIRS % 2 == 0
    pw = p_r.shape[1]
    n_pairs = width // PAIR
    n_chunks = tb // CHUNK
    tiles = seq // tb
    vec_specs = [pl.BlockSpec((1, v.shape[1]), lambda b, t: (0, 0)) for v in vecs]
    mat_specs = [pl.BlockSpec(m.shape, lambda b, t: (0, 0)) for m in (wd, wa, wg)]
    stacked2 = pltpu.VMEM((n_pairs, n_chunks, 4 * CHUNK, PAIR), BF16)
    return pl.pallas_call(
        functools.partial(_rwkv_kernel, width=width, tb=tb),
        grid=(batch, tiles),
        in_specs=[pl.BlockSpec((tb, pw), lambda b, t: (b * tiles + t, 0))] + vec_specs + mat_specs,
        out_specs=pl.BlockSpec((tb, width), lambda b, t: (b * tiles + t, 0)),
        out_shape=jax.ShapeDtypeStruct((batch * seq, width), F32),
        scratch_shapes=[pltpu.VMEM((1, pw), F32),
                        pltpu.VMEM((n_pairs, PAIR, PAIR), F32),
                        stacked2, stacked2, stacked2,
                        pltpu.VMEM((n_pairs, n_chunks, 2 * CHUNK, PAIR), F32),
                        pltpu.VMEM((n_pairs, tb, PAIR), F32),
                        pltpu.VMEM((n_pairs, tb, PAIR), F32),
                        pltpu.VMEM((n_pairs, n_chunks, 2 * CHUNK, PAIR), BF16),
                        pltpu.VMEM((n_pairs, n_chunks, 2 * CHUNK, PAIR), BF16),
                        pltpu.VMEM((n_pairs, n_chunks, 4 * CHUNK, PAIR), F32),
                        pltpu.VMEM((n_pairs, n_chunks, 2 * CHUNK, PAIR), F32)],
        compiler_params=pltpu.CompilerParams(
            dimension_semantics=("parallel", "arbitrary"), vmem_limit_bytes=VMEM_LIMIT),
    )(p_r, *vecs, wd, wa, wg)


def _attn_kernel(q_ref, kp_ref, kc_ref, vp_ref, vc_ref, bias_ref, beta_ref, o_ref, kwin, vwin,
                 *, width, qb):
    i = pl.program_id(1)
    n_pairs = width // PAIR
    kwin[0:qb, :] = kp_ref[...]
    kwin[qb:2 * qb, :] = kc_ref[...]
    vwin[0:qb, :] = vp_ref[...]
    vwin[qb:2 * qb, :] = vc_ref[...]
    lane = lax.broadcasted_iota(jnp.int32, (1, PAIR), 1)
    first_head = lane < HEAD_DIM
    rows = ATTN_BLOCK_CHUNKS * CHUNK
    win = BAND + rows - CHUNK
    col = lax.broadcasted_iota(jnp.int32, (1, win), 1)
    scale = HEAD_DIM ** -0.5
    left = LEFT_CHUNKS * CHUNK

    def attend(first_block):
        for q in range(n_pairs):
            lanes = slice(q * PAIR, (q + 1) * PAIR)
            for blk in range(qb // rows):
                r0 = blk * rows
                qc = q_ref[r0:r0 + rows, lanes]
                zero = jnp.zeros_like(qc)
                kw = kwin[r0:r0 + win, lanes]
                vw = vwin[r0:r0 + win, lanes]
                heads = []
                for hd in range(2):
                    qh = jnp.where(first_head, qc, zero) if hd == 0 else jnp.where(first_head, zero, qc)
                    s = _dot_nt(qh, kw) * scale + bias_ref[q, hd * rows:(hd + 1) * rows, :]
                    if first_block:
                        s = jnp.where(col + r0 >= left, s, NEG_INF)
                    e = jnp.exp(s - jnp.max(s, axis=-1, keepdims=True))
                    denom = jnp.sum(e, axis=-1, keepdims=True)
                    heads.append(_dot(e.astype(BF16), vw) / denom)
                oc = jnp.where(first_head, heads[0], heads[1])
                o_ref[r0:r0 + rows, lanes] = oc * beta_ref[:, lanes]

    pl.when(i == 0)(lambda: attend(True))
    pl.when(i > 0)(lambda: attend(False))


def _attn_call(p_a, bias, beta, batch, seq, width, qb):
    assert qb == LEFT_CHUNKS * CHUNK, "key window = previous block + current block"
    tiles = seq // qb
    n_pairs = width // PAIR

    def cur(col):
        return pl.BlockSpec((qb, width), lambda b, t: (b * tiles + t, col))

    def prev(col):
        return pl.BlockSpec((qb, width), lambda b, t: (b * tiles + jnp.maximum(t - 1, 0), col))

    return pl.pallas_call(
        functools.partial(_attn_kernel, width=width, qb=qb),
        grid=(batch, tiles),
        in_specs=[cur(0), prev(1), cur(1), prev(2), cur(2),
                  pl.BlockSpec(bias.shape, lambda b, t: (0, 0, 0)),
                  pl.BlockSpec((1, width), lambda b, t: (0, 0))],
        out_specs=pl.BlockSpec((qb, width), lambda b, t: (b * tiles + t, 0)),
        out_shape=jax.ShapeDtypeStruct((batch * seq, width), F32),
        scratch_shapes=[pltpu.VMEM((2 * qb, width), BF16), pltpu.VMEM((2 * qb, width), BF16)],
        compiler_params=pltpu.CompilerParams(
            dimension_semantics=("parallel", "arbitrary"), vmem_limit_bytes=VMEM_LIMIT),
    )(p_a, p_a, p_a, p_a, p_a, bias, beta)


def _out_router_kernel(yr_ref, ya_ref, wo_ref, x_ref, mod_ref, g2_ref, wr_ref, br_ref,
                       x1_ref, h2_ref, route_ref, *, r_width):
    y = (_dot(yr_ref[...].astype(BF16), wo_ref[0:r_width, :])
         + _dot(ya_ref[...].astype(BF16), wo_ref[r_width:, :]))
    m = mod_ref[0]
    x1 = x_ref[...] + m[GATE1:GATE1 + 1] * y
    x1_ref[...] = x1
    h2 = _rms(x1) * g2_ref[...] * (1.0 + m[SCALE2:SCALE2 + 1]) + m[SHIFT2:SHIFT2 + 1]
    h2_ref[...] = h2

    wr = wr_ref[...]
    wr_hi = wr.astype(BF16)
    wr_lo = (wr - wr_hi.astype(F32)).astype(BF16)
    h2_hi = h2.astype(BF16)
    h2_lo = (h2 - h2_hi.astype(F32)).astype(BF16)
    hi_both = _dot(h2_hi, jnp.concatenate([wr_hi, wr_lo], axis=1))
    logits = _dot(h2_lo, wr_hi) + hi_both[:, LANES:] + hi_both[:, :LANES] + br_ref[...]
    li = lax.broadcasted_iota(jnp.int32, logits.shape, 1).astype(F32)
    none = jnp.float32(-jnp.inf)
    far = jnp.float32(LANES)
    is_group = li < N_GROUPS
    gl = jnp.where(is_group, logits, none)
    gmax = jnp.max(gl, axis=-1, keepdims=True)
    gidx = jnp.min(jnp.where(gl == gmax, li, far), axis=-1, keepdims=True)
    g_w = 1.0 / jnp.sum(jnp.where(is_group, jnp.exp(logits - gmax), 0.0), axis=-1, keepdims=True)
    lo = N_GROUPS + EXPERTS_PER_GROUP * gidx
    el = jnp.where(jnp.logical_and(li >= lo, li < lo + EXPERTS_PER_GROUP), logits, none)
    m1 = jnp.max(el, axis=-1, keepdims=True)
    i1 = jnp.min(jnp.where(el == m1, li, far), axis=-1, keepdims=True)
    el2 = jnp.where(li == i1, none, el)
    m2 = jnp.max(el2, axis=-1, keepdims=True)
    i2 = jnp.min(jnp.where(el2 == m2, li, far), axis=-1, keepdims=True)
    e2 = jnp.exp(m2 - m1)
    w1 = g_w / (1.0 + e2)
    w2 = g_w * e2 / (1.0 + e2)
    route = jnp.where(li == 0.0, i1 - N_GROUPS, 0.0) + jnp.where(li == 1.0, i2 - N_GROUPS, 0.0)
    route_ref[...] = route + jnp.where(li == 2.0, w1, 0.0) + jnp.where(li == 3.0, w2, 0.0)


def _out_router_call(y_r, y_a, wo, x2d, mod, g2, wr, br, seq, tm):
    rows, d = x2d.shape
    r_width = y_r.shape[1]
    a_width = y_a.shape[1]
    tiles_per_seq = seq // tm
    return pl.pallas_call(
        functools.partial(_out_router_kernel, r_width=r_width),
        grid=(rows // tm,),
        in_specs=[pl.BlockSpec((tm, r_width), lambda i: (i, 0)),
                  pl.BlockSpec((tm, a_width), lambda i: (i, 0)),
                  pl.BlockSpec(wo.shape, lambda i: (0, 0)),
                  pl.BlockSpec((tm, d), lambda i: (i, 0)),
                  pl.BlockSpec((1, 6, d), lambda i: (i // tiles_per_seq, 0, 0)),
                  pl.BlockSpec((1, d), lambda i: (0, 0)),
                  pl.BlockSpec((d, LANES), lambda i: (0, 0)),
                  pl.BlockSpec((1, LANES), lambda i: (0, 0))],
        out_specs=[pl.BlockSpec((tm, d), lambda i: (i, 0)),
                   pl.BlockSpec((tm, d), lambda i: (i, 0)),
                   pl.BlockSpec((tm, LANES), lambda i: (i, 0))],
        out_shape=[jax.ShapeDtypeStruct((rows, d), F32),
                   jax.ShapeDtypeStruct((rows, d), F32),
                   jax.ShapeDtypeStruct((rows, LANES), F32)],
        compiler_params=pltpu.CompilerParams(
            dimension_semantics=("parallel",), vmem_limit_bytes=VMEM_LIMIT),
    )(y_r, y_a, wo, x2d, mod, g2, wr, br)


def _moe_kernel(texp_ref, first_ref, next_ref, rows_ref, meta_ref, src_ref,
                h2_hbm, wg_hbm, wu_hbm, wd_hbm, o_ref,
                xbuf, xsem, wg_land, wu_land, wd_land, wsem, wg_bf, wu_bf, wd_bf, *, tm):
    t = pl.program_id(0)
    n_used = meta_ref[0]
    slot = lax.rem(t, MOE_GATHER_SLOTS)

    def weight_copies(e):
        cps = []
        for m, (hbm, land) in enumerate(((wg_hbm, wg_land), (wu_hbm, wu_land), (wd_hbm, wd_land))):
            rows = hbm.shape[1] // W_CHUNKS
            for ck in range(W_CHUNKS):
                part = pl.ds(ck * rows, rows)
                cps.append(pltpu.make_async_copy(hbm.at[e, part], land.at[part], wsem.at[m, ck]))
        return cps

    def row_groups(tile):
        return (rows_ref[tile] + ROW_GROUP - 1) // ROW_GROUP

    def gather_rows(tile, unrolled=False):
        buf_slot = lax.rem(tile, MOE_GATHER_SLOTS)
        n_groups = row_groups(tile)

        def body(g, carry):
            for k in range(ROW_GROUP):
                i = g * ROW_GROUP + k
                tok = src_ref[tile * tm + i]
                pltpu.make_async_copy(h2_hbm.at[pl.ds(tok, 1)], xbuf.at[buf_slot, pl.ds(i, 1)],
                                      xsem.at[buf_slot]).start()
            return carry

        if unrolled:
            for g in range(tm // ROW_GROUP):
                @pl.when(g < n_groups)
                def _(g=g):
                    body(g, 0)
        else:
            lax.fori_loop(0, n_groups, body, 0)

    @pl.when(t == 0)
    def _():
        xbuf[...] = jnp.zeros_like(xbuf)
        for cp in weight_copies(texp_ref[0]):
            cp.start()
        for ahead in range(MOE_GATHER_SLOTS - 1):
            @pl.when(ahead < n_used)
            def _():
                gather_rows(ahead)

    is_first = jnp.logical_and(t < n_used, first_ref[t] == 1)

    @pl.when(is_first)
    def _():
        for cp in weight_copies(texp_ref[t]):
            cp.wait()
        wg_bf[...] = wg_land[...].astype(BF16)
        wu_bf[...] = wu_land[...].astype(BF16)
        wd_bf[...] = wd_land[...].astype(BF16)

    @pl.when(t + MOE_GATHER_SLOTS - 1 < n_used)
    def _():
        gather_rows(t + MOE_GATHER_SLOTS - 1, unrolled=True)

    @pl.when(jnp.logical_and(is_first, next_ref[t] >= 0))
    def _():
        for cp in weight_copies(next_ref[t]):
            cp.start()

    @pl.when(t < n_used)
    def _():
        def wait_group(g, carry):
            r0 = pl.multiple_of(g * ROW_GROUP, ROW_GROUP)
            pltpu.make_async_copy(h2_hbm.at[pl.ds(0, ROW_GROUP)],
                                  xbuf.at[slot, pl.ds(r0, ROW_GROUP)], xsem.at[slot]).wait()
            return carry

        lax.fori_loop(0, row_groups(t), wait_group, 0)
        x = xbuf[slot].astype(BF16)
        gate = _dot(x, wg_bf[...])
        up = _dot(x, wu_bf[...])
        h = gate * _sigmoid(gate) * up
        o_ref[...] = _dot(h.astype(BF16), wd_bf[...])

    @pl.when(t >= n_used)
    def _():
        o_ref[...] = jnp.zeros_like(o_ref)


def _moe_call(plan, src, h2, wg, wu, wd, tm):
    n_e, d, f = wg.shape
    n_tiles = plan[0].shape[0]
    grid_spec = pltpu.PrefetchScalarGridSpec(
        num_scalar_prefetch=len(plan) + 1,
        grid=(n_tiles,),
        in_specs=[pl.BlockSpec(memory_space=pl.ANY)] * 4,
        out_specs=pl.BlockSpec((tm, d), lambda t, *_: (t, 0)),
        scratch_shapes=[pltpu.VMEM((MOE_GATHER_SLOTS, tm, d), F32),
                        pltpu.SemaphoreType.DMA((MOE_GATHER_SLOTS,)),
                        pltpu.VMEM((d, f), F32), pltpu.VMEM((d, f), F32), pltpu.VMEM((f, d), F32),
                        pltpu.SemaphoreType.DMA((3, W_CHUNKS)),
                        pltpu.VMEM((d, f), BF16), pltpu.VMEM((d, f), BF16), pltpu.VMEM((f, d), BF16)])
    return pl.pallas_call(
        functools.partial(_moe_kernel, tm=tm),
        grid_spec=grid_spec,
        out_shape=jax.ShapeDtypeStruct((n_tiles * tm, d), F32),
        compiler_params=pltpu.CompilerParams(
            dimension_semantics=("arbitrary",), vmem_limit_bytes=VMEM_LIMIT),
    )(*plan, src, h2, wg, wu, wd)


def _final_kernel(pos_ref, ys_hbm, route_ref, x1_ref, mod_ref, gf_ref, o_ref, ybuf, sem, *, tm):
    i = pl.program_id(0)
    n = pl.num_programs(0)
    slot = i % 2

    def gather_rows(tile, buf_slot, unrolled):
        base = tile * tm * 2

        def body(r, carry):
            for s in range(2):
                pltpu.make_async_copy(ys_hbm.at[pl.ds(pos_ref[base + 2 * r + s], 1)],
                                      ybuf.at[buf_slot, s, pl.ds(r, 1)], sem.at[buf_slot]).start()
            return carry

        if unrolled:
            for r in range(tm):
                body(r, 0)
        else:
            lax.fori_loop(0, tm, body, 0, unroll=4)

    @pl.when(i == 0)
    def _():
        gather_rows(0, 0, unrolled=False)

    @pl.when(i + 1 < n)
    def _():
        gather_rows(i + 1, 1 - slot, unrolled=True)

    for s in range(2):
        pltpu.make_async_copy(ys_hbm.at[pl.ds(0, tm)], ybuf.at[slot, s], sem.at[slot]).wait()
    route = route_ref[...]
    w0 = route[:, 2:3]
    w1 = route[:, 3:4]
    f = w0 * ybuf[slot, 0] + w1 * ybuf[slot, 1]
    x2 = x1_ref[...] + mod_ref[0][GATE2:GATE2 + 1] * f
    o_ref[...] = _rms(x2) * gf_ref[...]


def _final_call(pos, ys, route, x1, mod, gf, seq, tm):
    rows, d = x1.shape
    tiles_per_seq = seq // tm
    grid_spec = pltpu.PrefetchScalarGridSpec(
        num_scalar_prefetch=1,
        grid=(rows // tm,),
        in_specs=[pl.BlockSpec(memory_space=pl.ANY),
                  pl.BlockSpec((tm, LANES), lambda i, pos: (i, 0)),
                  pl.BlockSpec((tm, d), lambda i, pos: (i, 0)),
                  pl.BlockSpec((1, 6, d), lambda i, pos: (i // tiles_per_seq, 0, 0)),
                  pl.BlockSpec((1, d), lambda i, pos: (0, 0))],
        out_specs=pl.BlockSpec((tm, d), lambda i, pos: (i, 0)),
        scratch_shapes=[pltpu.VMEM((2, 2, tm, d), F32), pltpu.SemaphoreType.DMA((2,))])
    return pl.pallas_call(
        functools.partial(_final_kernel, tm=tm),
        grid_spec=grid_spec,
        out_shape=jax.ShapeDtypeStruct((rows, d), F32),
        compiler_params=pltpu.CompilerParams(
            dimension_semantics=("arbitrary",), vmem_limit_bytes=VMEM_LIMIT),
    )(pos, ys, route, x1, mod, gf)


def _route_plan(route, tm):
    n = route.shape[0]
    flat_e = route[:, 0:2].astype(jnp.int32).reshape(-1)
    n_tiles = (2 * n + N_EXPERTS * (tm - 1)) // tm
    experts = jnp.arange(N_EXPERTS, dtype=jnp.int32)
    onehot = (flat_e[:, None] == experts[None, :]).astype(jnp.int32)
    csum = jnp.cumsum(onehot, axis=0)
    rank = jnp.sum(csum * onehot, axis=1) - 1
    counts = csum[-1]
    tiles_per_e = (counts + tm - 1) // tm
    tile_end = jnp.cumsum(tiles_per_e)
    tile_start = tile_end - tiles_per_e
    n_used = tile_end[-1]
    pos = jnp.sum(onehot * tile_start[None, :], axis=1) * tm + rank
    token = jnp.arange(2 * n, dtype=jnp.int32) // 2
    src = jnp.zeros((n_tiles * tm,), jnp.int32).at[pos].set(token)
    tiles = jnp.arange(n_tiles, dtype=jnp.int32)
    texp = jnp.sum((tiles[:, None] >= tile_end[None, :]).astype(jnp.int32), axis=1)
    texp = jnp.where(tiles < n_used, texp, texp[n_used - 1])
    first = jnp.concatenate([jnp.ones((1,), jnp.int32), (texp[1:] != texp[:-1]).astype(jnp.int32)])
    later_used = jnp.logical_and(experts[None, :] > experts[:, None], (tiles_per_e > 0)[None, :])
    next_used = jnp.min(jnp.where(later_used, experts[None, :], N_EXPERTS), axis=1)
    next_used = jnp.where(next_used == N_EXPERTS, -1, next_used)
    tile_rows = jnp.clip(counts[texp] - (tiles - tile_start[texp]) * tm, 0, tm)
    return (texp, first, next_used[texp], tile_rows, n_used.reshape(1)), src, pos


def _round_up(n, m):
    return (n + m - 1) // m * m


def _tile_sizes(seq):
    return dict(ada_cols=1536, rows=min(512, seq), proj_rows=min(1024, seq),
                rwkv_rows=min(4 * CHUNK, seq), attn_rows=min(LEFT_CHUNKS * CHUNK, seq),
                expert_rows=256, final_rows=min(256, seq))


def kernel(x, c, w_ada, b_ada, norm1_g, w_in, mu_shift, w0, w_decay_up, a0, w_aaa_up, w_gate_up,
           k_k, k_a, r_k, ln_x_w, ln_x_b, rel_bias, beta_rwkv, beta_attn, w_out, norm2_g, w_group,
           b_group, w_expert, b_expert, w_gate, w_up, w_down, norm_f_g):
    assert w_ada.shape[0] == 1, "single trunk layer"
    batch, seq, d = x.shape
    r_width = w0.shape[1]
    a_width = beta_attn.shape[1]
    shift_width = mu_shift.shape[1]
    assert shift_width == 3 * r_width + DECAY_LORA + AAA_LORA + GATE_LORA
    x2d = x.reshape(batch * seq, d)
    row = lambda t: t.reshape(1, -1)

    tiles = _tile_sizes(seq)
    tm = tiles["rows"]

    c8 = jnp.pad(c, ((0, SUBLANES - batch), (0, 0)))
    mod = _ada_call(c8, w_ada[0], row(b_ada[0]), tn=tiles["ada_cols"])[:batch].reshape(batch, 6, d)

    pw = _round_up(shift_width, 3 * LANES)
    h1 = _hmod_call(x2d, row(norm1_g[0]), mod, tm, seq)
    w_in_t = jnp.swapaxes(w_in[0], 0, 1)
    p_r = _matmul_call(h1, w_in_t, 0, pw, pw // 3, tiles["proj_rows"], F32)
    p_a = _matmul_call(h1, w_in_t, shift_width, 3 * a_width, a_width, tiles["proj_rows"], BF16)

    mu = jnp.pad(mu_shift[0], (0, pw - shift_width))
    vecs = [row(mu), row(w0[0]), row(a0[0]), row(k_k[0]), row(k_a[0]), row(r_k[0]), row(ln_x_w[0]),
            row(ln_x_b[0]), row(beta_rwkv[0])]
    y_r = _rwkv_call(p_r, vecs, w_decay_up[0], w_aaa_up[0], w_gate_up[0], batch, seq, r_width,
                     tb=tiles["rwkv_rows"])

    left = LEFT_CHUNKS * CHUNK
    dist = left + CHUNK - 1 - jnp.arange(BAND + CHUNK - 1)
    base = rel_bias[0][:, jnp.clip(dist, -REL_CLIP, REL_CLIP) + REL_CLIP].astype(F32)
    bias = jnp.stack([base[:, CHUNK - 1 - qi:CHUNK - 1 - qi + BAND] for qi in range(CHUNK)], axis=1)
    nb = ATTN_BLOCK_CHUNKS
    bias = jnp.concatenate(
        [jnp.pad(bias, ((0, 0), (0, 0), (cb * CHUNK, (nb - 1 - cb) * CHUNK)), constant_values=NEG_INF)
         for cb in range(nb)], axis=1)
    bias = bias.reshape(a_width // PAIR, 2 * nb * CHUNK, BAND + (nb - 1) * CHUNK)
    y_a = _attn_call(p_a, bias, row(beta_attn[0]), batch, seq, a_width, qb=tiles["attn_rows"])

    n_route = N_GROUPS + N_EXPERTS
    wr = jnp.concatenate([w_group[0], w_expert[0].transpose(1, 0, 2).reshape(d, N_EXPERTS)], axis=1)
    wr = jnp.pad(wr, ((0, 0), (0, LANES - n_route)))
    br = jnp.pad(jnp.concatenate([b_group[0], b_expert[0].reshape(-1)]), (0, LANES - n_route))
    x1, h2, route = _out_router_call(y_r, y_a, w_out[0].astype(BF16), x2d, mod, row(norm2_g[0]),
                                     wr, row(br), seq, tm)

    f = w_gate.shape[-1]
    tm_e = tiles["expert_rows"]
    plan, src, pos = _route_plan(route, tm_e)
    ys = _moe_call(plan, src, h2, w_gate[0].reshape(N_EXPERTS, d, f),
                   w_up[0].reshape(N_EXPERTS, d, f), w_down[0].reshape(N_EXPERTS, f, d), tm_e)
    out = _final_call(pos, ys, route, x1, mod, row(norm_f_g), seq, tiles["final_rows"])
    return out.reshape(batch, seq, d)
```
